```python
import jax, jax.numpy as jnp
from jax import lax
import numpy as np

D_MODEL = 2048
BATCH = 8
SEQ = 4096
DEPTH = 4

N_HEADS = 16
HEAD_DIM = 128
D_LIN = N_HEADS * HEAD_DIM
CONV_QKV = 4
CHUNK = 64
POOL_WINDOWS = (2, 4, 8, 16)
N_POOL_GROUPS = len(POOL_WINDOWS)
D_POOL = D_MODEL
POOL_GROUP_DIM = D_POOL // N_POOL_GROUPS
D_FF = 5632
CONV_FFN = 3
EPS = 1e-6
D_IN_PROJ = 4 * D_LIN + 2 * N_HEADS + D_POOL + 2 * D_MODEL

kernel_name = "hybrid_gdn_pool_convglu_trunk"


def rmsnorm(x, w):
    xf = x.astype(jnp.float32)
    y = xf * lax.rsqrt(jnp.mean(xf * xf, axis=-1, keepdims=True) + EPS)
    return (y * w.astype(jnp.float32)).astype(x.dtype)


def l2norm(x):
    xf = x.astype(jnp.float32)
    return xf * lax.rsqrt(jnp.sum(xf * xf, axis=-1, keepdims=True) + EPS)


def causal_dwconv(x, w):
    width = w.shape[0]
    seq = x.shape[1]
    xp = jnp.pad(x, ((0, 0), (width - 1, 0), (0, 0)))
    y = xp[:, 0:seq] * w[0]
    for j in range(1, width):
        y = y + xp[:, j:j + seq] * w[j]
    return y


def gated_delta_rule(q, k, v, g, beta):
    f32 = jnp.float32
    q, k, v, g, beta = (t.astype(f32) for t in (q, k, v, g, beta))
    bsz, seq, nh, dk = q.shape
    dv = v.shape[-1]
    pad = (-seq) % CHUNK
    if pad:
        q, k, v = (jnp.pad(t, ((0, 0), (0, pad), (0, 0), (0, 0))) for t in (q, k, v))
        g, beta = (jnp.pad(t, ((0, 0), (0, pad), (0, 0))) for t in (g, beta))
    n_chunks = (seq + pad) // CHUNK

    def to_chunks(t):
        return t.reshape(bsz, n_chunks, CHUNK, nh, t.shape[-1]).transpose(0, 3, 1, 2, 4)

    q, k, v = to_chunks(q), to_chunks(k), to_chunks(v)
    g = g.reshape(bsz, n_chunks, CHUNK, nh).transpose(0, 3, 1, 2)
    beta = beta.reshape(bsz, n_chunks, CHUNK, nh).transpose(0, 3, 1, 2)
    g = jnp.cumsum(g, axis=-1)

    tril = jnp.tril(jnp.ones((CHUNK, CHUNK), dtype=bool))
    strict = jnp.tril(jnp.ones((CHUNK, CHUNK), dtype=bool), k=-1)
    diff = g[..., :, None] - g[..., None, :]
    decay = jnp.exp(jnp.where(tril, diff, -jnp.inf))

    k_beta = k * beta[..., None]
    v_beta = v * beta[..., None]
    lmat = jnp.where(strict, jnp.einsum('bhnid,bhnjd->bhnij', k_beta, k) * decay, 0.0)
    amat = lmat + jnp.eye(CHUNK, dtype=f32)
    rhs = jnp.concatenate([v_beta, k_beta * jnp.exp(g)[..., None]], axis=-1)
    sol = lax.linalg.triangular_solve(amat, rhs, left_side=True, lower=True, unit_diagonal=True)
    u_val = sol[..., :dv]
    w_cum = sol[..., dv:]

    attn_intra = jnp.einsum('bhnid,bhnjd->bhnij', q, k) * decay
    q_decay = q * jnp.exp(g)[..., None]
    k_to_end = k * jnp.exp(g[..., -1:] - g)[..., None]
    chunk_decay = jnp.exp(g[..., -1])

    xs = tuple(jnp.moveaxis(t, 2, 0) for t in (u_val, w_cum, q_decay, attn_intra, k_to_end, chunk_decay))

    def step(state, inp):
        u_c, w_c, qd_c, at_c, ke_c, dec_c = inp
        v_new = u_c - jnp.einsum('bhcd,bhdv->bhcv', w_c, state)
        o_c = jnp.einsum('bhcd,bhdv->bhcv', qd_c, state) + jnp.einsum('bhij,bhjv->bhiv', at_c, v_new)
        state = state * dec_c[..., None, None] + jnp.einsum('bhcd,bhcv->bhdv', ke_c, v_new)
        return state, o_c

    state0 = jnp.zeros((bsz, nh, dk, dv), dtype=f32)
    _, o = lax.scan(step, state0, xs)
    o = o.transpose(1, 0, 3, 2, 4).reshape(bsz, n_chunks * CHUNK, nh, dv)
    return o[:, :seq]


def causal_multiscale_pool(u):
    seq = u.shape[1]
    uf = u.astype(jnp.float32)
    cs = jnp.pad(jnp.cumsum(uf, axis=1), ((0, 0), (1, 0), (0, 0), (0, 0)))
    t = jnp.arange(seq)
    outs = []
    for gi, win in enumerate(POOL_WINDOWS):
        hi = cs[:, 1:, gi]
        lo = cs[:, jnp.maximum(t + 1 - win, 0), gi]
        cnt = jnp.minimum(t + 1, win).astype(jnp.float32)
        outs.append((hi - lo) / cnt[None, :, None])
    pooled = jnp.stack(outs, axis=2)
    return (pooled - uf).astype(u.dtype)


def _fwd_setup_inputs(seed: int = 0) -> dict:
    key = jax.random.key(seed)
    ks = jax.random.split(key, 20)
    f32 = jnp.float32

    def nrm(k, shape, scale):
        return jax.random.normal(k, shape, f32) * scale

    def gain(k, shape):
        return 1.0 + 0.02 * jax.random.normal(k, shape, f32)

    x = jax.random.normal(ks[0], (BATCH, SEQ, D_MODEL), f32)
    a_init = jax.random.uniform(ks[4], (DEPTH, N_HEADS), f32, 1.0, 16.0)
    dt = jnp.exp(jax.random.uniform(ks[5], (DEPTH, N_HEADS), f32, np.log(1e-3), np.log(1e-1)))
    dt_bias = dt + jnp.log(-jnp.expm1(-dt))
    return {
        "x": x,
        "norm_mix_w": gain(ks[1], (DEPTH, D_MODEL)),
        "w_in": nrm(ks[2], (DEPTH, D_MODEL, D_IN_PROJ), D_MODEL ** -0.5),
        "conv_qkv_w": nrm(ks[3], (DEPTH, CONV_QKV, 3 * D_LIN), CONV_QKV ** -0.5),
        "a_log": jnp.log(a_init),
        "dt_bias": dt_bias,
        "gdn_norm_w": gain(ks[6], (DEPTH, HEAD_DIM)),
        "pool_w": nrm(ks[7], (DEPTH, N_POOL_GROUPS, POOL_GROUP_DIM, POOL_GROUP_DIM), POOL_GROUP_DIM ** -0.5),
        "pool_scale": gain(ks[8], (DEPTH, D_POOL)),
        "w_out": nrm(ks[9], (DEPTH, D_MODEL, D_MODEL), D_MODEL ** -0.5),
        "norm_ffn_w": gain(ks[10], (DEPTH, D_MODEL)),
        "w_up": nrm(ks[11], (DEPTH, D_MODEL, 2 * D_FF), D_MODEL ** -0.5),
        "conv_ffn_w": nrm(ks[12], (DEPTH, CONV_FFN, D_FF), CONV_FFN ** -0.5),
        "conv_ffn_b": nrm(ks[13], (DEPTH, D_FF), 0.02),
        "w_down": nrm(ks[14], (DEPTH, D_FF, D_MODEL), D_FF ** -0.5),
        "norm_final_w": gain(ks[15], (D_MODEL,)),
    }


def _fwd_reference(x, norm_mix_w, w_in, conv_qkv_w, a_log, dt_bias, gdn_norm_w, pool_w, pool_scale,
              w_out, norm_ffn_w, w_up, conv_ffn_w, conv_ffn_b, w_down, norm_final_w):
    bsz, seq, _ = x.shape
    splits = np.cumsum([D_LIN, D_LIN, D_LIN, D_LIN, N_HEADS, N_HEADS, D_POOL, D_MODEL]).tolist()
    for l in range(DEPTH):
        h = rmsnorm(x, norm_mix_w[l])
        proj = jnp.einsum('bsd,de->bse', h, w_in[l])
        q, k, v, z, b_raw, a_raw, p_in, g_a, g_b = jnp.split(proj, splits, axis=-1)

        qkv = jax.nn.silu(causal_dwconv(jnp.concatenate([q, k, v], axis=-1), conv_qkv_w[l]))
        q, k, v = jnp.split(qkv, 3, axis=-1)
        q = l2norm(q.reshape(bsz, seq, N_HEADS, HEAD_DIM)) * (HEAD_DIM ** -0.5)
        k = l2norm(k.reshape(bsz, seq, N_HEADS, HEAD_DIM))
        v = v.reshape(bsz, seq, N_HEADS, HEAD_DIM)
        beta = jax.nn.sigmoid(b_raw.astype(jnp.float32))
        g_log = -jnp.exp(a_log[l].astype(jnp.float32)) * jax.nn.softplus(
            a_raw.astype(jnp.float32) + dt_bias[l].astype(jnp.float32))
        o = gated_delta_rule(q, k, v, g_log, beta).astype(x.dtype)
        o = rmsnorm(o, gdn_norm_w[l]) * jax.nn.silu(z.reshape(bsz, seq, N_HEADS, HEAD_DIM))
        y_a = o.reshape(bsz, seq, D_LIN)

        pooled = causal_multiscale_pool(p_in.reshape(bsz, seq, N_POOL_GROUPS, POOL_GROUP_DIM))
        y_b = jnp.einsum('bsgc,gcd->bsgd', pooled, pool_w[l]).reshape(bsz, seq, D_POOL) * pool_scale[l]

        mixed = jax.nn.sigmoid(g_a) * y_a + jax.nn.sigmoid(g_b) * y_b
        x = x + jnp.einsum('bsd,de->bse', mixed, w_out[l])

        h = rmsnorm(x, norm_ffn_w[l])
        gate, up = jnp.split(jnp.einsum('bsd,df->bsf', h, w_up[l]), 2, axis=-1)
        gate = causal_dwconv(gate, conv_ffn_w[l]) + conv_ffn_b[l]
        x = x + jnp.einsum('bsf,fd->bsd', jax.nn.gelu(gate, approximate=False) * up, w_down[l])
    return rmsnorm(x, norm_final_w)


import jax as _jax
import jax.numpy as _jnp

TWIN_FORMAT = 'train_step'
FWD_PARAMS = ['x', 'norm_mix_w', 'w_in', 'conv_qkv_w', 'a_log', 'dt_bias', 'gdn_norm_w', 'pool_w', 'pool_scale', 'w_out', 'norm_ffn_w', 'w_up', 'conv_ffn_w', 'conv_ffn_b', 'w_down', 'norm_final_w']
TWIN_WEIGHTS = ['norm_mix_w', 'w_in', 'conv_qkv_w', 'a_log', 'dt_bias', 'gdn_norm_w', 'pool_w', 'pool_scale', 'w_out', 'norm_ffn_w', 'w_up', 'conv_ffn_w', 'conv_ffn_b', 'w_down', 'norm_final_w']
TWIN_DIFF_INPUT = 'x'
TWIN_INPUTS = ['x', 'norm_mix_w', 'w_in', 'conv_qkv_w', 'a_log', 'dt_bias', 'gdn_norm_w', 'pool_w', 'pool_scale', 'w_out', 'norm_ffn_w', 'w_up', 'conv_ffn_w', 'conv_ffn_b', 'w_down', 'norm_final_w', 'loss_target', 'm_norm_mix_w', 'm_w_in', 'm_conv_qkv_w', 'm_a_log', 'm_dt_bias', 'm_gdn_norm_w', 'm_pool_w', 'm_pool_scale', 'm_w_out', 'm_norm_ffn_w', 'm_w_up', 'm_conv_ffn_w', 'm_conv_ffn_b', 'm_w_down', 'm_norm_final_w', 'v_norm_mix_w', 'v_w_in', 'v_conv_qkv_w', 'v_a_log', 'v_dt_bias', 'v_gdn_norm_w', 'v_pool_w', 'v_pool_scale', 'v_w_out', 'v_norm_ffn_w', 'v_w_up', 'v_conv_ffn_w', 'v_conv_ffn_b', 'v_w_down', 'v_norm_final_w']
TWIN_OUTPUTS = ['loss', 'grad_x', 'grad_norm_mix_w', 'grad_w_in', 'grad_conv_qkv_w', 'grad_a_log', 'grad_dt_bias', 'grad_gdn_norm_w', 'grad_pool_w', 'grad_pool_scale', 'grad_w_out', 'grad_norm_ffn_w', 'grad_w_up', 'grad_conv_ffn_w', 'grad_conv_ffn_b', 'grad_w_down', 'grad_norm_final_w', 'delta_norm_mix_w', 'delta_w_in', 'delta_conv_qkv_w', 'delta_a_log', 'delta_dt_bias', 'delta_gdn_norm_w', 'delta_pool_w', 'delta_pool_scale', 'delta_w_out', 'delta_norm_ffn_w', 'delta_w_up', 'delta_conv_ffn_w', 'delta_conv_ffn_b', 'delta_w_down', 'delta_norm_final_w', 'new_m_norm_mix_w', 'new_m_w_in', 'new_m_conv_qkv_w', 'new_m_a_log', 'new_m_dt_bias', 'new_m_gdn_norm_w', 'new_m_pool_w', 'new_m_pool_scale', 'new_m_w_out', 'new_m_norm_ffn_w', 'new_m_w_up', 'new_m_conv_ffn_w', 'new_m_conv_ffn_b', 'new_m_w_down', 'new_m_norm_final_w', 'new_v_norm_mix_w', 'new_v_w_in', 'new_v_conv_qkv_w', 'new_v_a_log', 'new_v_dt_bias', 'new_v_gdn_norm_w', 'new_v_pool_w', 'new_v_pool_scale', 'new_v_w_out', 'new_v_norm_ffn_w', 'new_v_w_up', 'new_v_conv_ffn_w', 'new_v_conv_ffn_b', 'new_v_w_down', 'new_v_norm_final_w']
TWIN_LEAF_KINDS = {'loss': 'loss', 'grad_x': 'grad_x', 'grad_norm_mix_w': 'grad_w', 'grad_w_in': 'grad_w', 'grad_conv_qkv_w': 'grad_w', 'grad_a_log': 'grad_w', 'grad_dt_bias': 'grad_w', 'grad_gdn_norm_w': 'grad_w', 'grad_pool_w': 'grad_w', 'grad_pool_scale': 'grad_w', 'grad_w_out': 'grad_w', 'grad_norm_ffn_w': 'grad_w', 'grad_w_up': 'grad_w', 'grad_conv_ffn_w': 'grad_w', 'grad_conv_ffn_b': 'grad_w', 'grad_w_down': 'grad_w', 'grad_norm_final_w': 'grad_w', 'delta_norm_mix_w': 'delta_w', 'delta_w_in': 'delta_w', 'delta_conv_qkv_w': 'delta_w', 'delta_a_log': 'delta_w', 'delta_dt_bias': 'delta_w', 'delta_gdn_norm_w': 'delta_w', 'delta_pool_w': 'delta_w', 'delta_pool_scale': 'delta_w', 'delta_w_out': 'delta_w', 'delta_norm_ffn_w': 'delta_w', 'delta_w_up': 'delta_w', 'delta_conv_ffn_w': 'delta_w', 'delta_conv_ffn_b': 'delta_w', 'delta_w_down': 'delta_w', 'delta_norm_final_w': 'delta_w', 'new_m_norm_mix_w': 'new_m', 'new_m_w_in': 'new_m', 'new_m_conv_qkv_w': 'new_m', 'new_m_a_log': 'new_m', 'new_m_dt_bias': 'new_m', 'new_m_gdn_norm_w': 'new_m', 'new_m_pool_w': 'new_m', 'new_m_pool_scale': 'new_m', 'new_m_w_out': 'new_m', 'new_m_norm_ffn_w': 'new_m', 'new_m_w_up': 'new_m', 'new_m_conv_ffn_w': 'new_m', 'new_m_conv_ffn_b': 'new_m', 'new_m_w_down': 'new_m', 'new_m_norm_final_w': 'new_m', 'new_v_norm_mix_w': 'new_v', 'new_v_w_in': 'new_v', 'new_v_conv_qkv_w': 'new_v', 'new_v_a_log': 'new_v', 'new_v_dt_bias': 'new_v', 'new_v_gdn_norm_w': 'new_v', 'new_v_pool_w': 'new_v', 'new_v_pool_scale': 'new_v', 'new_v_w_out': 'new_v', 'new_v_norm_ffn_w': 'new_v', 'new_v_w_up': 'new_v', 'new_v_conv_ffn_w': 'new_v', 'new_v_conv_ffn_b': 'new_v', 'new_v_w_down': 'new_v', 'new_v_norm_final_w': 'new_v'}


def _forward(args):
    return _fwd_reference(*[args[k] for k in FWD_PARAMS])


def _output_shape():
    def fwd():
        inp = _fwd_setup_inputs(0)
        return _fwd_reference(*[inp[k] for k in FWD_PARAMS])
    out = _jax.eval_shape(fwd)
    return out.shape, out.dtype

N_MICROBATCH = 1
ADAM_LR = 0.001
ADAM_B1 = 0.9
ADAM_B2 = 0.999
ADAM_EPS = 1e-08
ADAM_WD = 0.01
ADAM_STEP = 10
PER_EXAMPLE_BATCH_AXIS = {'x': 0, 'loss_target': 0}
SHARED_INPUTS = []
_WEIGHT_DTYPES = {'norm_mix_w': _jnp.float32, 'w_in': _jnp.float32, 'conv_qkv_w': _jnp.float32, 'a_log': _jnp.float32, 'dt_bias': _jnp.float32, 'gdn_norm_w': _jnp.float32, 'pool_w': _jnp.float32, 'pool_scale': _jnp.float32, 'w_out': _jnp.float32, 'norm_ffn_w': _jnp.float32, 'w_up': _jnp.float32, 'conv_ffn_w': _jnp.float32, 'conv_ffn_b': _jnp.float32, 'w_down': _jnp.float32, 'norm_final_w': _jnp.float32}
MOMENT_SCALE = {'norm_mix_w': 6.300979e-02, 'w_in': 2.345386e-02, 'conv_qkv_w': 2.034755e-02, 'a_log': 1.132392e-01, 'dt_bias': 1.085807e-01, 'gdn_norm_w': 9.894071e-02, 'pool_w': 3.941816e-02, 'pool_scale': 3.962196e-02, 'w_out': 4.720887e-02, 'norm_ffn_w': 6.282593e-02, 'w_up': 2.723054e-02, 'conv_ffn_w': 2.765726e-02, 'conv_ffn_b': 2.649845e-02, 'w_down': 4.445139e-02, 'norm_final_w': 1.599134e+01}


def _to_microbatches(a, axis):
    t = _jnp.moveaxis(a, axis, 0)
    t = t.reshape((N_MICROBATCH, t.shape[0] // N_MICROBATCH) + t.shape[1:])
    return _jnp.moveaxis(t, 1, axis + 1)


def setup_inputs(seed: int = 0) -> dict:
    inp = _fwd_setup_inputs(seed)
    key = _jax.random.fold_in(_jax.random.key(seed), 7919)
    shape, _ = _output_shape()
    out = dict(inp)
    out["loss_target"] = _jax.random.normal(_jax.random.fold_in(key, 0), shape, _jnp.float32)
    for i, name in enumerate(TWIN_WEIGHTS):
        w = inp[name].astype(_jnp.float32)
        if MOMENT_SCALE is None:
            s = _jnp.sqrt(_jnp.mean(_jnp.square(w)) + 1e-30)
        else:
            s = MOMENT_SCALE[name]
        km, kv = _jax.random.split(_jax.random.fold_in(key, i + 1))
        out[name] = w
        out["m_" + name] = s * _jax.random.normal(km, w.shape, _jnp.float32)
        out["v_" + name] = (s * s) * _jax.random.uniform(kv, w.shape, _jnp.float32, 0.5, 1.5)
    if N_MICROBATCH > 1:
        for name, axis in PER_EXAMPLE_BATCH_AXIS.items():
            out[name] = _to_microbatches(out[name], axis)
    return {'x': out['x'], 'norm_mix_w': out['norm_mix_w'], 'w_in': out['w_in'], 'conv_qkv_w': out['conv_qkv_w'], 'a_log': out['a_log'], 'dt_bias': out['dt_bias'], 'gdn_norm_w': out['gdn_norm_w'], 'pool_w': out['pool_w'], 'pool_scale': out['pool_scale'], 'w_out': out['w_out'], 'norm_ffn_w': out['norm_ffn_w'], 'w_up': out['w_up'], 'conv_ffn_w': out['conv_ffn_w'], 'conv_ffn_b': out['conv_ffn_b'], 'w_down': out['w_down'], 'norm_final_w': out['norm_final_w'], 'loss_target': out['loss_target'], 'm_norm_mix_w': out['m_norm_mix_w'], 'm_w_in': out['m_w_in'], 'm_conv_qkv_w': out['m_conv_qkv_w'], 'm_a_log': out['m_a_log'], 'm_dt_bias': out['m_dt_bias'], 'm_gdn_norm_w': out['m_gdn_norm_w'], 'm_pool_w': out['m_pool_w'], 'm_pool_scale': out['m_pool_scale'], 'm_w_out': out['m_w_out'], 'm_norm_ffn_w': out['m_norm_ffn_w'], 'm_w_up': out['m_w_up'], 'm_conv_ffn_w': out['m_conv_ffn_w'], 'm_conv_ffn_b': out['m_conv_ffn_b'], 'm_w_down': out['m_w_down'], 'm_norm_final_w': out['m_norm_final_w'], 'v_norm_mix_w': out['v_norm_mix_w'], 'v_w_in': out['v_w_in'], 'v_conv_qkv_w': out['v_conv_qkv_w'], 'v_a_log': out['v_a_log'], 'v_dt_bias': out['v_dt_bias'], 'v_gdn_norm_w': out['v_gdn_norm_w'], 'v_pool_w': out['v_pool_w'], 'v_pool_scale': out['v_pool_scale'], 'v_w_out': out['v_w_out'], 'v_norm_ffn_w': out['v_norm_ffn_w'], 'v_w_up': out['v_w_up'], 'v_conv_ffn_w': out['v_conv_ffn_w'], 'v_conv_ffn_b': out['v_conv_ffn_b'], 'v_w_down': out['v_w_down'], 'v_norm_final_w': out['v_norm_final_w']}


def _loss(weights, diff, rest, loss_target):
    with _jax.named_scope("forward"):
        args = {**rest, TWIN_DIFF_INPUT: diff, **{k: w.astype(_WEIGHT_DTYPES[k]) for k, w in weights.items()}}
        y = _forward(args)
    with _jax.named_scope("loss_head"):
        err = _jnp.square(y.astype(_jnp.float32) - loss_target)
        return 0.5 * _jnp.sum(_jnp.mean(err, axis=-1)) if err.ndim else 0.5 * err


def _adamw(w, g, m, v):
    m = ADAM_B1 * m + (1.0 - ADAM_B1) * g
    v = ADAM_B2 * v + (1.0 - ADAM_B2) * _jnp.square(g)
    m_hat = m / (1.0 - ADAM_B1 ** ADAM_STEP)
    v_hat = v / (1.0 - ADAM_B2 ** ADAM_STEP)
    delta = -ADAM_LR * (m_hat / (_jnp.sqrt(v_hat) + ADAM_EPS) + ADAM_WD * w)
    return delta, m, v


def reference(x, norm_mix_w, w_in, conv_qkv_w, a_log, dt_bias, gdn_norm_w, pool_w, pool_scale, w_out, norm_ffn_w, w_up, conv_ffn_w, conv_ffn_b, w_down, norm_final_w, loss_target, m_norm_mix_w, m_w_in, m_conv_qkv_w, m_a_log, m_dt_bias, m_gdn_norm_w, m_pool_w, m_pool_scale, m_w_out, m_norm_ffn_w, m_w_up, m_conv_ffn_w, m_conv_ffn_b, m_w_down, m_norm_final_w, v_norm_mix_w, v_w_in, v_conv_qkv_w, v_a_log, v_dt_bias, v_gdn_norm_w, v_pool_w, v_pool_scale, v_w_out, v_norm_ffn_w, v_w_up, v_conv_ffn_w, v_conv_ffn_b, v_w_down, v_norm_final_w):
    given = dict(x=x, norm_mix_w=norm_mix_w, w_in=w_in, conv_qkv_w=conv_qkv_w, a_log=a_log, dt_bias=dt_bias, gdn_norm_w=gdn_norm_w, pool_w=pool_w, pool_scale=pool_scale, w_out=w_out, norm_ffn_w=norm_ffn_w, w_up=w_up, conv_ffn_w=conv_ffn_w, conv_ffn_b=conv_ffn_b, w_down=w_down, norm_final_w=norm_final_w, loss_target=loss_target, m_norm_mix_w=m_norm_mix_w, m_w_in=m_w_in, m_conv_qkv_w=m_conv_qkv_w, m_a_log=m_a_log, m_dt_bias=m_dt_bias, m_gdn_norm_w=m_gdn_norm_w, m_pool_w=m_pool_w, m_pool_scale=m_pool_scale, m_w_out=m_w_out, m_norm_ffn_w=m_norm_ffn_w, m_w_up=m_w_up, m_conv_ffn_w=m_conv_ffn_w, m_conv_ffn_b=m_conv_ffn_b, m_w_down=m_w_down, m_norm_final_w=m_norm_final_w, v_norm_mix_w=v_norm_mix_w, v_w_in=v_w_in, v_conv_qkv_w=v_conv_qkv_w, v_a_log=v_a_log, v_dt_bias=v_dt_bias, v_gdn_norm_w=v_gdn_norm_w, v_pool_w=v_pool_w, v_pool_scale=v_pool_scale, v_w_out=v_w_out, v_norm_ffn_w=v_norm_ffn_w, v_w_up=v_w_up, v_conv_ffn_w=v_conv_ffn_w, v_conv_ffn_b=v_conv_ffn_b, v_w_down=v_w_down, v_norm_final_w=v_norm_final_w)
    weights = {n: given[n] for n in TWIN_WEIGHTS}
    shared = {n: given[n] for n in SHARED_INPUTS}
    per_example = {n: given[n] for n in ['x']}
    grad_fn = _jax.value_and_grad(_loss, argnums=(0, 1))

    def one_microbatch(ex, loss_target):
        ex = dict(ex)
        diff = ex.pop(TWIN_DIFF_INPUT)
        return grad_fn(weights, diff, {**shared, **ex}, loss_target)

    if N_MICROBATCH == 1:
        loss, (grad_w, grad_x) = one_microbatch(per_example, given["loss_target"])
    else:
        def body(carry, xs):
            loss_sum, grad_sum = carry
            l_k, (gw_k, gx_k) = one_microbatch(xs[0], xs[1])
            with _jax.named_scope("update"):
                return (loss_sum + l_k, _jax.tree.map(_jnp.add, grad_sum, gw_k)), gx_k

        init = (_jnp.zeros((), _jnp.float32), _jax.tree.map(_jnp.zeros_like, weights))
        (loss, grad_w), grad_x = _jax.lax.scan(body, init, (per_example, given["loss_target"]))
    with _jax.named_scope("update"):
        delta_w, new_m, new_v = {}, {}, {}
        for n in TWIN_WEIGHTS:
            delta_w[n], new_m[n], new_v[n] = _adamw(weights[n], grad_w[n], given["m_" + n], given["v_" + n])
    return (loss, grad_x, *[grad_w[n] for n in TWIN_WEIGHTS], *[delta_w[n] for n in TWIN_WEIGHTS],
            *[new_m[n] for n in TWIN_WEIGHTS], *[new_v[n] for n in TWIN_WEIGHTS])
```

```python
import collections
import functools

import jax
import jax.numpy as jnp
from jax import lax
from jax.experimental import pallas as pl
from jax.experimental.pallas import tpu as pltpu

F32 = jnp.float32
BF16 = jnp.bfloat16
MESH_AXES = ("x", "y", "c")
N_DEV = 8

NORM_EPS = 1e-6
HEAD_DIM = 128
GDN_CHUNK = 64
POOL_WINDOWS = (2, 4, 8, 16)
POOL_HALO = 16
CONV_HALO = 8
LANES = 128
V7X_VMEM_LIMIT_BYTES = 56 * 1024 * 1024

ADAM_LR = 0.001
ADAM_B1 = 0.9
ADAM_B2 = 0.999
ADAM_EPS = 1e-08
ADAM_WD = 0.01
ADAM_STEP = 10


def _mx(v):
    return v.astype(BF16)


def _dot(a, b, ta=False, tb=False, precision=None):
    dims = (((0 if ta else 1,), (1 if tb else 0,)), ((), ()))
    return lax.dot_general(a, b, dims, precision=precision, preferred_element_type=F32)


def _dot_hi(a, b, ta=False, tb=False):
    return _dot(a, b, ta, tb, precision=lax.Precision.HIGHEST)


def _tile(dim, target, quantum=LANES):
    if dim <= target:
        return dim
    t = (target // quantum) * quantum
    while t >= quantum:
        if dim % t == 0:
            return t
        t -= quantum
    return dim


def _params(*semantics):
    return pltpu.CompilerParams(dimension_semantics=semantics, vmem_limit_bytes=V7X_VMEM_LIMIT_BYTES)


def _sigmoid(v):
    return 1.0 / (1.0 + jnp.exp(-v))


def _softplus(v):
    return jnp.maximum(v, 0.0) + jnp.log(1.0 + jnp.exp(-jnp.abs(v)))


_ERF_NUM = (-2.72614225801306e-10, 2.77068142495902e-08, -2.10102402082508e-06, -5.69250639462346e-05,
            -7.34990630326855e-04, -2.95459980854025e-03, -1.60960333262415e-02)
_ERF_DEN = (-1.45660718464996e-05, -2.13374055278905e-04, -1.68282697438203e-03, -7.37332916720468e-03,
            -1.42647390514189e-02)


def _erf(v):
    v = jnp.clip(v, -4.0, 4.0)
    v2 = v * v
    num = jnp.full_like(v, _ERF_NUM[0])
    for coef in _ERF_NUM[1:]:
        num = num * v2 + coef
    den = jnp.full_like(v, _ERF_DEN[0])
    for coef in _ERF_DEN[1:]:
        den = den * v2 + coef
    return v * num / den


def _gelu(v):
    return 0.5 * v * (1.0 + _erf(v * (2.0 ** -0.5)))


def _gelu_grad(v):
    return 0.5 * (1.0 + _erf(v * (2.0 ** -0.5))) + v * jnp.exp(-0.5 * v * v) * ((2.0 * jnp.pi) ** -0.5)


def _rows_before(cat, shift, halo):
    return pltpu.roll(cat, shift, 0)[halo:]


def _rows_after(cat, shift, rows):
    return pltpu.roll(cat, cat.shape[0] - shift, 0)[:rows]


def _accumulate(ref, value, first):
    @pl.when(first)
    def _():
        ref[...] = value

    @pl.when(jnp.logical_not(first))
    def _():
        ref[...] += value


def _matmul(a, b, *, name, ta=False, tb=False, add=None, out_dtype=F32, tm=512, tn=1024, tk=2048):
    m, k = (a.shape[1], a.shape[0]) if ta else a.shape
    n = b.shape[0] if tb else b.shape[1]
    assert (b.shape[1] if tb else b.shape[0]) == k
    tm, tn, tk = _tile(m, tm), _tile(n, tn), _tile(k, tk)
    nk = k // tk
    has_add = add is not None

    def body(*refs):
        a_ref, b_ref = refs[0], refs[1]
        add_ref = refs[2] if has_add else None
        o_ref = refs[3] if has_add else refs[2]
        acc_ref = refs[-1]
        kk = pl.program_id(2)
        part = _dot(_mx(a_ref[...]), _mx(b_ref[...]), ta, tb)

        def finish(total):
            if has_add:
                total = total + add_ref[...]
            o_ref[...] = total.astype(out_dtype)

        if nk == 1:
            finish(part)
        else:
            _accumulate(acc_ref, part, kk == 0)

            @pl.when(kk == nk - 1)
            def _():
                finish(acc_ref[...])

    a_spec = pl.BlockSpec((tk, tm), lambda j, i, kk: (kk, i)) if ta else pl.BlockSpec((tm, tk), lambda j, i, kk: (i, kk))
    b_spec = pl.BlockSpec((tn, tk), lambda j, i, kk: (j, kk)) if tb else pl.BlockSpec((tk, tn), lambda j, i, kk: (kk, j))
    o_spec = pl.BlockSpec((tm, tn), lambda j, i, kk: (i, j))
    in_specs = [a_spec, b_spec] + ([o_spec] if has_add else [])
    args = (a, b) + ((add,) if has_add else ())
    acc_shape = (tm, tn) if nk > 1 else (8, LANES)
    return pl.pallas_call(
        body, name=name, grid=(n // tn, m // tm, nk), in_specs=in_specs, out_specs=o_spec,
        out_shape=jax.ShapeDtypeStruct((m, n), out_dtype), scratch_shapes=[pltpu.VMEM(acc_shape, F32)],
        compiler_params=_params("parallel", "parallel", "arbitrary"))(*args)


def _rmsnorm_fwd(x, w, *, name):
    s, d = x.shape
    ts = _tile(s, 512, 16)

    def body(x_ref, w_ref, o_ref):
        xf = x_ref[...]
        r = lax.rsqrt(jnp.mean(xf * xf, axis=-1, keepdims=True) + NORM_EPS)
        o_ref[...] = (xf * r * w_ref[...]).astype(BF16)

    return pl.pallas_call(
        body, name=name, grid=(s // ts,),
        in_specs=[pl.BlockSpec((ts, d), lambda i: (i, 0)), pl.BlockSpec((1, d), lambda i: (0, 0))],
        out_specs=pl.BlockSpec((ts, d), lambda i: (i, 0)),
        out_shape=jax.ShapeDtypeStruct((s, d), BF16), compiler_params=_params("parallel"))(x, w.reshape(1, d))


def _rmsnorm_bwd(dy, x, w, dres, *, name):
    s, d = x.shape
    ts = _tile(s, 256, 8)

    def body(dy_ref, x_ref, w_ref, dres_ref, dx_ref, dw_ref):
        xf = x_ref[...]
        dyf = dy_ref[...]
        r = lax.rsqrt(jnp.mean(xf * xf, axis=-1, keepdims=True) + NORM_EPS)
        xh = xf * r
        dxh = dyf * w_ref[...]
        dx = r * (dxh - xh * jnp.mean(dxh * xh, axis=-1, keepdims=True))
        dx_ref[...] = dres_ref[...] + dx
        _accumulate(dw_ref, jnp.sum(dyf * xh, axis=0, keepdims=True), pl.program_id(0) == 0)

    row = pl.BlockSpec((ts, d), lambda i: (i, 0))
    vec = pl.BlockSpec((1, d), lambda i: (0, 0))
    return pl.pallas_call(
        body, name=name, grid=(s // ts,), in_specs=[row, row, vec, row], out_specs=[row, vec],
        out_shape=[jax.ShapeDtypeStruct((s, d), F32), jax.ShapeDtypeStruct((1, d), F32)],
        compiler_params=_params("arbitrary"))(dy, x, w.reshape(1, d), dres)


def _loss_head(x, w, target, *, name):
    s, d = x.shape
    ts = _tile(s, 256, 8)

    def body(x_ref, w_ref, t_ref, loss_ref, dx_ref, dw_ref):
        first = pl.program_id(0) == 0
        xf = x_ref[...]
        wv = w_ref[...]
        r = lax.rsqrt(jnp.mean(xf * xf, axis=-1, keepdims=True) + NORM_EPS)
        xh = xf * r
        err = xh * wv - t_ref[...]
        part = 0.5 * jnp.sum(jnp.mean(err * err, axis=-1, keepdims=True), axis=0, keepdims=True)
        _accumulate(loss_ref, jnp.broadcast_to(part, (1, LANES)), first)
        dyf = err * (1.0 / d)
        dxh = dyf * wv
        dx_ref[...] = r * (dxh - xh * jnp.mean(dxh * xh, axis=-1, keepdims=True))
        _accumulate(dw_ref, jnp.sum(dyf * xh, axis=0, keepdims=True), first)

    row = pl.BlockSpec((ts, d), lambda i: (i, 0))
    vec = pl.BlockSpec((1, d), lambda i: (0, 0))
    return pl.pallas_call(
        body, name=name, grid=(s // ts,), in_specs=[row, vec, row],
        out_specs=[pl.BlockSpec((1, LANES), lambda i: (0, 0)), row, vec],
        out_shape=[jax.ShapeDtypeStruct((1, LANES), F32), jax.ShapeDtypeStruct((s, d), F32),
                   jax.ShapeDtypeStruct((1, d), F32)],
        compiler_params=_params("arbitrary"))(x, w.reshape(1, d), target)


def _qkv_fwd(proj, conv_w, n_heads, *, name):
    s = proj.shape[0]
    dl = n_heads * HEAD_DIM
    width = conv_w.shape[0]
    ts = _tile(s, 512, 8)

    def body(x_ref, halo_ref, w_ref, o_ref):
        i, sec = pl.program_id(0), pl.program_id(1)
        xv = x_ref[...]
        cat = jnp.concatenate([jnp.where(i > 0, halo_ref[...], 0.0), xv], axis=0)
        c = xv * w_ref[pl.ds(width - 1, 1), :]
        for sh in range(1, width):
            c = c + _rows_before(cat, sh, CONV_HALO) * w_ref[pl.ds(width - 1 - sh, 1), :]
        act = c * _sigmoid(c)

        @pl.when(sec == 2)
        def _():
            o_ref[...] = act

        @pl.when(sec < 2)
        def _():
            scale = jnp.where(sec == 0, HEAD_DIM ** -0.5, 1.0)
            for h in range(n_heads):
                cols = slice(h * HEAD_DIM, (h + 1) * HEAD_DIM)
                ah = act[:, cols]
                o_ref[:, cols] = ah * lax.rsqrt(jnp.sum(ah * ah, axis=-1, keepdims=True) + NORM_EPS) * scale

    return pl.pallas_call(
        body, name=name, grid=(s // ts, 3),
        in_specs=[pl.BlockSpec((ts, dl), lambda i, sec: (i, sec)),
                  pl.BlockSpec((CONV_HALO, dl), lambda i, sec: (jnp.maximum(i * (ts // CONV_HALO) - 1, 0), sec)),
                  pl.BlockSpec((width, dl), lambda i, sec: (0, sec))],
        out_specs=pl.BlockSpec((None, ts, dl), lambda i, sec: (sec, i, 0)),
        out_shape=jax.ShapeDtypeStruct((3, s, dl), F32),
        compiler_params=_params("parallel", "parallel"))(proj, proj, conv_w)


def _qkv_bwd(dqkv, proj, conv_w, n_heads, *, name):
    s = proj.shape[0]
    dl = n_heads * HEAD_DIM
    width = conv_w.shape[0]
    ts = _tile(s, 256, 16)
    n_tiles = s // ts
    per = ts // CONV_HALO
    rows = ts + CONV_HALO

    def body(d_ref, dnext_ref, x_ref, xprev_ref, xnext_ref, w_ref, dx_ref, dw_ref):
        sec, i = pl.program_id(0), pl.program_id(1)
        xv = x_ref[...]
        cat = jnp.concatenate([jnp.where(i > 0, xprev_ref[...], 0.0), xv, xnext_ref[...]], axis=0)
        shifted = [cat[CONV_HALO:]] + [_rows_before(cat, sh, CONV_HALO) for sh in range(1, width)]
        c = shifted[0] * w_ref[pl.ds(width - 1, 1), :]
        for sh in range(1, width):
            c = c + shifted[sh] * w_ref[pl.ds(width - 1 - sh, 1), :]
        sig = _sigmoid(c)
        act = c * sig
        dout = jnp.concatenate([d_ref[...], dnext_ref[...]], axis=0)
        scale = jnp.where(sec == 0, HEAD_DIM ** -0.5, 1.0)
        is_v = sec == 2
        pieces = []
        for h in range(n_heads):
            cols = slice(h * HEAD_DIM, (h + 1) * HEAD_DIM)
            ah, dh = act[:, cols], dout[:, cols]
            nrm = lax.rsqrt(jnp.sum(ah * ah, axis=-1, keepdims=True) + NORM_EPS)
            dnormed = scale * nrm * (dh - ah * (nrm * nrm) * jnp.sum(dh * ah, axis=-1, keepdims=True))
            pieces.append(jnp.where(is_v, dh, dnormed))
        dact = jnp.concatenate(pieces, axis=1)
        dc = dact * sig * (1.0 + c * (1.0 - sig))
        live = jnp.logical_or(lax.broadcasted_iota(jnp.int32, (rows, 1), 0) < ts, i < n_tiles - 1)
        dc = jnp.where(live, dc, 0.0)
        dx = dc[:ts] * w_ref[pl.ds(width - 1, 1), :]
        for sh in range(1, width):
            dx = dx + _rows_after(dc, sh, ts) * w_ref[pl.ds(width - 1 - sh, 1), :]
        dx_ref[...] = dx.astype(BF16)
        dw_rows = [jnp.sum(dc[:ts] * shifted[width - 1 - j][:ts], axis=0, keepdims=True) for j in range(width)]
        _accumulate(dw_ref, jnp.concatenate(dw_rows, axis=0), i == 0)

    return pl.pallas_call(
        body, name=name, grid=(3, n_tiles),
        in_specs=[pl.BlockSpec((None, ts, dl), lambda sec, i: (sec, i, 0)),
                  pl.BlockSpec((None, CONV_HALO, dl), lambda sec, i: (sec, jnp.minimum((i + 1) * per, s // CONV_HALO - 1), 0)),
                  pl.BlockSpec((ts, dl), lambda sec, i: (i, sec)),
                  pl.BlockSpec((CONV_HALO, dl), lambda sec, i: (jnp.maximum(i * per - 1, 0), sec)),
                  pl.BlockSpec((CONV_HALO, dl), lambda sec, i: (jnp.minimum((i + 1) * per, s // CONV_HALO - 1), sec)),
                  pl.BlockSpec((width, dl), lambda sec, i: (0, sec))],
        out_specs=[pl.BlockSpec((ts, dl), lambda sec, i: (i, sec)), pl.BlockSpec((width, dl), lambda sec, i: (0, sec))],
        out_shape=[jax.ShapeDtypeStruct((s, 3 * dl), BF16), jax.ShapeDtypeStruct((width, 3 * dl), F32)],
        compiler_params=_params("parallel", "arbitrary"))(dqkv, dqkv, proj, proj, proj, conv_w)


def _ba_fwd(pba, alog_row, dtb_row, n_heads, *, name):
    s = pba.shape[0]
    ts = _tile(s, 1024, 8)

    def body(x_ref, alog_ref, dtb_ref, o_ref):
        xv = x_ref[...]
        lane = lax.broadcasted_iota(jnp.int32, xv.shape, 1)
        g = -jnp.exp(alog_ref[...]) * _softplus(xv + dtb_ref[...])
        o_ref[...] = jnp.where(lane < n_heads, _sigmoid(xv), jnp.where(lane < 2 * n_heads, g, 0.0))

    row = pl.BlockSpec((ts, LANES), lambda i: (i, 0))
    vec = pl.BlockSpec((1, LANES), lambda i: (0, 0))
    return pl.pallas_call(
        body, name=name, grid=(s // ts,), in_specs=[row, vec, vec], out_specs=row,
        out_shape=jax.ShapeDtypeStruct((s, LANES), F32), compiler_params=_params("parallel"))(pba, alog_row, dtb_row)


def _ba_bwd(dbg, pba, alog_row, dtb_row, n_heads, *, name):
    s = pba.shape[0]
    ts = _tile(s, 1024, 16)

    def body(d_ref, x_ref, alog_ref, dtb_ref, dx_ref, dalog_ref, ddtb_ref):
        first = pl.program_id(0) == 0
        xv, dv = x_ref[...], d_ref[...]
        lane = lax.broadcasted_iota(jnp.int32, xv.shape, 1)
        beta = _sigmoid(xv)
        neg_a = -jnp.exp(alog_ref[...])
        xa = xv + dtb_ref[...]
        is_a = jnp.logical_and(lane >= n_heads, lane < 2 * n_heads)
        d_xa = jnp.where(is_a, dv * neg_a * _sigmoid(xa), 0.0)
        d_g_times_g = jnp.where(is_a, dv * neg_a * _softplus(xa), 0.0)
        dx_ref[...] = jnp.where(lane < n_heads, dv * beta * (1.0 - beta), d_xa).astype(BF16)
        _accumulate(dalog_ref, jnp.sum(d_g_times_g, axis=0, keepdims=True), first)
        _accumulate(ddtb_ref, jnp.sum(d_xa, axis=0, keepdims=True), first)

    row = pl.BlockSpec((ts, LANES), lambda i: (i, 0))
    vec = pl.BlockSpec((1, LANES), lambda i: (0, 0))
    return pl.pallas_call(
        body, name=name, grid=(s // ts,), in_specs=[row, row, vec, vec], out_specs=[row, vec, vec],
        out_shape=[jax.ShapeDtypeStruct((s, LANES), BF16), jax.ShapeDtypeStruct((1, LANES), F32),
                   jax.ShapeDtypeStruct((1, LANES), F32)],
        compiler_params=_params("arbitrary"))(dbg, pba, alog_row, dtb_row)


_Chunk = collections.namedtuple(
    "_Chunk", "eye tril strict beta_col decay e_col f_col dec kb vb kbe lmat tmat sol u wc at qd ke")


def _row_to_col(row, eye):
    return jnp.sum(jnp.where(eye, row, 0.0), axis=1, keepdims=True)


def _col_to_row(col, eye):
    return jnp.sum(jnp.where(eye, col, 0.0), axis=0, keepdims=True)


def _unit_lower_inverse(lmat, eye):
    t = jnp.where(eye, 1.0, 0.0) - lmat
    p = _dot_hi(lmat, lmat)
    doublings = GDN_CHUNK.bit_length() - 2
    for r in range(doublings):
        t = t + _dot_hi(t, p)
        if r < doublings - 1:
            p = _dot_hi(p, p)
    return t


def _gdn_chunk(q, k, v, beta_row, g_row):
    c = GDN_CHUNK
    dv = v.shape[1]
    ri = lax.broadcasted_iota(jnp.int32, (c, c), 0)
    ci = lax.broadcasted_iota(jnp.int32, (c, c), 1)
    eye, tril, strict = ri == ci, ri >= ci, ri > ci
    g_col = _row_to_col(g_row, eye)
    beta_col = _row_to_col(beta_row, eye)
    gc_col = jnp.sum(jnp.where(tril, g_row, 0.0), axis=1, keepdims=True)
    gc_row = jnp.sum(jnp.where(ri <= ci, g_col, 0.0), axis=0, keepdims=True)
    g_last = jnp.sum(g_row, axis=1, keepdims=True)
    decay = jnp.exp(jnp.where(tril, gc_col - gc_row, -jnp.inf))
    e_col = jnp.exp(gc_col)
    f_col = jnp.exp(g_last - gc_col)
    dec = jnp.exp(g_last)
    kb = k * beta_col
    vb = v * beta_col
    lmat = jnp.where(strict, _dot(_mx(kb), _mx(k), tb=True) * decay, 0.0)
    tmat = _unit_lower_inverse(lmat, eye)
    kbe = kb * e_col
    sol = _dot_hi(tmat, jnp.concatenate([vb, kbe], axis=1))
    at = _dot(_mx(q), _mx(k), tb=True) * decay
    return _Chunk(eye, tril, strict, beta_col, decay, e_col, f_col, dec, kb, vb, kbe, lmat, tmat, sol,
                  sol[:, :dv], sol[:, dv:], at, q * e_col, k * f_col)


def _gdn_blocking(s):
    n_chunks = s // GDN_CHUNK
    per_step = 16 if n_chunks % 16 == 0 else n_chunks
    return n_chunks, per_step, n_chunks // per_step


def _gdn_fwd(qkv, beta_rows, g_rows, *, name):
    _, s, dl = qkv.shape
    n_heads = dl // HEAD_DIM
    c = GDN_CHUNK
    n_chunks, per_step, n_steps = _gdn_blocking(s)
    rows = per_step * c

    def body(qkv_ref, b_ref, g_ref, o_ref, st_ref, state_ref):
        @pl.when(pl.program_id(1) == 0)
        def _():
            state_ref[...] = jnp.zeros_like(state_ref)

        def step(n, state):
            r0 = pl.multiple_of(n * c, c)
            q, k, v = (qkv_ref[j, pl.ds(r0, c), :] for j in range(3))
            ch = _gdn_chunk(q, k, v, b_ref[0, n], g_ref[0, n])
            st_ref[0, n] = state
            v_new = ch.u - _dot(_mx(ch.wc), _mx(state))
            o_ref[pl.ds(r0, c), :] = _dot(_mx(ch.qd), _mx(state)) + _dot(_mx(ch.at), _mx(v_new))
            return state * ch.dec + _dot(_mx(ch.ke), _mx(v_new), ta=True)

        state_ref[...] = lax.fori_loop(0, per_step, step, state_ref[...])

    rowvec = pl.BlockSpec((1, per_step, 1, c), lambda h, j: (h, j, 0, 0))
    return pl.pallas_call(
        body, name=name, grid=(n_heads, n_steps),
        in_specs=[pl.BlockSpec((3, rows, HEAD_DIM), lambda h, j: (0, j, h)), rowvec, rowvec],
        out_specs=[pl.BlockSpec((rows, HEAD_DIM), lambda h, j: (j, h)),
                   pl.BlockSpec((1, per_step, HEAD_DIM, HEAD_DIM), lambda h, j: (h, j, 0, 0))],
        out_shape=[jax.ShapeDtypeStruct((s, dl), F32),
                   jax.ShapeDtypeStruct((n_heads, n_chunks, HEAD_DIM, HEAD_DIM), F32)],
        scratch_shapes=[pltpu.VMEM((HEAD_DIM, HEAD_DIM), F32)],
        compiler_params=_params("parallel", "arbitrary"))(qkv, beta_rows, g_rows)


def _gdn_bwd(do, qkv, beta_rows, g_rows, states, *, name):
    _, s, dl = qkv.shape
    n_heads = dl // HEAD_DIM
    c = GDN_CHUNK
    n_chunks, per_step, n_steps = _gdn_blocking(s)
    rows = per_step * c

    def body(do_ref, qkv_ref, b_ref, g_ref, st_ref, dqkv_ref, db_ref, dg_ref, dstate_ref):
        @pl.when(pl.program_id(1) == 0)
        def _():
            dstate_ref[...] = jnp.zeros_like(dstate_ref)

        def step(idx, dstate):
            n = per_step - 1 - idx
            r0 = pl.multiple_of(n * c, c)
            q, k, v = (qkv_ref[j, pl.ds(r0, c), :] for j in range(3))
            ch = _gdn_chunk(q, k, v, b_ref[0, n], g_ref[0, n])
            state = st_ref[0, n]
            d_o = do_ref[pl.ds(r0, c), :]
            mstate, md_o, mdstate = _mx(state), _mx(d_o), _mx(dstate)
            v_new = ch.u - _dot(_mx(ch.wc), mstate)
            dv_new = _dot(_mx(ch.at), md_o, ta=True) + _dot(_mx(ch.ke), mdstate)
            dqd = _dot(md_o, mstate, tb=True)
            dat = jnp.where(ch.tril, _dot(md_o, _mx(v_new), tb=True), 0.0)
            dke = _dot(_mx(v_new), mdstate, tb=True)
            dwc = -_dot(_mx(dv_new), mstate, tb=True)
            ddec = jnp.sum(jnp.sum(dstate * state, axis=1, keepdims=True), axis=0, keepdims=True)
            dstate_next = dstate * ch.dec + _dot(_mx(ch.qd), md_o, ta=True) - _dot(_mx(ch.wc), _mx(dv_new), ta=True)
            drhs = _dot_hi(ch.tmat, jnp.concatenate([dv_new, dwc], axis=1), ta=True)
            dvb, dkbe = drhs[:, :HEAD_DIM], drhs[:, HEAD_DIM:]
            dl_mat = jnp.where(ch.strict, -_dot(_mx(drhs), _mx(ch.sol), tb=True), 0.0)
            dkk = dl_mat * ch.decay
            dqk = dat * ch.decay
            mk, mq, mkb = _mx(k), _mx(q), _mx(ch.kb)
            dkb = _dot(_mx(dkk), mk) + dkbe * ch.e_col
            dq = _dot(_mx(dqk), mk) + dqd * ch.e_col
            dk = (_dot(_mx(dqk), mq, ta=True) + _dot(_mx(dkk), mkb, ta=True) + dke * ch.f_col + dkb * ch.beta_col)
            dqkv_ref[0, pl.ds(r0, c), :] = dq
            dqkv_ref[1, pl.ds(r0, c), :] = dk
            dqkv_ref[2, pl.ds(r0, c), :] = dvb * ch.beta_col
            dbeta_col = jnp.sum(dkb * k + dvb * v, axis=1, keepdims=True)
            through_decay = dl_mat * ch.lmat + dat * ch.at
            dke_ke = jnp.sum(dke * ch.ke, axis=1, keepdims=True)
            dgc_col = (jnp.sum(through_decay, axis=1, keepdims=True)
                       - _row_to_col(jnp.sum(through_decay, axis=0, keepdims=True), ch.eye)
                       + jnp.sum(dqd * ch.qd + dkbe * ch.kbe, axis=1, keepdims=True) - dke_ke)
            dg_last = jnp.sum(dke_ke, axis=0, keepdims=True) + ddec * ch.dec
            db_ref[0, n] = _col_to_row(dbeta_col, ch.eye)
            dg_ref[0, n] = jnp.sum(jnp.where(ch.tril, dgc_col, 0.0), axis=0, keepdims=True) + dg_last
            return dstate_next

        dstate_ref[...] = lax.fori_loop(0, per_step, step, dstate_ref[...])

    rev = lambda j: n_steps - 1 - j
    rowvec = pl.BlockSpec((1, per_step, 1, c), lambda h, j: (h, rev(j), 0, 0))
    qkv_spec = pl.BlockSpec((3, rows, HEAD_DIM), lambda h, j: (0, rev(j), h))
    return pl.pallas_call(
        body, name=name, grid=(n_heads, n_steps),
        in_specs=[pl.BlockSpec((rows, HEAD_DIM), lambda h, j: (rev(j), h)), qkv_spec, rowvec, rowvec,
                  pl.BlockSpec((1, per_step, HEAD_DIM, HEAD_DIM), lambda h, j: (h, rev(j), 0, 0))],
        out_specs=[qkv_spec, rowvec, rowvec],
        out_shape=[jax.ShapeDtypeStruct((3, s, dl), F32),
                   jax.ShapeDtypeStruct((n_heads, n_chunks, 1, c), F32),
                   jax.ShapeDtypeStruct((n_heads, n_chunks, 1, c), F32)],
        scratch_shapes=[pltpu.VMEM((HEAD_DIM, HEAD_DIM), F32)],
        compiler_params=_params("parallel", "arbitrary"))(do, qkv, beta_rows, g_rows, states)


def _pool_counts(tile, ts, extra, win):
    t = tile * ts + lax.broadcasted_iota(jnp.int32, (ts + extra, 1), 0)
    return jnp.minimum(t + 1, win).astype(F32)


def _pooled(cat, p_cols, tile, ts, win):
    acc, span = cat, 1
    while span < win:
        acc = acc + pltpu.roll(acc, span, 0)
        span *= 2
    return acc[POOL_HALO:] / _pool_counts(tile, ts, 0, win) - p_cols


def _merge_fwd(proj, o, gnw, pool_w, pool_scale, *, name):
    s, d = o.shape
    n_heads = d // HEAD_DIM
    n_groups, pg = pool_w.shape[0], pool_w.shape[1]
    assert n_groups == len(POOL_WINDOWS) and n_groups * pg == d and pg % HEAD_DIM == 0
    heads_per_group = pg // HEAD_DIM
    ts = _tile(s, 256, 16)

    def body(o_ref, z_ref, p_ref, halo_ref, ga_ref, gb_ref, gnw_ref, pw_ref, ps_ref, out_ref):
        i = pl.program_id(0)
        gnw_v = gnw_ref[...]
        halo = jnp.where(i > 0, halo_ref[...], 0.0)
        for gi, win in enumerate(POOL_WINDOWS):
            gcols = slice(gi * pg, (gi + 1) * pg)
            pv = p_ref[:, gcols]
            pooled = _pooled(jnp.concatenate([halo[:, gcols], pv], axis=0), pv, i, ts, win)
            yb = _dot(_mx(pooled), pw_ref[gi]) * ps_ref[:, gcols]
            for h in range(gi * heads_per_group, (gi + 1) * heads_per_group):
                cols = slice(h * HEAD_DIM, (h + 1) * HEAD_DIM)
                in_group = slice(h * HEAD_DIM - gi * pg, (h + 1) * HEAD_DIM - gi * pg)
                oh, zh = o_ref[:, cols], z_ref[:, cols]
                r = lax.rsqrt(jnp.mean(oh * oh, axis=-1, keepdims=True) + NORM_EPS)
                ya = oh * r * gnw_v * (zh * _sigmoid(zh))
                out_ref[:, cols] = (_sigmoid(ga_ref[:, cols]) * ya
                                    + _sigmoid(gb_ref[:, cols]) * yb[:, in_group]).astype(BF16)

    blk = lambda col: pl.BlockSpec((ts, d), lambda i, col=col: (i, col))
    vec = lambda width: pl.BlockSpec((1, width), lambda i: (0, 0))
    return pl.pallas_call(
        body, name=name, grid=(s // ts,),
        in_specs=[blk(0), blk(3), blk(4),
                  pl.BlockSpec((POOL_HALO, d), lambda i: (jnp.maximum(i * (ts // POOL_HALO) - 1, 0), 4)),
                  blk(5), blk(6), vec(HEAD_DIM), pl.BlockSpec((n_groups, pg, pg), lambda i: (0, 0, 0)), vec(d)],
        out_specs=blk(0), out_shape=jax.ShapeDtypeStruct((s, d), BF16),
        compiler_params=_params("parallel"))(o, proj, proj, proj, proj, proj, gnw.reshape(1, HEAD_DIM), pool_w,
                                              pool_scale.reshape(1, d))


def _merge_bwd(dmixed, proj, o, gnw, pool_w, pool_scale, *, name):
    s, d = o.shape
    n_heads = d // HEAD_DIM
    n_groups, pg = pool_w.shape[0], pool_w.shape[1]
    ts = _tile(s, 256, 16)

    def body(dm_ref, o_ref, z_ref, p_ref, halo_ref, ga_ref, gb_ref, gnw_ref, pw_ref, ps_ref,
             do_ref, dz_ref, dga_ref, dgb_ref, dpl_ref, dgnw_ref, dpw_ref, dps_ref):
        i = pl.program_id(0)
        first = i == 0
        gnw_v = gnw_ref[...]
        dgnw = jnp.zeros((1, HEAD_DIM), F32)
        for h in range(n_heads):
            cols = slice(h * HEAD_DIM, (h + 1) * HEAD_DIM)
            oh, zh, dm = o_ref[:, cols], z_ref[:, cols], dm_ref[:, cols]
            r = lax.rsqrt(jnp.mean(oh * oh, axis=-1, keepdims=True) + NORM_EPS)
            xh = oh * r
            sz = _sigmoid(zh)
            silu_z = zh * sz
            sa = _sigmoid(ga_ref[:, cols])
            on = xh * gnw_v
            dya = dm * sa
            dga_ref[:, cols] = (dm * on * silu_z * sa * (1.0 - sa)).astype(BF16)
            dz_ref[:, cols] = (dya * on * sz * (1.0 + zh * (1.0 - sz))).astype(BF16)
            don = dya * silu_z
            dgnw = dgnw + jnp.sum(don * xh, axis=0, keepdims=True)
            dxh = don * gnw_v
            do_ref[:, cols] = r * (dxh - xh * jnp.mean(dxh * xh, axis=-1, keepdims=True))
        _accumulate(dgnw_ref, dgnw, first)
        halo = jnp.where(first, 0.0, halo_ref[...])
        for gi, win in enumerate(POOL_WINDOWS):
            cols = slice(gi * pg, (gi + 1) * pg)
            pv, dm = p_ref[:, cols], dm_ref[:, cols]
            pooled = _pooled(jnp.concatenate([halo[:, cols], pv], axis=0), pv, i, ts, win)
            lin = _dot(_mx(pooled), pw_ref[gi])
            psv = ps_ref[:, cols]
            sb = _sigmoid(gb_ref[:, cols])
            dgb_ref[:, cols] = (dm * lin * psv * sb * (1.0 - sb)).astype(BF16)
            dyb = dm * sb
            _accumulate(dps_ref.at[:, cols], jnp.sum(dyb * lin, axis=0, keepdims=True), first)
            dlin = _mx(dyb * psv)
            _accumulate(dpw_ref.at[gi], _dot(_mx(pooled), dlin, ta=True), first)
            dpl_ref[:, cols] = _dot(dlin, pw_ref[gi], tb=True)

    blk = lambda col: pl.BlockSpec((ts, d), lambda i, col=col: (i, col))
    vec = lambda width: pl.BlockSpec((1, width), lambda i: (0, 0))
    pw_spec = pl.BlockSpec((n_groups, pg, pg), lambda i: (0, 0, 0))
    return pl.pallas_call(
        body, name=name, grid=(s // ts,),
        in_specs=[blk(0), blk(0), blk(3), blk(4),
                  pl.BlockSpec((POOL_HALO, d), lambda i: (jnp.maximum(i * (ts // POOL_HALO) - 1, 0), 4)),
                  blk(5), blk(6), vec(HEAD_DIM), pw_spec, vec(d)],
        out_specs=[blk(0), blk(0), blk(0), blk(0), blk(0), vec(HEAD_DIM), pw_spec, vec(d)],
        out_shape=[jax.ShapeDtypeStruct((s, d), F32), jax.ShapeDtypeStruct((s, d), BF16),
                   jax.ShapeDtypeStruct((s, d), BF16), jax.ShapeDtypeStruct((s, d), BF16),
                   jax.ShapeDtypeStruct((s, d), F32), jax.ShapeDtypeStruct((1, HEAD_DIM), F32),
                   jax.ShapeDtypeStruct((n_groups, pg, pg), F32), jax.ShapeDtypeStruct((1, d), F32)],
        compiler_params=_params("arbitrary"))(dmixed, o, proj, proj, proj, proj, proj, gnw.reshape(1, HEAD_DIM),
                                               pool_w, pool_scale.reshape(1, d))


def _pool_bwd(dpooled, *, name):
    s, d = dpooled.shape
    pg = d // len(POOL_WINDOWS)
    ts = _tile(s, 512, 16)
    n_tiles = s // ts
    per = ts // POOL_HALO

    def body(d_ref, next_ref, out_ref):
        i = pl.program_id(0)
        nxt = jnp.where(i < n_tiles - 1, next_ref[...], 0.0)
        for gi, win in enumerate(POOL_WINDOWS):
            cols = slice(gi * pg, (gi + 1) * pg)
            dv = d_ref[:, cols]
            acc = jnp.concatenate([dv, nxt[:, cols]], axis=0) / _pool_counts(i, ts, POOL_HALO, win)
            span = 1
            while span < win:
                acc = acc + pltpu.roll(acc, acc.shape[0] - span, 0)
                span *= 2
            out_ref[:, cols] = (acc[:ts] - dv).astype(BF16)

    return pl.pallas_call(
        body, name=name, grid=(n_tiles,),
        in_specs=[pl.BlockSpec((ts, d), lambda i: (i, 0)),
                  pl.BlockSpec((POOL_HALO, d), lambda i: (jnp.minimum((i + 1) * per, s // POOL_HALO - 1), 0))],
        out_specs=pl.BlockSpec((ts, d), lambda i: (i, 0)), out_shape=jax.ShapeDtypeStruct((s, d), BF16),
        compiler_params=_params("parallel"))(dpooled, dpooled)


def _ffn_tiles(s, f):
    tf = _tile(f, 1408)
    return _tile(s, 512, 16), tf, f // tf


def _ffn_act_fwd(gu, conv_w, conv_b, *, name):
    s, f = gu.shape[0], gu.shape[1] // 2
    width = conv_w.shape[0]
    ts, tf, nf = _ffn_tiles(s, f)

    def body(g_ref, halo_ref, u_ref, w_ref, b_ref, act_ref, gc_ref):
        i = pl.program_id(0)
        gv = g_ref[...]
        cat = jnp.concatenate([jnp.where(i > 0, halo_ref[...], 0.0), gv], axis=0)
        gc = gv * w_ref[pl.ds(width - 1, 1), :] + b_ref[...]
        for sh in range(1, width):
            gc = gc + _rows_before(cat, sh, CONV_HALO) * w_ref[pl.ds(width - 1 - sh, 1), :]
        gc_ref[...] = gc
        act_ref[...] = (_gelu(gc) * u_ref[...]).astype(BF16)

    blk = pl.BlockSpec((ts, tf), lambda i, j: (i, j))
    return pl.pallas_call(
        body, name=name, grid=(s // ts, nf),
        in_specs=[blk, pl.BlockSpec((CONV_HALO, tf), lambda i, j: (jnp.maximum(i * (ts // CONV_HALO) - 1, 0), j)),
                  pl.BlockSpec((ts, tf), lambda i, j: (i, nf + j)),
                  pl.BlockSpec((width, tf), lambda i, j: (0, j)), pl.BlockSpec((1, tf), lambda i, j: (0, j))],
        out_specs=[blk, blk],
        out_shape=[jax.ShapeDtypeStruct((s, f), BF16), jax.ShapeDtypeStruct((s, f), F32)],
        compiler_params=_params("parallel", "parallel"))(gu, gu, gu, conv_w, conv_b.reshape(1, f))


def _ffn_act_bwd(dact, gu, gc, conv_w, *, name):
    s, f = gc.shape
    width = conv_w.shape[0]
    ts, tf, nf = _ffn_tiles(s, f)
    n_tiles = s // ts
    per = ts // CONV_HALO
    rows = ts + CONV_HALO

    def body(da_ref, da_next, gc_ref, gc_next, u_ref, u_next, g_ref, g_prev, w_ref, dg_ref, du_ref, dw_ref, db_ref):
        i = pl.program_id(1)
        first = i == 0
        da = jnp.concatenate([da_ref[...], da_next[...]], axis=0)
        gcv = jnp.concatenate([gc_ref[...], gc_next[...]], axis=0)
        uv = jnp.concatenate([u_ref[...], u_next[...]], axis=0)
        du_ref[...] = (da[:ts] * _gelu(gcv[:ts])).astype(BF16)
        live = jnp.logical_or(lax.broadcasted_iota(jnp.int32, (rows, 1), 0) < ts, i < n_tiles - 1)
        dgc = jnp.where(live, da * uv * _gelu_grad(gcv), 0.0)
        dgate = dgc[:ts] * w_ref[pl.ds(width - 1, 1), :]
        for sh in range(1, width):
            dgate = dgate + _rows_after(dgc, sh, ts) * w_ref[pl.ds(width - 1 - sh, 1), :]
        dg_ref[...] = dgate.astype(BF16)
        gv = g_ref[...]
        cat = jnp.concatenate([jnp.where(first, 0.0, g_prev[...]), gv], axis=0)
        shifted = [gv] + [_rows_before(cat, sh, CONV_HALO) for sh in range(1, width)]
        dw_rows = [jnp.sum(dgc[:ts] * shifted[width - 1 - j], axis=0, keepdims=True) for j in range(width)]
        _accumulate(dw_ref, jnp.concatenate(dw_rows, axis=0), first)
        _accumulate(db_ref, jnp.sum(dgc[:ts], axis=0, keepdims=True), first)

    nxt_row = lambda i: jnp.minimum((i + 1) * per, s // CONV_HALO - 1)
    main = lambda off: pl.BlockSpec((ts, tf), lambda j, i, off=off: (i, off + j))
    nxt = lambda off: pl.BlockSpec((CONV_HALO, tf), lambda j, i, off=off: (nxt_row(i), off + j))
    return pl.pallas_call(
        body, name=name, grid=(nf, n_tiles),
        in_specs=[main(0), nxt(0), main(0), nxt(0), main(nf), nxt(nf), main(0),
                  pl.BlockSpec((CONV_HALO, tf), lambda j, i: (jnp.maximum(i * per - 1, 0), j)),
                  pl.BlockSpec((width, tf), lambda j, i: (0, j))],
        out_specs=[main(0), main(0), pl.BlockSpec((width, tf), lambda j, i: (0, j)),
                   pl.BlockSpec((1, tf), lambda j, i: (0, j))],
        out_shape=[jax.ShapeDtypeStruct((s, f), BF16), jax.ShapeDtypeStruct((s, f), BF16),
                   jax.ShapeDtypeStruct((width, f), F32), jax.ShapeDtypeStruct((1, f), F32)],
        compiler_params=_params("parallel", "arbitrary"))(dact, dact, gc, gc, gu, gu, gu, gu, conv_w)


def _rows_layout(bg, n_heads):
    s = bg.shape[0]
    shape = (n_heads, s // GDN_CHUNK, 1, GDN_CHUNK)
    return bg[:, :n_heads].T.reshape(shape), bg[:, n_heads:2 * n_heads].T.reshape(shape)


def _lane_layout(dbeta_rows, dg_rows):
    n_heads = dbeta_rows.shape[0]
    s = dbeta_rows.shape[1] * GDN_CHUNK
    both = jnp.concatenate([dbeta_rows.reshape(n_heads, s), dg_rows.reshape(n_heads, s)], axis=0).T
    return jnp.pad(both, ((0, 0), (0, LANES - 2 * n_heads)))


def _layer_fwd(x, w):
    n_heads = w["n_heads"]
    h = _rmsnorm_fwd(x, w["norm_mix_w"], name="norm_mix_fwd")
    proj = _matmul(h, w["w_main"], name="in_proj_fwd")
    pba = _matmul(h, w["w_ba"], name="ba_proj_fwd")
    qkv = _qkv_fwd(proj, w["conv_qkv_w"], n_heads, name="qkv_fwd")
    bg = _ba_fwd(pba, w["alog_row"], w["dtb_row"], n_heads, name="ba_fwd")
    beta_rows, g_rows = _rows_layout(bg, n_heads)
    o, states = _gdn_fwd(qkv, beta_rows, g_rows, name="gdn_fwd")
    mixed = _merge_fwd(proj, o, w["gdn_norm_w"], w["pool_w"], w["pool_scale"], name="merge_fwd")
    x2 = _matmul(mixed, w["w_out"], add=x, name="out_proj_fwd")
    h2 = _rmsnorm_fwd(x2, w["norm_ffn_w"], name="norm_ffn_fwd")
    gu = _matmul(h2, w["w_up"], name="up_proj_fwd")
    act, gc = _ffn_act_fwd(gu, w["conv_ffn_w"], w["conv_ffn_b"], name="ffn_act_fwd")
    x3 = _matmul(act, w["w_down"], add=x2, tk=1408, name="down_proj_fwd")
    saved = dict(x=x, h=h, proj=proj, pba=pba, qkv=qkv, beta_rows=beta_rows, g_rows=g_rows, o=o, states=states,
                 mixed=mixed, x2=x2, h2=h2, gu=gu, gc=gc, act=act)
    return x3, saved


def _layer_bwd(dx3, w, sv):
    n_heads = w["n_heads"]
    dl = n_heads * HEAD_DIM
    g = {}
    dact = _matmul(dx3, w["w_down"], tb=True, name="down_proj_dx")
    g["w_down"] = _matmul(sv["act"], dx3, ta=True, name="down_proj_dw")
    dgate, dup, g["conv_ffn_w"], g["conv_ffn_b"] = _ffn_act_bwd(dact, sv["gu"], sv["gc"], w["conv_ffn_w"],
                                                                 name="ffn_act_bwd")
    dgu = jnp.concatenate([dgate, dup], axis=1)
    dh2 = _matmul(dgu, w["w_up"], tb=True, name="up_proj_dx")
    g["w_up"] = _matmul(sv["h2"], dgu, ta=True, name="up_proj_dw")
    dx2, g["norm_ffn_w"] = _rmsnorm_bwd(dh2, sv["x2"], w["norm_ffn_w"], dx3, name="norm_ffn_bwd")
    dmixed = _matmul(dx2, w["w_out"], tb=True, name="out_proj_dx")
    g["w_out"] = _matmul(sv["mixed"], dx2, ta=True, name="out_proj_dw")
    do, dz, dga, dgb, dpooled, g["gdn_norm_w"], g["pool_w"], g["pool_scale"] = _merge_bwd(
        dmixed, sv["proj"], sv["o"], w["gdn_norm_w"], w["pool_w"], w["pool_scale"], name="merge_bwd")
    dp = _pool_bwd(dpooled, name="pool_bwd")
    dqkv, dbeta_rows, dg_rows = _gdn_bwd(do, sv["qkv"], sv["beta_rows"], sv["g_rows"], sv["states"], name="gdn_bwd")
    dproj_qkv, g["conv_qkv_w"] = _qkv_bwd(dqkv, sv["proj"], w["conv_qkv_w"], n_heads, name="qkv_bwd")
    dpba, g["alog_row"], g["dtb_row"] = _ba_bwd(_lane_layout(dbeta_rows, dg_rows), sv["pba"], w["alog_row"],
                                                w["dtb_row"], n_heads, name="ba_bwd")
    dproj = jnp.concatenate([dproj_qkv, dz, dp, dga, dgb], axis=1)
    dh = _matmul(dproj, w["w_main"], tb=True, name="in_proj_dx")
    dh = _matmul(dpba, w["w_ba"], tb=True, add=dh, name="ba_proj_dx")
    g["w_main"] = _matmul(sv["h"], dproj, ta=True, name="in_proj_dw")
    g["w_ba"] = _matmul(sv["h"], dpba, ta=True, name="ba_proj_dw")
    dx, g["norm_mix_w"] = _rmsnorm_bwd(dh, sv["x"], w["norm_mix_w"], dx2, name="norm_mix_bwd")
    return dx, g


def _local_step(x, target, stacked, norm_final_w, n_heads):
    def fwd(xc, wl):
        return _layer_fwd(xc, dict(wl, n_heads=n_heads))

    x_out, saved = lax.scan(fwd, x, stacked)
    loss, dx, d_final = _loss_head(x_out, norm_final_w, target, name="loss_head")

    def bwd(dxc, inp):
        wl, sv = inp
        return _layer_bwd(dxc, dict(wl, n_heads=n_heads), sv)

    grad_x, grads = lax.scan(bwd, dx, (stacked, saved), reverse=True)
    return loss, grad_x, grads, d_final


def _exchange(src, *, scatter, name):
    r_rows = src.shape[-2]

    def body(src_ref, out_ref, send_sems, recv_sems, local_sem):
        mx, my, mc = (lax.axis_index(a) for a in MESH_AXES)
        me = 4 * mx + 2 * my + mc

        def peer(r):
            px = 1 - mx if r & 4 else mx
            py = 1 - my if r & 2 else my
            pc = 1 - mc if r & 1 else mc
            return (px, py, pc), 4 * px + 2 * py + pc

        local = pltpu.make_async_copy(src_ref.at[me] if scatter else src_ref, out_ref.at[me], local_sem)
        local.start()
        copies = []
        for r in range(1, N_DEV):
            target, target_lin = peer(r)
            cp = pltpu.make_async_remote_copy(
                src_ref=src_ref.at[target_lin] if scatter else src_ref, dst_ref=out_ref.at[me],
                send_sem=send_sems.at[r - 1], recv_sem=recv_sems.at[r - 1],
                device_id=target, device_id_type=pl.DeviceIdType.MESH)
            cp.start()
            copies.append(cp)
        for r in range(1, N_DEV):
            target, target_lin = peer(r)
            pltpu.make_async_remote_copy(
                src_ref=src_ref.at[target_lin] if scatter else src_ref, dst_ref=out_ref.at[target_lin],
                send_sem=send_sems.at[r - 1], recv_sem=recv_sems.at[r - 1],
                device_id=target, device_id_type=pl.DeviceIdType.MESH).wait_recv()
        for cp in copies:
            cp.wait_send()
        local.wait()

    return pl.pallas_call(
        body, name=name,
        in_specs=[pl.BlockSpec(memory_space=pltpu.HBM)], out_specs=pl.BlockSpec(memory_space=pltpu.HBM),
        out_shape=jax.ShapeDtypeStruct((N_DEV, r_rows, LANES), src.dtype),
        scratch_shapes=[pltpu.SemaphoreType.DMA((N_DEV - 1,)), pltpu.SemaphoreType.DMA((N_DEV - 1,)),
                        pltpu.SemaphoreType.DMA])(src)


def _sum_slots(parts, *, name):
    _, r_rows, _ = parts.shape
    tr = _tile(r_rows, 2048, 16)

    def body(p_ref, o_ref):
        total = p_ref[0].astype(F32)
        for p in range(1, N_DEV):
            total = total + p_ref[p].astype(F32)
        o_ref[...] = total

    return pl.pallas_call(
        body, name=name, grid=(r_rows // tr,),
        in_specs=[pl.BlockSpec((N_DEV, tr, LANES), lambda i: (0, i, 0))],
        out_specs=pl.BlockSpec((tr, LANES), lambda i: (i, 0)),
        out_shape=jax.ShapeDtypeStruct((r_rows, LANES), F32), compiler_params=_params("parallel"))(parts)


def _adamw(w, g, m, v, *, name):
    rows, cols = w.shape
    tr = _tile(rows, max(8, (1 << 18) // cols // 8 * 8), 8)

    def body(w_ref, g_ref, m_ref, v_ref, d_ref, nm_ref, nv_ref):
        gv = g_ref[...]
        nm = ADAM_B1 * m_ref[...] + (1.0 - ADAM_B1) * gv
        nv = ADAM_B2 * v_ref[...] + (1.0 - ADAM_B2) * (gv * gv)
        m_hat = nm / (1.0 - ADAM_B1 ** ADAM_STEP)
        v_hat = nv / (1.0 - ADAM_B2 ** ADAM_STEP)
        d_ref[...] = -ADAM_LR * (m_hat / (jnp.sqrt(v_hat) + ADAM_EPS) + ADAM_WD * w_ref[...])
        nm_ref[...] = nm
        nv_ref[...] = nv

    blk = pl.BlockSpec((tr, cols), lambda i: (i, 0))
    out = jax.ShapeDtypeStruct((rows, cols), F32)
    return pl.pallas_call(
        body, name=name, grid=(rows // tr,), in_specs=[blk] * 4, out_specs=[blk] * 3, out_shape=[out] * 3,
        compiler_params=_params("parallel"))(w, g, m, v)


def _adamw_nd(w, g, m, v, *, name):
    two_d = (-1, w.shape[-1])
    outs = _adamw(w.reshape(two_d), g.reshape(two_d), m.reshape(two_d), v.reshape(two_d), name=name)
    return tuple(t.reshape(w.shape) for t in outs)


def _pack_rows(parts, dtype, quantum_rows=16):
    lead = parts[0].shape[:-1] if parts[0].ndim > 1 else ()
    flat = jnp.concatenate([p.astype(dtype) for p in parts], axis=-1)
    n = flat.shape[-1]
    padded = -(-n // (LANES * quantum_rows)) * (LANES * quantum_rows)
    flat = jnp.pad(flat, [(0, 0)] * len(lead) + [(0, padded - n)])
    return flat.reshape(lead + (padded // LANES, LANES))


def _unpack(flat, shapes):
    lead = flat.shape[:-1]
    out, at = [], 0
    for shape in shapes:
        size = 1
        for dim in shape:
            size *= dim
        out.append(flat[..., at:at + size].reshape(lead + tuple(shape)))
        at += size
    return out


SHARDED = ("w_in", "conv_qkv_w", "pool_w", "w_out", "w_up", "conv_ffn_w", "w_down")
SHARD_AXIS = dict(w_in=2, conv_qkv_w=2, pool_w=2, w_out=1, w_up=2, conv_ffn_w=2, w_down=1)
REPLICATED = ("norm_mix_w", "a_log", "dt_bias", "gdn_norm_w", "pool_scale", "norm_ffn_w", "conv_ffn_b",
              "norm_final_w")
WEIGHTS = ("norm_mix_w", "w_in", "conv_qkv_w", "a_log", "dt_bias", "gdn_norm_w", "pool_w", "pool_scale", "w_out",
           "norm_ffn_w", "w_up", "conv_ffn_w", "conv_ffn_b", "w_down", "norm_final_w")


def _whole_from_slots(slots, axis):
    moved = jnp.moveaxis(slots, 0, axis)
    shape = moved.shape
    return moved.reshape(shape[:axis] + (shape[axis] * shape[axis + 1],) + shape[axis + 2:])


def _slots_from_whole(whole, axis):
    shape = whole.shape
    split = whole.reshape(shape[:axis] + (N_DEV, shape[axis] // N_DEV) + shape[axis + 1:])
    return jnp.moveaxis(split, axis, 0)


def _lane_row(vec, n_heads):
    return jnp.pad(vec, ((0, 0), (n_heads, LANES - 2 * n_heads)))[:, None, :]


def kernel(x, norm_mix_w, w_in, conv_qkv_w, a_log, dt_bias, gdn_norm_w, pool_w, pool_scale, w_out, norm_ffn_w, w_up, conv_ffn_w, conv_ffn_b, w_down, norm_final_w, loss_target, m_norm_mix_w, m_w_in, m_conv_qkv_w, m_a_log, m_dt_bias, m_gdn_norm_w, m_pool_w, m_pool_scale, m_w_out, m_norm_ffn_w, m_w_up, m_conv_ffn_w, m_conv_ffn_b, m_w_down, m_norm_final_w, v_norm_mix_w, v_w_in, v_conv_qkv_w, v_a_log, v_dt_bias, v_gdn_norm_w, v_pool_w, v_pool_scale, v_w_out, v_norm_ffn_w, v_w_up, v_conv_ffn_w, v_conv_ffn_b, v_w_down, v_norm_final_w):
    local = dict(norm_mix_w=norm_mix_w, w_in=w_in, conv_qkv_w=conv_qkv_w, a_log=a_log, dt_bias=dt_bias,
                 gdn_norm_w=gdn_norm_w, pool_w=pool_w, pool_scale=pool_scale, w_out=w_out, norm_ffn_w=norm_ffn_w,
                 w_up=w_up, conv_ffn_w=conv_ffn_w, conv_ffn_b=conv_ffn_b, w_down=w_down, norm_final_w=norm_final_w)
    mom_m = dict(norm_mix_w=m_norm_mix_w, w_in=m_w_in, conv_qkv_w=m_conv_qkv_w, a_log=m_a_log, dt_bias=m_dt_bias,
                 gdn_norm_w=m_gdn_norm_w, pool_w=m_pool_w, pool_scale=m_pool_scale, w_out=m_w_out,
                 norm_ffn_w=m_norm_ffn_w, w_up=m_w_up, conv_ffn_w=m_conv_ffn_w, conv_ffn_b=m_conv_ffn_b,
                 w_down=m_w_down, norm_final_w=m_norm_final_w)
    mom_v = dict(norm_mix_w=v_norm_mix_w, w_in=v_w_in, conv_qkv_w=v_conv_qkv_w, a_log=v_a_log, dt_bias=v_dt_bias,
                 gdn_norm_w=v_gdn_norm_w, pool_w=v_pool_w, pool_scale=v_pool_scale, w_out=v_w_out,
                 norm_ffn_w=v_norm_ffn_w, w_up=v_w_up, conv_ffn_w=v_conv_ffn_w, conv_ffn_b=v_conv_ffn_b,
                 w_down=v_w_down, norm_final_w=v_norm_final_w)
    n_layers, n_heads = a_log.shape
    d_model = x.shape[-1]
    dl = n_heads * HEAD_DIM
    assert x.shape[0] == 1 and dl == d_model and pool_scale.shape[1] == d_model

    big = ("w_in", "pool_w", "w_out", "w_up", "w_down")
    small = ("conv_qkv_w", "conv_ffn_w")
    big_slots = _exchange(_pack_rows([local[n].reshape(-1) for n in big], BF16), scatter=False, name="gather_weights")
    small_slots = _exchange(_pack_rows([local[n].reshape(-1) for n in small], F32), scatter=False,
                            name="gather_conv_weights")
    whole = {}
    for names, slots in ((big, big_slots), (small, small_slots)):
        parts = _unpack(slots.reshape(N_DEV, -1), [local[n].shape for n in names])
        for n, part in zip(names, parts):
            whole[n] = _whole_from_slots(part, SHARD_AXIS[n])
    w_in_whole = whole["w_in"]
    stacked = dict(
        norm_mix_w=norm_mix_w, norm_ffn_w=norm_ffn_w, gdn_norm_w=gdn_norm_w, pool_scale=pool_scale,
        conv_ffn_b=conv_ffn_b, alog_row=_lane_row(a_log, n_heads), dtb_row=_lane_row(dt_bias, n_heads),
        w_main=jnp.concatenate([w_in_whole[..., :4 * dl], w_in_whole[..., 4 * dl + 2 * n_heads:]], axis=-1),
        w_ba=jnp.pad(w_in_whole[..., 4 * dl:4 * dl + 2 * n_heads], ((0, 0), (0, 0), (0, LANES - 2 * n_heads))),
        conv_qkv_w=whole["conv_qkv_w"], pool_w=whole["pool_w"], w_out=whole["w_out"], w_up=whole["w_up"],
        conv_ffn_w=whole["conv_ffn_w"], w_down=whole["w_down"])

    loss_row, grad_x, grads, d_final = _local_step(x[0], loss_target[0], stacked, norm_final_w, n_heads)
    loss = lax.psum(loss_row[0, 0], MESH_AXES)

    g_whole = dict(
        w_in=jnp.concatenate([grads["w_main"][..., :4 * dl], grads["w_ba"][..., :2 * n_heads],
                              grads["w_main"][..., 4 * dl:]], axis=-1),
        conv_qkv_w=grads["conv_qkv_w"], pool_w=grads["pool_w"], w_out=grads["w_out"], w_up=grads["w_up"],
        conv_ffn_w=grads["conv_ffn_w"], w_down=grads["w_down"])
    to_send = _pack_rows([_slots_from_whole(g_whole[n], SHARD_AXIS[n]).reshape(N_DEV, -1) for n in SHARDED], BF16)
    summed = _sum_slots(_exchange(to_send, scatter=True, name="scatter_grads"), name="sum_grads")
    grad = dict(zip(SHARDED, _unpack(summed.reshape(-1), [local[n].shape for n in SHARDED])))

    g_rep = dict(norm_mix_w=grads["norm_mix_w"][:, 0], a_log=grads["alog_row"][:, 0, n_heads:2 * n_heads],
                 dt_bias=grads["dtb_row"][:, 0, n_heads:2 * n_heads], gdn_norm_w=grads["gdn_norm_w"][:, 0],
                 pool_scale=grads["pool_scale"][:, 0], norm_ffn_w=grads["norm_ffn_w"][:, 0],
                 conv_ffn_b=grads["conv_ffn_b"][:, 0], norm_final_w=d_final[0])
    rep_slots = _exchange(_pack_rows([g_rep[n].reshape(-1) for n in REPLICATED], F32), scatter=False,
                          name="gather_small_grads")
    rep_sum = _sum_slots(rep_slots, name="sum_small_grads")
    grad.update(zip(REPLICATED, _unpack(rep_sum.reshape(-1), [local[n].shape for n in REPLICATED])))

    delta, new_m, new_v = {}, {}, {}
    for n in SHARDED:
        delta[n], new_m[n], new_v[n] = _adamw_nd(local[n], grad[n], mom_m[n], mom_v[n], name="adamw_" + n)
    packed = [_pack_rows([src[n].reshape(-1) for n in REPLICATED], F32, 8) for src in (local, grad, mom_m, mom_v)]
    rep_out = _adamw(*packed, name="adamw_replicated")
    for dst, arr in zip((delta, new_m, new_v), rep_out):
        dst.update(zip(REPLICATED, _unpack(arr.reshape(-1), [local[n].shape for n in REPLICATED])))

    return (loss, grad_x[None], *[grad[n] for n in WEIGHTS], *[delta[n] for n in WEIGHTS],
            *[new_m[n] for n in WEIGHTS], *[new_v[n] for n in WEIGHTS])
```

```python
import collections

import jax
import jax.numpy as jnp
from jax import lax
from jax.experimental import pallas as pl
from jax.experimental.pallas import tpu as pltpu

F32 = jnp.float32
BF16 = jnp.bfloat16
MESH_AXES = ("x", "y", "c")
N_DEV = 8

NORM_EPS = 1e-6
HEAD_DIM = 128
GDN_CHUNK = 64
GDN_BATCH = 4
GDN_HEADS = 2
POOL_WINDOWS = (2, 4, 8, 16)
POOL_HALO = 16
CONV_HALO = 8
LANES = 128
V7X_VMEM_LIMIT_BYTES = 56 * 1024 * 1024

ADAM_LR = 0.001
ADAM_B1 = 0.9
ADAM_B2 = 0.999
ADAM_EPS = 1e-08
ADAM_WD = 0.01
ADAM_STEP = 10


def _mx(v):
    return v.astype(BF16)


def _dot(a, b, ta=False, tb=False, precision=None):
    dims = (((0 if ta else 1,), (1 if tb else 0,)), ((), ()))
    return lax.dot_general(a, b, dims, precision=precision, preferred_element_type=F32)


def _tile(dim, target, quantum=LANES):
    if dim <= target:
        return dim
    t = (target // quantum) * quantum
    while t >= quantum:
        if dim % t == 0:
            return t
        t -= quantum
    return dim


def _params(*semantics):
    return pltpu.CompilerParams(dimension_semantics=semantics, vmem_limit_bytes=V7X_VMEM_LIMIT_BYTES)


def _sigmoid(v):
    return 1.0 / (1.0 + jnp.exp(-v))


def _softplus(v):
    return jnp.maximum(v, 0.0) + jnp.log(1.0 + jnp.exp(-jnp.abs(v)))


_ERF_NUM = (-2.72614225801306e-10, 2.77068142495902e-08, -2.10102402082508e-06, -5.69250639462346e-05,
            -7.34990630326855e-04, -2.95459980854025e-03, -1.60960333262415e-02)
_ERF_DEN = (-1.45660718464996e-05, -2.13374055278905e-04, -1.68282697438203e-03, -7.37332916720468e-03,
            -1.42647390514189e-02)


def _erf(v):
    v = jnp.clip(v, -4.0, 4.0)
    v2 = v * v
    num = jnp.full_like(v, _ERF_NUM[0])
    for coef in _ERF_NUM[1:]:
        num = num * v2 + coef
    den = jnp.full_like(v, _ERF_DEN[0])
    for coef in _ERF_DEN[1:]:
        den = den * v2 + coef
    return v * num / den


def _gelu(v):
    return 0.5 * v * (1.0 + _erf(v * (2.0 ** -0.5)))


def _gelu_grad(v):
    return 0.5 * (1.0 + _erf(v * (2.0 ** -0.5))) + v * jnp.exp(-0.5 * v * v) * ((2.0 * jnp.pi) ** -0.5)


def _rows_before(cat, shift, halo):
    return pltpu.roll(cat, shift, 0)[halo:]


def _rows_after(cat, shift, rows):
    return pltpu.roll(cat, cat.shape[0] - shift, 0)[:rows]


def _accumulate(ref, value, first):
    @pl.when(first)
    def _():
        ref[...] = value

    @pl.when(jnp.logical_not(first))
    def _():
        ref[...] += value


def _matmul(a, b, *, name, ta=False, tb=False, add=None, out_dtype=F32, tm=512, tn=1024, tk=2048,
            b_lead=None, b_slots=None):
    m, k = (a.shape[1], a.shape[0]) if ta else a.shape
    if b_slots is not None:
        b_rows, b_cols = b.shape[2], N_DEV * b.shape[3]
    else:
        b_rows, b_cols = b.shape[-2:]
    n, kb = (b_rows, b_cols) if tb else (b_cols, b_rows)
    assert kb == k
    if b_slots is not None:
        tn, tk = (tn, b.shape[3]) if tb else (b.shape[3], tk)
    tm, tn, tk = _tile(m, tm), _tile(n, tn), _tile(k, tk)
    nk = k // tk
    has_add = add is not None

    def body(*refs):
        a_ref, b_ref = refs[0], refs[1]
        add_ref = refs[2] if has_add else None
        o_ref = refs[3] if has_add else refs[2]
        acc_ref = refs[-1]
        kk = pl.program_id(2)
        part = _dot(_mx(a_ref[...]), _mx(b_ref[...]), ta, tb)

        def finish(total):
            if has_add:
                total = total + add_ref[...]
            o_ref[...] = total.astype(out_dtype)

        if nk == 1:
            finish(part)
        else:
            _accumulate(acc_ref, part, kk == 0)

            @pl.when(kk == nk - 1)
            def _():
                finish(acc_ref[...])

    a_spec = pl.BlockSpec((tk, tm), lambda j, i, kk: (kk, i)) if ta else pl.BlockSpec((tm, tk), lambda j, i, kk: (i, kk))
    b_block = (tn, tk) if tb else (tk, tn)
    if b_slots is not None:
        b_spec = pl.BlockSpec((None, None) + b_block,
                              (lambda j, i, kk: (kk, b_slots, j, 0)) if tb else (lambda j, i, kk: (j, b_slots, kk, 0)))
    elif b_lead is not None:
        b_spec = pl.BlockSpec((None,) + b_block,
                              (lambda j, i, kk: (b_lead, j, kk)) if tb else (lambda j, i, kk: (b_lead, kk, j)))
    else:
        b_spec = pl.BlockSpec(b_block, (lambda j, i, kk: (j, kk)) if tb else (lambda j, i, kk: (kk, j)))
    o_spec = pl.BlockSpec((tm, tn), lambda j, i, kk: (i, j))
    in_specs = [a_spec, b_spec] + ([o_spec] if has_add else [])
    args = (a, b) + ((add,) if has_add else ())
    acc_shape = (tm, tn) if nk > 1 else (8, LANES)
    return pl.pallas_call(
        body, name=name, grid=(n // tn, m // tm, nk), in_specs=in_specs, out_specs=o_spec,
        out_shape=jax.ShapeDtypeStruct((m, n), out_dtype), scratch_shapes=[pltpu.VMEM(acc_shape, F32)],
        compiler_params=_params("parallel", "parallel", "arbitrary"))(*args)


def _rmsnorm_fwd(x, w, *, name):
    s, d = x.shape
    ts = _tile(s, 512, 16)

    def body(x_ref, w_ref, o_ref):
        xf = x_ref[...]
        r = lax.rsqrt(jnp.mean(xf * xf, axis=-1, keepdims=True) + NORM_EPS)
        o_ref[...] = (xf * r * w_ref[...]).astype(BF16)

    return pl.pallas_call(
        body, name=name, grid=(s // ts,),
        in_specs=[pl.BlockSpec((ts, d), lambda i: (i, 0)), pl.BlockSpec((1, d), lambda i: (0, 0))],
        out_specs=pl.BlockSpec((ts, d), lambda i: (i, 0)),
        out_shape=jax.ShapeDtypeStruct((s, d), BF16), compiler_params=_params("parallel"))(x, w.reshape(1, d))


def _rmsnorm_bwd(dy, x, w, dres, *, name):
    s, d = x.shape
    ts = _tile(s, 256, 8)

    def body(dy_ref, x_ref, w_ref, dres_ref, dx_ref, dw_ref):
        xf = x_ref[...]
        dyf = dy_ref[...]
        r = lax.rsqrt(jnp.mean(xf * xf, axis=-1, keepdims=True) + NORM_EPS)
        xh = xf * r
        dxh = dyf * w_ref[...]
        dx = r * (dxh - xh * jnp.mean(dxh * xh, axis=-1, keepdims=True))
        dx_ref[...] = dres_ref[...] + dx
        _accumulate(dw_ref, jnp.sum(dyf * xh, axis=0, keepdims=True), pl.program_id(0) == 0)

    row = pl.BlockSpec((ts, d), lambda i: (i, 0))
    vec = pl.BlockSpec((1, d), lambda i: (0, 0))
    return pl.pallas_call(
        body, name=name, grid=(s // ts,), in_specs=[row, row, vec, row], out_specs=[row, vec],
        out_shape=[jax.ShapeDtypeStruct((s, d), F32), jax.ShapeDtypeStruct((1, d), F32)],
        compiler_params=_params("arbitrary"))(dy, x, w.reshape(1, d), dres)


def _loss_head(x, w, target, *, name):
    s, d = x.shape
    ts = _tile(s, 256, 8)

    def body(x_ref, w_ref, t_ref, loss_ref, dx_ref, dw_ref):
        first = pl.program_id(0) == 0
        xf = x_ref[...]
        wv = w_ref[...]
        r = lax.rsqrt(jnp.mean(xf * xf, axis=-1, keepdims=True) + NORM_EPS)
        xh = xf * r
        err = xh * wv - t_ref[...]
        part = 0.5 * jnp.sum(jnp.mean(err * err, axis=-1, keepdims=True), axis=0, keepdims=True)
        _accumulate(loss_ref, jnp.broadcast_to(part, (1, LANES)), first)
        dyf = err * (1.0 / d)
        dxh = dyf * wv
        dx_ref[...] = r * (dxh - xh * jnp.mean(dxh * xh, axis=-1, keepdims=True))
        _accumulate(dw_ref, jnp.sum(dyf * xh, axis=0, keepdims=True), first)

    row = pl.BlockSpec((ts, d), lambda i: (i, 0))
    vec = pl.BlockSpec((1, d), lambda i: (0, 0))
    return pl.pallas_call(
        body, name=name, grid=(s // ts,), in_specs=[row, vec, row],
        out_specs=[pl.BlockSpec((1, LANES), lambda i: (0, 0)), row, vec],
        out_shape=[jax.ShapeDtypeStruct((1, LANES), F32), jax.ShapeDtypeStruct((s, d), F32),
                   jax.ShapeDtypeStruct((1, d), F32)],
        compiler_params=_params("arbitrary"))(x, w.reshape(1, d), target)


def _qkv_fwd(proj, conv_w, n_heads, *, name):
    s = proj.shape[0]
    dl = n_heads * HEAD_DIM
    width = conv_w.shape[0]
    ts = _tile(s, 512, 8)

    def body(x_ref, halo_ref, w_ref, o_ref):
        i, sec = pl.program_id(0), pl.program_id(1)
        xv = x_ref[...]
        cat = jnp.concatenate([jnp.where(i > 0, halo_ref[...], 0.0), xv], axis=0)
        c = xv * w_ref[pl.ds(width - 1, 1), :]
        for sh in range(1, width):
            c = c + _rows_before(cat, sh, CONV_HALO) * w_ref[pl.ds(width - 1 - sh, 1), :]
        act = c * _sigmoid(c)

        @pl.when(sec == 2)
        def _():
            o_ref[...] = act

        @pl.when(sec < 2)
        def _():
            scale = jnp.where(sec == 0, HEAD_DIM ** -0.5, 1.0)
            for h in range(n_heads):
                cols = slice(h * HEAD_DIM, (h + 1) * HEAD_DIM)
                ah = act[:, cols]
                o_ref[:, cols] = ah * lax.rsqrt(jnp.sum(ah * ah, axis=-1, keepdims=True) + NORM_EPS) * scale

    return pl.pallas_call(
        body, name=name, grid=(s // ts, 3),
        in_specs=[pl.BlockSpec((ts, dl), lambda i, sec: (i, sec)),
                  pl.BlockSpec((CONV_HALO, dl), lambda i, sec: (jnp.maximum(i * (ts // CONV_HALO) - 1, 0), sec)),
                  pl.BlockSpec((width, dl), lambda i, sec: (0, sec))],
        out_specs=pl.BlockSpec((None, ts, dl), lambda i, sec: (sec, i, 0)),
        out_shape=jax.ShapeDtypeStruct((3, s, dl), F32),
        compiler_params=_params("parallel", "parallel"))(proj, proj, conv_w)


def _qkv_bwd(dqkv, proj, conv_w, n_heads, *, name):
    s = proj.shape[0]
    dl = n_heads * HEAD_DIM
    width = conv_w.shape[0]
    ts = _tile(s, 256, 16)
    n_tiles = s // ts
    per = ts // CONV_HALO
    rows = ts + CONV_HALO

    def body(d_ref, dnext_ref, x_ref, xprev_ref, xnext_ref, w_ref, dx_ref, dw_ref):
        sec, i = pl.program_id(0), pl.program_id(1)
        xv = x_ref[...]
        cat = jnp.concatenate([jnp.where(i > 0, xprev_ref[...], 0.0), xv, xnext_ref[...]], axis=0)
        shifted = [cat[CONV_HALO:]] + [_rows_before(cat, sh, CONV_HALO) for sh in range(1, width)]
        c = shifted[0] * w_ref[pl.ds(width - 1, 1), :]
        for sh in range(1, width):
            c = c + shifted[sh] * w_ref[pl.ds(width - 1 - sh, 1), :]
        sig = _sigmoid(c)
        act = c * sig
        dout = jnp.concatenate([d_ref[...], dnext_ref[...]], axis=0)
        scale = jnp.where(sec == 0, HEAD_DIM ** -0.5, 1.0)
        is_v = sec == 2
        pieces = []
        for h in range(n_heads):
            cols = slice(h * HEAD_DIM, (h + 1) * HEAD_DIM)
            ah, dh = act[:, cols], dout[:, cols]
            nrm = lax.rsqrt(jnp.sum(ah * ah, axis=-1, keepdims=True) + NORM_EPS)
            dnormed = scale * nrm * (dh - ah * (nrm * nrm) * jnp.sum(dh * ah, axis=-1, keepdims=True))
            pieces.append(jnp.where(is_v, dh, dnormed))
        dact = jnp.concatenate(pieces, axis=1)
        dc = dact * sig * (1.0 + c * (1.0 - sig))
        live = jnp.logical_or(lax.broadcasted_iota(jnp.int32, (rows, 1), 0) < ts, i < n_tiles - 1)
        dc = jnp.where(live, dc, 0.0)
        dx = dc[:ts] * w_ref[pl.ds(width - 1, 1), :]
        for sh in range(1, width):
            dx = dx + _rows_after(dc, sh, ts) * w_ref[pl.ds(width - 1 - sh, 1), :]
        dx_ref[...] = dx.astype(BF16)
        dw_rows = [jnp.sum(dc[:ts] * shifted[width - 1 - j][:ts], axis=0, keepdims=True) for j in range(width)]
        _accumulate(dw_ref, jnp.concatenate(dw_rows, axis=0), i == 0)

    return pl.pallas_call(
        body, name=name, grid=(3, n_tiles),
        in_specs=[pl.BlockSpec((None, ts, dl), lambda sec, i: (sec, i, 0)),
                  pl.BlockSpec((None, CONV_HALO, dl), lambda sec, i: (sec, jnp.minimum((i + 1) * per, s // CONV_HALO - 1), 0)),
                  pl.BlockSpec((ts, dl), lambda sec, i: (i, sec)),
                  pl.BlockSpec((CONV_HALO, dl), lambda sec, i: (jnp.maximum(i * per - 1, 0), sec)),
                  pl.BlockSpec((CONV_HALO, dl), lambda sec, i: (jnp.minimum((i + 1) * per, s // CONV_HALO - 1), sec)),
                  pl.BlockSpec((width, dl), lambda sec, i: (0, sec))],
        out_specs=[pl.BlockSpec((ts, dl), lambda sec, i: (i, sec)), pl.BlockSpec((width, dl), lambda sec, i: (0, sec))],
        out_shape=[jax.ShapeDtypeStruct((s, 3 * dl), BF16), jax.ShapeDtypeStruct((width, 3 * dl), F32)],
        compiler_params=_params("parallel", "arbitrary"))(dqkv, dqkv, proj, proj, proj, conv_w)


def _ba_fwd(pba, alog_row, dtb_row, n_heads, *, name):
    s = pba.shape[0]
    ts = _tile(s, 1024, 8)

    def body(x_ref, alog_ref, dtb_ref, o_ref):
        xv = x_ref[...]
        lane = lax.broadcasted_iota(jnp.int32, xv.shape, 1)
        g = -jnp.exp(alog_ref[...]) * _softplus(xv + dtb_ref[...])
        o_ref[...] = jnp.where(lane < n_heads, _sigmoid(xv), jnp.where(lane < 2 * n_heads, g, 0.0))

    row = pl.BlockSpec((ts, LANES), lambda i: (i, 0))
    vec = pl.BlockSpec((1, LANES), lambda i: (0, 0))
    return pl.pallas_call(
        body, name=name, grid=(s // ts,), in_specs=[row, vec, vec], out_specs=row,
        out_shape=jax.ShapeDtypeStruct((s, LANES), F32), compiler_params=_params("parallel"))(pba, alog_row, dtb_row)


def _ba_bwd(dbg, pba, alog_row, dtb_row, n_heads, *, name):
    s = pba.shape[0]
    ts = _tile(s, 1024, 16)

    def body(d_ref, x_ref, alog_ref, dtb_ref, dx_ref, dalog_ref, ddtb_ref):
        first = pl.program_id(0) == 0
        xv, dv = x_ref[...], d_ref[...]
        lane = lax.broadcasted_iota(jnp.int32, xv.shape, 1)
        beta = _sigmoid(xv)
        neg_a = -jnp.exp(alog_ref[...])
        xa = xv + dtb_ref[...]
        is_a = jnp.logical_and(lane >= n_heads, lane < 2 * n_heads)
        d_xa = jnp.where(is_a, dv * neg_a * _sigmoid(xa), 0.0)
        d_g_times_g = jnp.where(is_a, dv * neg_a * _softplus(xa), 0.0)
        dx_ref[...] = jnp.where(lane < n_heads, dv * beta * (1.0 - beta), d_xa).astype(BF16)
        _accumulate(dalog_ref, jnp.sum(d_g_times_g, axis=0, keepdims=True), first)
        _accumulate(ddtb_ref, jnp.sum(d_xa, axis=0, keepdims=True), first)

    row = pl.BlockSpec((ts, LANES), lambda i: (i, 0))
    vec = pl.BlockSpec((1, LANES), lambda i: (0, 0))
    return pl.pallas_call(
        body, name=name, grid=(s // ts,), in_specs=[row, row, vec, vec], out_specs=[row, vec, vec],
        out_shape=[jax.ShapeDtypeStruct((s, LANES), BF16), jax.ShapeDtypeStruct((1, LANES), F32),
                   jax.ShapeDtypeStruct((1, LANES), F32)],
        compiler_params=_params("arbitrary"))(dbg, pba, alog_row, dtb_row)


def _bdot(a, b, ta=False, tb=False, precision=None):
    dims = (((1 if ta else 2,), (2 if tb else 1,)), ((0,), (0,)))
    return lax.dot_general(a, b, dims, precision=precision, preferred_element_type=F32)


def _bdot_hi(a, b, ta=False, tb=False):
    return _bdot(a, b, ta, tb, precision=lax.Precision.HIGHEST)


def _chunk_masks():
    ri = lax.broadcasted_iota(jnp.int32, (GDN_CHUNK, GDN_CHUNK), 0)
    ci = lax.broadcasted_iota(jnp.int32, (GDN_CHUNK, GDN_CHUNK), 1)
    return ri == ci, ri >= ci, ri > ci, ri <= ci


def _row_to_col(row, eye):
    return jnp.sum(jnp.where(eye, row, 0.0), axis=2, keepdims=True)


def _col_to_row(col, eye):
    return jnp.sum(jnp.where(eye, col, 0.0), axis=1, keepdims=True)


_Gates = collections.namedtuple("_Gates", "beta_col decay e_col f_col dec")


def _gdn_gates(beta_row, g_row):
    eye, tril, _, triu = _chunk_masks()
    g_col = _row_to_col(g_row, eye)
    gc_col = jnp.sum(jnp.where(tril, g_row, 0.0), axis=2, keepdims=True)
    gc_row = jnp.sum(jnp.where(triu, g_col, 0.0), axis=1, keepdims=True)
    g_last = jnp.sum(g_row, axis=2, keepdims=True)
    decay = jnp.exp(jnp.where(tril, gc_col - gc_row, -jnp.inf))
    return _Gates(_row_to_col(beta_row, eye), decay, jnp.exp(gc_col), jnp.exp(g_last - gc_col), jnp.exp(g_last))


def _unit_lower_inverse(lmat):
    eye = _chunk_masks()[0]
    t = jnp.where(eye, 1.0, 0.0) - lmat
    p = _bdot_hi(lmat, lmat)
    doublings = GDN_CHUNK.bit_length() - 2
    for r in range(doublings):
        t = t + _bdot_hi(t, p)
        if r < doublings - 1:
            p = _bdot_hi(p, p)
    return t


def _gdn_solve(q, k, v, gates):
    strict = _chunk_masks()[2]
    kb = k * gates.beta_col
    lmat = jnp.where(strict, _bdot(_mx(kb), _mx(k), tb=True) * gates.decay, 0.0)
    tmat = _unit_lower_inverse(lmat)
    sol = _bdot_hi(tmat, jnp.concatenate([v * gates.beta_col, kb * gates.e_col], axis=2))
    at = _bdot(_mx(q), _mx(k), tb=True) * gates.decay
    return lmat, tmat, sol, at


def _gdn_blocking(s):
    n_chunks = s // GDN_CHUNK
    per_step = 16 if n_chunks % 16 == 0 else n_chunks
    assert per_step % GDN_BATCH == 0
    return n_chunks, per_step, n_chunks // per_step


def _load_chunks(ref, sec, n0, hp):
    r0 = pl.multiple_of(n0 * GDN_CHUNK, GDN_BATCH * GDN_CHUNK)
    rows = pl.ds(r0, GDN_BATCH * GDN_CHUNK)
    cols = slice(hp * HEAD_DIM, (hp + 1) * HEAD_DIM)
    val = ref[rows, cols] if sec is None else ref[sec, rows, cols]
    return val.reshape(GDN_BATCH, GDN_CHUNK, HEAD_DIM)


def _store_chunks(ref, sec, n0, hp, val):
    r0 = pl.multiple_of(n0 * GDN_CHUNK, GDN_BATCH * GDN_CHUNK)
    rows = pl.ds(r0, GDN_BATCH * GDN_CHUNK)
    cols = slice(hp * HEAD_DIM, (hp + 1) * HEAD_DIM)
    flat = val.reshape(GDN_BATCH * GDN_CHUNK, HEAD_DIM)
    if sec is None:
        ref[rows, cols] = flat
    else:
        ref[sec, rows, cols] = flat


def _gdn_specs(n_heads, n_steps, per_step, order):
    rows, width = per_step * GDN_CHUNK, GDN_HEADS * HEAD_DIM
    rowvec = pl.BlockSpec((GDN_HEADS, per_step, 1, GDN_CHUNK), lambda h, j: (h, order(j), 0, 0))
    qkv = pl.BlockSpec((3, rows, width), lambda h, j: (0, order(j), h))
    act = pl.BlockSpec((rows, width), lambda h, j: (order(j), h))
    states = pl.BlockSpec((GDN_HEADS, per_step, HEAD_DIM, HEAD_DIM), lambda h, j: (h, order(j), 0, 0))
    return rowvec, qkv, act, states


def _gdn_fwd(qkv, beta_rows, g_rows, *, name):
    _, s, dl = qkv.shape
    n_heads = dl // HEAD_DIM
    c = GDN_CHUNK
    n_chunks, per_step, n_steps = _gdn_blocking(s)
    n_groups = per_step // GDN_BATCH
    heads = range(GDN_HEADS)

    def body(qkv_ref, b_ref, g_ref, o_ref, st_ref, state_ref, sol_s, at_s, qd_s, ke_s, vn_s, dec_s):
        @pl.when(pl.program_id(1) == 0)
        def _():
            state_ref[...] = jnp.zeros_like(state_ref)

        def solve(gi, carry):
            n0 = gi * GDN_BATCH
            grp = pl.ds(n0, GDN_BATCH)
            for hp in heads:
                q, k, v = (_load_chunks(qkv_ref, j, n0, hp) for j in range(3))
                gates = _gdn_gates(b_ref[hp, grp], g_ref[hp, grp])
                _, _, sol, at = _gdn_solve(q, k, v, gates)
                sol_s[hp, grp] = sol
                at_s[hp, grp] = at
                qd_s[hp, grp] = q * gates.e_col
                ke_s[hp, grp] = k * gates.f_col
                dec_s[hp, grp] = jnp.broadcast_to(gates.dec, (GDN_BATCH, 1, LANES))
            return carry

        lax.fori_loop(0, n_groups, solve, 0)

        def recur(n, states):
            out = []
            for hp in heads:
                state = states[hp]
                st_ref[hp, n] = state
                sol = sol_s[hp, n]
                v_new = sol[:, :HEAD_DIM] - _dot(_mx(sol[:, HEAD_DIM:]), _mx(state))
                vn_s[hp, n] = v_new
                out.append(state * dec_s[hp, n] + _dot(_mx(ke_s[hp, n]), _mx(v_new), ta=True))
            return tuple(out)

        final = lax.fori_loop(0, per_step, recur, tuple(state_ref[hp] for hp in heads))
        for hp in heads:
            state_ref[hp] = final[hp]

        def emit(gi, carry):
            n0 = gi * GDN_BATCH
            grp = pl.ds(n0, GDN_BATCH)
            for hp in heads:
                o = (_bdot(_mx(qd_s[hp, grp]), _mx(st_ref[hp, grp]))
                     + _bdot(_mx(at_s[hp, grp]), _mx(vn_s[hp, grp])))
                _store_chunks(o_ref, None, n0, hp, o)
            return carry

        lax.fori_loop(0, n_groups, emit, 0)

    rowvec, qkv_spec, act_spec, st_spec = _gdn_specs(n_heads, n_steps, per_step, lambda j: j)
    wide = lambda w: pltpu.VMEM((GDN_HEADS, per_step, c, w), F32)
    return pl.pallas_call(
        body, name=name, grid=(n_heads // GDN_HEADS, n_steps),
        in_specs=[qkv_spec, rowvec, rowvec], out_specs=[act_spec, st_spec],
        out_shape=[jax.ShapeDtypeStruct((s, dl), F32),
                   jax.ShapeDtypeStruct((n_heads, n_chunks, HEAD_DIM, HEAD_DIM), F32)],
        scratch_shapes=[pltpu.VMEM((GDN_HEADS, HEAD_DIM, HEAD_DIM), F32), wide(2 * HEAD_DIM), wide(c), wide(HEAD_DIM),
                        wide(HEAD_DIM), wide(HEAD_DIM), pltpu.VMEM((GDN_HEADS, per_step, 1, LANES), F32)],
        compiler_params=_params("parallel", "arbitrary"))(qkv, beta_rows, g_rows)


def _gdn_bwd(do, qkv, beta_rows, g_rows, states, *, name):
    _, s, dl = qkv.shape
    n_heads = dl // HEAD_DIM
    c = GDN_CHUNK
    n_chunks, per_step, n_steps = _gdn_blocking(s)
    n_groups = per_step // GDN_BATCH
    heads = range(GDN_HEADS)

    def body(do_ref, qkv_ref, b_ref, g_ref, st_ref, dqkv_ref, db_ref, dg_ref,
             dstate_ref, lmat_s, tmat_s, at_s, dat_s, sol_s, vn_s, dvn_s, dqd_s, ke_s, qdo_s, dst_s, dec_s):
        @pl.when(pl.program_id(1) == 0)
        def _():
            dstate_ref[...] = jnp.zeros_like(dstate_ref)

        eye, tril, strict, _ = _chunk_masks()

        def solve(gi, carry):
            n0 = gi * GDN_BATCH
            grp = pl.ds(n0, GDN_BATCH)
            for hp in heads:
                q, k, v = (_load_chunks(qkv_ref, j, n0, hp) for j in range(3))
                gates = _gdn_gates(b_ref[hp, grp], g_ref[hp, grp])
                lmat, tmat, sol, at = _gdn_solve(q, k, v, gates)
                mstate = _mx(st_ref[hp, grp])
                md_o = _mx(_load_chunks(do_ref, None, n0, hp))
                v_new = sol[:, :, :HEAD_DIM] - _bdot(_mx(sol[:, :, HEAD_DIM:]), mstate)
                lmat_s[hp, grp] = lmat
                tmat_s[hp, grp] = tmat
                sol_s[hp, grp] = sol
                at_s[hp, grp] = at
                vn_s[hp, grp] = v_new
                dat_s[hp, grp] = jnp.where(tril, _bdot(md_o, _mx(v_new), tb=True), 0.0)
                dvn_s[hp, grp] = _bdot(_mx(at), md_o, ta=True)
                dqd_s[hp, grp] = _bdot(md_o, mstate, tb=True)
                qdo_s[hp, grp] = _bdot(_mx(q * gates.e_col), md_o, ta=True)
                ke_s[hp, grp] = k * gates.f_col
                dec_s[hp, grp] = jnp.broadcast_to(gates.dec, (GDN_BATCH, 1, LANES))
            return carry

        lax.fori_loop(0, n_groups, solve, 0)

        def recur(idx, dstates):
            n = per_step - 1 - idx
            out = []
            for hp in heads:
                dstate = dstates[hp]
                dst_s[hp, n] = dstate
                dv_new = dvn_s[hp, n] + _dot(_mx(ke_s[hp, n]), _mx(dstate))
                dvn_s[hp, n] = dv_new
                wc = sol_s[hp, n][:, HEAD_DIM:]
                out.append(dstate * dec_s[hp, n] + qdo_s[hp, n] - _dot(_mx(wc), _mx(dv_new), ta=True))
            return tuple(out)

        final = lax.fori_loop(0, per_step, recur, tuple(dstate_ref[hp] for hp in heads))
        for hp in heads:
            dstate_ref[hp] = final[hp]

        def emit_head(hp, n0):
            grp = pl.ds(n0, GDN_BATCH)
            q, k, v = (_load_chunks(qkv_ref, j, n0, hp) for j in range(3))
            gates = _gdn_gates(b_ref[hp, grp], g_ref[hp, grp])
            state, dstate = st_ref[hp, grp], dst_s[hp, grp]
            lmat, at, dat, sol = lmat_s[hp, grp], at_s[hp, grp], dat_s[hp, grp], sol_s[hp, grp]
            v_new, dv_new, dqd = vn_s[hp, grp], dvn_s[hp, grp], dqd_s[hp, grp]
            dke = _bdot(_mx(v_new), _mx(dstate), tb=True)
            dwc = -_bdot(_mx(dv_new), _mx(state), tb=True)
            ddec = jnp.sum(jnp.sum(dstate * state, axis=2, keepdims=True), axis=1, keepdims=True)
            drhs = _bdot_hi(tmat_s[hp, grp], jnp.concatenate([dv_new, dwc], axis=2), ta=True)
            dvb, dkbe = drhs[:, :, :HEAD_DIM], drhs[:, :, HEAD_DIM:]
            dl_mat = jnp.where(strict, -_bdot(_mx(drhs), _mx(sol), tb=True), 0.0)
            dkk = dl_mat * gates.decay
            dqk = dat * gates.decay
            kb = k * gates.beta_col
            mk = _mx(k)
            dkb = _bdot(_mx(dkk), mk) + dkbe * gates.e_col
            dq = _bdot(_mx(dqk), mk) + dqd * gates.e_col
            dk = (_bdot(_mx(dqk), _mx(q), ta=True) + _bdot(_mx(dkk), _mx(kb), ta=True) + dke * gates.f_col
                  + dkb * gates.beta_col)
            _store_chunks(dqkv_ref, 0, n0, hp, dq)
            _store_chunks(dqkv_ref, 1, n0, hp, dk)
            _store_chunks(dqkv_ref, 2, n0, hp, dvb * gates.beta_col)
            dbeta_col = jnp.sum(dkb * k + dvb * v, axis=2, keepdims=True)
            through_decay = dl_mat * lmat + dat * at
            dke_ke = jnp.sum(dke * (k * gates.f_col), axis=2, keepdims=True)
            dgc_col = (jnp.sum(through_decay, axis=2, keepdims=True)
                       - _row_to_col(jnp.sum(through_decay, axis=1, keepdims=True), eye)
                       + jnp.sum(dqd * (q * gates.e_col) + dkbe * (kb * gates.e_col), axis=2, keepdims=True) - dke_ke)
            dg_last = jnp.sum(dke_ke, axis=1, keepdims=True) + ddec * gates.dec
            db_ref[hp, grp] = _col_to_row(dbeta_col, eye)
            dg_ref[hp, grp] = jnp.sum(jnp.where(tril, dgc_col, 0.0), axis=1, keepdims=True) + dg_last

        def emit(gi, carry):
            for hp in heads:
                emit_head(hp, gi * GDN_BATCH)
            return carry

        lax.fori_loop(0, n_groups, emit, 0)

    rowvec, qkv_spec, act_spec, st_spec = _gdn_specs(n_heads, n_steps, per_step, lambda j: n_steps - 1 - j)
    wide = lambda w: pltpu.VMEM((GDN_HEADS, per_step, c, w), F32)
    square = pltpu.VMEM((GDN_HEADS, per_step, HEAD_DIM, HEAD_DIM), F32)
    return pl.pallas_call(
        body, name=name, grid=(n_heads // GDN_HEADS, n_steps),
        in_specs=[act_spec, qkv_spec, rowvec, rowvec, st_spec], out_specs=[qkv_spec, rowvec, rowvec],
        out_shape=[jax.ShapeDtypeStruct((3, s, dl), F32),
                   jax.ShapeDtypeStruct((n_heads, n_chunks, 1, c), F32),
                   jax.ShapeDtypeStruct((n_heads, n_chunks, 1, c), F32)],
        scratch_shapes=[pltpu.VMEM((GDN_HEADS, HEAD_DIM, HEAD_DIM), F32), wide(c), wide(c), wide(c), wide(c),
                        wide(2 * HEAD_DIM), wide(HEAD_DIM), wide(HEAD_DIM), wide(HEAD_DIM), wide(HEAD_DIM),
                        square, square, pltpu.VMEM((GDN_HEADS, per_step, 1, LANES), F32)],
        compiler_params=_params("parallel", "arbitrary"))(do, qkv, beta_rows, g_rows, states)


def _pool_counts(tile, ts, extra, win):
    t = tile * ts + lax.broadcasted_iota(jnp.int32, (ts + extra, 1), 0)
    return jnp.minimum(t + 1, win).astype(F32)


def _pooled(cat, p_cols, tile, ts, win):
    acc, span = cat, 1
    while span < win:
        acc = acc + pltpu.roll(acc, span, 0)
        span *= 2
    return acc[POOL_HALO:] / _pool_counts(tile, ts, 0, win) - p_cols


def _merge_fwd(proj, o, gnw, pool_w, layer, pool_scale, *, name):
    s, d = o.shape
    n_heads = d // HEAD_DIM
    n_groups, pg = pool_w.shape[1], pool_w.shape[2]
    assert n_groups == len(POOL_WINDOWS) and n_groups * pg == d and pg % HEAD_DIM == 0
    heads_per_group = pg // HEAD_DIM
    ts = _tile(s, 256, 16)

    def body(o_ref, z_ref, p_ref, halo_ref, ga_ref, gb_ref, gnw_ref, pw_ref, ps_ref, out_ref):
        i = pl.program_id(0)
        gnw_v = gnw_ref[...]
        halo = jnp.where(i > 0, halo_ref[...], 0.0)
        for gi, win in enumerate(POOL_WINDOWS):
            gcols = slice(gi * pg, (gi + 1) * pg)
            pv = p_ref[:, gcols]
            pooled = _pooled(jnp.concatenate([halo[:, gcols], pv], axis=0), pv, i, ts, win)
            yb = _dot(_mx(pooled), pw_ref[gi]) * ps_ref[:, gcols]
            for h in range(gi * heads_per_group, (gi + 1) * heads_per_group):
                cols = slice(h * HEAD_DIM, (h + 1) * HEAD_DIM)
                in_group = slice(h * HEAD_DIM - gi * pg, (h + 1) * HEAD_DIM - gi * pg)
                oh, zh = o_ref[:, cols], z_ref[:, cols]
                r = lax.rsqrt(jnp.mean(oh * oh, axis=-1, keepdims=True) + NORM_EPS)
                ya = oh * r * gnw_v * (zh * _sigmoid(zh))
                out_ref[:, cols] = (_sigmoid(ga_ref[:, cols]) * ya
                                    + _sigmoid(gb_ref[:, cols]) * yb[:, in_group]).astype(BF16)

    blk = lambda col: pl.BlockSpec((ts, d), lambda i, col=col: (i, col))
    vec = lambda width: pl.BlockSpec((1, width), lambda i: (0, 0))
    return pl.pallas_call(
        body, name=name, grid=(s // ts,),
        in_specs=[blk(0), blk(3), blk(4),
                  pl.BlockSpec((POOL_HALO, d), lambda i: (jnp.maximum(i * (ts // POOL_HALO) - 1, 0), 4)),
                  blk(5), blk(6), vec(HEAD_DIM), pl.BlockSpec((None, n_groups, pg, pg), lambda i: (layer, 0, 0, 0)), vec(d)],
        out_specs=blk(0), out_shape=jax.ShapeDtypeStruct((s, d), BF16),
        compiler_params=_params("parallel"))(o, proj, proj, proj, proj, proj, gnw.reshape(1, HEAD_DIM), pool_w,
                                              pool_scale.reshape(1, d))


def _merge_bwd(dmixed, proj, o, gnw, pool_w, layer, pool_scale, *, name):
    s, d = o.shape
    n_heads = d // HEAD_DIM
    n_groups, pg = pool_w.shape[1], pool_w.shape[2]
    ts = _tile(s, 256, 16)

    def body(dm_ref, o_ref, z_ref, p_ref, halo_ref, ga_ref, gb_ref, gnw_ref, pw_ref, ps_ref,
             do_ref, dz_ref, dga_ref, dgb_ref, dpl_ref, dgnw_ref, dpw_ref, dps_ref):
        i = pl.program_id(0)
        first = i == 0
        gnw_v = gnw_ref[...]
        dgnw = jnp.zeros((1, HEAD_DIM), F32)
        for h in range(n_heads):
            cols = slice(h * HEAD_DIM, (h + 1) * HEAD_DIM)
            oh, zh, dm = o_ref[:, cols], z_ref[:, cols], dm_ref[:, cols]
            r = lax.rsqrt(jnp.mean(oh * oh, axis=-1, keepdims=True) + NORM_EPS)
            xh = oh * r
            sz = _sigmoid(zh)
            silu_z = zh * sz
            sa = _sigmoid(ga_ref[:, cols])
            on = xh * gnw_v
            dya = dm * sa
            dga_ref[:, cols] = (dm * on * silu_z * sa * (1.0 - sa)).astype(BF16)
            dz_ref[:, cols] = (dya * on * sz * (1.0 + zh * (1.0 - sz))).astype(BF16)
            don = dya * silu_z
            dgnw = dgnw + jnp.sum(don * xh, axis=0, keepdims=True)
            dxh = don * gnw_v
            do_ref[:, cols] = r * (dxh - xh * jnp.mean(dxh * xh, axis=-1, keepdims=True))
        _accumulate(dgnw_ref, dgnw, first)
        halo = jnp.where(first, 0.0, halo_ref[...])
        for gi, win in enumerate(POOL_WINDOWS):
            cols = slice(gi * pg, (gi + 1) * pg)
            pv, dm = p_ref[:, cols], dm_ref[:, cols]
            pooled = _pooled(jnp.concatenate([halo[:, cols], pv], axis=0), pv, i, ts, win)
            lin = _dot(_mx(pooled), pw_ref[gi])
            psv = ps_ref[:, cols]
            sb = _sigmoid(gb_ref[:, cols])
            dgb_ref[:, cols] = (dm * lin * psv * sb * (1.0 - sb)).astype(BF16)
            dyb = dm * sb
            _accumulate(dps_ref.at[:, cols], jnp.sum(dyb * lin, axis=0, keepdims=True), first)
            dlin = _mx(dyb * psv)
            _accumulate(dpw_ref.at[gi], _dot(_mx(pooled), dlin, ta=True), first)
            dpl_ref[:, cols] = _dot(dlin, pw_ref[gi], tb=True)

    blk = lambda col: pl.BlockSpec((ts, d), lambda i, col=col: (i, col))
    vec = lambda width: pl.BlockSpec((1, width), lambda i: (0, 0))
    pw_spec = pl.BlockSpec((None, n_groups, pg, pg), lambda i: (layer, 0, 0, 0))
    dpw_spec = pl.BlockSpec((n_groups, pg, pg), lambda i: (0, 0, 0))
    return pl.pallas_call(
        body, name=name, grid=(s // ts,),
        in_specs=[blk(0), blk(0), blk(3), blk(4),
                  pl.BlockSpec((POOL_HALO, d), lambda i: (jnp.maximum(i * (ts // POOL_HALO) - 1, 0), 4)),
                  blk(5), blk(6), vec(HEAD_DIM), pw_spec, vec(d)],
        out_specs=[blk(0), blk(0), blk(0), blk(0), blk(0), vec(HEAD_DIM), dpw_spec, vec(d)],
        out_shape=[jax.ShapeDtypeStruct((s, d), F32), jax.ShapeDtypeStruct((s, d), BF16),
                   jax.ShapeDtypeStruct((s, d), BF16), jax.ShapeDtypeStruct((s, d), BF16),
                   jax.ShapeDtypeStruct((s, d), F32), jax.ShapeDtypeStruct((1, HEAD_DIM), F32),
                   jax.ShapeDtypeStruct((n_groups, pg, pg), F32), jax.ShapeDtypeStruct((1, d), F32)],
        compiler_params=_params("arbitrary"))(dmixed, o, proj, proj, proj, proj, proj, gnw.reshape(1, HEAD_DIM),
                                               pool_w, pool_scale.reshape(1, d))


def _pool_bwd(dpooled, *, name):
    s, d = dpooled.shape
    pg = d // len(POOL_WINDOWS)
    ts = _tile(s, 512, 16)
    n_tiles = s // ts
    per = ts // POOL_HALO

    def body(d_ref, next_ref, out_ref):
        i = pl.program_id(0)
        nxt = jnp.where(i < n_tiles - 1, next_ref[...], 0.0)
        for gi, win in enumerate(POOL_WINDOWS):
            cols = slice(gi * pg, (gi + 1) * pg)
            dv = d_ref[:, cols]
            acc = jnp.concatenate([dv, nxt[:, cols]], axis=0) / _pool_counts(i, ts, POOL_HALO, win)
            span = 1
            while span < win:
                acc = acc + pltpu.roll(acc, acc.shape[0] - span, 0)
                span *= 2
            out_ref[:, cols] = (acc[:ts] - dv).astype(BF16)

    return pl.pallas_call(
        body, name=name, grid=(n_tiles,),
        in_specs=[pl.BlockSpec((ts, d), lambda i: (i, 0)),
                  pl.BlockSpec((POOL_HALO, d), lambda i: (jnp.minimum((i + 1) * per, s // POOL_HALO - 1), 0))],
        out_specs=pl.BlockSpec((ts, d), lambda i: (i, 0)), out_shape=jax.ShapeDtypeStruct((s, d), BF16),
        compiler_params=_params("parallel"))(dpooled, dpooled)


def _ffn_tiles(s, f):
    tf = _tile(f, 1408)
    return _tile(s, 512, 16), tf, f // tf


def _ffn_act_fwd(gu, conv_w, conv_b, *, name):
    s, f = gu.shape[0], gu.shape[1] // 2
    width = conv_w.shape[0]
    ts, tf, nf = _ffn_tiles(s, f)

    def body(g_ref, halo_ref, u_ref, w_ref, b_ref, act_ref, gc_ref):
        i = pl.program_id(0)
        gv = g_ref[...]
        cat = jnp.concatenate([jnp.where(i > 0, halo_ref[...], 0.0), gv], axis=0)
        gc = gv * w_ref[pl.ds(width - 1, 1), :] + b_ref[...]
        for sh in range(1, width):
            gc = gc + _rows_before(cat, sh, CONV_HALO) * w_ref[pl.ds(width - 1 - sh, 1), :]
        gc_ref[...] = gc
        act_ref[...] = (_gelu(gc) * u_ref[...]).astype(BF16)

    blk = pl.BlockSpec((ts, tf), lambda i, j: (i, j))
    return pl.pallas_call(
        body, name=name, grid=(s // ts, nf),
        in_specs=[blk, pl.BlockSpec((CONV_HALO, tf), lambda i, j: (jnp.maximum(i * (ts // CONV_HALO) - 1, 0), j)),
                  pl.BlockSpec((ts, tf), lambda i, j: (i, nf + j)),
                  pl.BlockSpec((width, tf), lambda i, j: (0, j)), pl.BlockSpec((1, tf), lambda i, j: (0, j))],
        out_specs=[blk, blk],
        out_shape=[jax.ShapeDtypeStruct((s, f), BF16), jax.ShapeDtypeStruct((s, f), F32)],
        compiler_params=_params("parallel", "parallel"))(gu, gu, gu, conv_w, conv_b.reshape(1, f))


def _ffn_act_bwd(dact, gu, gc, conv_w, *, name):
    s, f = gc.shape
    width = conv_w.shape[0]
    ts, tf, nf = _ffn_tiles(s, f)
    n_tiles = s // ts
    per = ts // CONV_HALO
    rows = ts + CONV_HALO

    def body(da_ref, da_next, gc_ref, gc_next, u_ref, u_next, g_ref, g_prev, w_ref, dg_ref, du_ref, dw_ref, db_ref):
        i = pl.program_id(1)
        first = i == 0
        da = jnp.concatenate([da_ref[...], da_next[...]], axis=0)
        gcv = jnp.concatenate([gc_ref[...], gc_next[...]], axis=0)
        uv = jnp.concatenate([u_ref[...], u_next[...]], axis=0)
        du_ref[...] = (da[:ts] * _gelu(gcv[:ts])).astype(BF16)
        live = jnp.logical_or(lax.broadcasted_iota(jnp.int32, (rows, 1), 0) < ts, i < n_tiles - 1)
        dgc = jnp.where(live, da * uv * _gelu_grad(gcv), 0.0)
        dgate = dgc[:ts] * w_ref[pl.ds(width - 1, 1), :]
        for sh in range(1, width):
            dgate = dgate + _rows_after(dgc, sh, ts) * w_ref[pl.ds(width - 1 - sh, 1), :]
        dg_ref[...] = dgate.astype(BF16)
        gv = g_ref[...]
        cat = jnp.concatenate([jnp.where(first, 0.0, g_prev[...]), gv], axis=0)
        shifted = [gv] + [_rows_before(cat, sh, CONV_HALO) for sh in range(1, width)]
        dw_rows = [jnp.sum(dgc[:ts] * shifted[width - 1 - j], axis=0, keepdims=True) for j in range(width)]
        _accumulate(dw_ref, jnp.concatenate(dw_rows, axis=0), first)
        _accumulate(db_ref, jnp.sum(dgc[:ts], axis=0, keepdims=True), first)

    nxt_row = lambda i: jnp.minimum((i + 1) * per, s // CONV_HALO - 1)
    main = lambda off: pl.BlockSpec((ts, tf), lambda j, i, off=off: (i, off + j))
    nxt = lambda off: pl.BlockSpec((CONV_HALO, tf), lambda j, i, off=off: (nxt_row(i), off + j))
    return pl.pallas_call(
        body, name=name, grid=(nf, n_tiles),
        in_specs=[main(0), nxt(0), main(0), nxt(0), main(nf), nxt(nf), main(0),
                  pl.BlockSpec((CONV_HALO, tf), lambda j, i: (jnp.maximum(i * per - 1, 0), j)),
                  pl.BlockSpec((width, tf), lambda j, i: (0, j))],
        out_specs=[main(0), main(0), pl.BlockSpec((width, tf), lambda j, i: (0, j)),
                   pl.BlockSpec((1, tf), lambda j, i: (0, j))],
        out_shape=[jax.ShapeDtypeStruct((s, f), BF16), jax.ShapeDtypeStruct((s, f), BF16),
                   jax.ShapeDtypeStruct((width, f), F32), jax.ShapeDtypeStruct((1, f), F32)],
        compiler_params=_params("parallel", "arbitrary"))(dact, dact, gc, gc, gu, gu, gu, gu, conv_w)


def _rows_layout(bg, n_heads):
    s = bg.shape[0]
    shape = (n_heads, s // GDN_CHUNK, 1, GDN_CHUNK)
    return bg[:, :n_heads].T.reshape(shape), bg[:, n_heads:2 * n_heads].T.reshape(shape)


def _lane_layout(dbeta_rows, dg_rows):
    n_heads = dbeta_rows.shape[0]
    s = dbeta_rows.shape[1] * GDN_CHUNK
    both = jnp.concatenate([dbeta_rows.reshape(n_heads, s), dg_rows.reshape(n_heads, s)], axis=0).T
    return jnp.pad(both, ((0, 0), (0, LANES - 2 * n_heads)))


def _layer_fwd(x, w, l):
    n_heads = w["n_heads"]
    h = _rmsnorm_fwd(x, w["norm_mix_w"][l], name="norm_mix_fwd")
    proj = _matmul(h, w["w_main"], b_lead=l, name="in_proj_fwd")
    pba = _matmul(h, w["w_ba"], b_lead=l, name="ba_proj_fwd")
    qkv = _qkv_fwd(proj, w["conv_qkv_w"][l], n_heads, name="qkv_fwd")
    bg = _ba_fwd(pba, w["alog_row"][l], w["dtb_row"][l], n_heads, name="ba_fwd")
    beta_rows, g_rows = _rows_layout(bg, n_heads)
    o, states = _gdn_fwd(qkv, beta_rows, g_rows, name="gdn_fwd")
    mixed = _merge_fwd(proj, o, w["gdn_norm_w"][l], w["pool_w"], l, w["pool_scale"][l], name="merge_fwd")
    x2 = _matmul(mixed, w["w_out"], b_lead=l, add=x, name="out_proj_fwd")
    h2 = _rmsnorm_fwd(x2, w["norm_ffn_w"][l], name="norm_ffn_fwd")
    gu = _matmul(h2, w["w_up_slots"], b_slots=l, name="up_proj_fwd")
    act, gc = _ffn_act_fwd(gu, w["conv_ffn_w"][l], w["conv_ffn_b"][l], name="ffn_act_fwd")
    x3 = _matmul(act, w["w_down"], b_lead=l, add=x2, tk=1408, name="down_proj_fwd")
    saved = dict(x=x, h=h, proj=proj, pba=pba, qkv=qkv, beta_rows=beta_rows, g_rows=g_rows, o=o, states=states,
                 mixed=mixed, x2=x2, h2=h2, gu=gu, gc=gc, act=act)
    return x3, saved


def _layer_bwd(dx3, w, l, sv):
    n_heads = w["n_heads"]
    g = {}
    dact = _matmul(dx3, w["w_down"], b_lead=l, tb=True, name="down_proj_dx")
    g["w_down"] = _matmul(sv["act"], dx3, ta=True, out_dtype=BF16, name="down_proj_dw")
    dgate, dup, g["conv_ffn_w"], g["conv_ffn_b"] = _ffn_act_bwd(dact, sv["gu"], sv["gc"], w["conv_ffn_w"][l],
                                                                 name="ffn_act_bwd")
    dgu = jnp.concatenate([dgate, dup], axis=1)
    dh2 = _matmul(dgu, w["w_up_slots"], b_slots=l, tb=True, name="up_proj_dx")
    g["w_up"] = _matmul(sv["h2"], dgu, ta=True, out_dtype=BF16, name="up_proj_dw")
    dx2, g["norm_ffn_w"] = _rmsnorm_bwd(dh2, sv["x2"], w["norm_ffn_w"][l], dx3, name="norm_ffn_bwd")
    dmixed = _matmul(dx2, w["w_out"], b_lead=l, tb=True, name="out_proj_dx")
    g["w_out"] = _matmul(sv["mixed"], dx2, ta=True, out_dtype=BF16, name="out_proj_dw")
    do, dz, dga, dgb, dpooled, g["gdn_norm_w"], g["pool_w"], g["pool_scale"] = _merge_bwd(
        dmixed, sv["proj"], sv["o"], w["gdn_norm_w"][l], w["pool_w"], l, w["pool_scale"][l], name="merge_bwd")
    dp = _pool_bwd(dpooled, name="pool_bwd")
    dqkv, dbeta_rows, dg_rows = _gdn_bwd(do, sv["qkv"], sv["beta_rows"], sv["g_rows"], sv["states"], name="gdn_bwd")
    dproj_qkv, g["conv_qkv_w"] = _qkv_bwd(dqkv, sv["proj"], w["conv_qkv_w"][l], n_heads, name="qkv_bwd")
    dpba, g["alog_row"], g["dtb_row"] = _ba_bwd(_lane_layout(dbeta_rows, dg_rows), sv["pba"], w["alog_row"][l],
                                                w["dtb_row"][l], n_heads, name="ba_bwd")
    dproj = jnp.concatenate([dproj_qkv, dz, dp, dga, dgb], axis=1)
    dh = _matmul(dproj, w["w_main"], b_lead=l, tb=True, name="in_proj_dx")
    dh = _matmul(dpba, w["w_ba"], b_lead=l, tb=True, add=dh, name="ba_proj_dx")
    g["w_main"] = _matmul(sv["h"], dproj, ta=True, out_dtype=BF16, name="in_proj_dw")
    g["w_ba"] = _matmul(sv["h"], dpba, ta=True, name="ba_proj_dw")
    dx, g["norm_mix_w"] = _rmsnorm_bwd(dh, sv["x"], w["norm_mix_w"][l], dx2, name="norm_mix_bwd")
    return dx, g


def _here():
    mx, my, mc = (lax.axis_index(a) for a in MESH_AXES)
    return (mx, my, mc), 4 * mx + 2 * my + mc


def _peer(pos, r):
    mx, my, mc = pos
    px = 1 - mx if r & 4 else mx
    py = 1 - my if r & 2 else my
    pc = 1 - mc if r & 1 else mc
    return (px, py, pc), 4 * px + 2 * py + pc


def _run_exchange(n_tensors, src_view, dst_view, sems):
    send_sems, recv_sems, local_sems = sems
    pos, me = _here()
    started = []
    for t in range(n_tensors):
        cp = pltpu.make_async_copy(src_view(t, me), dst_view(t, me), local_sems.at[t])
        cp.start()
        started.append(cp)

    def remote(t, r, landing):
        target, target_lin = _peer(pos, r)
        return pltpu.make_async_remote_copy(
            src_ref=src_view(t, target_lin), dst_ref=dst_view(t, target_lin if landing else me),
            send_sem=send_sems.at[t, r - 1], recv_sem=recv_sems.at[t, r - 1],
            device_id=target, device_id_type=pl.DeviceIdType.MESH)

    sends = []
    for r in range(1, N_DEV):
        for t in range(n_tensors):
            cp = remote(t, r, landing=False)
            cp.start()
            sends.append(cp)
    for r in range(1, N_DEV):
        for t in range(n_tensors):
            remote(t, r, landing=True).wait_recv()
    for cp in sends:
        cp.wait_send()
    for cp in started:
        cp.wait()


def _exchange_scratch(n_tensors):
    return [pltpu.SemaphoreType.DMA((n_tensors, N_DEV - 1)), pltpu.SemaphoreType.DMA((n_tensors, N_DEV - 1)),
            pltpu.SemaphoreType.DMA((n_tensors,))]


def _slot_view(ref, axis, index):
    return ref.at[(slice(None),) * axis + (index,)]


def _gather(srcs, slot_axes, *, name):
    n = len(srcs)

    def body(*refs):
        src_refs, out_refs = refs[:n], refs[n:2 * n]
        _run_exchange(n, lambda t, to: src_refs[t], lambda t, frm: _slot_view(out_refs[t], slot_axes[t], frm),
                      refs[2 * n:])

    hbm = pl.BlockSpec(memory_space=pltpu.HBM)
    out_shape = [jax.ShapeDtypeStruct(s.shape[:a] + (N_DEV,) + s.shape[a:], s.dtype) for s, a in zip(srcs, slot_axes)]
    return pl.pallas_call(body, name=name, in_specs=[hbm] * n, out_specs=[hbm] * n, out_shape=out_shape,
                          scratch_shapes=_exchange_scratch(n))(*srcs)


def _scatter_layer_grads(grads, windows, recvs, layer, *, name):
    n = len(grads)

    def body(*refs):
        src_refs, out_refs = refs[:n], refs[2 * n:3 * n]
        _run_exchange(n, lambda t, to: windows[t](src_refs[t], to), lambda t, frm: out_refs[t].at[frm, layer],
                      refs[3 * n:])

    hbm = pl.BlockSpec(memory_space=pltpu.HBM)
    return pl.pallas_call(
        body, name=name, in_specs=[hbm] * (2 * n), out_specs=[hbm] * n,
        out_shape=[jax.ShapeDtypeStruct(r.shape, r.dtype) for r in recvs],
        input_output_aliases={n + t: t for t in range(n)}, scratch_shapes=_exchange_scratch(n))(*grads, *recvs)


def _sum_slots(parts, *, name):
    _, n_lead, r_rows, cols = parts.shape
    tr = _tile(r_rows, max(16, (1 << 17) // cols // 16 * 16), 16)

    def body(p_ref, o_ref):
        total = p_ref[0].astype(F32)
        for p in range(1, N_DEV):
            total = total + p_ref[p].astype(F32)
        o_ref[...] = total

    return pl.pallas_call(
        body, name=name, grid=(n_lead, r_rows // tr),
        in_specs=[pl.BlockSpec((N_DEV, None, tr, cols), lambda a, i: (0, a, i, 0))],
        out_specs=pl.BlockSpec((None, tr, cols), lambda a, i: (a, i, 0)),
        out_shape=jax.ShapeDtypeStruct((n_lead, r_rows, cols), F32),
        compiler_params=_params("parallel", "parallel"))(parts)


_WinLayout = collections.namedtuple("_WinLayout", "shard_w n_main ba_dev ba_off n_ba slot_w")


def _win_layout(shard_w, n_main, ba_start, n_ba):
    ba_dev = ba_start // shard_w
    assert (ba_start + n_ba - 1) // shard_w == ba_dev and n_main % LANES == 0
    slot_w = -(-(LANES - 1 + shard_w) // LANES) * LANES
    return _WinLayout(shard_w, n_main, ba_dev, ba_start - ba_dev * shard_w, n_ba, slot_w)


def _main_start(lay, dev):
    return lay.shard_w * dev - jnp.where(dev > lay.ba_dev, lay.n_ba, 0)


def _slab_origin(lay, dev):
    return jnp.minimum(_main_start(lay, dev) // LANES * LANES, lay.n_main - lay.slot_w)


def _assemble_plan(lay):
    plan = [[] for _ in range(lay.n_main // LANES)]
    for dev in range(N_DEV):
        start = lay.shard_w * dev - (lay.n_ba if dev > lay.ba_dev else 0)
        width = lay.shard_w - (lay.n_ba if dev == lay.ba_dev else 0)
        origin = min(start // LANES, (lay.n_main - lay.slot_w) // LANES)
        pad = start - origin * LANES
        for t in range(pad // LANES, (pad + width - 1) // LANES + 1):
            plan[origin + t].append((dev, t))
    return plan


def _assemble_w_main(slabs, lay, *, name):
    _, n_layers, d, slot_w = slabs.shape
    plan = _assemble_plan(lay)
    runs = []
    shared = []
    for tile, parts in enumerate(plan):
        if len(parts) != 1:
            shared.append((tile, parts))
        elif runs and runs[-1][2] == parts[0][0] and runs[-1][0] + runs[-1][1] == tile:
            runs[-1][1] += 1
        else:
            runs.append([tile, 1, parts[0][0], parts[0][1]])
    tr = _tile(d, 256, 16)

    def body(in_ref, out_ref):
        for first, count, dev, t0 in runs:
            out_ref[:, first * LANES:(first + count) * LANES] = in_ref[dev, :, t0 * LANES:(t0 + count) * LANES]
        for tile, parts in shared:
            total = in_ref[parts[0][0], :, parts[0][1] * LANES:(parts[0][1] + 1) * LANES]
            for dev, t in parts[1:]:
                total = total + in_ref[dev, :, t * LANES:(t + 1) * LANES]
            out_ref[:, tile * LANES:(tile + 1) * LANES] = total

    return pl.pallas_call(
        body, name=name, grid=(n_layers, d // tr),
        in_specs=[pl.BlockSpec((N_DEV, None, tr, slot_w), lambda a, i: (0, a, i, 0))],
        out_specs=pl.BlockSpec((None, tr, lay.n_main), lambda a, i: (a, i, 0)),
        out_shape=jax.ShapeDtypeStruct((n_layers, d, lay.n_main), slabs.dtype),
        compiler_params=_params("parallel", "parallel"))(slabs)


def _adam_update(w, g, m, v):
    nm = ADAM_B1 * m + (1.0 - ADAM_B1) * g
    nv = ADAM_B2 * v + (1.0 - ADAM_B2) * (g * g)
    m_hat = nm / (1.0 - ADAM_B1 ** ADAM_STEP)
    v_hat = nv / (1.0 - ADAM_B2 ** ADAM_STEP)
    return -ADAM_LR * (m_hat / (jnp.sqrt(v_hat) + ADAM_EPS) + ADAM_WD * w), nm, nv


def _adamw(w, g, m, v, *, name):
    rows, cols = w.shape
    tr = _tile(rows, max(8, (1 << 18) // cols // 8 * 8), 8)

    def body(w_ref, g_ref, m_ref, v_ref, d_ref, nm_ref, nv_ref):
        d_ref[...], nm_ref[...], nv_ref[...] = _adam_update(w_ref[...], g_ref[...], m_ref[...], v_ref[...])

    blk = pl.BlockSpec((tr, cols), lambda i: (i, 0))
    out = jax.ShapeDtypeStruct((rows, cols), F32)
    return pl.pallas_call(
        body, name=name, grid=(rows // tr,), in_specs=[blk] * 4, out_specs=[blk] * 3, out_shape=[out] * 3,
        compiler_params=_params("parallel"))(w, g, m, v)


def _adamw_nd(w, g, m, v, *, name):
    two_d = (-1, w.shape[-1])
    outs = _adamw(w.reshape(two_d), g.reshape(two_d), m.reshape(two_d), v.reshape(two_d), name=name)
    return tuple(t.reshape(w.shape) for t in outs)


def _adamw_slots(parts, w, m, v, *, name):
    n_layers, rows, cols = w.shape
    tr = _tile(rows, max(16, (1 << 18) // cols // 16 * 16), 16)

    def body(p_ref, w_ref, m_ref, v_ref, g_ref, d_ref, nm_ref, nv_ref):
        total = p_ref[0].astype(F32)
        for p in range(1, N_DEV):
            total = total + p_ref[p].astype(F32)
        g_ref[...] = total
        d_ref[...], nm_ref[...], nv_ref[...] = _adam_update(w_ref[...], total, m_ref[...], v_ref[...])

    blk = pl.BlockSpec((None, tr, cols), lambda a, i: (a, i, 0))
    out = jax.ShapeDtypeStruct((n_layers, rows, cols), F32)
    return pl.pallas_call(
        body, name=name, grid=(n_layers, rows // tr),
        in_specs=[pl.BlockSpec((N_DEV, None, tr, cols), lambda a, i: (0, a, i, 0)), blk, blk, blk],
        out_specs=[blk] * 4, out_shape=[out] * 4, compiler_params=_params("parallel", "parallel"))(parts, w, m, v)


def _pack_rows(parts, dtype, quantum_rows):
    flat = jnp.concatenate([p.reshape(-1).astype(dtype) for p in parts])
    n = flat.shape[0]
    padded = -(-n // (LANES * quantum_rows)) * (LANES * quantum_rows)
    return jnp.pad(flat, (0, padded - n)).reshape(padded // LANES, LANES)


def _unpack(flat, shapes):
    lead = flat.shape[:-1]
    out, at = [], 0
    for shape in shapes:
        size = 1
        for dim in shape:
            size *= dim
        out.append(flat[..., at:at + size].reshape(lead + tuple(shape)))
        at += size
    return out


def _whole_from_slots(slots, axis):
    moved = jnp.moveaxis(slots, 0, axis)
    shape = moved.shape
    return moved.reshape(shape[:axis] + (shape[axis] * shape[axis + 1],) + shape[axis + 2:])


def _lane_row(vec, n_heads):
    return jnp.pad(vec, ((0, 0), (n_heads, LANES - 2 * n_heads)))[:, None, :]


REPLICATED = ("norm_mix_w", "a_log", "dt_bias", "gdn_norm_w", "pool_scale", "norm_ffn_w", "conv_ffn_b",
              "norm_final_w")
WEIGHTS = ("norm_mix_w", "w_in", "conv_qkv_w", "a_log", "dt_bias", "gdn_norm_w", "pool_w", "pool_scale", "w_out",
           "norm_ffn_w", "w_up", "conv_ffn_w", "conv_ffn_b", "w_down", "norm_final_w")
SMALL_QUANTUM_ROWS = 512


def kernel(x, norm_mix_w, w_in, conv_qkv_w, a_log, dt_bias, gdn_norm_w, pool_w, pool_scale, w_out, norm_ffn_w, w_up, conv_ffn_w, conv_ffn_b, w_down, norm_final_w, loss_target, m_norm_mix_w, m_w_in, m_conv_qkv_w, m_a_log, m_dt_bias, m_gdn_norm_w, m_pool_w, m_pool_scale, m_w_out, m_norm_ffn_w, m_w_up, m_conv_ffn_w, m_conv_ffn_b, m_w_down, m_norm_final_w, v_norm_mix_w, v_w_in, v_conv_qkv_w, v_a_log, v_dt_bias, v_gdn_norm_w, v_pool_w, v_pool_scale, v_w_out, v_norm_ffn_w, v_w_up, v_conv_ffn_w, v_conv_ffn_b, v_w_down, v_norm_final_w):
    local = dict(norm_mix_w=norm_mix_w, w_in=w_in, conv_qkv_w=conv_qkv_w, a_log=a_log, dt_bias=dt_bias,
                 gdn_norm_w=gdn_norm_w, pool_w=pool_w, pool_scale=pool_scale, w_out=w_out, norm_ffn_w=norm_ffn_w,
                 w_up=w_up, conv_ffn_w=conv_ffn_w, conv_ffn_b=conv_ffn_b, w_down=w_down, norm_final_w=norm_final_w)
    mom_m = dict(norm_mix_w=m_norm_mix_w, w_in=m_w_in, conv_qkv_w=m_conv_qkv_w, a_log=m_a_log, dt_bias=m_dt_bias,
                 gdn_norm_w=m_gdn_norm_w, pool_w=m_pool_w, pool_scale=m_pool_scale, w_out=m_w_out,
                 norm_ffn_w=m_norm_ffn_w, w_up=m_w_up, conv_ffn_w=m_conv_ffn_w, conv_ffn_b=m_conv_ffn_b,
                 w_down=m_w_down, norm_final_w=m_norm_final_w)
    mom_v = dict(norm_mix_w=v_norm_mix_w, w_in=v_w_in, conv_qkv_w=v_conv_qkv_w, a_log=v_a_log, dt_bias=v_dt_bias,
                 gdn_norm_w=v_gdn_norm_w, pool_w=v_pool_w, pool_scale=v_pool_scale, w_out=v_w_out,
                 norm_ffn_w=v_norm_ffn_w, w_up=v_w_up, conv_ffn_w=v_conv_ffn_w, conv_ffn_b=v_conv_ffn_b,
                 w_down=v_w_down, norm_final_w=v_norm_final_w)
    n_layers, n_heads = a_log.shape
    d_model = x.shape[-1]
    dl = n_heads * HEAD_DIM
    n_ba = 2 * n_heads
    assert x.shape[0] == 1 and dl == d_model and pool_scale.shape[1] == d_model
    lay = _win_layout(w_in.shape[2], N_DEV * w_in.shape[2] - n_ba, 4 * dl, n_ba)
    _, me = _here()
    is_ba_dev = me == lay.ba_dev
    my_pad = _main_start(lay, me) - _slab_origin(lay, me)
    ba_cols = slice(lay.ba_off, lay.ba_off + n_ba)

    w_in_bf = w_in.astype(BF16)
    without_ba = jnp.concatenate([w_in_bf[..., :lay.ba_off], w_in_bf[..., lay.ba_off + n_ba:],
                                  jnp.zeros(w_in.shape[:2] + (n_ba,), BF16)], axis=-1)
    slab = lax.dynamic_update_slice(jnp.zeros(w_in.shape[:2] + (lay.slot_w,), BF16),
                                    jnp.where(is_ba_dev, without_ba, w_in_bf), (0, 0, my_pad))
    ba_part = jnp.pad(jnp.where(is_ba_dev, w_in_bf[..., ba_cols], jnp.zeros((), BF16)),
                      ((0, 0), (0, 0), (0, LANES - n_ba)))
    convs = _pack_rows([conv_qkv_w, conv_ffn_w], F32, 16)
    slabs, ba_slots, up_slots, out_slots, down_slots, pool_slots, conv_slots = _gather(
        [slab, ba_part, w_up.astype(BF16), w_out.astype(BF16), w_down.astype(BF16), pool_w.astype(BF16), convs],
        [0, 0, 0, 1, 1, 2, 0], name="gather_weights")
    conv_parts = _unpack(conv_slots.reshape(N_DEV, -1), [conv_qkv_w.shape, conv_ffn_w.shape])
    merge_slots = lambda t, axis: t.reshape(t.shape[:axis] + (N_DEV * t.shape[axis + 1],) + t.shape[axis + 2:])
    w = dict(
        n_heads=n_heads, norm_mix_w=norm_mix_w, norm_ffn_w=norm_ffn_w, gdn_norm_w=gdn_norm_w, pool_scale=pool_scale,
        conv_ffn_b=conv_ffn_b, alog_row=_lane_row(a_log, n_heads), dtb_row=_lane_row(dt_bias, n_heads),
        w_main=_assemble_w_main(slabs, lay, name="assemble_w_main"), w_ba=ba_slots[lay.ba_dev],
        w_up_slots=up_slots, w_out=merge_slots(out_slots, 1), w_down=merge_slots(down_slots, 1),
        pool_w=merge_slots(pool_slots, 2), conv_qkv_w=_whole_from_slots(conv_parts[0], 2),
        conv_ffn_w=_whole_from_slots(conv_parts[1], 2))

    xc = x[0]
    saved = []
    for l in range(n_layers):
        xc, sv = _layer_fwd(xc, w, l)
        saved.append(sv)
    loss_row, dx, d_final = _loss_head(xc, norm_final_w, loss_target[0], name="loss_head")
    loss = lax.psum(loss_row[0, 0], MESH_AXES)

    shard = {n: local[n].shape[1:] for n in ("w_up", "w_out", "w_down", "pool_w")}
    recvs = [jnp.zeros((N_DEV, n_layers) + shape, BF16) for shape in
             [(d_model, lay.slot_w), shard["w_up"], shard["w_out"], shard["w_down"], shard["pool_w"]]]
    up_w, out_rows, down_rows, pool_rows = shard["w_up"][1], shard["w_out"][0], shard["w_down"][0], shard["pool_w"][1]
    windows = [
        lambda ref, to: ref.at[:, pl.ds(pl.multiple_of(_slab_origin(lay, to), LANES), lay.slot_w)],
        lambda ref, to: ref.at[:, pl.ds(pl.multiple_of(to * up_w, LANES), up_w)],
        lambda ref, to: ref.at[pl.ds(pl.multiple_of(to * out_rows, 16), out_rows), :],
        lambda ref, to: ref.at[pl.ds(pl.multiple_of(to * down_rows, 16), down_rows), :],
        lambda ref, to: ref.at[:, pl.ds(pl.multiple_of(to * pool_rows, 16), pool_rows), :]]
    layer_grads = [None] * n_layers
    for l in reversed(range(n_layers)):
        dx, g = _layer_bwd(dx, w, l, saved[l])
        layer_grads[l] = g
        recvs = _scatter_layer_grads([g["w_main"], g["w_up"], g["w_out"], g["w_down"], g["pool_w"].astype(BF16)],
                                     windows, recvs, l, name="scatter_grads")
    grad_x = dx
    stack = lambda name: jnp.stack([g[name] for g in layer_grads])

    small_names = REPLICATED + ("conv_qkv_w", "conv_ffn_w", "w_ba")
    g_small = dict(norm_mix_w=stack("norm_mix_w")[:, 0], a_log=stack("alog_row")[:, 0, n_heads:n_ba],
                   dt_bias=stack("dtb_row")[:, 0, n_heads:n_ba], gdn_norm_w=stack("gdn_norm_w")[:, 0],
                   pool_scale=stack("pool_scale")[:, 0], norm_ffn_w=stack("norm_ffn_w")[:, 0],
                   conv_ffn_b=stack("conv_ffn_b")[:, 0], norm_final_w=d_final[0], conv_qkv_w=stack("conv_qkv_w"),
                   conv_ffn_w=stack("conv_ffn_w"), w_ba=stack("w_ba")[..., :n_ba])
    small_shapes = [g_small[n].shape for n in small_names]
    small_slots, = _gather([_pack_rows([g_small[n] for n in small_names], F32, SMALL_QUANTUM_ROWS)], [0],
                           name="gather_small_grads")
    small_sum = _sum_slots(small_slots[:, None], name="sum_small_grads")
    grad = dict(zip(small_names, _unpack(small_sum.reshape(-1), small_shapes)))
    for n in ("conv_qkv_w", "conv_ffn_w"):
        width = local[n].shape[2]
        grad[n] = lax.dynamic_slice_in_dim(grad[n], me * width, width, axis=2)

    delta, new_m, new_v = {}, {}, {}
    for n, parts in zip(("w_up", "w_out", "w_down", "pool_w"), recvs[1:]):
        flat = lambda t, lead: t.reshape(t.shape[:lead] + (-1, t.shape[-1]))
        outs = _adamw_slots(flat(parts, 2), flat(local[n], 1), flat(mom_m[n], 1), flat(mom_v[n], 1), name="adamw_" + n)
        grad[n], delta[n], new_m[n], new_v[n] = (t.reshape(local[n].shape) for t in outs)
    main_sum = _sum_slots(recvs[0], name="sum_w_main_grads")
    g_main = lax.dynamic_slice_in_dim(main_sum, my_pad, lay.shard_w, axis=2)
    with_ba = jnp.concatenate([g_main[..., :lay.ba_off], grad.pop("w_ba"),
                               g_main[..., lay.ba_off:lay.shard_w - n_ba]], axis=-1)
    grad["w_in"] = jnp.where(is_ba_dev, with_ba, g_main)
    for n in ("w_in", "conv_qkv_w", "conv_ffn_w"):
        delta[n], new_m[n], new_v[n] = _adamw_nd(local[n], grad[n], mom_m[n], mom_v[n], name="adamw_" + n)
    packed = [_pack_rows([src[n] for n in REPLICATED], F32, 8) for src in (local, grad, mom_m, mom_v)]
    rep_out = _adamw(*packed, name="adamw_replicated")
    rep_shapes = [local[n].shape for n in REPLICATED]
    for dst, arr in zip((delta, new_m, new_v), rep_out):
        dst.update(zip(REPLICATED, _unpack(arr.reshape(-1), rep_shapes)))

    return (loss, grad_x[None], *[grad[n] for n in WEIGHTS], *[delta[n] for n in WEIGHTS],
            *[new_m[n] for n in WEIGHTS], *[new_v[n] for n in WEIGHTS])
```

```python
import collections

import jax
import jax.numpy as jnp
from jax import lax
from jax.experimental import pallas as pl
from jax.experimental.pallas import tpu as pltpu

F32 = jnp.float32
BF16 = jnp.bfloat16
MESH_AXES = ("x", "y", "c")
N_DEV = 8

NORM_EPS = 1e-6
HEAD_DIM = 128
GDN_CHUNK = 64
GDN_BATCH = 4
GDN_HEADS = 2
POOL_WINDOWS = (2, 4, 8, 16)
POOL_HALO = 16
CONV_HALO = 8
LANES = 128
V7X_VMEM_LIMIT_BYTES = 56 * 1024 * 1024

ADAM_LR = 0.001
ADAM_B1 = 0.9
ADAM_B2 = 0.999
ADAM_EPS = 1e-08
ADAM_WD = 0.01
ADAM_STEP = 10


def _mx(v):
    return v.astype(BF16)


def _dot(a, b, ta=False, tb=False, precision=None):
    dims = (((0 if ta else 1,), (1 if tb else 0,)), ((), ()))
    return lax.dot_general(a, b, dims, precision=precision, preferred_element_type=F32)


def _tile(dim, target, quantum=LANES):
    if dim <= target:
        return dim
    t = (target // quantum) * quantum
    while t >= quantum:
        if dim % t == 0:
            return t
        t -= quantum
    return dim


def _params(*semantics):
    return pltpu.CompilerParams(dimension_semantics=semantics, vmem_limit_bytes=V7X_VMEM_LIMIT_BYTES)


def _sigmoid(v):
    return 1.0 / (1.0 + jnp.exp(-v))


def _softplus(v):
    return jnp.maximum(v, 0.0) + jnp.log(1.0 + jnp.exp(-jnp.abs(v)))


_ERF_NUM = (-2.72614225801306e-10, 2.77068142495902e-08, -2.10102402082508e-06, -5.69250639462346e-05,
            -7.34990630326855e-04, -2.95459980854025e-03, -1.60960333262415e-02)
_ERF_DEN = (-1.45660718464996e-05, -2.13374055278905e-04, -1.68282697438203e-03, -7.37332916720468e-03,
            -1.42647390514189e-02)


def _erf(v):
    v = jnp.clip(v, -4.0, 4.0)
    v2 = v * v
    num = jnp.full_like(v, _ERF_NUM[0])
    for coef in _ERF_NUM[1:]:
        num = num * v2 + coef
    den = jnp.full_like(v, _ERF_DEN[0])
    for coef in _ERF_DEN[1:]:
        den = den * v2 + coef
    return v * num / den


def _gelu(v):
    return 0.5 * v * (1.0 + _erf(v * (2.0 ** -0.5)))


def _gelu_grad(v):
    return 0.5 * (1.0 + _erf(v * (2.0 ** -0.5))) + v * jnp.exp(-0.5 * v * v) * ((2.0 * jnp.pi) ** -0.5)


def _rows_before(cat, shift, halo):
    return pltpu.roll(cat, shift, 0)[halo:]


def _rows_after(cat, shift, rows):
    return pltpu.roll(cat, cat.shape[0] - shift, 0)[:rows]


def _accumulate(ref, value, first):
    @pl.when(first)
    def _():
        ref[...] = value

    @pl.when(jnp.logical_not(first))
    def _():
        ref[...] += value


def _matmul(a, b, *, name, ta=False, tb=False, add=None, out_dtype=F32, tm=512, tn=1024, tk=2048,
            b_slots=False, after=None):
    m, k = (a.shape[1], a.shape[0]) if ta else a.shape
    b_rows, b_cols = (b.shape[1], N_DEV * b.shape[2]) if b_slots else b.shape
    n, kb = (b_rows, b_cols) if tb else (b_cols, b_rows)
    assert kb == k
    if b_slots:
        tn, tk = (tn, b.shape[2]) if tb else (b.shape[2], tk)
    tm, tn, tk = _tile(m, tm), _tile(n, tn), _tile(k, tk)
    nk = k // tk
    has_add = add is not None
    n_in = 2 + has_add + (after is not None)

    def body(*refs):
        a_ref, b_ref = refs[0], refs[1]
        add_ref = refs[2] if has_add else None
        o_ref, acc_ref = refs[n_in], refs[n_in + 1]
        kk = pl.program_id(2)
        part = _dot(_mx(a_ref[...]), _mx(b_ref[...]), ta, tb)

        def finish(total):
            if has_add:
                total = total + add_ref[...]
            o_ref[...] = total.astype(out_dtype)

        if nk == 1:
            finish(part)
        else:
            _accumulate(acc_ref, part, kk == 0)

            @pl.when(kk == nk - 1)
            def _():
                finish(acc_ref[...])

    a_spec = pl.BlockSpec((tk, tm), lambda j, i, kk: (kk, i)) if ta else pl.BlockSpec((tm, tk), lambda j, i, kk: (i, kk))
    b_block = (tn, tk) if tb else (tk, tn)
    if b_slots:
        b_spec = pl.BlockSpec((None,) + b_block, (lambda j, i, kk: (kk, j, 0)) if tb else (lambda j, i, kk: (j, kk, 0)))
    else:
        b_spec = pl.BlockSpec(b_block, (lambda j, i, kk: (j, kk)) if tb else (lambda j, i, kk: (kk, j)))
    o_spec = pl.BlockSpec((tm, tn), lambda j, i, kk: (i, j))
    in_specs = [a_spec, b_spec] + ([o_spec] if has_add else [])
    args = (a, b) + ((add,) if has_add else ())
    if after is not None:
        in_specs.append(pl.BlockSpec(memory_space=pl.ANY))
        args += (after,)
    acc_shape = (tm, tn) if nk > 1 else (8, LANES)
    return pl.pallas_call(
        body, name=name, grid=(n // tn, m // tm, nk), in_specs=in_specs, out_specs=o_spec,
        out_shape=jax.ShapeDtypeStruct((m, n), out_dtype), scratch_shapes=[pltpu.VMEM(acc_shape, F32)],
        compiler_params=_params("parallel", "parallel", "arbitrary"))(*args)


def _rmsnorm_fwd(x, w, *, name):
    s, d = x.shape
    ts = _tile(s, 512, 16)

    def body(x_ref, w_ref, o_ref):
        xf = x_ref[...]
        r = lax.rsqrt(jnp.mean(xf * xf, axis=-1, keepdims=True) + NORM_EPS)
        o_ref[...] = (xf * r * w_ref[...]).astype(BF16)

    return pl.pallas_call(
        body, name=name, grid=(s // ts,),
        in_specs=[pl.BlockSpec((ts, d), lambda i: (i, 0)), pl.BlockSpec((1, d), lambda i: (0, 0))],
        out_specs=pl.BlockSpec((ts, d), lambda i: (i, 0)),
        out_shape=jax.ShapeDtypeStruct((s, d), BF16), compiler_params=_params("parallel"))(x, w.reshape(1, d))


def _rmsnorm_bwd(dy, x, w, dres, *, name):
    s, d = x.shape
    ts = _tile(s, 256, 8)

    def body(dy_ref, x_ref, w_ref, dres_ref, dx_ref, dw_ref):
        xf = x_ref[...]
        dyf = dy_ref[...]
        r = lax.rsqrt(jnp.mean(xf * xf, axis=-1, keepdims=True) + NORM_EPS)
        xh = xf * r
        dxh = dyf * w_ref[...]
        dx = r * (dxh - xh * jnp.mean(dxh * xh, axis=-1, keepdims=True))
        dx_ref[...] = dres_ref[...] + dx
        _accumulate(dw_ref, jnp.sum(dyf * xh, axis=0, keepdims=True), pl.program_id(0) == 0)

    row = pl.BlockSpec((ts, d), lambda i: (i, 0))
    vec = pl.BlockSpec((1, d), lambda i: (0, 0))
    return pl.pallas_call(
        body, name=name, grid=(s // ts,), in_specs=[row, row, vec, row], out_specs=[row, vec],
        out_shape=[jax.ShapeDtypeStruct((s, d), F32), jax.ShapeDtypeStruct((1, d), F32)],
        compiler_params=_params("arbitrary"))(dy, x, w.reshape(1, d), dres)


def _loss_head(x, w, target, *, name):
    s, d = x.shape
    ts = _tile(s, 256, 8)

    def body(x_ref, w_ref, t_ref, loss_ref, dx_ref, dw_ref):
        first = pl.program_id(0) == 0
        xf = x_ref[...]
        wv = w_ref[...]
        r = lax.rsqrt(jnp.mean(xf * xf, axis=-1, keepdims=True) + NORM_EPS)
        xh = xf * r
        err = xh * wv - t_ref[...]
        part = 0.5 * jnp.sum(jnp.mean(err * err, axis=-1, keepdims=True), axis=0, keepdims=True)
        _accumulate(loss_ref, jnp.broadcast_to(part, (1, LANES)), first)
        dyf = err * (1.0 / d)
        dxh = dyf * wv
        dx_ref[...] = r * (dxh - xh * jnp.mean(dxh * xh, axis=-1, keepdims=True))
        _accumulate(dw_ref, jnp.sum(dyf * xh, axis=0, keepdims=True), first)

    row = pl.BlockSpec((ts, d), lambda i: (i, 0))
    vec = pl.BlockSpec((1, d), lambda i: (0, 0))
    return pl.pallas_call(
        body, name=name, grid=(s // ts,), in_specs=[row, vec, row],
        out_specs=[pl.BlockSpec((1, LANES), lambda i: (0, 0)), row, vec],
        out_shape=[jax.ShapeDtypeStruct((1, LANES), F32), jax.ShapeDtypeStruct((s, d), F32),
                   jax.ShapeDtypeStruct((1, d), F32)],
        compiler_params=_params("arbitrary"))(x, w.reshape(1, d), target)


def _qkv_fwd(proj, conv_w, n_heads, *, name):
    s = proj.shape[0]
    dl = n_heads * HEAD_DIM
    width = conv_w.shape[0]
    ts = _tile(s, 512, 8)

    def body(x_ref, halo_ref, w_ref, o_ref):
        i, sec = pl.program_id(0), pl.program_id(1)
        xv = x_ref[...]
        cat = jnp.concatenate([jnp.where(i > 0, halo_ref[...], 0.0), xv], axis=0)
        c = xv * w_ref[pl.ds(width - 1, 1), :]
        for sh in range(1, width):
            c = c + _rows_before(cat, sh, CONV_HALO) * w_ref[pl.ds(width - 1 - sh, 1), :]
        act = c * _sigmoid(c)

        @pl.when(sec == 2)
        def _():
            o_ref[...] = act

        @pl.when(sec < 2)
        def _():
            scale = jnp.where(sec == 0, HEAD_DIM ** -0.5, 1.0)
            for h in range(n_heads):
                cols = slice(h * HEAD_DIM, (h + 1) * HEAD_DIM)
                ah = act[:, cols]
                o_ref[:, cols] = ah * lax.rsqrt(jnp.sum(ah * ah, axis=-1, keepdims=True) + NORM_EPS) * scale

    return pl.pallas_call(
        body, name=name, grid=(s // ts, 3),
        in_specs=[pl.BlockSpec((ts, dl), lambda i, sec: (i, sec)),
                  pl.BlockSpec((CONV_HALO, dl), lambda i, sec: (jnp.maximum(i * (ts // CONV_HALO) - 1, 0), sec)),
                  pl.BlockSpec((width, dl), lambda i, sec: (0, sec))],
        out_specs=pl.BlockSpec((None, ts, dl), lambda i, sec: (sec, i, 0)),
        out_shape=jax.ShapeDtypeStruct((3, s, dl), F32),
        compiler_params=_params("parallel", "parallel"))(proj, proj, conv_w)


def _qkv_bwd(dqkv, proj, conv_w, n_heads, *, name):
    s = proj.shape[0]
    dl = n_heads * HEAD_DIM
    width = conv_w.shape[0]
    ts = _tile(s, 256, 16)
    n_tiles = s // ts
    per = ts // CONV_HALO
    rows = ts + CONV_HALO

    def body(d_ref, dnext_ref, x_ref, xprev_ref, xnext_ref, w_ref, dx_ref, dw_ref):
        sec, i = pl.program_id(0), pl.program_id(1)
        xv = x_ref[...]
        cat = jnp.concatenate([jnp.where(i > 0, xprev_ref[...], 0.0), xv, xnext_ref[...]], axis=0)
        shifted = [cat[CONV_HALO:]] + [_rows_before(cat, sh, CONV_HALO) for sh in range(1, width)]
        c = shifted[0] * w_ref[pl.ds(width - 1, 1), :]
        for sh in range(1, width):
            c = c + shifted[sh] * w_ref[pl.ds(width - 1 - sh, 1), :]
        sig = _sigmoid(c)
        act = c * sig
        dout = jnp.concatenate([d_ref[...], dnext_ref[...]], axis=0)
        scale = jnp.where(sec == 0, HEAD_DIM ** -0.5, 1.0)
        is_v = sec == 2
        pieces = []
        for h in range(n_heads):
            cols = slice(h * HEAD_DIM, (h + 1) * HEAD_DIM)
            ah, dh = act[:, cols], dout[:, cols]
            nrm = lax.rsqrt(jnp.sum(ah * ah, axis=-1, keepdims=True) + NORM_EPS)
            dnormed = scale * nrm * (dh - ah * (nrm * nrm) * jnp.sum(dh * ah, axis=-1, keepdims=True))
            pieces.append(jnp.where(is_v, dh, dnormed))
        dact = jnp.concatenate(pieces, axis=1)
        dc = dact * sig * (1.0 + c * (1.0 - sig))
        live = jnp.logical_or(lax.broadcasted_iota(jnp.int32, (rows, 1), 0) < ts, i < n_tiles - 1)
        dc = jnp.where(live, dc, 0.0)
        dx = dc[:ts] * w_ref[pl.ds(width - 1, 1), :]
        for sh in range(1, width):
            dx = dx + _rows_after(dc, sh, ts) * w_ref[pl.ds(width - 1 - sh, 1), :]
        dx_ref[...] = dx.astype(BF16)
        dw_rows = [jnp.sum(dc[:ts] * shifted[width - 1 - j][:ts], axis=0, keepdims=True) for j in range(width)]
        _accumulate(dw_ref, jnp.concatenate(dw_rows, axis=0), i == 0)

    return pl.pallas_call(
        body, name=name, grid=(3, n_tiles),
        in_specs=[pl.BlockSpec((None, ts, dl), lambda sec, i: (sec, i, 0)),
                  pl.BlockSpec((None, CONV_HALO, dl), lambda sec, i: (sec, jnp.minimum((i + 1) * per, s // CONV_HALO - 1), 0)),
                  pl.BlockSpec((ts, dl), lambda sec, i: (i, sec)),
                  pl.BlockSpec((CONV_HALO, dl), lambda sec, i: (jnp.maximum(i * per - 1, 0), sec)),
                  pl.BlockSpec((CONV_HALO, dl), lambda sec, i: (jnp.minimum((i + 1) * per, s // CONV_HALO - 1), sec)),
                  pl.BlockSpec((width, dl), lambda sec, i: (0, sec))],
        out_specs=[pl.BlockSpec((ts, dl), lambda sec, i: (i, sec)), pl.BlockSpec((width, dl), lambda sec, i: (0, sec))],
        out_shape=[jax.ShapeDtypeStruct((s, 3 * dl), BF16), jax.ShapeDtypeStruct((width, 3 * dl), F32)],
        compiler_params=_params("parallel", "arbitrary"))(dqkv, dqkv, proj, proj, proj, conv_w)


def _ba_fwd(pba, alog_row, dtb_row, n_heads, *, name):
    s = pba.shape[0]
    ts = _tile(s, 1024, 8)

    def body(x_ref, alog_ref, dtb_ref, o_ref):
        xv = x_ref[...]
        lane = lax.broadcasted_iota(jnp.int32, xv.shape, 1)
        g = -jnp.exp(alog_ref[...]) * _softplus(xv + dtb_ref[...])
        o_ref[...] = jnp.where(lane < n_heads, _sigmoid(xv), jnp.where(lane < 2 * n_heads, g, 0.0))

    row = pl.BlockSpec((ts, LANES), lambda i: (i, 0))
    vec = pl.BlockSpec((1, LANES), lambda i: (0, 0))
    return pl.pallas_call(
        body, name=name, grid=(s // ts,), in_specs=[row, vec, vec], out_specs=row,
        out_shape=jax.ShapeDtypeStruct((s, LANES), F32), compiler_params=_params("parallel"))(pba, alog_row, dtb_row)


def _ba_bwd(dbg, pba, alog_row, dtb_row, n_heads, *, name):
    s = pba.shape[0]
    ts = _tile(s, 1024, 16)

    def body(d_ref, x_ref, alog_ref, dtb_ref, dx_ref, dalog_ref, ddtb_ref):
        first = pl.program_id(0) == 0
        xv, dv = x_ref[...], d_ref[...]
        lane = lax.broadcasted_iota(jnp.int32, xv.shape, 1)
        beta = _sigmoid(xv)
        neg_a = -jnp.exp(alog_ref[...])
        xa = xv + dtb_ref[...]
        is_a = jnp.logical_and(lane >= n_heads, lane < 2 * n_heads)
        d_xa = jnp.where(is_a, dv * neg_a * _sigmoid(xa), 0.0)
        d_g_times_g = jnp.where(is_a, dv * neg_a * _softplus(xa), 0.0)
        dx_ref[...] = jnp.where(lane < n_heads, dv * beta * (1.0 - beta), d_xa).astype(BF16)
        _accumulate(dalog_ref, jnp.sum(d_g_times_g, axis=0, keepdims=True), first)
        _accumulate(ddtb_ref, jnp.sum(d_xa, axis=0, keepdims=True), first)

    row = pl.BlockSpec((ts, LANES), lambda i: (i, 0))
    vec = pl.BlockSpec((1, LANES), lambda i: (0, 0))
    return pl.pallas_call(
        body, name=name, grid=(s // ts,), in_specs=[row, row, vec, vec], out_specs=[row, vec, vec],
        out_shape=[jax.ShapeDtypeStruct((s, LANES), BF16), jax.ShapeDtypeStruct((1, LANES), F32),
                   jax.ShapeDtypeStruct((1, LANES), F32)],
        compiler_params=_params("arbitrary"))(dbg, pba, alog_row, dtb_row)


def _bdot(a, b, ta=False, tb=False, precision=None):
    dims = (((1 if ta else 2,), (2 if tb else 1,)), ((0,), (0,)))
    return lax.dot_general(a, b, dims, precision=precision, preferred_element_type=F32)


def _bdot_hi(a, b, ta=False, tb=False):
    return _bdot(a, b, ta, tb, precision=lax.Precision.HIGHEST)


def _chunk_masks():
    ri = lax.broadcasted_iota(jnp.int32, (GDN_CHUNK, GDN_CHUNK), 0)
    ci = lax.broadcasted_iota(jnp.int32, (GDN_CHUNK, GDN_CHUNK), 1)
    return ri == ci, ri >= ci, ri > ci, ri <= ci


def _row_to_col(row, eye):
    return jnp.sum(jnp.where(eye, row, 0.0), axis=2, keepdims=True)


def _col_to_row(col, eye):
    return jnp.sum(jnp.where(eye, col, 0.0), axis=1, keepdims=True)


_Gates = collections.namedtuple("_Gates", "beta_col decay e_col f_col dec")


def _gdn_gates(beta_row, g_row):
    eye, tril, _, triu = _chunk_masks()
    g_col = _row_to_col(g_row, eye)
    gc_col = jnp.sum(jnp.where(tril, g_row, 0.0), axis=2, keepdims=True)
    gc_row = jnp.sum(jnp.where(triu, g_col, 0.0), axis=1, keepdims=True)
    g_last = jnp.sum(g_row, axis=2, keepdims=True)
    decay = jnp.exp(jnp.where(tril, gc_col - gc_row, -jnp.inf))
    return _Gates(_row_to_col(beta_row, eye), decay, jnp.exp(gc_col), jnp.exp(g_last - gc_col), jnp.exp(g_last))


def _unit_lower_inverse(lmat):
    eye = _chunk_masks()[0]
    t = jnp.where(eye, 1.0, 0.0) - lmat
    p = _bdot_hi(lmat, lmat)
    doublings = GDN_CHUNK.bit_length() - 2
    for r in range(doublings):
        t = t + _bdot_hi(t, p)
        if r < doublings - 1:
            p = _bdot_hi(p, p)
    return t


def _gdn_solve(q, k, v, gates):
    strict = _chunk_masks()[2]
    kb = k * gates.beta_col
    lmat = jnp.where(strict, _bdot(_mx(kb), _mx(k), tb=True) * gates.decay, 0.0)
    tmat = _unit_lower_inverse(lmat)
    sol = _bdot_hi(tmat, jnp.concatenate([v * gates.beta_col, kb * gates.e_col], axis=2))
    at = _bdot(_mx(q), _mx(k), tb=True) * gates.decay
    return lmat, tmat, sol, at


def _gdn_blocking(s):
    n_chunks = s // GDN_CHUNK
    per_step = 16 if n_chunks % 16 == 0 else n_chunks
    assert per_step % GDN_BATCH == 0
    return n_chunks, per_step, n_chunks // per_step


def _load_chunks(ref, sec, n0, hp):
    r0 = pl.multiple_of(n0 * GDN_CHUNK, GDN_BATCH * GDN_CHUNK)
    rows = pl.ds(r0, GDN_BATCH * GDN_CHUNK)
    cols = slice(hp * HEAD_DIM, (hp + 1) * HEAD_DIM)
    val = ref[rows, cols] if sec is None else ref[sec, rows, cols]
    return val.reshape(GDN_BATCH, GDN_CHUNK, HEAD_DIM)


def _store_chunks(ref, sec, n0, hp, val):
    r0 = pl.multiple_of(n0 * GDN_CHUNK, GDN_BATCH * GDN_CHUNK)
    rows = pl.ds(r0, GDN_BATCH * GDN_CHUNK)
    cols = slice(hp * HEAD_DIM, (hp + 1) * HEAD_DIM)
    flat = val.reshape(GDN_BATCH * GDN_CHUNK, HEAD_DIM)
    if sec is None:
        ref[rows, cols] = flat
    else:
        ref[sec, rows, cols] = flat


def _gdn_specs(n_heads, n_steps, per_step, order):
    rows, width = per_step * GDN_CHUNK, GDN_HEADS * HEAD_DIM
    rowvec = pl.BlockSpec((GDN_HEADS, per_step, 1, GDN_CHUNK), lambda h, j: (h, order(j), 0, 0))
    qkv = pl.BlockSpec((3, rows, width), lambda h, j: (0, order(j), h))
    act = pl.BlockSpec((rows, width), lambda h, j: (order(j), h))
    states = pl.BlockSpec((GDN_HEADS, per_step, HEAD_DIM, HEAD_DIM), lambda h, j: (h, order(j), 0, 0))
    return rowvec, qkv, act, states


def _gdn_fwd(qkv, beta_rows, g_rows, *, name):
    _, s, dl = qkv.shape
    n_heads = dl // HEAD_DIM
    c = GDN_CHUNK
    n_chunks, per_step, n_steps = _gdn_blocking(s)
    n_groups = per_step // GDN_BATCH
    heads = range(GDN_HEADS)

    def body(qkv_ref, b_ref, g_ref, o_ref, st_ref, state_ref, sol_s, at_s, qd_s, ke_s, vn_s, dec_s):
        @pl.when(pl.program_id(1) == 0)
        def _():
            state_ref[...] = jnp.zeros_like(state_ref)

        def solve(gi, carry):
            n0 = gi * GDN_BATCH
            grp = pl.ds(n0, GDN_BATCH)
            for hp in heads:
                q, k, v = (_load_chunks(qkv_ref, j, n0, hp) for j in range(3))
                gates = _gdn_gates(b_ref[hp, grp], g_ref[hp, grp])
                _, _, sol, at = _gdn_solve(q, k, v, gates)
                sol_s[hp, grp] = sol
                at_s[hp, grp] = at
                qd_s[hp, grp] = q * gates.e_col
                ke_s[hp, grp] = k * gates.f_col
                dec_s[hp, grp] = jnp.broadcast_to(gates.dec, (GDN_BATCH, 1, LANES))
            return carry

        lax.fori_loop(0, n_groups, solve, 0)

        def recur(n, states):
            out = []
            for hp in heads:
                state = states[hp]
                st_ref[hp, n] = state
                sol = sol_s[hp, n]
                v_new = sol[:, :HEAD_DIM] - _dot(_mx(sol[:, HEAD_DIM:]), _mx(state))
                vn_s[hp, n] = v_new
                out.append(state * dec_s[hp, n] + _dot(_mx(ke_s[hp, n]), _mx(v_new), ta=True))
            return tuple(out)

        final = lax.fori_loop(0, per_step, recur, tuple(state_ref[hp] for hp in heads))
        for hp in heads:
            state_ref[hp] = final[hp]

        def emit(gi, carry):
            n0 = gi * GDN_BATCH
            grp = pl.ds(n0, GDN_BATCH)
            for hp in heads:
                o = (_bdot(_mx(qd_s[hp, grp]), _mx(st_ref[hp, grp]))
                     + _bdot(_mx(at_s[hp, grp]), _mx(vn_s[hp, grp])))
                _store_chunks(o_ref, None, n0, hp, o)
            return carry

        lax.fori_loop(0, n_groups, emit, 0)

    rowvec, qkv_spec, act_spec, st_spec = _gdn_specs(n_heads, n_steps, per_step, lambda j: j)
    wide = lambda w: pltpu.VMEM((GDN_HEADS, per_step, c, w), F32)
    return pl.pallas_call(
        body, name=name, grid=(n_heads // GDN_HEADS, n_steps),
        in_specs=[qkv_spec, rowvec, rowvec], out_specs=[act_spec, st_spec],
        out_shape=[jax.ShapeDtypeStruct((s, dl), F32),
                   jax.ShapeDtypeStruct((n_heads, n_chunks, HEAD_DIM, HEAD_DIM), F32)],
        scratch_shapes=[pltpu.VMEM((GDN_HEADS, HEAD_DIM, HEAD_DIM), F32), wide(2 * HEAD_DIM), wide(c), wide(HEAD_DIM),
                        wide(HEAD_DIM), wide(HEAD_DIM), pltpu.VMEM((GDN_HEADS, per_step, 1, LANES), F32)],
        compiler_params=_params("parallel", "arbitrary"))(qkv, beta_rows, g_rows)


def _gdn_bwd(do, qkv, beta_rows, g_rows, states, *, name):
    _, s, dl = qkv.shape
    n_heads = dl // HEAD_DIM
    c = GDN_CHUNK
    n_chunks, per_step, n_steps = _gdn_blocking(s)
    n_groups = per_step // GDN_BATCH
    heads = range(GDN_HEADS)

    def body(do_ref, qkv_ref, b_ref, g_ref, st_ref, dqkv_ref, db_ref, dg_ref,
             dstate_ref, lmat_s, tmat_s, at_s, dat_s, sol_s, vn_s, dvn_s, dqd_s, ke_s, qdo_s, dst_s, dec_s):
        @pl.when(pl.program_id(1) == 0)
        def _():
            dstate_ref[...] = jnp.zeros_like(dstate_ref)

        eye, tril, strict, _ = _chunk_masks()

        def solve(gi, carry):
            n0 = gi * GDN_BATCH
            grp = pl.ds(n0, GDN_BATCH)
            for hp in heads:
                q, k, v = (_load_chunks(qkv_ref, j, n0, hp) for j in range(3))
                gates = _gdn_gates(b_ref[hp, grp], g_ref[hp, grp])
                lmat, tmat, sol, at = _gdn_solve(q, k, v, gates)
                mstate = _mx(st_ref[hp, grp])
                md_o = _mx(_load_chunks(do_ref, None, n0, hp))
                v_new = sol[:, :, :HEAD_DIM] - _bdot(_mx(sol[:, :, HEAD_DIM:]), mstate)
                lmat_s[hp, grp] = lmat
                tmat_s[hp, grp] = tmat
                sol_s[hp, grp] = sol
                at_s[hp, grp] = at
                vn_s[hp, grp] = v_new
                dat_s[hp, grp] = jnp.where(tril, _bdot(md_o, _mx(v_new), tb=True), 0.0)
                dvn_s[hp, grp] = _bdot(_mx(at), md_o, ta=True)
                dqd_s[hp, grp] = _bdot(md_o, mstate, tb=True)
                qdo_s[hp, grp] = _bdot(_mx(q * gates.e_col), md_o, ta=True)
                ke_s[hp, grp] = k * gates.f_col
                dec_s[hp, grp] = jnp.broadcast_to(gates.dec, (GDN_BATCH, 1, LANES))
            return carry

        lax.fori_loop(0, n_groups, solve, 0)

        def recur(idx, dstates):
            n = per_step - 1 - idx
            out = []
            for hp in heads:
                dstate = dstates[hp]
                dst_s[hp, n] = dstate
                dv_new = dvn_s[hp, n] + _dot(_mx(ke_s[hp, n]), _mx(dstate))
                dvn_s[hp, n] = dv_new
                wc = sol_s[hp, n][:, HEAD_DIM:]
                out.append(dstate * dec_s[hp, n] + qdo_s[hp, n] - _dot(_mx(wc), _mx(dv_new), ta=True))
            return tuple(out)

        final = lax.fori_loop(0, per_step, recur, tuple(dstate_ref[hp] for hp in heads))
        for hp in heads:
            dstate_ref[hp] = final[hp]

        def emit_head(hp, n0):
            grp = pl.ds(n0, GDN_BATCH)
            q, k, v = (_load_chunks(qkv_ref, j, n0, hp) for j in range(3))
            gates = _gdn_gates(b_ref[hp, grp], g_ref[hp, grp])
            state, dstate = st_ref[hp, grp], dst_s[hp, grp]
            lmat, at, dat, sol = lmat_s[hp, grp], at_s[hp, grp], dat_s[hp, grp], sol_s[hp, grp]
            v_new, dv_new, dqd = vn_s[hp, grp], dvn_s[hp, grp], dqd_s[hp, grp]
            dke = _bdot(_mx(v_new), _mx(dstate), tb=True)
            dwc = -_bdot(_mx(dv_new), _mx(state), tb=True)
            ddec = jnp.sum(jnp.sum(dstate * state, axis=2, keepdims=True), axis=1, keepdims=True)
            drhs = _bdot_hi(tmat_s[hp, grp], jnp.concatenate([dv_new, dwc], axis=2), ta=True)
            dvb, dkbe = drhs[:, :, :HEAD_DIM], drhs[:, :, HEAD_DIM:]
            dl_mat = jnp.where(strict, -_bdot(_mx(drhs), _mx(sol), tb=True), 0.0)
            dkk = dl_mat * gates.decay
            dqk = dat * gates.decay
            kb = k * gates.beta_col
            mk = _mx(k)
            dkb = _bdot(_mx(dkk), mk) + dkbe * gates.e_col
            dq = _bdot(_mx(dqk), mk) + dqd * gates.e_col
            dk = (_bdot(_mx(dqk), _mx(q), ta=True) + _bdot(_mx(dkk), _mx(kb), ta=True) + dke * gates.f_col
                  + dkb * gates.beta_col)
            _store_chunks(dqkv_ref, 0, n0, hp, dq)
            _store_chunks(dqkv_ref, 1, n0, hp, dk)
            _store_chunks(dqkv_ref, 2, n0, hp, dvb * gates.beta_col)
            dbeta_col = jnp.sum(dkb * k + dvb * v, axis=2, keepdims=True)
            through_decay = dl_mat * lmat + dat * at
            dke_ke = jnp.sum(dke * (k * gates.f_col), axis=2, keepdims=True)
            dgc_col = (jnp.sum(through_decay, axis=2, keepdims=True)
                       - _row_to_col(jnp.sum(through_decay, axis=1, keepdims=True), eye)
                       + jnp.sum(dqd * (q * gates.e_col) + dkbe * (kb * gates.e_col), axis=2, keepdims=True) - dke_ke)
            dg_last = jnp.sum(dke_ke, axis=1, keepdims=True) + ddec * gates.dec
            db_ref[hp, grp] = _col_to_row(dbeta_col, eye)
            dg_ref[hp, grp] = jnp.sum(jnp.where(tril, dgc_col, 0.0), axis=1, keepdims=True) + dg_last

        def emit(gi, carry):
            for hp in heads:
                emit_head(hp, gi * GDN_BATCH)
            return carry

        lax.fori_loop(0, n_groups, emit, 0)

    rowvec, qkv_spec, act_spec, st_spec = _gdn_specs(n_heads, n_steps, per_step, lambda j: n_steps - 1 - j)
    wide = lambda w: pltpu.VMEM((GDN_HEADS, per_step, c, w), F32)
    square = pltpu.VMEM((GDN_HEADS, per_step, HEAD_DIM, HEAD_DIM), F32)
    return pl.pallas_call(
        body, name=name, grid=(n_heads // GDN_HEADS, n_steps),
        in_specs=[act_spec, qkv_spec, rowvec, rowvec, st_spec], out_specs=[qkv_spec, rowvec, rowvec],
        out_shape=[jax.ShapeDtypeStruct((3, s, dl), F32),
                   jax.ShapeDtypeStruct((n_heads, n_chunks, 1, c), F32),
                   jax.ShapeDtypeStruct((n_heads, n_chunks, 1, c), F32)],
        scratch_shapes=[pltpu.VMEM((GDN_HEADS, HEAD_DIM, HEAD_DIM), F32), wide(c), wide(c), wide(c), wide(c),
                        wide(2 * HEAD_DIM), wide(HEAD_DIM), wide(HEAD_DIM), wide(HEAD_DIM), wide(HEAD_DIM),
                        square, square, pltpu.VMEM((GDN_HEADS, per_step, 1, LANES), F32)],
        compiler_params=_params("parallel", "arbitrary"))(do, qkv, beta_rows, g_rows, states)


def _pool_counts(tile, ts, extra, win):
    t = tile * ts + lax.broadcasted_iota(jnp.int32, (ts + extra, 1), 0)
    return jnp.minimum(t + 1, win).astype(F32)


def _pooled(cat, p_cols, tile, ts, win):
    acc, span = cat, 1
    while span < win:
        acc = acc + pltpu.roll(acc, span, 0)
        span *= 2
    return acc[POOL_HALO:] / _pool_counts(tile, ts, 0, win) - p_cols


def _merge_fwd(proj, o, gnw, pool_w, pool_scale, *, name):
    s, d = o.shape
    n_heads = d // HEAD_DIM
    n_groups, pg = pool_w.shape[0], pool_w.shape[1]
    assert n_groups == len(POOL_WINDOWS) and n_groups * pg == d and pg % HEAD_DIM == 0
    heads_per_group = pg // HEAD_DIM
    ts = _tile(s, 256, 16)

    def body(o_ref, z_ref, p_ref, halo_ref, ga_ref, gb_ref, gnw_ref, pw_ref, ps_ref, out_ref):
        i = pl.program_id(0)
        gnw_v = gnw_ref[...]
        halo = jnp.where(i > 0, halo_ref[...], 0.0)
        for gi, win in enumerate(POOL_WINDOWS):
            gcols = slice(gi * pg, (gi + 1) * pg)
            pv = p_ref[:, gcols]
            pooled = _pooled(jnp.concatenate([halo[:, gcols], pv], axis=0), pv, i, ts, win)
            yb = _dot(_mx(pooled), pw_ref[gi]) * ps_ref[:, gcols]
            for h in range(gi * heads_per_group, (gi + 1) * heads_per_group):
                cols = slice(h * HEAD_DIM, (h + 1) * HEAD_DIM)
                in_group = slice(h * HEAD_DIM - gi * pg, (h + 1) * HEAD_DIM - gi * pg)
                oh, zh = o_ref[:, cols], z_ref[:, cols]
                r = lax.rsqrt(jnp.mean(oh * oh, axis=-1, keepdims=True) + NORM_EPS)
                ya = oh * r * gnw_v * (zh * _sigmoid(zh))
                out_ref[:, cols] = (_sigmoid(ga_ref[:, cols]) * ya
                                    + _sigmoid(gb_ref[:, cols]) * yb[:, in_group]).astype(BF16)

    blk = lambda col: pl.BlockSpec((ts, d), lambda i, col=col: (i, col))
    vec = lambda width: pl.BlockSpec((1, width), lambda i: (0, 0))
    return pl.pallas_call(
        body, name=name, grid=(s // ts,),
        in_specs=[blk(0), blk(3), blk(4),
                  pl.BlockSpec((POOL_HALO, d), lambda i: (jnp.maximum(i * (ts // POOL_HALO) - 1, 0), 4)),
                  blk(5), blk(6), vec(HEAD_DIM), pl.BlockSpec((n_groups, pg, pg), lambda i: (0, 0, 0)), vec(d)],
        out_specs=blk(0), out_shape=jax.ShapeDtypeStruct((s, d), BF16),
        compiler_params=_params("parallel"))(o, proj, proj, proj, proj, proj, gnw.reshape(1, HEAD_DIM), pool_w,
                                              pool_scale.reshape(1, d))


def _merge_bwd(dmixed, proj, o, gnw, pool_w, pool_scale, *, name):
    s, d = o.shape
    n_heads = d // HEAD_DIM
    n_groups, pg = pool_w.shape[0], pool_w.shape[1]
    ts = _tile(s, 256, 16)

    def body(dm_ref, o_ref, z_ref, p_ref, halo_ref, ga_ref, gb_ref, gnw_ref, pw_ref, ps_ref,
             do_ref, dz_ref, dga_ref, dgb_ref, dpl_ref, dgnw_ref, dpw_ref, dps_ref):
        i = pl.program_id(0)
        first = i == 0
        gnw_v = gnw_ref[...]
        dgnw = jnp.zeros((1, HEAD_DIM), F32)
        for h in range(n_heads):
            cols = slice(h * HEAD_DIM, (h + 1) * HEAD_DIM)
            oh, zh, dm = o_ref[:, cols], z_ref[:, cols], dm_ref[:, cols]
            r = lax.rsqrt(jnp.mean(oh * oh, axis=-1, keepdims=True) + NORM_EPS)
            xh = oh * r
            sz = _sigmoid(zh)
            silu_z = zh * sz
            sa = _sigmoid(ga_ref[:, cols])
            on = xh * gnw_v
            dya = dm * sa
            dga_ref[:, cols] = (dm * on * silu_z * sa * (1.0 - sa)).astype(BF16)
            dz_ref[:, cols] = (dya * on * sz * (1.0 + zh * (1.0 - sz))).astype(BF16)
            don = dya * silu_z
            dgnw = dgnw + jnp.sum(don * xh, axis=0, keepdims=True)
            dxh = don * gnw_v
            do_ref[:, cols] = r * (dxh - xh * jnp.mean(dxh * xh, axis=-1, keepdims=True))
        _accumulate(dgnw_ref, dgnw, first)
        halo = jnp.where(first, 0.0, halo_ref[...])
        for gi, win in enumerate(POOL_WINDOWS):
            cols = slice(gi * pg, (gi + 1) * pg)
            pv, dm = p_ref[:, cols], dm_ref[:, cols]
            pooled = _pooled(jnp.concatenate([halo[:, cols], pv], axis=0), pv, i, ts, win)
            lin = _dot(_mx(pooled), pw_ref[gi])
            psv = ps_ref[:, cols]
            sb = _sigmoid(gb_ref[:, cols])
            dgb_ref[:, cols] = (dm * lin * psv * sb * (1.0 - sb)).astype(BF16)
            dyb = dm * sb
            _accumulate(dps_ref.at[:, cols], jnp.sum(dyb * lin, axis=0, keepdims=True), first)
            dlin = _mx(dyb * psv)
            _accumulate(dpw_ref.at[gi], _dot(_mx(pooled), dlin, ta=True), first)
            dpl_ref[:, cols] = _dot(dlin, pw_ref[gi], tb=True)

    blk = lambda col: pl.BlockSpec((ts, d), lambda i, col=col: (i, col))
    vec = lambda width: pl.BlockSpec((1, width), lambda i: (0, 0))
    pw_spec = pl.BlockSpec((n_groups, pg, pg), lambda i: (0, 0, 0))
    return pl.pallas_call(
        body, name=name, grid=(s // ts,),
        in_specs=[blk(0), blk(0), blk(3), blk(4),
                  pl.BlockSpec((POOL_HALO, d), lambda i: (jnp.maximum(i * (ts // POOL_HALO) - 1, 0), 4)),
                  blk(5), blk(6), vec(HEAD_DIM), pw_spec, vec(d)],
        out_specs=[blk(0), blk(0), blk(0), blk(0), blk(0), vec(HEAD_DIM), pw_spec, vec(d)],
        out_shape=[jax.ShapeDtypeStruct((s, d), F32), jax.ShapeDtypeStruct((s, d), BF16),
                   jax.ShapeDtypeStruct((s, d), BF16), jax.ShapeDtypeStruct((s, d), BF16),
                   jax.ShapeDtypeStruct((s, d), F32), jax.ShapeDtypeStruct((1, HEAD_DIM), F32),
                   jax.ShapeDtypeStruct((n_groups, pg, pg), F32), jax.ShapeDtypeStruct((1, d), F32)],
        compiler_params=_params("arbitrary"))(dmixed, o, proj, proj, proj, proj, proj, gnw.reshape(1, HEAD_DIM),
                                               pool_w, pool_scale.reshape(1, d))


def _pool_bwd(dpooled, *, name):
    s, d = dpooled.shape
    pg = d // len(POOL_WINDOWS)
    ts = _tile(s, 512, 16)
    n_tiles = s // ts
    per = ts // POOL_HALO

    def body(d_ref, next_ref, out_ref):
        i = pl.program_id(0)
        nxt = jnp.where(i < n_tiles - 1, next_ref[...], 0.0)
        for gi, win in enumerate(POOL_WINDOWS):
            cols = slice(gi * pg, (gi + 1) * pg)
            dv = d_ref[:, cols]
            acc = jnp.concatenate([dv, nxt[:, cols]], axis=0) / _pool_counts(i, ts, POOL_HALO, win)
            span = 1
            while span < win:
                acc = acc + pltpu.roll(acc, acc.shape[0] - span, 0)
                span *= 2
            out_ref[:, cols] = (acc[:ts] - dv).astype(BF16)

    return pl.pallas_call(
        body, name=name, grid=(n_tiles,),
        in_specs=[pl.BlockSpec((ts, d), lambda i: (i, 0)),
                  pl.BlockSpec((POOL_HALO, d), lambda i: (jnp.minimum((i + 1) * per, s // POOL_HALO - 1), 0))],
        out_specs=pl.BlockSpec((ts, d), lambda i: (i, 0)), out_shape=jax.ShapeDtypeStruct((s, d), BF16),
        compiler_params=_params("parallel"))(dpooled, dpooled)


def _ffn_tiles(s, f):
    tf = _tile(f, 1408)
    return _tile(s, 512, 16), tf, f // tf


def _ffn_act_fwd(gu, conv_w, conv_b, *, name):
    s, f = gu.shape[0], gu.shape[1] // 2
    width = conv_w.shape[0]
    ts, tf, nf = _ffn_tiles(s, f)

    def body(g_ref, halo_ref, u_ref, w_ref, b_ref, act_ref, gc_ref):
        i = pl.program_id(0)
        gv = g_ref[...]
        cat = jnp.concatenate([jnp.where(i > 0, halo_ref[...], 0.0), gv], axis=0)
        gc = gv * w_ref[pl.ds(width - 1, 1), :] + b_ref[...]
        for sh in range(1, width):
            gc = gc + _rows_before(cat, sh, CONV_HALO) * w_ref[pl.ds(width - 1 - sh, 1), :]
        gc_ref[...] = gc
        act_ref[...] = (_gelu(gc) * u_ref[...]).astype(BF16)

    blk = pl.BlockSpec((ts, tf), lambda i, j: (i, j))
    return pl.pallas_call(
        body, name=name, grid=(s // ts, nf),
        in_specs=[blk, pl.BlockSpec((CONV_HALO, tf), lambda i, j: (jnp.maximum(i * (ts // CONV_HALO) - 1, 0), j)),
                  pl.BlockSpec((ts, tf), lambda i, j: (i, nf + j)),
                  pl.BlockSpec((width, tf), lambda i, j: (0, j)), pl.BlockSpec((1, tf), lambda i, j: (0, j))],
        out_specs=[blk, blk],
        out_shape=[jax.ShapeDtypeStruct((s, f), BF16), jax.ShapeDtypeStruct((s, f), F32)],
        compiler_params=_params("parallel", "parallel"))(gu, gu, gu, conv_w, conv_b.reshape(1, f))


def _ffn_act_bwd(dact, gu, gc, conv_w, *, name):
    s, f = gc.shape
    width = conv_w.shape[0]
    ts, tf, nf = _ffn_tiles(s, f)
    n_tiles = s // ts
    per = ts // CONV_HALO
    rows = ts + CONV_HALO

    def body(da_ref, da_next, gc_ref, gc_next, u_ref, u_next, g_ref, g_prev, w_ref, dg_ref, du_ref, dw_ref, db_ref):
        i = pl.program_id(1)
        first = i == 0
        da = jnp.concatenate([da_ref[...], da_next[...]], axis=0)
        gcv = jnp.concatenate([gc_ref[...], gc_next[...]], axis=0)
        uv = jnp.concatenate([u_ref[...], u_next[...]], axis=0)
        du_ref[...] = (da[:ts] * _gelu(gcv[:ts])).astype(BF16)
        live = jnp.logical_or(lax.broadcasted_iota(jnp.int32, (rows, 1), 0) < ts, i < n_tiles - 1)
        dgc = jnp.where(live, da * uv * _gelu_grad(gcv), 0.0)
        dgate = dgc[:ts] * w_ref[pl.ds(width - 1, 1), :]
        for sh in range(1, width):
            dgate = dgate + _rows_after(dgc, sh, ts) * w_ref[pl.ds(width - 1 - sh, 1), :]
        dg_ref[...] = dgate.astype(BF16)
        gv = g_ref[...]
        cat = jnp.concatenate([jnp.where(first, 0.0, g_prev[...]), gv], axis=0)
        shifted = [gv] + [_rows_before(cat, sh, CONV_HALO) for sh in range(1, width)]
        dw_rows = [jnp.sum(dgc[:ts] * shifted[width - 1 - j], axis=0, keepdims=True) for j in range(width)]
        _accumulate(dw_ref, jnp.concatenate(dw_rows, axis=0), first)
        _accumulate(db_ref, jnp.sum(dgc[:ts], axis=0, keepdims=True), first)

    nxt_row = lambda i: jnp.minimum((i + 1) * per, s // CONV_HALO - 1)
    main = lambda off: pl.BlockSpec((ts, tf), lambda j, i, off=off: (i, off + j))
    nxt = lambda off: pl.BlockSpec((CONV_HALO, tf), lambda j, i, off=off: (nxt_row(i), off + j))
    return pl.pallas_call(
        body, name=name, grid=(nf, n_tiles),
        in_specs=[main(0), nxt(0), main(0), nxt(0), main(nf), nxt(nf), main(0),
                  pl.BlockSpec((CONV_HALO, tf), lambda j, i: (jnp.maximum(i * per - 1, 0), j)),
                  pl.BlockSpec((width, tf), lambda j, i: (0, j))],
        out_specs=[main(0), main(0), pl.BlockSpec((width, tf), lambda j, i: (0, j)),
                   pl.BlockSpec((1, tf), lambda j, i: (0, j))],
        out_shape=[jax.ShapeDtypeStruct((s, f), BF16), jax.ShapeDtypeStruct((s, f), BF16),
                   jax.ShapeDtypeStruct((width, f), F32), jax.ShapeDtypeStruct((1, f), F32)],
        compiler_params=_params("parallel", "arbitrary"))(dact, dact, gc, gc, gu, gu, gu, gu, conv_w)


def _rows_layout(bg, n_heads):
    s = bg.shape[0]
    shape = (n_heads, s // GDN_CHUNK, 1, GDN_CHUNK)
    return bg[:, :n_heads].T.reshape(shape), bg[:, n_heads:2 * n_heads].T.reshape(shape)


def _lane_layout(dbeta_rows, dg_rows):
    n_heads = dbeta_rows.shape[0]
    s = dbeta_rows.shape[1] * GDN_CHUNK
    both = jnp.concatenate([dbeta_rows.reshape(n_heads, s), dg_rows.reshape(n_heads, s)], axis=0).T
    return jnp.pad(both, ((0, 0), (0, LANES - 2 * n_heads)))


def _layer_fwd(x, w):
    n_heads = w["n_heads"]
    h = _rmsnorm_fwd(x, w["norm_mix_w"], name="norm_mix_fwd")
    proj = _matmul(h, w["w_main"], name="in_proj_fwd")
    pba = _matmul(h, w["w_ba"], name="ba_proj_fwd")
    qkv = _qkv_fwd(proj, w["conv_qkv_w"], n_heads, name="qkv_fwd")
    bg = _ba_fwd(pba, w["alog_row"], w["dtb_row"], n_heads, name="ba_fwd")
    beta_rows, g_rows = _rows_layout(bg, n_heads)
    o, states = _gdn_fwd(qkv, beta_rows, g_rows, name="gdn_fwd")
    mixed = _merge_fwd(proj, o, w["gdn_norm_w"], w["pool_w"], w["pool_scale"], name="merge_fwd")
    x2 = _matmul(mixed, w["w_out"], add=x, name="out_proj_fwd")
    h2 = _rmsnorm_fwd(x2, w["norm_ffn_w"], name="norm_ffn_fwd")
    gu = _matmul(h2, w["w_up_slots"], b_slots=True, name="up_proj_fwd")
    act, gc = _ffn_act_fwd(gu, w["conv_ffn_w"], w["conv_ffn_b"], name="ffn_act_fwd")
    x3 = _matmul(act, w["w_down"], add=x2, tk=1408, name="down_proj_fwd")
    saved = dict(x=x, h=h, proj=proj, pba=pba, qkv=qkv, beta_rows=beta_rows, g_rows=g_rows, o=o, states=states,
                 mixed=mixed, x2=x2, h2=h2, gu=gu, gc=gc, act=act)
    return x3, saved


def _layer_bwd(dx3, w, sv, after):
    n_heads = w["n_heads"]
    g = {}
    dact = _matmul(dx3, w["w_down"], tb=True, after=after, name="down_proj_dx")
    g["w_down"] = _matmul(sv["act"], dx3, ta=True, out_dtype=BF16, name="down_proj_dw")
    dgate, dup, g["conv_ffn_w"], g["conv_ffn_b"] = _ffn_act_bwd(dact, sv["gu"], sv["gc"], w["conv_ffn_w"],
                                                                 name="ffn_act_bwd")
    dgu = jnp.concatenate([dgate, dup], axis=1)
    dh2 = _matmul(dgu, w["w_up_slots"], b_slots=True, tb=True, name="up_proj_dx")
    g["w_up"] = _matmul(sv["h2"], dgu, ta=True, out_dtype=BF16, name="up_proj_dw")
    dx2, g["norm_ffn_w"] = _rmsnorm_bwd(dh2, sv["x2"], w["norm_ffn_w"], dx3, name="norm_ffn_bwd")
    dmixed = _matmul(dx2, w["w_out"], tb=True, name="out_proj_dx")
    g["w_out"] = _matmul(sv["mixed"], dx2, ta=True, out_dtype=BF16, name="out_proj_dw")
    do, dz, dga, dgb, dpooled, g["gdn_norm_w"], g["pool_w"], g["pool_scale"] = _merge_bwd(
        dmixed, sv["proj"], sv["o"], w["gdn_norm_w"], w["pool_w"], w["pool_scale"], name="merge_bwd")
    dp = _pool_bwd(dpooled, name="pool_bwd")
    dqkv, dbeta_rows, dg_rows = _gdn_bwd(do, sv["qkv"], sv["beta_rows"], sv["g_rows"], sv["states"], name="gdn_bwd")
    dproj_qkv, g["conv_qkv_w"] = _qkv_bwd(dqkv, sv["proj"], w["conv_qkv_w"], n_heads, name="qkv_bwd")
    dpba, g["alog_row"], g["dtb_row"] = _ba_bwd(_lane_layout(dbeta_rows, dg_rows), sv["pba"], w["alog_row"],
                                                w["dtb_row"], n_heads, name="ba_bwd")
    dproj = jnp.concatenate([dproj_qkv, dz, dp, dga, dgb], axis=1)
    dh = _matmul(dproj, w["w_main"], tb=True, name="in_proj_dx")
    dh = _matmul(dpba, w["w_ba"], tb=True, add=dh, name="ba_proj_dx")
    g["w_main"] = _matmul(sv["h"], dproj, ta=True, out_dtype=BF16, name="in_proj_dw")
    g["w_ba"] = _matmul(sv["h"], dpba, ta=True, name="ba_proj_dw")
    dx, g["norm_mix_w"] = _rmsnorm_bwd(dh, sv["x"], w["norm_mix_w"], dx2, name="norm_mix_bwd")
    return dx, g


def _here():
    mx, my, mc = (lax.axis_index(a) for a in MESH_AXES)
    return (mx, my, mc), 4 * mx + 2 * my + mc


def _peer(pos, r):
    mx, my, mc = pos
    px = 1 - mx if r & 4 else mx
    py = 1 - my if r & 2 else my
    pc = 1 - mc if r & 1 else mc
    return (px, py, pc), 4 * px + 2 * py + pc


def _run_exchange(n_tensors, src_view, dst_view, sems):
    send_sems, recv_sems, local_sems = sems
    pos, me = _here()
    started = []
    for t in range(n_tensors):
        cp = pltpu.make_async_copy(src_view(t, me), dst_view(t, me), local_sems.at[t])
        cp.start()
        started.append(cp)

    def remote(t, r, landing):
        target, target_lin = _peer(pos, r)
        return pltpu.make_async_remote_copy(
            src_ref=src_view(t, target_lin), dst_ref=dst_view(t, target_lin if landing else me),
            send_sem=send_sems.at[t, r - 1], recv_sem=recv_sems.at[t, r - 1],
            device_id=target, device_id_type=pl.DeviceIdType.MESH)

    sends = []
    for r in range(1, N_DEV):
        for t in range(n_tensors):
            cp = remote(t, r, landing=False)
            cp.start()
            sends.append(cp)
    for r in range(1, N_DEV):
        for t in range(n_tensors):
            remote(t, r, landing=True).wait_recv()
    for cp in sends:
        cp.wait_send()
    for cp in started:
        cp.wait()


def _exchange_scratch(n_tensors):
    return [pltpu.SemaphoreType.DMA((n_tensors, N_DEV - 1)), pltpu.SemaphoreType.DMA((n_tensors, N_DEV - 1)),
            pltpu.SemaphoreType.DMA((n_tensors,))]


def _slot_view(ref, axis, index):
    return ref.at[(slice(None),) * axis + (index,)]


def _gather(srcs, slot_axes, *, name):
    n = len(srcs)

    def body(*refs):
        src_refs, out_refs = refs[:n], refs[n:2 * n]
        _run_exchange(n, lambda t, to: src_refs[t], lambda t, frm: _slot_view(out_refs[t], slot_axes[t], frm),
                      refs[2 * n:])

    hbm = pl.BlockSpec(memory_space=pltpu.HBM)
    out_shape = [jax.ShapeDtypeStruct(s.shape[:a] + (N_DEV,) + s.shape[a:], s.dtype) for s, a in zip(srcs, slot_axes)]
    return pl.pallas_call(body, name=name, in_specs=[hbm] * n, out_specs=[hbm] * n, out_shape=out_shape,
                          scratch_shapes=_exchange_scratch(n))(*srcs)


_SIDE_EFFECT = pltpu.SideEffectType.DATAFLOW_SIDE_EFFECTING


def _split_copy(t, r, pos, src_refs, land_refs, send_sems, recv_sems, src_view, dst_view, landing):
    _, me = _here()
    target, target_lin = _peer(pos, r)
    return pltpu.make_async_remote_copy(
        src_ref=src_view(t, src_refs[t], target_lin), dst_ref=dst_view(t, land_refs[t], target_lin if landing else me),
        send_sem=send_sems.at[t * (N_DEV - 1) + r - 1], recv_sem=recv_sems.at[t * (N_DEV - 1) + r - 1],
        device_id=target, device_id_type=pl.DeviceIdType.MESH)


def _start_exchange(srcs, lands, src_view, dst_view, after, *, name):
    n = len(srcs)
    has_after = after is not None

    def body(*refs):
        src_refs, land_refs = refs[:n], refs[n:2 * n]
        outs = refs[2 * n + has_after:]
        send_sems, recv_sems, token = outs[0], outs[1], outs[2 + 2 * n]
        pos, _ = _here()
        for r in range(1, N_DEV):
            for t in range(n):
                _split_copy(t, r, pos, src_refs, land_refs, send_sems, recv_sems, src_view, dst_view, False).start()
        token[...] = jnp.zeros_like(token)

    hbm = pl.BlockSpec(memory_space=pltpu.HBM)
    sem = pl.BlockSpec(memory_space=pltpu.SEMAPHORE)
    sem_shape = pltpu.SemaphoreType.DMA((n * (N_DEV - 1),))
    through = [pltpu.HBM(t.shape, t.dtype) for t in list(srcs) + list(lands)]
    args = [pltpu.with_memory_space_constraint(t, pltpu.HBM) for t in list(srcs) + list(lands)]
    outs = pl.pallas_call(
        body, name=name, in_specs=[hbm] * (2 * n) + ([pl.BlockSpec(memory_space=pl.ANY)] if has_after else []),
        out_specs=(sem, sem, *[hbm] * (2 * n), pl.BlockSpec(memory_space=pltpu.VMEM)),
        out_shape=(sem_shape, sem_shape, *through, jax.ShapeDtypeStruct((8, LANES), F32)),
        input_output_aliases={i: 2 + i for i in range(2 * n)},
        compiler_params=pltpu.CompilerParams(has_side_effects=_SIDE_EFFECT))(*args, *([after] if has_after else []))
    return outs[0], outs[1], list(outs[2:2 + n]), list(outs[2 + n:2 + 2 * n]), outs[-1]


def _wait_exchange(send_sems, recv_sems, srcs, lands, src_view, dst_view, after, *, name):
    n = len(srcs)

    def body(*refs):
        src_refs, land_refs = refs[:n], refs[n:2 * n]
        send_refs, recv_refs = refs[2 * n], refs[2 * n + 1]
        pos, _ = _here()
        for r in range(1, N_DEV):
            for t in range(n):
                cp = _split_copy(t, r, pos, src_refs, land_refs, send_refs, recv_refs, src_view, dst_view, True)
                cp.wait_send()
                cp.wait_recv()

    hbm = pl.BlockSpec(memory_space=pltpu.HBM)
    sem = pl.BlockSpec(memory_space=pltpu.SEMAPHORE)
    outs = pl.pallas_call(
        body, name=name, in_specs=[hbm] * (2 * n) + [sem, sem, pl.BlockSpec(memory_space=pl.ANY)],
        out_specs=[hbm] * (2 * n), out_shape=[pltpu.HBM(t.shape, t.dtype) for t in list(srcs) + list(lands)],
        input_output_aliases={i: i for i in range(2 * n)},
        compiler_params=pltpu.CompilerParams(has_side_effects=_SIDE_EFFECT))(*srcs, *lands, send_sems, recv_sems, after)
    return list(outs[n:])


def _sum_slots(parts, *, name):
    _, n_lead, r_rows, cols = parts.shape
    tr = _tile(r_rows, max(16, (1 << 17) // cols // 16 * 16), 16)

    def body(p_ref, o_ref):
        total = p_ref[0].astype(F32)
        for p in range(1, N_DEV):
            total = total + p_ref[p].astype(F32)
        o_ref[...] = total

    return pl.pallas_call(
        body, name=name, grid=(n_lead, r_rows // tr),
        in_specs=[pl.BlockSpec((N_DEV, None, tr, cols), lambda a, i: (0, a, i, 0))],
        out_specs=pl.BlockSpec((None, tr, cols), lambda a, i: (a, i, 0)),
        out_shape=jax.ShapeDtypeStruct((n_lead, r_rows, cols), F32),
        compiler_params=_params("parallel", "parallel"))(parts)


_WinLayout = collections.namedtuple("_WinLayout", "shard_w n_main ba_dev ba_off n_ba slot_w")


def _win_layout(shard_w, n_main, ba_start, n_ba):
    ba_dev = ba_start // shard_w
    assert (ba_start + n_ba - 1) // shard_w == ba_dev and n_main % LANES == 0
    slot_w = -(-(LANES - 1 + shard_w) // LANES) * LANES
    return _WinLayout(shard_w, n_main, ba_dev, ba_start - ba_dev * shard_w, n_ba, slot_w)


def _main_start(lay, dev):
    return lay.shard_w * dev - jnp.where(dev > lay.ba_dev, lay.n_ba, 0)


def _slab_origin(lay, dev):
    return jnp.minimum(_main_start(lay, dev) // LANES * LANES, lay.n_main - lay.slot_w)


def _assemble_plan(lay):
    plan = [[] for _ in range(lay.n_main // LANES)]
    for dev in range(N_DEV):
        start = lay.shard_w * dev - (lay.n_ba if dev > lay.ba_dev else 0)
        width = lay.shard_w - (lay.n_ba if dev == lay.ba_dev else 0)
        origin = min(start // LANES, (lay.n_main - lay.slot_w) // LANES)
        pad = start - origin * LANES
        for t in range(pad // LANES, (pad + width - 1) // LANES + 1):
            plan[origin + t].append((dev, t))
    return plan


def _assemble_w_main(slabs, lay, *, name):
    _, d, slot_w = slabs.shape
    plan = _assemble_plan(lay)
    runs = []
    shared = []
    for tile, parts in enumerate(plan):
        if len(parts) != 1:
            shared.append((tile, parts))
        elif runs and runs[-1][2] == parts[0][0] and runs[-1][0] + runs[-1][1] == tile:
            runs[-1][1] += 1
        else:
            runs.append([tile, 1, parts[0][0], parts[0][1]])
    tr = _tile(d, 256, 16)

    def body(in_ref, out_ref):
        for first, count, dev, t0 in runs:
            out_ref[:, first * LANES:(first + count) * LANES] = in_ref[dev, :, t0 * LANES:(t0 + count) * LANES]
        for tile, parts in shared:
            total = in_ref[parts[0][0], :, parts[0][1] * LANES:(parts[0][1] + 1) * LANES]
            for dev, t in parts[1:]:
                total = total + in_ref[dev, :, t * LANES:(t + 1) * LANES]
            out_ref[:, tile * LANES:(tile + 1) * LANES] = total

    return pl.pallas_call(
        body, name=name, grid=(d // tr,),
        in_specs=[pl.BlockSpec((N_DEV, tr, slot_w), lambda i: (0, i, 0))],
        out_specs=pl.BlockSpec((tr, lay.n_main), lambda i: (i, 0)),
        out_shape=jax.ShapeDtypeStruct((d, lay.n_main), slabs.dtype), compiler_params=_params("parallel"))(slabs)


def _adam_update(w, g, m, v):
    nm = ADAM_B1 * m + (1.0 - ADAM_B1) * g
    nv = ADAM_B2 * v + (1.0 - ADAM_B2) * (g * g)
    m_hat = nm / (1.0 - ADAM_B1 ** ADAM_STEP)
    v_hat = nv / (1.0 - ADAM_B2 ** ADAM_STEP)
    return -ADAM_LR * (m_hat / (jnp.sqrt(v_hat) + ADAM_EPS) + ADAM_WD * w), nm, nv


def _adamw(w, g, m, v, *, name):
    rows, cols = w.shape
    tr = _tile(rows, max(8, (1 << 18) // cols // 8 * 8), 8)

    def body(w_ref, g_ref, m_ref, v_ref, d_ref, nm_ref, nv_ref):
        d_ref[...], nm_ref[...], nv_ref[...] = _adam_update(w_ref[...], g_ref[...], m_ref[...], v_ref[...])

    blk = pl.BlockSpec((tr, cols), lambda i: (i, 0))
    out = jax.ShapeDtypeStruct((rows, cols), F32)
    return pl.pallas_call(
        body, name=name, grid=(rows // tr,), in_specs=[blk] * 4, out_specs=[blk] * 3, out_shape=[out] * 3,
        compiler_params=_params("parallel"))(w, g, m, v)


def _adamw_nd(w, g, m, v, *, name):
    two_d = (-1, w.shape[-1])
    outs = _adamw(w.reshape(two_d), g.reshape(two_d), m.reshape(two_d), v.reshape(two_d), name=name)
    return tuple(t.reshape(w.shape) for t in outs)


def _adamw_slots(parts, w, m, v, *, name):
    n_layers, rows, cols = w.shape
    tr = _tile(rows, max(16, (1 << 18) // cols // 16 * 16), 16)

    def body(p_ref, w_ref, m_ref, v_ref, g_ref, d_ref, nm_ref, nv_ref):
        total = p_ref[0].astype(F32)
        for p in range(1, N_DEV):
            total = total + p_ref[p].astype(F32)
        g_ref[...] = total
        d_ref[...], nm_ref[...], nv_ref[...] = _adam_update(w_ref[...], total, m_ref[...], v_ref[...])

    blk = pl.BlockSpec((None, tr, cols), lambda a, i: (a, i, 0))
    out = jax.ShapeDtypeStruct((n_layers, rows, cols), F32)
    return pl.pallas_call(
        body, name=name, grid=(n_layers, rows // tr),
        in_specs=[pl.BlockSpec((N_DEV, None, tr, cols), lambda a, i: (0, a, i, 0)), blk, blk, blk],
        out_specs=[blk] * 4, out_shape=[out] * 4, compiler_params=_params("parallel", "parallel"))(parts, w, m, v)


def _pack_rows(parts, dtype, quantum_rows):
    flat = jnp.concatenate([p.reshape(-1).astype(dtype) for p in parts])
    n = flat.shape[0]
    padded = -(-n // (LANES * quantum_rows)) * (LANES * quantum_rows)
    return jnp.pad(flat, (0, padded - n)).reshape(padded // LANES, LANES)


def _unpack(flat, shapes):
    lead = flat.shape[:-1]
    out, at = [], 0
    for shape in shapes:
        size = 1
        for dim in shape:
            size *= dim
        out.append(flat[..., at:at + size].reshape(lead + tuple(shape)))
        at += size
    return out


def _whole_from_slots(slots, axis):
    moved = jnp.moveaxis(slots, 0, axis)
    shape = moved.shape
    return moved.reshape(shape[:axis] + (shape[axis] * shape[axis + 1],) + shape[axis + 2:])


def _lane_row(vec, n_heads):
    return jnp.pad(vec, ((0, 0), (n_heads, LANES - 2 * n_heads)))[:, None, :]


REPLICATED = ("norm_mix_w", "a_log", "dt_bias", "gdn_norm_w", "pool_scale", "norm_ffn_w", "conv_ffn_b",
              "norm_final_w")
WEIGHTS = ("norm_mix_w", "w_in", "conv_qkv_w", "a_log", "dt_bias", "gdn_norm_w", "pool_w", "pool_scale", "w_out",
           "norm_ffn_w", "w_up", "conv_ffn_w", "conv_ffn_b", "w_down", "norm_final_w")
SMALL_QUANTUM_ROWS = 512


def kernel(x, norm_mix_w, w_in, conv_qkv_w, a_log, dt_bias, gdn_norm_w, pool_w, pool_scale, w_out, norm_ffn_w, w_up, conv_ffn_w, conv_ffn_b, w_down, norm_final_w, loss_target, m_norm_mix_w, m_w_in, m_conv_qkv_w, m_a_log, m_dt_bias, m_gdn_norm_w, m_pool_w, m_pool_scale, m_w_out, m_norm_ffn_w, m_w_up, m_conv_ffn_w, m_conv_ffn_b, m_w_down, m_norm_final_w, v_norm_mix_w, v_w_in, v_conv_qkv_w, v_a_log, v_dt_bias, v_gdn_norm_w, v_pool_w, v_pool_scale, v_w_out, v_norm_ffn_w, v_w_up, v_conv_ffn_w, v_conv_ffn_b, v_w_down, v_norm_final_w):
    local = dict(norm_mix_w=norm_mix_w, w_in=w_in, conv_qkv_w=conv_qkv_w, a_log=a_log, dt_bias=dt_bias,
                 gdn_norm_w=gdn_norm_w, pool_w=pool_w, pool_scale=pool_scale, w_out=w_out, norm_ffn_w=norm_ffn_w,
                 w_up=w_up, conv_ffn_w=conv_ffn_w, conv_ffn_b=conv_ffn_b, w_down=w_down, norm_final_w=norm_final_w)
    mom_m = dict(norm_mix_w=m_norm_mix_w, w_in=m_w_in, conv_qkv_w=m_conv_qkv_w, a_log=m_a_log, dt_bias=m_dt_bias,
                 gdn_norm_w=m_gdn_norm_w, pool_w=m_pool_w, pool_scale=m_pool_scale, w_out=m_w_out,
                 norm_ffn_w=m_norm_ffn_w, w_up=m_w_up, conv_ffn_w=m_conv_ffn_w, conv_ffn_b=m_conv_ffn_b,
                 w_down=m_w_down, norm_final_w=m_norm_final_w)
    mom_v = dict(norm_mix_w=v_norm_mix_w, w_in=v_w_in, conv_qkv_w=v_conv_qkv_w, a_log=v_a_log, dt_bias=v_dt_bias,
                 gdn_norm_w=v_gdn_norm_w, pool_w=v_pool_w, pool_scale=v_pool_scale, w_out=v_w_out,
                 norm_ffn_w=v_norm_ffn_w, w_up=v_w_up, conv_ffn_w=v_conv_ffn_w, conv_ffn_b=v_conv_ffn_b,
                 w_down=v_w_down, norm_final_w=v_norm_final_w)
    n_layers, n_heads = a_log.shape
    d_model = x.shape[-1]
    dl = n_heads * HEAD_DIM
    n_ba = 2 * n_heads
    assert x.shape[0] == 1 and dl == d_model and pool_scale.shape[1] == d_model
    lay = _win_layout(w_in.shape[2], N_DEV * w_in.shape[2] - n_ba, 4 * dl, n_ba)
    _, me = _here()
    is_ba_dev = me == lay.ba_dev
    my_pad = _main_start(lay, me) - _slab_origin(lay, me)
    ba_cols = slice(lay.ba_off, lay.ba_off + n_ba)

    w_in_bf = w_in.astype(BF16)
    without_ba = jnp.concatenate([w_in_bf[..., :lay.ba_off], w_in_bf[..., lay.ba_off + n_ba:],
                                  jnp.zeros(w_in.shape[:2] + (n_ba,), BF16)], axis=-1)
    slab = lax.dynamic_update_slice(jnp.zeros(w_in.shape[:2] + (lay.slot_w,), BF16),
                                    jnp.where(is_ba_dev, without_ba, w_in_bf), (0, 0, my_pad))
    ba_part = jnp.pad(jnp.where(is_ba_dev, w_in_bf[..., ba_cols], jnp.zeros((), BF16)),
                      ((0, 0), (0, 0), (0, LANES - n_ba)))
    convs = _pack_rows([conv_qkv_w, conv_ffn_w], F32, 16)
    ba_slots, conv_slots = _gather([ba_part, convs], [0, 0], name="gather_small_weights")
    conv_parts = _unpack(conv_slots.reshape(N_DEV, -1), [conv_qkv_w.shape, conv_ffn_w.shape])
    conv_qkv_whole, conv_ffn_whole = (_whole_from_slots(p, 2) for p in conv_parts)
    alog_rows, dtb_rows = _lane_row(a_log, n_heads), _lane_row(dt_bias, n_heads)

    def with_own_slot(own, axis):
        zone = lax.empty(own.shape[:axis] + (N_DEV,) + own.shape[axis:], own.dtype)
        return lax.dynamic_update_slice(zone, jnp.expand_dims(own, axis), (0,) * axis + (me,) + (0,) * (own.ndim - axis))

    gather_axes = (0, 0, 0, 0, 1)
    gather_src = lambda t, ref, to: ref
    gather_dst = lambda t, ref, frm: _slot_view(ref, gather_axes[t], frm)
    in_flight = []
    token = ba_slots
    for l in range(n_layers):
        srcs = [slab[l], w_up[l].astype(BF16), w_out[l].astype(BF16), w_down[l].astype(BF16), pool_w[l].astype(BF16)]
        lands = [with_own_slot(s, a) for s, a in zip(srcs, gather_axes)]
        *handles, token = _start_exchange(srcs, lands, gather_src, gather_dst, token, name="gather_start_%d" % l)
        in_flight.append(handles)

    xc = x[0]
    layer_w, saved = [], []
    after = token
    for l in range(n_layers):
        send_sems, recv_sems, srcs, lands = in_flight[l]
        slabs, up_slots, out_slots, down_slots, pool_slots = _wait_exchange(
            send_sems, recv_sems, srcs, lands, gather_src, gather_dst, after, name="gather_wait_%d" % l)
        wl = dict(
            n_heads=n_heads, norm_mix_w=norm_mix_w[l], norm_ffn_w=norm_ffn_w[l], gdn_norm_w=gdn_norm_w[l],
            pool_scale=pool_scale[l], conv_ffn_b=conv_ffn_b[l], alog_row=alog_rows[l], dtb_row=dtb_rows[l],
            w_main=_assemble_w_main(slabs, lay, name="assemble_w_main"), w_ba=ba_slots[lay.ba_dev, l],
            w_up_slots=up_slots, w_out=out_slots.reshape(-1, d_model), w_down=down_slots.reshape(-1, d_model),
            pool_w=pool_slots.reshape(pool_slots.shape[0], -1, pool_slots.shape[-1]),
            conv_qkv_w=conv_qkv_whole[l], conv_ffn_w=conv_ffn_whole[l])
        xc, sv = _layer_fwd(xc, wl)
        layer_w.append(wl)
        saved.append(sv)
        after = xc
    loss_row, dx, d_final = _loss_head(xc, norm_final_w, loss_target[0], name="loss_head")
    loss = lax.psum(loss_row[0, 0], MESH_AXES)

    shard = {n: local[n].shape[1:] for n in ("w_up", "w_out", "w_down", "pool_w")}
    recvs = [lax.empty((N_DEV, n_layers) + shape, BF16) for shape in
             [(d_model, lay.slot_w), shard["w_up"], shard["w_out"], shard["w_down"], shard["pool_w"]]]
    up_w, out_rows, down_rows, pool_rows = shard["w_up"][1], shard["w_out"][0], shard["w_down"][0], shard["pool_w"][1]
    owned = [lambda to: (1, _slab_origin(lay, to), lay.slot_w), lambda to: (1, to * up_w, up_w),
             lambda to: (0, to * out_rows, out_rows), lambda to: (0, to * down_rows, down_rows),
             lambda to: (1, to * pool_rows, pool_rows)]

    def scatter_src(t, ref, to):
        axis, first, length = owned[t](to)
        return ref.at[(slice(None),) * axis + (pl.ds(pl.multiple_of(first, LANES if axis == ref.ndim - 1 else 16), length),)]

    layer_grads = [None] * n_layers
    pending, token = None, None
    for l in reversed(range(n_layers)):
        dx, g = _layer_bwd(dx, layer_w[l], saved[l], token)
        layer_grads[l] = g
        if pending is not None:
            recvs = _wait_exchange(*pending, after=dx, name="scatter_wait_%d" % (l + 1))
        grads = [g["w_main"], g["w_up"], g["w_out"], g["w_down"], g["pool_w"].astype(BF16)]
        for t in range(len(grads)):
            axis, first, length = owned[t](me)
            own = lax.dynamic_slice_in_dim(grads[t], first, length, axis=axis)
            recvs[t] = lax.dynamic_update_slice(recvs[t], own[None, None], (me, l) + (0,) * own.ndim)
        scatter_dst = lambda t, ref, frm, l=l: ref.at[frm, l]
        send_sems, recv_sems, grads, recvs, token = _start_exchange(grads, recvs, scatter_src, scatter_dst, None,
                                                                    name="scatter_start_%d" % l)
        pending = (send_sems, recv_sems, grads, recvs, scatter_src, scatter_dst)
    recvs = _wait_exchange(*pending, after=token, name="scatter_wait_0")
    grad_x = dx
    stack = lambda name: jnp.stack([g[name] for g in layer_grads])

    small_names = REPLICATED + ("conv_qkv_w", "conv_ffn_w", "w_ba")
    g_small = dict(norm_mix_w=stack("norm_mix_w")[:, 0], a_log=stack("alog_row")[:, 0, n_heads:n_ba],
                   dt_bias=stack("dtb_row")[:, 0, n_heads:n_ba], gdn_norm_w=stack("gdn_norm_w")[:, 0],
                   pool_scale=stack("pool_scale")[:, 0], norm_ffn_w=stack("norm_ffn_w")[:, 0],
                   conv_ffn_b=stack("conv_ffn_b")[:, 0], norm_final_w=d_final[0], conv_qkv_w=stack("conv_qkv_w"),
                   conv_ffn_w=stack("conv_ffn_w"), w_ba=stack("w_ba")[..., :n_ba])
    small_shapes = [g_small[n].shape for n in small_names]
    small_slots, = _gather([_pack_rows([g_small[n] for n in small_names], F32, SMALL_QUANTUM_ROWS)], [0],
                           name="gather_small_grads")
    small_sum = _sum_slots(small_slots[:, None], name="sum_small_grads")
    grad = dict(zip(small_names, _unpack(small_sum.reshape(-1), small_shapes)))
    for n in ("conv_qkv_w", "conv_ffn_w"):
        width = local[n].shape[2]
        grad[n] = lax.dynamic_slice_in_dim(grad[n], me * width, width, axis=2)

    delta, new_m, new_v = {}, {}, {}
    for n, parts in zip(("w_up", "w_out", "w_down", "pool_w"), recvs[1:]):
        flat = lambda t, lead: t.reshape(t.shape[:lead] + (-1, t.shape[-1]))
        outs = _adamw_slots(flat(parts, 2), flat(local[n], 1), flat(mom_m[n], 1), flat(mom_v[n], 1), name="adamw_" + n)
        grad[n], delta[n], new_m[n], new_v[n] = (t.reshape(local[n].shape) for t in outs)
    main_sum = _sum_slots(recvs[0], name="sum_w_main_grads")
    g_main = lax.dynamic_slice_in_dim(main_sum, my_pad, lay.shard_w, axis=2)
    with_ba = jnp.concatenate([g_main[..., :lay.ba_off], grad.pop("w_ba"),
                               g_main[..., lay.ba_off:lay.shard_w - n_ba]], axis=-1)
    grad["w_in"] = jnp.where(is_ba_dev, with_ba, g_main)
    for n in ("w_in", "conv_qkv_w", "conv_ffn_w"):
        delta[n], new_m[n], new_v[n] = _adamw_nd(local[n], grad[n], mom_m[n], mom_v[n], name="adamw_" + n)
    packed = [_pack_rows([src[n] for n in REPLICATED], F32, 8) for src in (local, grad, mom_m, mom_v)]
    rep_out = _adamw(*packed, name="adamw_replicated")
    rep_shapes = [local[n].shape for n in REPLICATED]
    for dst, arr in zip((delta, new_m, new_v), rep_out):
        dst.update(zip(REPLICATED, _unpack(arr.reshape(-1), rep_shapes)))

    return (loss, grad_x[None], *[grad[n] for n in WEIGHTS], *[delta[n] for n in WEIGHTS],
            *[new_m[n] for n in WEIGHTS], *[new_v[n] for n in WEIGHTS])
```

```python
import collections

import jax
import jax.numpy as jnp
from jax import lax
from jax.experimental import pallas as pl
from jax.experimental.pallas import tpu as pltpu

F32 = jnp.float32
BF16 = jnp.bfloat16
MESH_AXES = ("x", "y", "c")
N_DEV = 8

NORM_EPS = 1e-6
HEAD_DIM = 128
GDN_CHUNK = 64
GDN_BATCH = 4
GDN_HEADS = 2
POOL_WINDOWS = (2, 4, 8, 16)
POOL_HALO = 16
CONV_HALO = 8
LANES = 128
V7X_VMEM_LIMIT_BYTES = 56 * 1024 * 1024

ADAM_LR = 0.001
ADAM_B1 = 0.9
ADAM_B2 = 0.999
ADAM_EPS = 1e-08
ADAM_WD = 0.01
ADAM_STEP = 10


def _mx(v):
    return v.astype(BF16)


def _dot(a, b, ta=False, tb=False, precision=None):
    dims = (((0 if ta else 1,), (1 if tb else 0,)), ((), ()))
    return lax.dot_general(a, b, dims, precision=precision, preferred_element_type=F32)


def _tile(dim, target, quantum=LANES):
    if dim <= target:
        return dim
    t = (target // quantum) * quantum
    while t >= quantum:
        if dim % t == 0:
            return t
        t -= quantum
    return dim


def _params(*semantics):
    return pltpu.CompilerParams(dimension_semantics=semantics, vmem_limit_bytes=V7X_VMEM_LIMIT_BYTES)


def _sigmoid(v):
    return 1.0 / (1.0 + jnp.exp(-v))


def _softplus(v):
    return jnp.maximum(v, 0.0) + jnp.log(1.0 + jnp.exp(-jnp.abs(v)))


_ERF_NUM = (-2.72614225801306e-10, 2.77068142495902e-08, -2.10102402082508e-06, -5.69250639462346e-05,
            -7.34990630326855e-04, -2.95459980854025e-03, -1.60960333262415e-02)
_ERF_DEN = (-1.45660718464996e-05, -2.13374055278905e-04, -1.68282697438203e-03, -7.37332916720468e-03,
            -1.42647390514189e-02)


def _erf(v):
    v = jnp.clip(v, -4.0, 4.0)
    v2 = v * v
    num = jnp.full_like(v, _ERF_NUM[0])
    for coef in _ERF_NUM[1:]:
        num = num * v2 + coef
    den = jnp.full_like(v, _ERF_DEN[0])
    for coef in _ERF_DEN[1:]:
        den = den * v2 + coef
    return v * num / den


def _gelu(v):
    return 0.5 * v * (1.0 + _erf(v * (2.0 ** -0.5)))


def _gelu_grad(v):
    return 0.5 * (1.0 + _erf(v * (2.0 ** -0.5))) + v * jnp.exp(-0.5 * v * v) * ((2.0 * jnp.pi) ** -0.5)


def _rows_before(cat, shift, halo):
    return pltpu.roll(cat, shift, 0)[halo:]


def _rows_after(cat, shift, rows):
    return pltpu.roll(cat, cat.shape[0] - shift, 0)[:rows]


def _accumulate(ref, value, first):
    @pl.when(first)
    def _():
        ref[...] = value

    @pl.when(jnp.logical_not(first))
    def _():
        ref[...] += value


def _matmul(a, b, *, name, ta=False, tb=False, add=None, out_dtype=F32, tm=512, tn=1024, tk=2048,
            b_slots=False, after=None):
    m, k = (a.shape[1], a.shape[0]) if ta else a.shape
    b_rows, b_cols = (b.shape[1], N_DEV * b.shape[2]) if b_slots else b.shape
    n, kb = (b_rows, b_cols) if tb else (b_cols, b_rows)
    assert kb == k
    if b_slots:
        tn, tk = (tn, b.shape[2]) if tb else (b.shape[2], tk)
    tm, tn, tk = _tile(m, tm), _tile(n, tn), _tile(k, tk)
    nk = k // tk
    has_add = add is not None
    n_in = 2 + has_add + (after is not None)

    def body(*refs):
        a_ref, b_ref = refs[0], refs[1]
        add_ref = refs[2] if has_add else None
        o_ref, acc_ref = refs[n_in], refs[n_in + 1]
        kk = pl.program_id(2)
        part = _dot(_mx(a_ref[...]), _mx(b_ref[...]), ta, tb)

        def finish(total):
            if has_add:
                total = total + add_ref[...]
            o_ref[...] = total.astype(out_dtype)

        if nk == 1:
            finish(part)
        else:
            _accumulate(acc_ref, part, kk == 0)

            @pl.when(kk == nk - 1)
            def _():
                finish(acc_ref[...])

    a_spec = pl.BlockSpec((tk, tm), lambda j, i, kk: (kk, i)) if ta else pl.BlockSpec((tm, tk), lambda j, i, kk: (i, kk))
    b_block = (tn, tk) if tb else (tk, tn)
    if b_slots:
        b_spec = pl.BlockSpec((None,) + b_block, (lambda j, i, kk: (kk, j, 0)) if tb else (lambda j, i, kk: (j, kk, 0)))
    else:
        b_spec = pl.BlockSpec(b_block, (lambda j, i, kk: (j, kk)) if tb else (lambda j, i, kk: (kk, j)))
    o_spec = pl.BlockSpec((tm, tn), lambda j, i, kk: (i, j))
    in_specs = [a_spec, b_spec] + ([o_spec] if has_add else [])
    args = (a, b) + ((add,) if has_add else ())
    if after is not None:
        in_specs.append(pl.BlockSpec(memory_space=pl.ANY))
        args += (after,)
    acc_shape = (tm, tn) if nk > 1 else (8, LANES)
    return pl.pallas_call(
        body, name=name, grid=(n // tn, m // tm, nk), in_specs=in_specs, out_specs=o_spec,
        out_shape=jax.ShapeDtypeStruct((m, n), out_dtype), scratch_shapes=[pltpu.VMEM(acc_shape, F32)],
        compiler_params=_params("parallel", "parallel", "arbitrary"))(*args)


def _rmsnorm_fwd(x, w, *, name):
    s, d = x.shape
    ts = _tile(s, 512, 16)

    def body(x_ref, w_ref, o_ref):
        xf = x_ref[...]
        r = lax.rsqrt(jnp.mean(xf * xf, axis=-1, keepdims=True) + NORM_EPS)
        o_ref[...] = (xf * r * w_ref[...]).astype(BF16)

    return pl.pallas_call(
        body, name=name, grid=(s // ts,),
        in_specs=[pl.BlockSpec((ts, d), lambda i: (i, 0)), pl.BlockSpec((1, d), lambda i: (0, 0))],
        out_specs=pl.BlockSpec((ts, d), lambda i: (i, 0)),
        out_shape=jax.ShapeDtypeStruct((s, d), BF16), compiler_params=_params("parallel"))(x, w.reshape(1, d))


def _rmsnorm_bwd(dy, x, w, dres, *, name):
    s, d = x.shape
    ts = _tile(s, 256, 16)

    def body(dy_ref, x_ref, w_ref, dres_ref, dx_ref, dxb_ref, dw_ref):
        xf = x_ref[...]
        dyf = dy_ref[...]
        r = lax.rsqrt(jnp.mean(xf * xf, axis=-1, keepdims=True) + NORM_EPS)
        xh = xf * r
        dxh = dyf * w_ref[...]
        dx = dres_ref[...] + r * (dxh - xh * jnp.mean(dxh * xh, axis=-1, keepdims=True))
        dx_ref[...] = dx
        dxb_ref[...] = dx.astype(BF16)
        _accumulate(dw_ref, jnp.sum(dyf * xh, axis=0, keepdims=True), pl.program_id(0) == 0)

    row = pl.BlockSpec((ts, d), lambda i: (i, 0))
    vec = pl.BlockSpec((1, d), lambda i: (0, 0))
    return pl.pallas_call(
        body, name=name, grid=(s // ts,), in_specs=[row, row, vec, row], out_specs=[row, row, vec],
        out_shape=[jax.ShapeDtypeStruct((s, d), F32), jax.ShapeDtypeStruct((s, d), BF16),
                   jax.ShapeDtypeStruct((1, d), F32)],
        compiler_params=_params("arbitrary"))(dy, x, w.reshape(1, d), dres)


def _loss_head(x, w, target, *, name):
    s, d = x.shape
    ts = _tile(s, 256, 16)

    def body(x_ref, w_ref, t_ref, loss_ref, dx_ref, dxb_ref, dw_ref):
        first = pl.program_id(0) == 0
        xf = x_ref[...]
        wv = w_ref[...]
        r = lax.rsqrt(jnp.mean(xf * xf, axis=-1, keepdims=True) + NORM_EPS)
        xh = xf * r
        err = xh * wv - t_ref[...]
        part = 0.5 * jnp.sum(jnp.mean(err * err, axis=-1, keepdims=True), axis=0, keepdims=True)
        _accumulate(loss_ref, jnp.broadcast_to(part, (1, LANES)), first)
        dyf = err * (1.0 / d)
        dxh = dyf * wv
        dx = r * (dxh - xh * jnp.mean(dxh * xh, axis=-1, keepdims=True))
        dx_ref[...] = dx
        dxb_ref[...] = dx.astype(BF16)
        _accumulate(dw_ref, jnp.sum(dyf * xh, axis=0, keepdims=True), first)

    row = pl.BlockSpec((ts, d), lambda i: (i, 0))
    vec = pl.BlockSpec((1, d), lambda i: (0, 0))
    return pl.pallas_call(
        body, name=name, grid=(s // ts,), in_specs=[row, vec, row],
        out_specs=[pl.BlockSpec((1, LANES), lambda i: (0, 0)), row, row, vec],
        out_shape=[jax.ShapeDtypeStruct((1, LANES), F32), jax.ShapeDtypeStruct((s, d), F32),
                   jax.ShapeDtypeStruct((s, d), BF16), jax.ShapeDtypeStruct((1, d), F32)],
        compiler_params=_params("arbitrary"))(x, w.reshape(1, d), target)


def _qkv_fwd(proj, conv_w, n_heads, *, name):
    s = proj.shape[0]
    dl = n_heads * HEAD_DIM
    width = conv_w.shape[0]
    ts = _tile(s, 512, 8)

    def body(x_ref, halo_ref, w_ref, o_ref):
        i, sec = pl.program_id(0), pl.program_id(1)
        xv = x_ref[...]
        cat = jnp.concatenate([jnp.where(i > 0, halo_ref[...], 0.0), xv], axis=0)
        c = xv * w_ref[pl.ds(width - 1, 1), :]
        for sh in range(1, width):
            c = c + _rows_before(cat, sh, CONV_HALO) * w_ref[pl.ds(width - 1 - sh, 1), :]
        act = c * _sigmoid(c)

        @pl.when(sec == 2)
        def _():
            o_ref[...] = act

        @pl.when(sec < 2)
        def _():
            scale = jnp.where(sec == 0, HEAD_DIM ** -0.5, 1.0)
            for h in range(n_heads):
                cols = slice(h * HEAD_DIM, (h + 1) * HEAD_DIM)
                ah = act[:, cols]
                o_ref[:, cols] = ah * lax.rsqrt(jnp.sum(ah * ah, axis=-1, keepdims=True) + NORM_EPS) * scale

    return pl.pallas_call(
        body, name=name, grid=(s // ts, 3),
        in_specs=[pl.BlockSpec((ts, dl), lambda i, sec: (i, sec)),
                  pl.BlockSpec((CONV_HALO, dl), lambda i, sec: (jnp.maximum(i * (ts // CONV_HALO) - 1, 0), sec)),
                  pl.BlockSpec((width, dl), lambda i, sec: (0, sec))],
        out_specs=pl.BlockSpec((None, ts, dl), lambda i, sec: (sec, i, 0)),
        out_shape=jax.ShapeDtypeStruct((3, s, dl), F32),
        compiler_params=_params("parallel", "parallel"))(proj, proj, conv_w)


def _qkv_bwd(dqkv, proj, conv_w, n_heads, *, name):
    s = proj.shape[0]
    dl = n_heads * HEAD_DIM
    width = conv_w.shape[0]
    ts = _tile(s, 256, 16)
    n_tiles = s // ts
    per = ts // CONV_HALO
    rows = ts + CONV_HALO

    def body(d_ref, dnext_ref, x_ref, xprev_ref, xnext_ref, w_ref, dx_ref, dw_ref):
        sec, i = pl.program_id(0), pl.program_id(1)
        xv = x_ref[...]
        cat = jnp.concatenate([jnp.where(i > 0, xprev_ref[...], 0.0), xv, xnext_ref[...]], axis=0)
        shifted = [cat[CONV_HALO:]] + [_rows_before(cat, sh, CONV_HALO) for sh in range(1, width)]
        c = shifted[0] * w_ref[pl.ds(width - 1, 1), :]
        for sh in range(1, width):
            c = c + shifted[sh] * w_ref[pl.ds(width - 1 - sh, 1), :]
        sig = _sigmoid(c)
        act = c * sig
        dout = jnp.concatenate([d_ref[...], dnext_ref[...]], axis=0)
        scale = jnp.where(sec == 0, HEAD_DIM ** -0.5, 1.0)
        is_v = sec == 2
        pieces = []
        for h in range(n_heads):
            cols = slice(h * HEAD_DIM, (h + 1) * HEAD_DIM)
            ah, dh = act[:, cols], dout[:, cols]
            nrm = lax.rsqrt(jnp.sum(ah * ah, axis=-1, keepdims=True) + NORM_EPS)
            dnormed = scale * nrm * (dh - ah * (nrm * nrm) * jnp.sum(dh * ah, axis=-1, keepdims=True))
            pieces.append(jnp.where(is_v, dh, dnormed))
        dact = jnp.concatenate(pieces, axis=1)
        dc = dact * sig * (1.0 + c * (1.0 - sig))
        live = jnp.logical_or(lax.broadcasted_iota(jnp.int32, (rows, 1), 0) < ts, i < n_tiles - 1)
        dc = jnp.where(live, dc, 0.0)
        dx = dc[:ts] * w_ref[pl.ds(width - 1, 1), :]
        for sh in range(1, width):
            dx = dx + _rows_after(dc, sh, ts) * w_ref[pl.ds(width - 1 - sh, 1), :]
        dx_ref[...] = dx.astype(BF16)
        dw_rows = [jnp.sum(dc[:ts] * shifted[width - 1 - j][:ts], axis=0, keepdims=True) for j in range(width)]
        _accumulate(dw_ref, jnp.concatenate(dw_rows, axis=0), i == 0)

    return pl.pallas_call(
        body, name=name, grid=(3, n_tiles),
        in_specs=[pl.BlockSpec((None, ts, dl), lambda sec, i: (sec, i, 0)),
                  pl.BlockSpec((None, CONV_HALO, dl), lambda sec, i: (sec, jnp.minimum((i + 1) * per, s // CONV_HALO - 1), 0)),
                  pl.BlockSpec((ts, dl), lambda sec, i: (i, sec)),
                  pl.BlockSpec((CONV_HALO, dl), lambda sec, i: (jnp.maximum(i * per - 1, 0), sec)),
                  pl.BlockSpec((CONV_HALO, dl), lambda sec, i: (jnp.minimum((i + 1) * per, s // CONV_HALO - 1), sec)),
                  pl.BlockSpec((width, dl), lambda sec, i: (0, sec))],
        out_specs=[pl.BlockSpec((ts, dl), lambda sec, i: (i, sec)), pl.BlockSpec((width, dl), lambda sec, i: (0, sec))],
        out_shape=[jax.ShapeDtypeStruct((s, 3 * dl), BF16), jax.ShapeDtypeStruct((width, 3 * dl), F32)],
        compiler_params=_params("parallel", "arbitrary"))(dqkv, dqkv, proj, proj, proj, conv_w)


def _ba_fwd(pba, alog_row, dtb_row, n_heads, *, name):
    s = pba.shape[0]
    ts = _tile(s, 1024, 8)

    def body(x_ref, alog_ref, dtb_ref, o_ref):
        xv = x_ref[...]
        lane = lax.broadcasted_iota(jnp.int32, xv.shape, 1)
        g = -jnp.exp(alog_ref[...]) * _softplus(xv + dtb_ref[...])
        o_ref[...] = jnp.where(lane < n_heads, _sigmoid(xv), jnp.where(lane < 2 * n_heads, g, 0.0))

    row = pl.BlockSpec((ts, LANES), lambda i: (i, 0))
    vec = pl.BlockSpec((1, LANES), lambda i: (0, 0))
    return pl.pallas_call(
        body, name=name, grid=(s // ts,), in_specs=[row, vec, vec], out_specs=row,
        out_shape=jax.ShapeDtypeStruct((s, LANES), F32), compiler_params=_params("parallel"))(pba, alog_row, dtb_row)


def _ba_bwd(dbg, pba, alog_row, dtb_row, n_heads, *, name):
    s = pba.shape[0]
    ts = _tile(s, 1024, 16)

    def body(d_ref, x_ref, alog_ref, dtb_ref, dx_ref, dalog_ref, ddtb_ref):
        first = pl.program_id(0) == 0
        xv, dv = x_ref[...], d_ref[...]
        lane = lax.broadcasted_iota(jnp.int32, xv.shape, 1)
        beta = _sigmoid(xv)
        neg_a = -jnp.exp(alog_ref[...])
        xa = xv + dtb_ref[...]
        is_a = jnp.logical_and(lane >= n_heads, lane < 2 * n_heads)
        d_xa = jnp.where(is_a, dv * neg_a * _sigmoid(xa), 0.0)
        d_g_times_g = jnp.where(is_a, dv * neg_a * _softplus(xa), 0.0)
        dx_ref[...] = jnp.where(lane < n_heads, dv * beta * (1.0 - beta), d_xa).astype(BF16)
        _accumulate(dalog_ref, jnp.sum(d_g_times_g, axis=0, keepdims=True), first)
        _accumulate(ddtb_ref, jnp.sum(d_xa, axis=0, keepdims=True), first)

    row = pl.BlockSpec((ts, LANES), lambda i: (i, 0))
    vec = pl.BlockSpec((1, LANES), lambda i: (0, 0))
    return pl.pallas_call(
        body, name=name, grid=(s // ts,), in_specs=[row, row, vec, vec], out_specs=[row, vec, vec],
        out_shape=[jax.ShapeDtypeStruct((s, LANES), BF16), jax.ShapeDtypeStruct((1, LANES), F32),
                   jax.ShapeDtypeStruct((1, LANES), F32)],
        compiler_params=_params("arbitrary"))(dbg, pba, alog_row, dtb_row)


def _bdot(a, b, ta=False, tb=False, precision=None):
    dims = (((1 if ta else 2,), (2 if tb else 1,)), ((0,), (0,)))
    return lax.dot_general(a, b, dims, precision=precision, preferred_element_type=F32)


def _split_bf16(v):
    hi = v.astype(BF16)
    return hi, (v - hi.astype(F32)).astype(BF16)


def _bdot_x3(a, b, ta=False, tb=False):
    return _bdot(a[0], b[0], ta, tb) + (_bdot(a[0], b[1], ta, tb) + _bdot(a[1], b[0], ta, tb))


def _chunk_masks():
    ri = lax.broadcasted_iota(jnp.int32, (GDN_CHUNK, GDN_CHUNK), 0)
    ci = lax.broadcasted_iota(jnp.int32, (GDN_CHUNK, GDN_CHUNK), 1)
    return ri == ci, ri >= ci, ri > ci, ri <= ci


def _row_to_col(row, eye):
    return jnp.sum(jnp.where(eye, row, 0.0), axis=2, keepdims=True)


def _col_to_row(col, eye):
    return jnp.sum(jnp.where(eye, col, 0.0), axis=1, keepdims=True)


_Gates = collections.namedtuple("_Gates", "beta_col decay e_col f_col dec")


def _gdn_gates(beta_row, g_row):
    eye, tril, _, triu = _chunk_masks()
    g_col = _row_to_col(g_row, eye)
    gc_col = jnp.sum(jnp.where(tril, g_row, 0.0), axis=2, keepdims=True)
    gc_row = jnp.sum(jnp.where(triu, g_col, 0.0), axis=1, keepdims=True)
    g_last = jnp.sum(g_row, axis=2, keepdims=True)
    decay = jnp.exp(jnp.where(tril, gc_col - gc_row, -jnp.inf))
    return _Gates(_row_to_col(beta_row, eye), decay, jnp.exp(gc_col), jnp.exp(g_last - gc_col), jnp.exp(g_last))


def _unit_lower_inverse(lmat):
    c = GDN_CHUNK
    t = jnp.where(_chunk_masks()[0], 1.0, 0.0) - lmat
    l_parts = _split_bf16(lmat)
    p = _bdot_x3(l_parts, l_parts)
    doublings = c.bit_length() - 2
    for r in range(doublings):
        p_parts = _split_bf16(p)
        if r < doublings - 1:
            both = _bdot_x3(_split_bf16(jnp.concatenate([t, p], axis=1)), p_parts)
            t, p = t + both[:, :c], both[:, c:]
        else:
            t = t + _bdot_x3(_split_bf16(t), p_parts)
    return t


def _gdn_solve(q, k, v, gates):
    strict = _chunk_masks()[2]
    kb = k * gates.beta_col
    lmat = jnp.where(strict, _bdot(_mx(kb), _mx(k), tb=True) * gates.decay, 0.0)
    tmat = _unit_lower_inverse(lmat)
    sol = _bdot_x3(_split_bf16(tmat), _split_bf16(jnp.concatenate([v * gates.beta_col, kb * gates.e_col], axis=2)))
    at = _bdot(_mx(q), _mx(k), tb=True) * gates.decay
    return lmat, tmat, sol, at


def _gdn_blocking(s):
    n_chunks = s // GDN_CHUNK
    per_step = 16 if n_chunks % 16 == 0 else n_chunks
    assert per_step % GDN_BATCH == 0
    return n_chunks, per_step, n_chunks // per_step


def _load_chunks(ref, sec, n0, hp):
    r0 = pl.multiple_of(n0 * GDN_CHUNK, GDN_BATCH * GDN_CHUNK)
    rows = pl.ds(r0, GDN_BATCH * GDN_CHUNK)
    cols = slice(hp * HEAD_DIM, (hp + 1) * HEAD_DIM)
    val = ref[rows, cols] if sec is None else ref[sec, rows, cols]
    return val.reshape(GDN_BATCH, GDN_CHUNK, HEAD_DIM)


def _store_chunks(ref, sec, n0, hp, val):
    r0 = pl.multiple_of(n0 * GDN_CHUNK, GDN_BATCH * GDN_CHUNK)
    rows = pl.ds(r0, GDN_BATCH * GDN_CHUNK)
    cols = slice(hp * HEAD_DIM, (hp + 1) * HEAD_DIM)
    flat = val.reshape(GDN_BATCH * GDN_CHUNK, HEAD_DIM)
    if sec is None:
        ref[rows, cols] = flat
    else:
        ref[sec, rows, cols] = flat


def _gdn_specs(n_heads, n_steps, per_step, order):
    rows, width = per_step * GDN_CHUNK, GDN_HEADS * HEAD_DIM
    rowvec = pl.BlockSpec((GDN_HEADS, per_step, 1, GDN_CHUNK), lambda h, j: (h, order(j), 0, 0))
    qkv = pl.BlockSpec((3, rows, width), lambda h, j: (0, order(j), h))
    act = pl.BlockSpec((rows, width), lambda h, j: (order(j), h))
    states = pl.BlockSpec((GDN_HEADS, per_step, HEAD_DIM, HEAD_DIM), lambda h, j: (h, order(j), 0, 0))
    return rowvec, qkv, act, states


def _gdn_fwd(qkv, beta_rows, g_rows, *, name):
    _, s, dl = qkv.shape
    n_heads = dl // HEAD_DIM
    c = GDN_CHUNK
    n_chunks, per_step, n_steps = _gdn_blocking(s)
    n_groups = per_step // GDN_BATCH
    heads = range(GDN_HEADS)

    def body(qkv_ref, b_ref, g_ref, o_ref, st_ref, state_ref, sol_s, at_s, qd_s, ke_s, vn_s, dec_s):
        @pl.when(pl.program_id(1) == 0)
        def _():
            state_ref[...] = jnp.zeros_like(state_ref)

        def solve(gi, carry):
            n0 = gi * GDN_BATCH
            grp = pl.ds(n0, GDN_BATCH)
            for hp in heads:
                q, k, v = (_load_chunks(qkv_ref, j, n0, hp) for j in range(3))
                gates = _gdn_gates(b_ref[hp, grp], g_ref[hp, grp])
                _, _, sol, at = _gdn_solve(q, k, v, gates)
                sol_s[hp, grp] = sol
                at_s[hp, grp] = at
                qd_s[hp, grp] = q * gates.e_col
                ke_s[hp, grp] = k * gates.f_col
                dec_s[hp, grp] = jnp.broadcast_to(gates.dec, (GDN_BATCH, 1, LANES))
            return carry

        lax.fori_loop(0, n_groups, solve, 0)

        def recur(n, states):
            out = []
            for hp in heads:
                state = states[hp]
                st_ref[hp, n] = state
                sol = sol_s[hp, n]
                v_new = sol[:, :HEAD_DIM] - _dot(_mx(sol[:, HEAD_DIM:]), _mx(state))
                vn_s[hp, n] = v_new
                out.append(state * dec_s[hp, n] + _dot(_mx(ke_s[hp, n]), _mx(v_new), ta=True))
            return tuple(out)

        final = lax.fori_loop(0, per_step, recur, tuple(state_ref[hp] for hp in heads))
        for hp in heads:
            state_ref[hp] = final[hp]

        def emit(gi, carry):
            n0 = gi * GDN_BATCH
            grp = pl.ds(n0, GDN_BATCH)
            for hp in heads:
                o = (_bdot(_mx(qd_s[hp, grp]), _mx(st_ref[hp, grp]))
                     + _bdot(_mx(at_s[hp, grp]), _mx(vn_s[hp, grp])))
                _store_chunks(o_ref, None, n0, hp, o)
            return carry

        lax.fori_loop(0, n_groups, emit, 0)

    rowvec, qkv_spec, act_spec, st_spec = _gdn_specs(n_heads, n_steps, per_step, lambda j: j)
    wide = lambda w: pltpu.VMEM((GDN_HEADS, per_step, c, w), F32)
    return pl.pallas_call(
        body, name=name, grid=(n_heads // GDN_HEADS, n_steps),
        in_specs=[qkv_spec, rowvec, rowvec], out_specs=[act_spec, st_spec],
        out_shape=[jax.ShapeDtypeStruct((s, dl), F32),
                   jax.ShapeDtypeStruct((n_heads, n_chunks, HEAD_DIM, HEAD_DIM), F32)],
        scratch_shapes=[pltpu.VMEM((GDN_HEADS, HEAD_DIM, HEAD_DIM), F32), wide(2 * HEAD_DIM), wide(c), wide(HEAD_DIM),
                        wide(HEAD_DIM), wide(HEAD_DIM), pltpu.VMEM((GDN_HEADS, per_step, 1, LANES), F32)],
        compiler_params=_params("parallel", "arbitrary"))(qkv, beta_rows, g_rows)


def _gdn_bwd(do, qkv, beta_rows, g_rows, states, *, name):
    _, s, dl = qkv.shape
    n_heads = dl // HEAD_DIM
    c = GDN_CHUNK
    n_chunks, per_step, n_steps = _gdn_blocking(s)
    n_groups = per_step // GDN_BATCH
    heads = range(GDN_HEADS)

    def body(do_ref, qkv_ref, b_ref, g_ref, st_ref, dqkv_ref, db_ref, dg_ref,
             dstate_ref, lmat_s, tmat_s, at_s, dat_s, sol_s, vn_s, dvn_s, dqd_s, ke_s, qdo_s, dst_s, dec_s):
        @pl.when(pl.program_id(1) == 0)
        def _():
            dstate_ref[...] = jnp.zeros_like(dstate_ref)

        eye, tril, strict, _ = _chunk_masks()

        def solve(gi, carry):
            n0 = gi * GDN_BATCH
            grp = pl.ds(n0, GDN_BATCH)
            for hp in heads:
                q, k, v = (_load_chunks(qkv_ref, j, n0, hp) for j in range(3))
                gates = _gdn_gates(b_ref[hp, grp], g_ref[hp, grp])
                lmat, tmat, sol, at = _gdn_solve(q, k, v, gates)
                mstate = _mx(st_ref[hp, grp])
                md_o = _mx(_load_chunks(do_ref, None, n0, hp))
                v_new = sol[:, :, :HEAD_DIM] - _bdot(_mx(sol[:, :, HEAD_DIM:]), mstate)
                lmat_s[hp, grp] = lmat
                tmat_s[hp, grp] = tmat
                sol_s[hp, grp] = sol
                at_s[hp, grp] = at
                vn_s[hp, grp] = v_new
                dat_s[hp, grp] = jnp.where(tril, _bdot(md_o, _mx(v_new), tb=True), 0.0)
                dvn_s[hp, grp] = _bdot(_mx(at), md_o, ta=True)
                dqd_s[hp, grp] = _bdot(md_o, mstate, tb=True)
                qdo_s[hp, grp] = _bdot(_mx(q * gates.e_col), md_o, ta=True)
                ke_s[hp, grp] = k * gates.f_col
                dec_s[hp, grp] = jnp.broadcast_to(gates.dec, (GDN_BATCH, 1, LANES))
            return carry

        lax.fori_loop(0, n_groups, solve, 0)

        def recur(idx, dstates):
            n = per_step - 1 - idx
            out = []
            for hp in heads:
                dstate = dstates[hp]
                dst_s[hp, n] = dstate
                dv_new = dvn_s[hp, n] + _dot(_mx(ke_s[hp, n]), _mx(dstate))
                dvn_s[hp, n] = dv_new
                wc = sol_s[hp, n][:, HEAD_DIM:]
                out.append(dstate * dec_s[hp, n] + qdo_s[hp, n] - _dot(_mx(wc), _mx(dv_new), ta=True))
            return tuple(out)

        final = lax.fori_loop(0, per_step, recur, tuple(dstate_ref[hp] for hp in heads))
        for hp in heads:
            dstate_ref[hp] = final[hp]

        def emit_head(hp, n0):
            grp = pl.ds(n0, GDN_BATCH)
            q, k, v = (_load_chunks(qkv_ref, j, n0, hp) for j in range(3))
            gates = _gdn_gates(b_ref[hp, grp], g_ref[hp, grp])
            state, dstate = st_ref[hp, grp], dst_s[hp, grp]
            lmat, at, dat, sol = lmat_s[hp, grp], at_s[hp, grp], dat_s[hp, grp], sol_s[hp, grp]
            v_new, dv_new, dqd = vn_s[hp, grp], dvn_s[hp, grp], dqd_s[hp, grp]
            dke = _bdot(_mx(v_new), _mx(dstate), tb=True)
            dwc = -_bdot(_mx(dv_new), _mx(state), tb=True)
            ddec = jnp.sum(jnp.sum(dstate * state, axis=2, keepdims=True), axis=1, keepdims=True)
            drhs = _bdot_x3(_split_bf16(tmat_s[hp, grp]), _split_bf16(jnp.concatenate([dv_new, dwc], axis=2)), ta=True)
            dvb, dkbe = drhs[:, :, :HEAD_DIM], drhs[:, :, HEAD_DIM:]
            dl_mat = jnp.where(strict, -_bdot(_mx(drhs), _mx(sol), tb=True), 0.0)
            dkk = dl_mat * gates.decay
            dqk = dat * gates.decay
            kb = k * gates.beta_col
            mk = _mx(k)
            dkb = _bdot(_mx(dkk), mk) + dkbe * gates.e_col
            dq = _bdot(_mx(dqk), mk) + dqd * gates.e_col
            dk = (_bdot(_mx(dqk), _mx(q), ta=True) + _bdot(_mx(dkk), _mx(kb), ta=True) + dke * gates.f_col
                  + dkb * gates.beta_col)
            _store_chunks(dqkv_ref, 0, n0, hp, dq)
            _store_chunks(dqkv_ref, 1, n0, hp, dk)
            _store_chunks(dqkv_ref, 2, n0, hp, dvb * gates.beta_col)
            dbeta_col = jnp.sum(dkb * k + dvb * v, axis=2, keepdims=True)
            through_decay = dl_mat * lmat + dat * at
            dke_ke = jnp.sum(dke * (k * gates.f_col), axis=2, keepdims=True)
            dgc_col = (jnp.sum(through_decay, axis=2, keepdims=True)
                       - _row_to_col(jnp.sum(through_decay, axis=1, keepdims=True), eye)
                       + jnp.sum(dqd * (q * gates.e_col) + dkbe * (kb * gates.e_col), axis=2, keepdims=True) - dke_ke)
            dg_last = jnp.sum(dke_ke, axis=1, keepdims=True) + ddec * gates.dec
            db_ref[hp, grp] = _col_to_row(dbeta_col, eye)
            dg_ref[hp, grp] = jnp.sum(jnp.where(tril, dgc_col, 0.0), axis=1, keepdims=True) + dg_last

        def emit(gi, carry):
            for hp in heads:
                emit_head(hp, gi * GDN_BATCH)
            return carry

        lax.fori_loop(0, n_groups, emit, 0)

    rowvec, qkv_spec, act_spec, st_spec = _gdn_specs(n_heads, n_steps, per_step, lambda j: n_steps - 1 - j)
    wide = lambda w: pltpu.VMEM((GDN_HEADS, per_step, c, w), F32)
    square = pltpu.VMEM((GDN_HEADS, per_step, HEAD_DIM, HEAD_DIM), F32)
    return pl.pallas_call(
        body, name=name, grid=(n_heads // GDN_HEADS, n_steps),
        in_specs=[act_spec, qkv_spec, rowvec, rowvec, st_spec], out_specs=[qkv_spec, rowvec, rowvec],
        out_shape=[jax.ShapeDtypeStruct((3, s, dl), F32),
                   jax.ShapeDtypeStruct((n_heads, n_chunks, 1, c), F32),
                   jax.ShapeDtypeStruct((n_heads, n_chunks, 1, c), F32)],
        scratch_shapes=[pltpu.VMEM((GDN_HEADS, HEAD_DIM, HEAD_DIM), F32), wide(c), wide(c), wide(c), wide(c),
                        wide(2 * HEAD_DIM), wide(HEAD_DIM), wide(HEAD_DIM), wide(HEAD_DIM), wide(HEAD_DIM),
                        square, square, pltpu.VMEM((GDN_HEADS, per_step, 1, LANES), F32)],
        compiler_params=_params("parallel", "arbitrary"))(do, qkv, beta_rows, g_rows, states)


def _pool_counts(tile, ts, extra, win):
    t = tile * ts + lax.broadcasted_iota(jnp.int32, (ts + extra, 1), 0)
    return jnp.minimum(t + 1, win).astype(F32)


def _pooled(cat, p_cols, tile, ts, win):
    acc, span = cat, 1
    while span < win:
        acc = acc + pltpu.roll(acc, span, 0)
        span *= 2
    return acc[POOL_HALO:] / _pool_counts(tile, ts, 0, win) - p_cols


def _merge_fwd(proj, o, gnw, pool_w, pool_scale, *, name):
    s, d = o.shape
    n_heads = d // HEAD_DIM
    n_groups, pg = pool_w.shape[0], pool_w.shape[1]
    assert n_groups == len(POOL_WINDOWS) and n_groups * pg == d and pg % HEAD_DIM == 0
    heads_per_group = pg // HEAD_DIM
    ts = _tile(s, 256, 16)

    def body(o_ref, z_ref, p_ref, halo_ref, ga_ref, gb_ref, gnw_ref, pw_ref, ps_ref, out_ref):
        i = pl.program_id(0)
        gnw_v = gnw_ref[...]
        halo = jnp.where(i > 0, halo_ref[...], 0.0)
        for gi, win in enumerate(POOL_WINDOWS):
            gcols = slice(gi * pg, (gi + 1) * pg)
            pv = p_ref[:, gcols]
            pooled = _pooled(jnp.concatenate([halo[:, gcols], pv], axis=0), pv, i, ts, win)
            yb = _dot(_mx(pooled), pw_ref[gi]) * ps_ref[:, gcols]
            for h in range(gi * heads_per_group, (gi + 1) * heads_per_group):
                cols = slice(h * HEAD_DIM, (h + 1) * HEAD_DIM)
                in_group = slice(h * HEAD_DIM - gi * pg, (h + 1) * HEAD_DIM - gi * pg)
                oh, zh = o_ref[:, cols], z_ref[:, cols]
                r = lax.rsqrt(jnp.mean(oh * oh, axis=-1, keepdims=True) + NORM_EPS)
                ya = oh * r * gnw_v * (zh * _sigmoid(zh))
                out_ref[:, cols] = (_sigmoid(ga_ref[:, cols]) * ya
                                    + _sigmoid(gb_ref[:, cols]) * yb[:, in_group]).astype(BF16)

    blk = lambda col: pl.BlockSpec((ts, d), lambda i, col=col: (i, col))
    vec = lambda width: pl.BlockSpec((1, width), lambda i: (0, 0))
    return pl.pallas_call(
        body, name=name, grid=(s // ts,),
        in_specs=[blk(0), blk(3), blk(4),
                  pl.BlockSpec((POOL_HALO, d), lambda i: (jnp.maximum(i * (ts // POOL_HALO) - 1, 0), 4)),
                  blk(5), blk(6), vec(HEAD_DIM), pl.BlockSpec((n_groups, pg, pg), lambda i: (0, 0, 0)), vec(d)],
        out_specs=blk(0), out_shape=jax.ShapeDtypeStruct((s, d), BF16),
        compiler_params=_params("parallel"))(o, proj, proj, proj, proj, proj, gnw.reshape(1, HEAD_DIM), pool_w,
                                              pool_scale.reshape(1, d))


def _merge_bwd(dmixed, proj, o, gnw, pool_w, pool_scale, *, name):
    s, d = o.shape
    n_heads = d // HEAD_DIM
    n_groups, pg = pool_w.shape[0], pool_w.shape[1]
    ts = _tile(s, 256, 16)

    def body(dm_ref, o_ref, z_ref, p_ref, halo_ref, ga_ref, gb_ref, gnw_ref, pw_ref, ps_ref,
             do_ref, dz_ref, dga_ref, dgb_ref, dpl_ref, dgnw_ref, dpw_ref, dps_ref):
        i = pl.program_id(0)
        first = i == 0
        gnw_v = gnw_ref[...]
        dgnw = jnp.zeros((1, HEAD_DIM), F32)
        for h in range(n_heads):
            cols = slice(h * HEAD_DIM, (h + 1) * HEAD_DIM)
            oh, zh, dm = o_ref[:, cols], z_ref[:, cols], dm_ref[:, cols]
            r = lax.rsqrt(jnp.mean(oh * oh, axis=-1, keepdims=True) + NORM_EPS)
            xh = oh * r
            sz = _sigmoid(zh)
            silu_z = zh * sz
            sa = _sigmoid(ga_ref[:, cols])
            on = xh * gnw_v
            dya = dm * sa
            dga_ref[:, cols] = (dm * on * silu_z * sa * (1.0 - sa)).astype(BF16)
            dz_ref[:, cols] = (dya * on * sz * (1.0 + zh * (1.0 - sz))).astype(BF16)
            don = dya * silu_z
            dgnw = dgnw + jnp.sum(don * xh, axis=0, keepdims=True)
            dxh = don * gnw_v
            do_ref[:, cols] = r * (dxh - xh * jnp.mean(dxh * xh, axis=-1, keepdims=True))
        _accumulate(dgnw_ref, dgnw, first)
        halo = jnp.where(first, 0.0, halo_ref[...])
        for gi, win in enumerate(POOL_WINDOWS):
            cols = slice(gi * pg, (gi + 1) * pg)
            pv, dm = p_ref[:, cols], dm_ref[:, cols]
            pooled = _pooled(jnp.concatenate([halo[:, cols], pv], axis=0), pv, i, ts, win)
            lin = _dot(_mx(pooled), pw_ref[gi])
            psv = ps_ref[:, cols]
            sb = _sigmoid(gb_ref[:, cols])
            dgb_ref[:, cols] = (dm * lin * psv * sb * (1.0 - sb)).astype(BF16)
            dyb = dm * sb
            _accumulate(dps_ref.at[:, cols], jnp.sum(dyb * lin, axis=0, keepdims=True), first)
            dlin = _mx(dyb * psv)
            _accumulate(dpw_ref.at[gi], _dot(_mx(pooled), dlin, ta=True), first)
            dpl_ref[:, cols] = _dot(dlin, pw_ref[gi], tb=True)

    blk = lambda col: pl.BlockSpec((ts, d), lambda i, col=col: (i, col))
    vec = lambda width: pl.BlockSpec((1, width), lambda i: (0, 0))
    pw_spec = pl.BlockSpec((n_groups, pg, pg), lambda i: (0, 0, 0))
    return pl.pallas_call(
        body, name=name, grid=(s // ts,),
        in_specs=[blk(0), blk(0), blk(3), blk(4),
                  pl.BlockSpec((POOL_HALO, d), lambda i: (jnp.maximum(i * (ts // POOL_HALO) - 1, 0), 4)),
                  blk(5), blk(6), vec(HEAD_DIM), pw_spec, vec(d)],
        out_specs=[blk(0), blk(0), blk(0), blk(0), blk(0), vec(HEAD_DIM), pw_spec, vec(d)],
        out_shape=[jax.ShapeDtypeStruct((s, d), F32), jax.ShapeDtypeStruct((s, d), BF16),
                   jax.ShapeDtypeStruct((s, d), BF16), jax.ShapeDtypeStruct((s, d), BF16),
                   jax.ShapeDtypeStruct((s, d), F32), jax.ShapeDtypeStruct((1, HEAD_DIM), F32),
                   jax.ShapeDtypeStruct((n_groups, pg, pg), F32), jax.ShapeDtypeStruct((1, d), F32)],
        compiler_params=_params("arbitrary"))(dmixed, o, proj, proj, proj, proj, proj, gnw.reshape(1, HEAD_DIM),
                                               pool_w, pool_scale.reshape(1, d))


def _pool_bwd(dpooled, *, name):
    s, d = dpooled.shape
    pg = d // len(POOL_WINDOWS)
    ts = _tile(s, 512, 16)
    n_tiles = s // ts
    per = ts // POOL_HALO

    def body(d_ref, next_ref, out_ref):
        i = pl.program_id(0)
        nxt = jnp.where(i < n_tiles - 1, next_ref[...], 0.0)
        for gi, win in enumerate(POOL_WINDOWS):
            cols = slice(gi * pg, (gi + 1) * pg)
            dv = d_ref[:, cols]
            acc = jnp.concatenate([dv, nxt[:, cols]], axis=0) / _pool_counts(i, ts, POOL_HALO, win)
            span = 1
            while span < win:
                acc = acc + pltpu.roll(acc, acc.shape[0] - span, 0)
                span *= 2
            out_ref[:, cols] = (acc[:ts] - dv).astype(BF16)

    return pl.pallas_call(
        body, name=name, grid=(n_tiles,),
        in_specs=[pl.BlockSpec((ts, d), lambda i: (i, 0)),
                  pl.BlockSpec((POOL_HALO, d), lambda i: (jnp.minimum((i + 1) * per, s // POOL_HALO - 1), 0))],
        out_specs=pl.BlockSpec((ts, d), lambda i: (i, 0)), out_shape=jax.ShapeDtypeStruct((s, d), BF16),
        compiler_params=_params("parallel"))(dpooled, dpooled)


def _ffn_tiles(s, f):
    tf = _tile(f, 1408)
    return _tile(s, 512, 16), tf, f // tf


def _ffn_act_fwd(gu, conv_w, conv_b, *, name):
    s, f = gu.shape[0], gu.shape[1] // 2
    width = conv_w.shape[0]
    ts, tf, nf = _ffn_tiles(s, f)

    def body(g_ref, halo_ref, u_ref, w_ref, b_ref, act_ref, gc_ref):
        i = pl.program_id(0)
        gv = g_ref[...]
        cat = jnp.concatenate([jnp.where(i > 0, halo_ref[...], 0.0), gv], axis=0)
        gc = gv * w_ref[pl.ds(width - 1, 1), :] + b_ref[...]
        for sh in range(1, width):
            gc = gc + _rows_before(cat, sh, CONV_HALO) * w_ref[pl.ds(width - 1 - sh, 1), :]
        gc_ref[...] = gc
        act_ref[...] = (_gelu(gc) * u_ref[...]).astype(BF16)

    blk = pl.BlockSpec((ts, tf), lambda i, j: (i, j))
    return pl.pallas_call(
        body, name=name, grid=(s // ts, nf),
        in_specs=[blk, pl.BlockSpec((CONV_HALO, tf), lambda i, j: (jnp.maximum(i * (ts // CONV_HALO) - 1, 0), j)),
                  pl.BlockSpec((ts, tf), lambda i, j: (i, nf + j)),
                  pl.BlockSpec((width, tf), lambda i, j: (0, j)), pl.BlockSpec((1, tf), lambda i, j: (0, j))],
        out_specs=[blk, blk],
        out_shape=[jax.ShapeDtypeStruct((s, f), BF16), jax.ShapeDtypeStruct((s, f), F32)],
        compiler_params=_params("parallel", "parallel"))(gu, gu, gu, conv_w, conv_b.reshape(1, f))


def _ffn_act_bwd(dact, gu, gc, conv_w, *, name):
    s, f = gc.shape
    width = conv_w.shape[0]
    ts, tf, nf = _ffn_tiles(s, f)
    n_tiles = s // ts
    per = ts // CONV_HALO
    rows = ts + CONV_HALO

    def body(da_ref, da_next, gc_ref, gc_next, u_ref, u_next, g_ref, g_prev, w_ref, dg_ref, du_ref, dw_ref, db_ref):
        i = pl.program_id(1)
        first = i == 0
        da = jnp.concatenate([da_ref[...], da_next[...]], axis=0)
        gcv = jnp.concatenate([gc_ref[...], gc_next[...]], axis=0)
        uv = jnp.concatenate([u_ref[...], u_next[...]], axis=0)
        du_ref[...] = (da[:ts] * _gelu(gcv[:ts])).astype(BF16)
        live = jnp.logical_or(lax.broadcasted_iota(jnp.int32, (rows, 1), 0) < ts, i < n_tiles - 1)
        dgc = jnp.where(live, da * uv * _gelu_grad(gcv), 0.0)
        dgate = dgc[:ts] * w_ref[pl.ds(width - 1, 1), :]
        for sh in range(1, width):
            dgate = dgate + _rows_after(dgc, sh, ts) * w_ref[pl.ds(width - 1 - sh, 1), :]
        dg_ref[...] = dgate.astype(BF16)
        gv = g_ref[...]
        cat = jnp.concatenate([jnp.where(first, 0.0, g_prev[...]), gv], axis=0)
        shifted = [gv] + [_rows_before(cat, sh, CONV_HALO) for sh in range(1, width)]
        dw_rows = [jnp.sum(dgc[:ts] * shifted[width - 1 - j], axis=0, keepdims=True) for j in range(width)]
        _accumulate(dw_ref, jnp.concatenate(dw_rows, axis=0), first)
        _accumulate(db_ref, jnp.sum(dgc[:ts], axis=0, keepdims=True), first)

    nxt_row = lambda i: jnp.minimum((i + 1) * per, s // CONV_HALO - 1)
    main = lambda off: pl.BlockSpec((ts, tf), lambda j, i, off=off: (i, off + j))
    nxt = lambda off: pl.BlockSpec((CONV_HALO, tf), lambda j, i, off=off: (nxt_row(i), off + j))
    return pl.pallas_call(
        body, name=name, grid=(nf, n_tiles),
        in_specs=[main(0), nxt(0), main(0), nxt(0), main(nf), nxt(nf), main(0),
                  pl.BlockSpec((CONV_HALO, tf), lambda j, i: (jnp.maximum(i * per - 1, 0), j)),
                  pl.BlockSpec((width, tf), lambda j, i: (0, j))],
        out_specs=[main(0), main(0), pl.BlockSpec((width, tf), lambda j, i: (0, j)),
                   pl.BlockSpec((1, tf), lambda j, i: (0, j))],
        out_shape=[jax.ShapeDtypeStruct((s, f), BF16), jax.ShapeDtypeStruct((s, f), BF16),
                   jax.ShapeDtypeStruct((width, f), F32), jax.ShapeDtypeStruct((1, f), F32)],
        compiler_params=_params("parallel", "arbitrary"))(dact, dact, gc, gc, gu, gu, gu, gu, conv_w)


def _rows_layout(bg, n_heads):
    s = bg.shape[0]
    shape = (n_heads, s // GDN_CHUNK, 1, GDN_CHUNK)
    return bg[:, :n_heads].T.reshape(shape), bg[:, n_heads:2 * n_heads].T.reshape(shape)


def _lane_layout(dbeta_rows, dg_rows):
    n_heads = dbeta_rows.shape[0]
    s = dbeta_rows.shape[1] * GDN_CHUNK
    both = jnp.concatenate([dbeta_rows.reshape(n_heads, s), dg_rows.reshape(n_heads, s)], axis=0).T
    return jnp.pad(both, ((0, 0), (0, LANES - 2 * n_heads)))


def _layer_fwd(x, w, late_weights):
    n_heads = w["n_heads"]
    h = _rmsnorm_fwd(x, w["norm_mix_w"], name="norm_mix_fwd")
    proj = _matmul(h, w["w_main"], name="in_proj_fwd")
    pba = _matmul(h, w["w_ba"], name="ba_proj_fwd")
    qkv = _qkv_fwd(proj, w["conv_qkv_w"], n_heads, name="qkv_fwd")
    bg = _ba_fwd(pba, w["alog_row"], w["dtb_row"], n_heads, name="ba_fwd")
    beta_rows, g_rows = _rows_layout(bg, n_heads)
    o, states = _gdn_fwd(qkv, beta_rows, g_rows, name="gdn_fwd")
    w = dict(w, **late_weights(o))
    mixed = _merge_fwd(proj, o, w["gdn_norm_w"], w["pool_w"], w["pool_scale"], name="merge_fwd")
    x2 = _matmul(mixed, w["w_out"], add=x, name="out_proj_fwd")
    h2 = _rmsnorm_fwd(x2, w["norm_ffn_w"], name="norm_ffn_fwd")
    gu = _matmul(h2, w["w_up_slots"], b_slots=True, name="up_proj_fwd")
    act, gc = _ffn_act_fwd(gu, w["conv_ffn_w"], w["conv_ffn_b"], name="ffn_act_fwd")
    x3 = _matmul(act, w["w_down"], add=x2, tk=1408, name="down_proj_fwd")
    saved = dict(x=x, h=h, proj=proj, pba=pba, qkv=qkv, beta_rows=beta_rows, g_rows=g_rows, o=o, states=states,
                 mixed=mixed, x2=x2, h2=h2, gu=gu, gc=gc, act=act)
    return x3, saved, w


def _layer_bwd_ffn(dx3, dx3_bf, w, sv, after):
    g = {}
    dact = _matmul(dx3_bf, w["w_down"], tb=True, tn=1408, after=after, name="down_proj_dx")
    g["w_down"] = _matmul(sv["act"], dx3_bf, ta=True, out_dtype=BF16, name="down_proj_dw")
    dgate, dup, g["conv_ffn_w"], g["conv_ffn_b"] = _ffn_act_bwd(dact, sv["gu"], sv["gc"], w["conv_ffn_w"],
                                                                 name="ffn_act_bwd")
    dgu = jnp.concatenate([dgate, dup], axis=1)
    dh2 = _matmul(dgu, w["w_up_slots"], b_slots=True, tb=True, name="up_proj_dx")
    g["w_up"] = _matmul(sv["h2"], dgu, ta=True, out_dtype=BF16, name="up_proj_dw")
    dx2, dx2_bf, g["norm_ffn_w"] = _rmsnorm_bwd(dh2, sv["x2"], w["norm_ffn_w"], dx3, name="norm_ffn_bwd")
    return dx2, dx2_bf, g


def _layer_bwd_mix(dx2, dx2_bf, w, sv, after):
    n_heads = w["n_heads"]
    g = {}
    dmixed = _matmul(dx2_bf, w["w_out"], tb=True, after=after, name="out_proj_dx")
    g["w_out"] = _matmul(sv["mixed"], dx2_bf, ta=True, out_dtype=BF16, name="out_proj_dw")
    do, dz, dga, dgb, dpooled, g["gdn_norm_w"], g["pool_w"], g["pool_scale"] = _merge_bwd(
        dmixed, sv["proj"], sv["o"], w["gdn_norm_w"], w["pool_w"], w["pool_scale"], name="merge_bwd")
    dp = _pool_bwd(dpooled, name="pool_bwd")
    dqkv, dbeta_rows, dg_rows = _gdn_bwd(do, sv["qkv"], sv["beta_rows"], sv["g_rows"], sv["states"], name="gdn_bwd")
    dproj_qkv, g["conv_qkv_w"] = _qkv_bwd(dqkv, sv["proj"], w["conv_qkv_w"], n_heads, name="qkv_bwd")
    dpba, g["alog_row"], g["dtb_row"] = _ba_bwd(_lane_layout(dbeta_rows, dg_rows), sv["pba"], w["alog_row"],
                                                w["dtb_row"], n_heads, name="ba_bwd")
    dproj = jnp.concatenate([dproj_qkv, dz, dp, dga, dgb], axis=1)
    dh = _matmul(dproj, w["w_main"], tb=True, name="in_proj_dx")
    dh = _matmul(dpba, w["w_ba"], tb=True, add=dh, name="ba_proj_dx")
    g["w_main"] = _matmul(sv["h"], dproj, ta=True, out_dtype=BF16, name="in_proj_dw")
    g["w_ba"] = _matmul(sv["h"], dpba, ta=True, name="ba_proj_dw")
    dx, dx_bf, g["norm_mix_w"] = _rmsnorm_bwd(dh, sv["x"], w["norm_mix_w"], dx2, name="norm_mix_bwd")
    return dx, dx_bf, g


def _here():
    mx, my, mc = (lax.axis_index(a) for a in MESH_AXES)
    return (mx, my, mc), 4 * mx + 2 * my + mc


def _peer(pos, r):
    mx, my, mc = pos
    px = 1 - mx if r & 4 else mx
    py = 1 - my if r & 2 else my
    pc = 1 - mc if r & 1 else mc
    return (px, py, pc), 4 * px + 2 * py + pc


def _run_exchange(n_tensors, src_view, dst_view, sems):
    send_sems, recv_sems, local_sems = sems
    pos, me = _here()
    started = []
    for t in range(n_tensors):
        cp = pltpu.make_async_copy(src_view(t, me), dst_view(t, me), local_sems.at[t])
        cp.start()
        started.append(cp)

    def remote(t, r, landing):
        target, target_lin = _peer(pos, r)
        return pltpu.make_async_remote_copy(
            src_ref=src_view(t, target_lin), dst_ref=dst_view(t, target_lin if landing else me),
            send_sem=send_sems.at[t, r - 1], recv_sem=recv_sems.at[t, r - 1],
            device_id=target, device_id_type=pl.DeviceIdType.MESH)

    sends = []
    for r in range(1, N_DEV):
        for t in range(n_tensors):
            cp = remote(t, r, landing=False)
            cp.start()
            sends.append(cp)
    for r in range(1, N_DEV):
        for t in range(n_tensors):
            remote(t, r, landing=True).wait_recv()
    for cp in sends:
        cp.wait_send()
    for cp in started:
        cp.wait()


def _exchange_scratch(n_tensors):
    return [pltpu.SemaphoreType.DMA((n_tensors, N_DEV - 1)), pltpu.SemaphoreType.DMA((n_tensors, N_DEV - 1)),
            pltpu.SemaphoreType.DMA((n_tensors,))]


def _slot_view(ref, axis, index):
    return ref.at[(slice(None),) * axis + (index,)]


def _gather(srcs, slot_axes, *, name):
    n = len(srcs)

    def body(*refs):
        src_refs, out_refs = refs[:n], refs[n:2 * n]
        _run_exchange(n, lambda t, to: src_refs[t], lambda t, frm: _slot_view(out_refs[t], slot_axes[t], frm),
                      refs[2 * n:])

    hbm = pl.BlockSpec(memory_space=pltpu.HBM)
    out_shape = [jax.ShapeDtypeStruct(s.shape[:a] + (N_DEV,) + s.shape[a:], s.dtype) for s, a in zip(srcs, slot_axes)]
    return pl.pallas_call(body, name=name, in_specs=[hbm] * n, out_specs=[hbm] * n, out_shape=out_shape,
                          scratch_shapes=_exchange_scratch(n))(*srcs)


_SIDE_EFFECT = pltpu.SideEffectType.DATAFLOW_SIDE_EFFECTING


def _split_copy(t, r, pos, src_refs, land_refs, send_sems, recv_sems, src_view, dst_view, landing):
    _, me = _here()
    target, target_lin = _peer(pos, r)
    return pltpu.make_async_remote_copy(
        src_ref=src_view(t, src_refs[t], target_lin), dst_ref=dst_view(t, land_refs[t], target_lin if landing else me),
        send_sem=send_sems.at[t * (N_DEV - 1) + r - 1], recv_sem=recv_sems.at[t * (N_DEV - 1) + r - 1],
        device_id=target, device_id_type=pl.DeviceIdType.MESH)


def _start_exchange(srcs, lands, src_view, dst_view, after, *, name):
    n = len(srcs)
    has_after = after is not None

    def body(*refs):
        src_refs, land_refs = refs[:n], refs[n:2 * n]
        outs = refs[2 * n + has_after:]
        send_sems, recv_sems, token = outs[0], outs[1], outs[2 + 2 * n]
        pos, _ = _here()
        for r in range(1, N_DEV):
            for t in range(n):
                _split_copy(t, r, pos, src_refs, land_refs, send_sems, recv_sems, src_view, dst_view, False).start()
        token[...] = jnp.zeros_like(token)

    hbm = pl.BlockSpec(memory_space=pltpu.HBM)
    sem = pl.BlockSpec(memory_space=pltpu.SEMAPHORE)
    sem_shape = pltpu.SemaphoreType.DMA((n * (N_DEV - 1),))
    through = [pltpu.HBM(t.shape, t.dtype) for t in list(srcs) + list(lands)]
    args = [pltpu.with_memory_space_constraint(t, pltpu.HBM) for t in list(srcs) + list(lands)]
    outs = pl.pallas_call(
        body, name=name, in_specs=[hbm] * (2 * n) + ([pl.BlockSpec(memory_space=pl.ANY)] if has_after else []),
        out_specs=(sem, sem, *[hbm] * (2 * n), pl.BlockSpec(memory_space=pltpu.VMEM)),
        out_shape=(sem_shape, sem_shape, *through, jax.ShapeDtypeStruct((8, LANES), F32)),
        input_output_aliases={i: 2 + i for i in range(2 * n)},
        compiler_params=pltpu.CompilerParams(has_side_effects=_SIDE_EFFECT))(*args, *([after] if has_after else []))
    return outs[0], outs[1], list(outs[2:2 + n]), list(outs[2 + n:2 + 2 * n]), outs[-1]


def _wait_exchange(send_sems, recv_sems, srcs, lands, src_view, dst_view, after, *, name):
    n = len(srcs)

    def body(*refs):
        src_refs, land_refs = refs[:n], refs[n:2 * n]
        send_refs, recv_refs = refs[2 * n], refs[2 * n + 1]
        pos, _ = _here()
        for r in range(1, N_DEV):
            for t in range(n):
                cp = _split_copy(t, r, pos, src_refs, land_refs, send_refs, recv_refs, src_view, dst_view, True)
                cp.wait_send()
                cp.wait_recv()

    hbm = pl.BlockSpec(memory_space=pltpu.HBM)
    sem = pl.BlockSpec(memory_space=pltpu.SEMAPHORE)
    outs = pl.pallas_call(
        body, name=name, in_specs=[hbm] * (2 * n) + [sem, sem, pl.BlockSpec(memory_space=pl.ANY)],
        out_specs=[hbm] * (2 * n), out_shape=[pltpu.HBM(t.shape, t.dtype) for t in list(srcs) + list(lands)],
        input_output_aliases={i: i for i in range(2 * n)},
        compiler_params=pltpu.CompilerParams(has_side_effects=_SIDE_EFFECT))(*srcs, *lands, send_sems, recv_sems, after)
    return list(outs[n:])


def _sum_slots(parts, *, name):
    _, n_lead, r_rows, cols = parts.shape
    tr = _tile(r_rows, max(16, (1 << 17) // cols // 16 * 16), 16)

    def body(p_ref, o_ref):
        total = p_ref[0].astype(F32)
        for p in range(1, N_DEV):
            total = total + p_ref[p].astype(F32)
        o_ref[...] = total

    return pl.pallas_call(
        body, name=name, grid=(n_lead, r_rows // tr),
        in_specs=[pl.BlockSpec((N_DEV, None, tr, cols), lambda a, i: (0, a, i, 0))],
        out_specs=pl.BlockSpec((None, tr, cols), lambda a, i: (a, i, 0)),
        out_shape=jax.ShapeDtypeStruct((n_lead, r_rows, cols), F32),
        compiler_params=_params("parallel", "parallel"))(parts)


_WinLayout = collections.namedtuple("_WinLayout", "shard_w n_main ba_dev ba_off n_ba slot_w")


def _win_layout(shard_w, n_main, ba_start, n_ba):
    ba_dev = ba_start // shard_w
    assert (ba_start + n_ba - 1) // shard_w == ba_dev and n_main % LANES == 0
    slot_w = -(-(LANES - 1 + shard_w) // LANES) * LANES
    return _WinLayout(shard_w, n_main, ba_dev, ba_start - ba_dev * shard_w, n_ba, slot_w)


def _main_start(lay, dev):
    return lay.shard_w * dev - jnp.where(dev > lay.ba_dev, lay.n_ba, 0)


def _slab_origin(lay, dev):
    return jnp.minimum(_main_start(lay, dev) // LANES * LANES, lay.n_main - lay.slot_w)


def _assemble_plan(lay):
    plan = [[] for _ in range(lay.n_main // LANES)]
    for dev in range(N_DEV):
        start = lay.shard_w * dev - (lay.n_ba if dev > lay.ba_dev else 0)
        width = lay.shard_w - (lay.n_ba if dev == lay.ba_dev else 0)
        origin = min(start // LANES, (lay.n_main - lay.slot_w) // LANES)
        pad = start - origin * LANES
        for t in range(pad // LANES, (pad + width - 1) // LANES + 1):
            plan[origin + t].append((dev, t))
    return plan


def _assemble_w_main(slabs, lay, *, name):
    _, d, slot_w = slabs.shape
    plan = _assemble_plan(lay)
    runs = []
    shared = []
    for tile, parts in enumerate(plan):
        if len(parts) != 1:
            shared.append((tile, parts))
        elif runs and runs[-1][2] == parts[0][0] and runs[-1][0] + runs[-1][1] == tile:
            runs[-1][1] += 1
        else:
            runs.append([tile, 1, parts[0][0], parts[0][1]])
    tr = _tile(d, 256, 16)

    def body(in_ref, out_ref):
        for first, count, dev, t0 in runs:
            out_ref[:, first * LANES:(first + count) * LANES] = in_ref[dev, :, t0 * LANES:(t0 + count) * LANES]
        for tile, parts in shared:
            total = in_ref[parts[0][0], :, parts[0][1] * LANES:(parts[0][1] + 1) * LANES]
            for dev, t in parts[1:]:
                total = total + in_ref[dev, :, t * LANES:(t + 1) * LANES]
            out_ref[:, tile * LANES:(tile + 1) * LANES] = total

    return pl.pallas_call(
        body, name=name, grid=(d // tr,),
        in_specs=[pl.BlockSpec((N_DEV, tr, slot_w), lambda i: (0, i, 0))],
        out_specs=pl.BlockSpec((tr, lay.n_main), lambda i: (i, 0)),
        out_shape=jax.ShapeDtypeStruct((d, lay.n_main), slabs.dtype), compiler_params=_params("parallel"))(slabs)


def _adam_update(w, g, m, v):
    nm = ADAM_B1 * m + (1.0 - ADAM_B1) * g
    nv = ADAM_B2 * v + (1.0 - ADAM_B2) * (g * g)
    m_hat = nm / (1.0 - ADAM_B1 ** ADAM_STEP)
    v_hat = nv / (1.0 - ADAM_B2 ** ADAM_STEP)
    return -ADAM_LR * (m_hat / (jnp.sqrt(v_hat) + ADAM_EPS) + ADAM_WD * w), nm, nv


def _adamw(w, g, m, v, *, name):
    rows, cols = w.shape
    tr = _tile(rows, max(8, (1 << 18) // cols // 8 * 8), 8)

    def body(w_ref, g_ref, m_ref, v_ref, d_ref, nm_ref, nv_ref):
        d_ref[...], nm_ref[...], nv_ref[...] = _adam_update(w_ref[...], g_ref[...], m_ref[...], v_ref[...])

    blk = pl.BlockSpec((tr, cols), lambda i: (i, 0))
    out = jax.ShapeDtypeStruct((rows, cols), F32)
    return pl.pallas_call(
        body, name=name, grid=(rows // tr,), in_specs=[blk] * 4, out_specs=[blk] * 3, out_shape=[out] * 3,
        compiler_params=_params("parallel"))(w, g, m, v)


def _adamw_nd(w, g, m, v, *, name):
    two_d = (-1, w.shape[-1])
    outs = _adamw(w.reshape(two_d), g.reshape(two_d), m.reshape(two_d), v.reshape(two_d), name=name)
    return tuple(t.reshape(w.shape) for t in outs)


def _adamw_slots(parts, w, m, v, after, *, name):
    n_layers, rows, cols = w.shape
    tr = _tile(rows, max(16, (1 << 18) // cols // 16 * 16), 16)

    def body(p_ref, w_ref, m_ref, v_ref, after_ref, g_ref, d_ref, nm_ref, nv_ref):
        total = p_ref[0].astype(F32)
        for p in range(1, N_DEV):
            total = total + p_ref[p].astype(F32)
        g_ref[...] = total
        d_ref[...], nm_ref[...], nv_ref[...] = _adam_update(w_ref[...], total, m_ref[...], v_ref[...])

    blk = pl.BlockSpec((None, tr, cols), lambda a, i: (a, i, 0))
    out = jax.ShapeDtypeStruct((n_layers, rows, cols), F32)
    return pl.pallas_call(
        body, name=name, grid=(n_layers, rows // tr),
        in_specs=[pl.BlockSpec((N_DEV, None, tr, cols), lambda a, i: (0, a, i, 0)), blk, blk, blk,
                  pl.BlockSpec(memory_space=pl.ANY)],
        out_specs=[blk] * 4, out_shape=[out] * 4,
        compiler_params=_params("parallel", "parallel"))(parts, w, m, v, after)


def _pack_rows(parts, dtype, quantum_rows):
    flat = jnp.concatenate([p.reshape(-1).astype(dtype) for p in parts])
    n = flat.shape[0]
    padded = -(-n // (LANES * quantum_rows)) * (LANES * quantum_rows)
    return jnp.pad(flat, (0, padded - n)).reshape(padded // LANES, LANES)


def _unpack(flat, shapes):
    lead = flat.shape[:-1]
    out, at = [], 0
    for shape in shapes:
        size = 1
        for dim in shape:
            size *= dim
        out.append(flat[..., at:at + size].reshape(lead + tuple(shape)))
        at += size
    return out


def _whole_from_slots(slots, axis):
    moved = jnp.moveaxis(slots, 0, axis)
    shape = moved.shape
    return moved.reshape(shape[:axis] + (shape[axis] * shape[axis + 1],) + shape[axis + 2:])


def _lane_row(vec, n_heads):
    return jnp.pad(vec, ((0, 0), (n_heads, LANES - 2 * n_heads)))[:, None, :]


REPLICATED = ("norm_mix_w", "a_log", "dt_bias", "gdn_norm_w", "pool_scale", "norm_ffn_w", "conv_ffn_b",
              "norm_final_w")
WEIGHTS = ("norm_mix_w", "w_in", "conv_qkv_w", "a_log", "dt_bias", "gdn_norm_w", "pool_w", "pool_scale", "w_out",
           "norm_ffn_w", "w_up", "conv_ffn_w", "conv_ffn_b", "w_down", "norm_final_w")
SMALL_QUANTUM_ROWS = 512


def kernel(x, norm_mix_w, w_in, conv_qkv_w, a_log, dt_bias, gdn_norm_w, pool_w, pool_scale, w_out, norm_ffn_w, w_up, conv_ffn_w, conv_ffn_b, w_down, norm_final_w, loss_target, m_norm_mix_w, m_w_in, m_conv_qkv_w, m_a_log, m_dt_bias, m_gdn_norm_w, m_pool_w, m_pool_scale, m_w_out, m_norm_ffn_w, m_w_up, m_conv_ffn_w, m_conv_ffn_b, m_w_down, m_norm_final_w, v_norm_mix_w, v_w_in, v_conv_qkv_w, v_a_log, v_dt_bias, v_gdn_norm_w, v_pool_w, v_pool_scale, v_w_out, v_norm_ffn_w, v_w_up, v_conv_ffn_w, v_conv_ffn_b, v_w_down, v_norm_final_w):
    local = dict(norm_mix_w=norm_mix_w, w_in=w_in, conv_qkv_w=conv_qkv_w, a_log=a_log, dt_bias=dt_bias,
                 gdn_norm_w=gdn_norm_w, pool_w=pool_w, pool_scale=pool_scale, w_out=w_out, norm_ffn_w=norm_ffn_w,
                 w_up=w_up, conv_ffn_w=conv_ffn_w, conv_ffn_b=conv_ffn_b, w_down=w_down, norm_final_w=norm_final_w)
    mom_m = dict(norm_mix_w=m_norm_mix_w, w_in=m_w_in, conv_qkv_w=m_conv_qkv_w, a_log=m_a_log, dt_bias=m_dt_bias,
                 gdn_norm_w=m_gdn_norm_w, pool_w=m_pool_w, pool_scale=m_pool_scale, w_out=m_w_out,
                 norm_ffn_w=m_norm_ffn_w, w_up=m_w_up, conv_ffn_w=m_conv_ffn_w, conv_ffn_b=m_conv_ffn_b,
                 w_down=m_w_down, norm_final_w=m_norm_final_w)
    mom_v = dict(norm_mix_w=v_norm_mix_w, w_in=v_w_in, conv_qkv_w=v_conv_qkv_w, a_log=v_a_log, dt_bias=v_dt_bias,
                 gdn_norm_w=v_gdn_norm_w, pool_w=v_pool_w, pool_scale=v_pool_scale, w_out=v_w_out,
                 norm_ffn_w=v_norm_ffn_w, w_up=v_w_up, conv_ffn_w=v_conv_ffn_w, conv_ffn_b=v_conv_ffn_b,
                 w_down=v_w_down, norm_final_w=v_norm_final_w)
    n_layers, n_heads = a_log.shape
    d_model = x.shape[-1]
    dl = n_heads * HEAD_DIM
    n_ba = 2 * n_heads
    assert x.shape[0] == 1 and dl == d_model and pool_scale.shape[1] == d_model
    lay = _win_layout(w_in.shape[2], N_DEV * w_in.shape[2] - n_ba, 4 * dl, n_ba)
    _, me = _here()
    is_ba_dev = me == lay.ba_dev
    my_pad = _main_start(lay, me) - _slab_origin(lay, me)
    ba_cols = slice(lay.ba_off, lay.ba_off + n_ba)

    w_in_bf = w_in.astype(BF16)
    without_ba = jnp.concatenate([w_in_bf[..., :lay.ba_off], w_in_bf[..., lay.ba_off + n_ba:],
                                  jnp.zeros(w_in.shape[:2] + (n_ba,), BF16)], axis=-1)
    slab = lax.dynamic_update_slice(jnp.zeros(w_in.shape[:2] + (lay.slot_w,), BF16),
                                    jnp.where(is_ba_dev, without_ba, w_in_bf), (0, 0, my_pad))
    ba_part = jnp.pad(jnp.where(is_ba_dev, w_in_bf[..., ba_cols], jnp.zeros((), BF16)),
                      ((0, 0), (0, 0), (0, LANES - n_ba)))
    convs = _pack_rows([conv_qkv_w, conv_ffn_w], F32, 16)
    ba_slots, conv_slots = _gather([ba_part, convs], [0, 0], name="gather_small_weights")
    conv_parts = _unpack(conv_slots.reshape(N_DEV, -1), [conv_qkv_w.shape, conv_ffn_w.shape])
    conv_qkv_whole, conv_ffn_whole = (_whole_from_slots(p, 2) for p in conv_parts)
    alog_rows, dtb_rows = _lane_row(a_log, n_heads), _lane_row(dt_bias, n_heads)

    def with_own_slot(own, axis):
        zone = lax.empty(own.shape[:axis] + (N_DEV,) + own.shape[axis:], own.dtype)
        return lax.dynamic_update_slice(zone, jnp.expand_dims(own, axis), (0,) * axis + (me,) + (0,) * (own.ndim - axis))

    slot_axis = dict(slab=0, w_up=0, w_out=0, w_down=0, pool_w=1)
    gather_groups = (("slab",), ("w_up", "w_out", "w_down", "pool_w"))
    gather_src = lambda t, ref, to: ref
    in_flight = {}
    token = conv_slots
    for l in range(n_layers):
        own = dict(slab=slab[l], w_up=w_up[l].astype(BF16), w_out=w_out[l].astype(BF16),
                   w_down=w_down[l].astype(BF16), pool_w=pool_w[l].astype(BF16))
        for part, names in zip("ab", gather_groups):
            axes = [slot_axis[n] for n in names]
            dst = lambda t, ref, frm, axes=axes: _slot_view(ref, axes[t], frm)
            *handles, token = _start_exchange([own[n] for n in names], [with_own_slot(own[n], slot_axis[n]) for n in names],
                                              gather_src, dst, token, name="gather_start_%d%s" % (l, part))
            in_flight[l, part] = (handles, dst)

    def arrived(l, part, after):
        (send_sems, recv_sems, srcs, lands), dst = in_flight[l, part]
        return _wait_exchange(send_sems, recv_sems, srcs, lands, gather_src, dst, after, name="gather_wait_%d%s" % (l, part))

    xc = x[0]
    layer_w, saved = [], []
    after = token
    for l in range(n_layers):
        slabs, = arrived(l, "a", after)
        early = dict(n_heads=n_heads, norm_mix_w=norm_mix_w[l], alog_row=alog_rows[l], dtb_row=dtb_rows[l],
                     w_main=_assemble_w_main(slabs, lay, name="assemble_w_main"), w_ba=ba_slots[lay.ba_dev, l],
                     conv_qkv_w=conv_qkv_whole[l])

        def late_weights(o, l=l):
            up_slots, out_slots, down_slots, pool_slots = arrived(l, "b", o)
            return dict(norm_ffn_w=norm_ffn_w[l], gdn_norm_w=gdn_norm_w[l], pool_scale=pool_scale[l],
                        conv_ffn_b=conv_ffn_b[l], conv_ffn_w=conv_ffn_whole[l], w_up_slots=up_slots,
                        w_out=out_slots.reshape(-1, d_model), w_down=down_slots.reshape(-1, d_model),
                        pool_w=pool_slots.reshape(pool_slots.shape[0], -1, pool_slots.shape[-1]))

        xc, sv, wl = _layer_fwd(xc, early, late_weights)
        layer_w.append(wl)
        saved.append(sv)
        after = xc
    loss_row, dx, dx_bf, d_final = _loss_head(xc, norm_final_w, loss_target[0], name="loss_head")
    loss = lax.psum(loss_row[0, 0], MESH_AXES)

    shard = {n: local[n].shape[1:] for n in ("w_up", "w_out", "w_down", "pool_w")}
    shard["w_main"] = (d_model, lay.slot_w)
    recvs = {n: lax.empty((N_DEV, n_layers) + shard[n], BF16) for n in shard}
    up_w, out_rows, down_rows, pool_rows = shard["w_up"][1], shard["w_out"][0], shard["w_down"][0], shard["pool_w"][1]
    owned = dict(w_main=lambda to: (1, _slab_origin(lay, to), lay.slot_w), w_up=lambda to: (1, to * up_w, up_w),
                 w_out=lambda to: (0, to * out_rows, out_rows), w_down=lambda to: (0, to * down_rows, down_rows),
                 pool_w=lambda to: (1, to * pool_rows, pool_rows))
    scatter_groups = dict(e=("w_up", "w_down"), l=("w_main", "w_out", "pool_w"))
    pending = {}

    def send_grads(part, l, g):
        names = scatter_groups[part]

        def src(t, ref, to):
            axis, first, length = owned[names[t]](to)
            return ref.at[(slice(None),) * axis
                          + (pl.ds(pl.multiple_of(first, LANES if axis == ref.ndim - 1 else 16), length),)]

        dst = lambda t, ref, frm: ref.at[frm, l]
        received(part, g[names[0]])
        lands = []
        for n in names:
            axis, first, length = owned[n](me)
            mine = lax.dynamic_slice_in_dim(g[n], first, length, axis=axis)
            lands.append(lax.dynamic_update_slice(recvs[n], mine[None, None], (me, l) + (0,) * mine.ndim))
        send_sems, recv_sems, grads, lands, token = _start_exchange([g[n] for n in names], lands, src, dst, None,
                                                                    name="scatter_start_%d%s" % (l, part))
        pending[part] = (send_sems, recv_sems, grads, lands, src, dst, "scatter_wait_%d%s" % (l, part))
        return token

    def received(part, after):
        if part in pending:
            *args, name = pending.pop(part)
            recvs.update(zip(scatter_groups[part], _wait_exchange(*args, after=after, name=name)))

    layer_grads = [None] * n_layers
    token = None
    for l in reversed(range(n_layers)):
        dx2, dx2_bf, g = _layer_bwd_ffn(dx, dx_bf, layer_w[l], saved[l], token)
        token = send_grads("e", l, g)
        dx, dx_bf, g_mix = _layer_bwd_mix(dx2, dx2_bf, layer_w[l], saved[l], token)
        g.update(g_mix)
        g["pool_w"] = g["pool_w"].astype(BF16)
        token = send_grads("l", l, g)
        layer_grads[l] = g
    received("e", token)
    grad_x = dx
    stack = lambda name: jnp.stack([g[name] for g in layer_grads])

    small_names = REPLICATED + ("conv_qkv_w", "conv_ffn_w", "w_ba")
    g_small = dict(norm_mix_w=stack("norm_mix_w")[:, 0], a_log=stack("alog_row")[:, 0, n_heads:n_ba],
                   dt_bias=stack("dtb_row")[:, 0, n_heads:n_ba], gdn_norm_w=stack("gdn_norm_w")[:, 0],
                   pool_scale=stack("pool_scale")[:, 0], norm_ffn_w=stack("norm_ffn_w")[:, 0],
                   conv_ffn_b=stack("conv_ffn_b")[:, 0], norm_final_w=d_final[0], conv_qkv_w=stack("conv_qkv_w"),
                   conv_ffn_w=stack("conv_ffn_w"), w_ba=stack("w_ba")[..., :n_ba])
    small_shapes = [g_small[n].shape for n in small_names]
    small_slots, = _gather([_pack_rows([g_small[n] for n in small_names], F32, SMALL_QUANTUM_ROWS)], [0],
                           name="gather_small_grads")
    small_sum = _sum_slots(small_slots[:, None], name="sum_small_grads")
    grad = dict(zip(small_names, _unpack(small_sum.reshape(-1), small_shapes)))
    for n in ("conv_qkv_w", "conv_ffn_w"):
        width = local[n].shape[2]
        grad[n] = lax.dynamic_slice_in_dim(grad[n], me * width, width, axis=2)

    delta, new_m, new_v = {}, {}, {}
    flat = lambda t, lead: t.reshape(t.shape[:lead] + (-1, t.shape[-1]))

    def update_shard(n, after):
        outs = _adamw_slots(flat(recvs[n], 2), flat(local[n], 1), flat(mom_m[n], 1), flat(mom_v[n], 1), after,
                            name="adamw_" + n)
        grad[n], delta[n], new_m[n], new_v[n] = (t.reshape(local[n].shape) for t in outs)

    update_shard("w_up", token)
    update_shard("w_down", token)
    received("l", new_v["w_down"])
    update_shard("w_out", token)
    update_shard("pool_w", token)
    main_sum = _sum_slots(recvs["w_main"], name="sum_w_main_grads")
    g_main = lax.dynamic_slice_in_dim(main_sum, my_pad, lay.shard_w, axis=2)
    with_ba = jnp.concatenate([g_main[..., :lay.ba_off], grad.pop("w_ba"),
                               g_main[..., lay.ba_off:lay.shard_w - n_ba]], axis=-1)
    grad["w_in"] = jnp.where(is_ba_dev, with_ba, g_main)
    for n in ("w_in", "conv_qkv_w", "conv_ffn_w"):
        delta[n], new_m[n], new_v[n] = _adamw_nd(local[n], grad[n], mom_m[n], mom_v[n], name="adamw_" + n)
    packed = [_pack_rows([src[n] for n in REPLICATED], F32, 8) for src in (local, grad, mom_m, mom_v)]
    rep_out = _adamw(*packed, name="adamw_replicated")
    rep_shapes = [local[n].shape for n in REPLICATED]
    for dst, arr in zip((delta, new_m, new_v), rep_out):
        dst.update(zip(REPLICATED, _unpack(arr.reshape(-1), rep_shapes)))

    return (loss, grad_x[None], *[grad[n] for n in WEIGHTS], *[delta[n] for n in WEIGHTS],
            *[new_m[n] for n in WEIGHTS], *[new_v[n] for n in WEIGHTS])
```

```python
import collections

import jax
import jax.numpy as jnp
from jax import lax
from jax.experimental import pallas as pl
from jax.experimental.pallas import tpu as pltpu

F32 = jnp.float32
BF16 = jnp.bfloat16
MESH_AXES = ("x", "y", "c")
N_DEV = 8

NORM_EPS = 1e-6
HEAD_DIM = 128
GDN_CHUNK = 64
GDN_BATCH = 8
GDN_HEADS = 2
POOL_WINDOWS = (2, 4, 8, 16)
POOL_HALO = 16
CONV_HALO = 8
LANES = 128
V7X_VMEM_LIMIT_BYTES = 56 * 1024 * 1024

ADAM_LR = 0.001
ADAM_B1 = 0.9
ADAM_B2 = 0.999
ADAM_EPS = 1e-08
ADAM_WD = 0.01
ADAM_STEP = 10


def _mx(v):
    return v.astype(BF16)


def _dot(a, b, ta=False, tb=False, precision=None):
    dims = (((0 if ta else 1,), (1 if tb else 0,)), ((), ()))
    return lax.dot_general(a, b, dims, precision=precision, preferred_element_type=F32)


def _tile(dim, target, quantum=LANES):
    if dim <= target:
        return dim
    t = (target // quantum) * quantum
    while t >= quantum:
        if dim % t == 0:
            return t
        t -= quantum
    return dim


def _params(*semantics):
    return pltpu.CompilerParams(dimension_semantics=semantics, vmem_limit_bytes=V7X_VMEM_LIMIT_BYTES)


def _sigmoid(v):
    return 1.0 / (1.0 + jnp.exp(-v))


def _softplus(v):
    return jnp.maximum(v, 0.0) + jnp.log(1.0 + jnp.exp(-jnp.abs(v)))


_ERF_NUM = (-2.72614225801306e-10, 2.77068142495902e-08, -2.10102402082508e-06, -5.69250639462346e-05,
            -7.34990630326855e-04, -2.95459980854025e-03, -1.60960333262415e-02)
_ERF_DEN = (-1.45660718464996e-05, -2.13374055278905e-04, -1.68282697438203e-03, -7.37332916720468e-03,
            -1.42647390514189e-02)


def _erf(v):
    v = jnp.clip(v, -4.0, 4.0)
    v2 = v * v
    num = jnp.full_like(v, _ERF_NUM[0])
    for coef in _ERF_NUM[1:]:
        num = num * v2 + coef
    den = jnp.full_like(v, _ERF_DEN[0])
    for coef in _ERF_DEN[1:]:
        den = den * v2 + coef
    return v * num / den


def _gelu(v):
    return 0.5 * v * (1.0 + _erf(v * (2.0 ** -0.5)))


def _gelu_grad(v):
    return 0.5 * (1.0 + _erf(v * (2.0 ** -0.5))) + v * jnp.exp(-0.5 * v * v) * ((2.0 * jnp.pi) ** -0.5)


def _rows_before(cat, shift, halo):
    return pltpu.roll(cat, shift, 0)[halo:]


def _rows_after(cat, shift, rows):
    return pltpu.roll(cat, cat.shape[0] - shift, 0)[:rows]


def _accumulate(ref, value, first):
    @pl.when(first)
    def _():
        ref[...] = value

    @pl.when(jnp.logical_not(first))
    def _():
        ref[...] += value


def _matmul(a, b, *, name, ta=False, tb=False, add=None, out_dtype=F32, tm=512, tn=1024, tk=2048,
            b_slots=False, after=None):
    m, k = (a.shape[1], a.shape[0]) if ta else a.shape
    b_rows, b_cols = (b.shape[1], N_DEV * b.shape[2]) if b_slots else b.shape
    n, kb = (b_rows, b_cols) if tb else (b_cols, b_rows)
    assert kb == k
    if b_slots:
        tn, tk = (tn, b.shape[2]) if tb else (b.shape[2], tk)
    tm, tn, tk = _tile(m, tm), _tile(n, tn), _tile(k, tk)
    nk = k // tk
    has_add = add is not None
    n_in = 2 + has_add + (after is not None)

    def body(*refs):
        a_ref, b_ref = refs[0], refs[1]
        add_ref = refs[2] if has_add else None
        o_ref, acc_ref = refs[n_in], refs[n_in + 1]
        kk = pl.program_id(2)
        part = _dot(_mx(a_ref[...]), _mx(b_ref[...]), ta, tb)

        def finish(total):
            if has_add:
                total = total + add_ref[...]
            o_ref[...] = total.astype(out_dtype)

        if nk == 1:
            finish(part)
        else:
            _accumulate(acc_ref, part, kk == 0)

            @pl.when(kk == nk - 1)
            def _():
                finish(acc_ref[...])

    a_spec = pl.BlockSpec((tk, tm), lambda j, i, kk: (kk, i)) if ta else pl.BlockSpec((tm, tk), lambda j, i, kk: (i, kk))
    b_block = (tn, tk) if tb else (tk, tn)
    if b_slots:
        b_spec = pl.BlockSpec((None,) + b_block, (lambda j, i, kk: (kk, j, 0)) if tb else (lambda j, i, kk: (j, kk, 0)))
    else:
        b_spec = pl.BlockSpec(b_block, (lambda j, i, kk: (j, kk)) if tb else (lambda j, i, kk: (kk, j)))
    o_spec = pl.BlockSpec((tm, tn), lambda j, i, kk: (i, j))
    in_specs = [a_spec, b_spec] + ([o_spec] if has_add else [])
    args = (a, b) + ((add,) if has_add else ())
    if after is not None:
        in_specs.append(pl.BlockSpec(memory_space=pl.ANY))
        args += (after,)
    acc_shape = (tm, tn) if nk > 1 else (8, LANES)
    return pl.pallas_call(
        body, name=name, grid=(n // tn, m // tm, nk), in_specs=in_specs, out_specs=o_spec,
        out_shape=jax.ShapeDtypeStruct((m, n), out_dtype), scratch_shapes=[pltpu.VMEM(acc_shape, F32)],
        compiler_params=_params("parallel", "parallel", "arbitrary"))(*args)


def _rmsnorm_fwd(x, w, *, name):
    s, d = x.shape
    ts = _tile(s, 512, 16)

    def body(x_ref, w_ref, o_ref):
        xf = x_ref[...]
        r = lax.rsqrt(jnp.mean(xf * xf, axis=-1, keepdims=True) + NORM_EPS)
        o_ref[...] = (xf * r * w_ref[...]).astype(BF16)

    return pl.pallas_call(
        body, name=name, grid=(s // ts,),
        in_specs=[pl.BlockSpec((ts, d), lambda i: (i, 0)), pl.BlockSpec((1, d), lambda i: (0, 0))],
        out_specs=pl.BlockSpec((ts, d), lambda i: (i, 0)),
        out_shape=jax.ShapeDtypeStruct((s, d), BF16), compiler_params=_params("parallel"))(x, w.reshape(1, d))


def _rmsnorm_bwd(dy, x, w, dres, *, name):
    s, d = x.shape
    ts = _tile(s, 256, 16)

    def body(dy_ref, x_ref, w_ref, dres_ref, dx_ref, dxb_ref, dw_ref):
        xf = x_ref[...]
        dyf = dy_ref[...]
        r = lax.rsqrt(jnp.mean(xf * xf, axis=-1, keepdims=True) + NORM_EPS)
        xh = xf * r
        dxh = dyf * w_ref[...]
        dx = dres_ref[...] + r * (dxh - xh * jnp.mean(dxh * xh, axis=-1, keepdims=True))
        dx_ref[...] = dx
        dxb_ref[...] = dx.astype(BF16)
        _accumulate(dw_ref, jnp.sum(dyf * xh, axis=0, keepdims=True), pl.program_id(0) == 0)

    row = pl.BlockSpec((ts, d), lambda i: (i, 0))
    vec = pl.BlockSpec((1, d), lambda i: (0, 0))
    return pl.pallas_call(
        body, name=name, grid=(s // ts,), in_specs=[row, row, vec, row], out_specs=[row, row, vec],
        out_shape=[jax.ShapeDtypeStruct((s, d), F32), jax.ShapeDtypeStruct((s, d), BF16),
                   jax.ShapeDtypeStruct((1, d), F32)],
        compiler_params=_params("arbitrary"))(dy, x, w.reshape(1, d), dres)


def _loss_head(x, w, target, *, name):
    s, d = x.shape
    ts = _tile(s, 256, 16)

    def body(x_ref, w_ref, t_ref, loss_ref, dx_ref, dxb_ref, dw_ref):
        first = pl.program_id(0) == 0
        xf = x_ref[...]
        wv = w_ref[...]
        r = lax.rsqrt(jnp.mean(xf * xf, axis=-1, keepdims=True) + NORM_EPS)
        xh = xf * r
        err = xh * wv - t_ref[...]
        part = 0.5 * jnp.sum(jnp.mean(err * err, axis=-1, keepdims=True), axis=0, keepdims=True)
        _accumulate(loss_ref, jnp.broadcast_to(part, (1, LANES)), first)
        dyf = err * (1.0 / d)
        dxh = dyf * wv
        dx = r * (dxh - xh * jnp.mean(dxh * xh, axis=-1, keepdims=True))
        dx_ref[...] = dx
        dxb_ref[...] = dx.astype(BF16)
        _accumulate(dw_ref, jnp.sum(dyf * xh, axis=0, keepdims=True), first)

    row = pl.BlockSpec((ts, d), lambda i: (i, 0))
    vec = pl.BlockSpec((1, d), lambda i: (0, 0))
    return pl.pallas_call(
        body, name=name, grid=(s // ts,), in_specs=[row, vec, row],
        out_specs=[pl.BlockSpec((1, LANES), lambda i: (0, 0)), row, row, vec],
        out_shape=[jax.ShapeDtypeStruct((1, LANES), F32), jax.ShapeDtypeStruct((s, d), F32),
                   jax.ShapeDtypeStruct((s, d), BF16), jax.ShapeDtypeStruct((1, d), F32)],
        compiler_params=_params("arbitrary"))(x, w.reshape(1, d), target)


def _qkv_fwd(proj, conv_w, n_heads, *, name):
    s = proj.shape[0]
    dl = n_heads * HEAD_DIM
    width = conv_w.shape[0]
    ts = _tile(s, 512, 8)

    def body(x_ref, halo_ref, w_ref, o_ref):
        i, sec = pl.program_id(0), pl.program_id(1)
        xv = x_ref[...]
        cat = jnp.concatenate([jnp.where(i > 0, halo_ref[...], 0.0), xv], axis=0)
        c = xv * w_ref[pl.ds(width - 1, 1), :]
        for sh in range(1, width):
            c = c + _rows_before(cat, sh, CONV_HALO) * w_ref[pl.ds(width - 1 - sh, 1), :]
        act = c * _sigmoid(c)

        @pl.when(sec == 2)
        def _():
            o_ref[...] = act

        @pl.when(sec < 2)
        def _():
            scale = jnp.where(sec == 0, HEAD_DIM ** -0.5, 1.0)
            for h in range(n_heads):
                cols = slice(h * HEAD_DIM, (h + 1) * HEAD_DIM)
                ah = act[:, cols]
                o_ref[:, cols] = ah * lax.rsqrt(jnp.sum(ah * ah, axis=-1, keepdims=True) + NORM_EPS) * scale

    return pl.pallas_call(
        body, name=name, grid=(s // ts, 3),
        in_specs=[pl.BlockSpec((ts, dl), lambda i, sec: (i, sec)),
                  pl.BlockSpec((CONV_HALO, dl), lambda i, sec: (jnp.maximum(i * (ts // CONV_HALO) - 1, 0), sec)),
                  pl.BlockSpec((width, dl), lambda i, sec: (0, sec))],
        out_specs=pl.BlockSpec((None, ts, dl), lambda i, sec: (sec, i, 0)),
        out_shape=jax.ShapeDtypeStruct((3, s, dl), F32),
        compiler_params=_params("parallel", "parallel"))(proj, proj, conv_w)


def _qkv_bwd(dqkv, proj, conv_w, n_heads, *, name):
    s = proj.shape[0]
    dl = n_heads * HEAD_DIM
    width = conv_w.shape[0]
    ts = _tile(s, 256, 16)
    n_tiles = s // ts
    per = ts // CONV_HALO
    rows = ts + CONV_HALO

    def body(d_ref, dnext_ref, x_ref, xprev_ref, xnext_ref, w_ref, dx_ref, dw_ref):
        sec, i = pl.program_id(0), pl.program_id(1)
        xv = x_ref[...]
        cat = jnp.concatenate([jnp.where(i > 0, xprev_ref[...], 0.0), xv, xnext_ref[...]], axis=0)
        shifted = [cat[CONV_HALO:]] + [_rows_before(cat, sh, CONV_HALO) for sh in range(1, width)]
        c = shifted[0] * w_ref[pl.ds(width - 1, 1), :]
        for sh in range(1, width):
            c = c + shifted[sh] * w_ref[pl.ds(width - 1 - sh, 1), :]
        sig = _sigmoid(c)
        act = c * sig
        dout = jnp.concatenate([d_ref[...], dnext_ref[...]], axis=0)
        scale = jnp.where(sec == 0, HEAD_DIM ** -0.5, 1.0)
        is_v = sec == 2
        pieces = []
        for h in range(n_heads):
            cols = slice(h * HEAD_DIM, (h + 1) * HEAD_DIM)
            ah, dh = act[:, cols], dout[:, cols]
            nrm = lax.rsqrt(jnp.sum(ah * ah, axis=-1, keepdims=True) + NORM_EPS)
            dnormed = scale * nrm * (dh - ah * (nrm * nrm) * jnp.sum(dh * ah, axis=-1, keepdims=True))
            pieces.append(jnp.where(is_v, dh, dnormed))
        dact = jnp.concatenate(pieces, axis=1)
        dc = dact * sig * (1.0 + c * (1.0 - sig))
        live = jnp.logical_or(lax.broadcasted_iota(jnp.int32, (rows, 1), 0) < ts, i < n_tiles - 1)
        dc = jnp.where(live, dc, 0.0)
        dx = dc[:ts] * w_ref[pl.ds(width - 1, 1), :]
        for sh in range(1, width):
            dx = dx + _rows_after(dc, sh, ts) * w_ref[pl.ds(width - 1 - sh, 1), :]
        dx_ref[...] = dx.astype(BF16)
        dw_rows = [jnp.sum(dc[:ts] * shifted[width - 1 - j][:ts], axis=0, keepdims=True) for j in range(width)]
        _accumulate(dw_ref, jnp.concatenate(dw_rows, axis=0), i == 0)

    return pl.pallas_call(
        body, name=name, grid=(3, n_tiles),
        in_specs=[pl.BlockSpec((None, ts, dl), lambda sec, i: (sec, i, 0)),
                  pl.BlockSpec((None, CONV_HALO, dl), lambda sec, i: (sec, jnp.minimum((i + 1) * per, s // CONV_HALO - 1), 0)),
                  pl.BlockSpec((ts, dl), lambda sec, i: (i, sec)),
                  pl.BlockSpec((CONV_HALO, dl), lambda sec, i: (jnp.maximum(i * per - 1, 0), sec)),
                  pl.BlockSpec((CONV_HALO, dl), lambda sec, i: (jnp.minimum((i + 1) * per, s // CONV_HALO - 1), sec)),
                  pl.BlockSpec((width, dl), lambda sec, i: (0, sec))],
        out_specs=[pl.BlockSpec((ts, dl), lambda sec, i: (i, sec)), pl.BlockSpec((width, dl), lambda sec, i: (0, sec))],
        out_shape=[jax.ShapeDtypeStruct((s, 3 * dl), BF16), jax.ShapeDtypeStruct((width, 3 * dl), F32)],
        compiler_params=_params("parallel", "arbitrary"))(dqkv, dqkv, proj, proj, proj, conv_w)


def _ba_fwd(pba, alog_row, dtb_row, n_heads, *, name):
    s = pba.shape[0]
    ts = _tile(s, 1024, 8)

    def body(x_ref, alog_ref, dtb_ref, o_ref):
        xv = x_ref[...]
        lane = lax.broadcasted_iota(jnp.int32, xv.shape, 1)
        g = -jnp.exp(alog_ref[...]) * _softplus(xv + dtb_ref[...])
        o_ref[...] = jnp.where(lane < n_heads, _sigmoid(xv), jnp.where(lane < 2 * n_heads, g, 0.0))

    row = pl.BlockSpec((ts, LANES), lambda i: (i, 0))
    vec = pl.BlockSpec((1, LANES), lambda i: (0, 0))
    return pl.pallas_call(
        body, name=name, grid=(s // ts,), in_specs=[row, vec, vec], out_specs=row,
        out_shape=jax.ShapeDtypeStruct((s, LANES), F32), compiler_params=_params("parallel"))(pba, alog_row, dtb_row)


def _ba_bwd(dbg, pba, alog_row, dtb_row, n_heads, *, name):
    s = pba.shape[0]
    ts = _tile(s, 1024, 16)

    def body(d_ref, x_ref, alog_ref, dtb_ref, dx_ref, dalog_ref, ddtb_ref):
        first = pl.program_id(0) == 0
        xv, dv = x_ref[...], d_ref[...]
        lane = lax.broadcasted_iota(jnp.int32, xv.shape, 1)
        beta = _sigmoid(xv)
        neg_a = -jnp.exp(alog_ref[...])
        xa = xv + dtb_ref[...]
        is_a = jnp.logical_and(lane >= n_heads, lane < 2 * n_heads)
        d_xa = jnp.where(is_a, dv * neg_a * _sigmoid(xa), 0.0)
        d_g_times_g = jnp.where(is_a, dv * neg_a * _softplus(xa), 0.0)
        dx_ref[...] = jnp.where(lane < n_heads, dv * beta * (1.0 - beta), d_xa).astype(BF16)
        _accumulate(dalog_ref, jnp.sum(d_g_times_g, axis=0, keepdims=True), first)
        _accumulate(ddtb_ref, jnp.sum(d_xa, axis=0, keepdims=True), first)

    row = pl.BlockSpec((ts, LANES), lambda i: (i, 0))
    vec = pl.BlockSpec((1, LANES), lambda i: (0, 0))
    return pl.pallas_call(
        body, name=name, grid=(s // ts,), in_specs=[row, row, vec, vec], out_specs=[row, vec, vec],
        out_shape=[jax.ShapeDtypeStruct((s, LANES), BF16), jax.ShapeDtypeStruct((1, LANES), F32),
                   jax.ShapeDtypeStruct((1, LANES), F32)],
        compiler_params=_params("arbitrary"))(dbg, pba, alog_row, dtb_row)


def _bdot(a, b, ta=False, tb=False, precision=None):
    dims = (((1 if ta else 2,), (2 if tb else 1,)), ((0,), (0,)))
    return lax.dot_general(a, b, dims, precision=precision, preferred_element_type=F32)


def _split_bf16(v):
    hi = v.astype(BF16)
    return hi, (v - hi.astype(F32)).astype(BF16)


def _bdot_x3(a, b, ta=False, tb=False):
    return _bdot(a[0], b[0], ta, tb) + (_bdot(a[0], b[1], ta, tb) + _bdot(a[1], b[0], ta, tb))


def _chunk_masks():
    ri = lax.broadcasted_iota(jnp.int32, (GDN_CHUNK, GDN_CHUNK), 0)
    ci = lax.broadcasted_iota(jnp.int32, (GDN_CHUNK, GDN_CHUNK), 1)
    return ri == ci, ri >= ci, ri > ci, ri <= ci


def _row_to_col(row, eye):
    return jnp.sum(jnp.where(eye, row, 0.0), axis=2, keepdims=True)


def _col_to_row(col, eye):
    return jnp.sum(jnp.where(eye, col, 0.0), axis=1, keepdims=True)


_Gates = collections.namedtuple("_Gates", "beta_col decay e_col f_col dec")


def _gdn_gates(beta_row, g_row):
    eye, tril, _, triu = _chunk_masks()
    g_col = _row_to_col(g_row, eye)
    gc_col = jnp.sum(jnp.where(tril, g_row, 0.0), axis=2, keepdims=True)
    gc_row = jnp.sum(jnp.where(triu, g_col, 0.0), axis=1, keepdims=True)
    g_last = jnp.sum(g_row, axis=2, keepdims=True)
    decay = jnp.exp(jnp.where(tril, gc_col - gc_row, -jnp.inf))
    return _Gates(_row_to_col(beta_row, eye), decay, jnp.exp(gc_col), jnp.exp(g_last - gc_col), jnp.exp(g_last))


def _unit_lower_inverse(lmat):
    c = GDN_CHUNK
    t = jnp.where(_chunk_masks()[0], 1.0, 0.0) - lmat
    l_parts = _split_bf16(lmat)
    p = _bdot_x3(l_parts, l_parts)
    doublings = c.bit_length() - 2
    for r in range(doublings):
        p_parts = _split_bf16(p)
        if r < doublings - 1:
            both = _bdot_x3(_split_bf16(jnp.concatenate([t, p], axis=1)), p_parts)
            t, p = t + both[:, :c], both[:, c:]
        else:
            t = t + _bdot_x3(_split_bf16(t), p_parts)
    return t


def _gdn_solve(q, k, v, gates):
    strict = _chunk_masks()[2]
    kb = k * gates.beta_col
    lmat = jnp.where(strict, _bdot(_mx(kb), _mx(k), tb=True) * gates.decay, 0.0)
    tmat = _unit_lower_inverse(lmat)
    sol = _bdot_x3(_split_bf16(tmat), _split_bf16(jnp.concatenate([v * gates.beta_col, kb * gates.e_col], axis=2)))
    at = _bdot(_mx(q), _mx(k), tb=True) * gates.decay
    return lmat, tmat, sol, at


def _gdn_blocking(s):
    n_chunks = s // GDN_CHUNK
    per_step = 16 if n_chunks % 16 == 0 else n_chunks
    assert per_step % GDN_BATCH == 0
    return n_chunks, per_step, n_chunks // per_step


def _load_chunks(ref, sec, n0, hp):
    r0 = pl.multiple_of(n0 * GDN_CHUNK, GDN_BATCH * GDN_CHUNK)
    rows = pl.ds(r0, GDN_BATCH * GDN_CHUNK)
    cols = slice(hp * HEAD_DIM, (hp + 1) * HEAD_DIM)
    val = ref[rows, cols] if sec is None else ref[sec, rows, cols]
    return val.reshape(GDN_BATCH, GDN_CHUNK, HEAD_DIM)


def _store_chunks(ref, sec, n0, hp, val):
    r0 = pl.multiple_of(n0 * GDN_CHUNK, GDN_BATCH * GDN_CHUNK)
    rows = pl.ds(r0, GDN_BATCH * GDN_CHUNK)
    cols = slice(hp * HEAD_DIM, (hp + 1) * HEAD_DIM)
    flat = val.reshape(GDN_BATCH * GDN_CHUNK, HEAD_DIM)
    if sec is None:
        ref[rows, cols] = flat
    else:
        ref[sec, rows, cols] = flat


def _gdn_specs(n_heads, n_steps, per_step, order):
    rows, width = per_step * GDN_CHUNK, GDN_HEADS * HEAD_DIM
    rowvec = pl.BlockSpec((GDN_HEADS, per_step, 1, GDN_CHUNK), lambda h, j: (h, order(j), 0, 0))
    qkv = pl.BlockSpec((3, rows, width), lambda h, j: (0, order(j), h))
    act = pl.BlockSpec((rows, width), lambda h, j: (order(j), h))
    states = pl.BlockSpec((GDN_HEADS, per_step, HEAD_DIM, HEAD_DIM), lambda h, j: (h, order(j), 0, 0))
    return rowvec, qkv, act, states


def _gdn_fwd(qkv, beta_rows, g_rows, *, name):
    _, s, dl = qkv.shape
    n_heads = dl // HEAD_DIM
    c = GDN_CHUNK
    n_chunks, per_step, n_steps = _gdn_blocking(s)
    n_groups = per_step // GDN_BATCH
    heads = range(GDN_HEADS)

    def body(qkv_ref, b_ref, g_ref, o_ref, st_ref, state_ref, sol_s, at_s, qd_s, kmat_s, nmat_s, dec_s):
        @pl.when(pl.program_id(1) == 0)
        def _():
            state_ref[...] = jnp.zeros_like(state_ref)

        def solve(gi, carry):
            n0 = gi * GDN_BATCH
            grp = pl.ds(n0, GDN_BATCH)
            for hp in heads:
                q, k, v = (_load_chunks(qkv_ref, j, n0, hp) for j in range(3))
                gates = _gdn_gates(b_ref[hp, grp], g_ref[hp, grp])
                _, _, sol, at = _gdn_solve(q, k, v, gates)
                mke = _mx(k * gates.f_col)
                sol_s[hp, grp] = sol
                at_s[hp, grp] = at
                qd_s[hp, grp] = q * gates.e_col
                nmat_s[hp, grp] = _bdot(mke, _mx(sol[:, :, :HEAD_DIM]), ta=True)
                kmat_s[hp, grp] = _bdot(mke, _mx(sol[:, :, HEAD_DIM:]), ta=True)
                dec_s[hp, grp] = jnp.broadcast_to(gates.dec, (GDN_BATCH, 1, LANES))
            return carry

        lax.fori_loop(0, n_groups, solve, 0)

        def recur(n, states):
            out = []
            for hp in heads:
                state = states[hp]
                st_ref[hp, n] = state
                out.append(state * dec_s[hp, n] + nmat_s[hp, n] - _dot(_mx(kmat_s[hp, n]), _mx(state)))
            return tuple(out)

        final = lax.fori_loop(0, per_step, recur, tuple(state_ref[hp] for hp in heads))
        for hp in heads:
            state_ref[hp] = final[hp]

        def emit(gi, carry):
            n0 = gi * GDN_BATCH
            grp = pl.ds(n0, GDN_BATCH)
            for hp in heads:
                sol, mstate = sol_s[hp, grp], _mx(st_ref[hp, grp])
                v_new = sol[:, :, :HEAD_DIM] - _bdot(_mx(sol[:, :, HEAD_DIM:]), mstate)
                o = _bdot(_mx(qd_s[hp, grp]), mstate) + _bdot(_mx(at_s[hp, grp]), _mx(v_new))
                _store_chunks(o_ref, None, n0, hp, o)
            return carry

        lax.fori_loop(0, n_groups, emit, 0)

    rowvec, qkv_spec, act_spec, st_spec = _gdn_specs(n_heads, n_steps, per_step, lambda j: j)
    wide = lambda w: pltpu.VMEM((GDN_HEADS, per_step, c, w), F32)
    square = pltpu.VMEM((GDN_HEADS, per_step, HEAD_DIM, HEAD_DIM), F32)
    return pl.pallas_call(
        body, name=name, grid=(n_heads // GDN_HEADS, n_steps),
        in_specs=[qkv_spec, rowvec, rowvec], out_specs=[act_spec, st_spec],
        out_shape=[jax.ShapeDtypeStruct((s, dl), F32),
                   jax.ShapeDtypeStruct((n_heads, n_chunks, HEAD_DIM, HEAD_DIM), F32)],
        scratch_shapes=[pltpu.VMEM((GDN_HEADS, HEAD_DIM, HEAD_DIM), F32), wide(2 * HEAD_DIM), wide(c), wide(HEAD_DIM),
                        square, square, pltpu.VMEM((GDN_HEADS, per_step, 1, LANES), F32)],
        compiler_params=_params("parallel", "arbitrary"))(qkv, beta_rows, g_rows)


def _gdn_bwd(do, qkv, beta_rows, g_rows, states, *, name):
    _, s, dl = qkv.shape
    n_heads = dl // HEAD_DIM
    c = GDN_CHUNK
    n_chunks, per_step, n_steps = _gdn_blocking(s)
    n_groups = per_step // GDN_BATCH
    heads = range(GDN_HEADS)

    def body(do_ref, qkv_ref, b_ref, g_ref, st_ref, dqkv_ref, db_ref, dg_ref,
             dstate_ref, lmat_s, tmat_s, at_s, dat_s, sol_s, vn_s, dvn_s, dqd_s, kmat_s, nmat_s, dst_s, dec_s):
        @pl.when(pl.program_id(1) == 0)
        def _():
            dstate_ref[...] = jnp.zeros_like(dstate_ref)

        eye, tril, strict, _ = _chunk_masks()

        def solve(gi, carry):
            n0 = gi * GDN_BATCH
            grp = pl.ds(n0, GDN_BATCH)
            for hp in heads:
                q, k, v = (_load_chunks(qkv_ref, j, n0, hp) for j in range(3))
                gates = _gdn_gates(b_ref[hp, grp], g_ref[hp, grp])
                lmat, tmat, sol, at = _gdn_solve(q, k, v, gates)
                mstate = _mx(st_ref[hp, grp])
                md_o = _mx(_load_chunks(do_ref, None, n0, hp))
                mwc = _mx(sol[:, :, HEAD_DIM:])
                v_new = sol[:, :, :HEAD_DIM] - _bdot(mwc, mstate)
                dv_new0 = _bdot(_mx(at), md_o, ta=True)
                lmat_s[hp, grp] = lmat
                tmat_s[hp, grp] = tmat
                sol_s[hp, grp] = sol
                at_s[hp, grp] = at
                vn_s[hp, grp] = v_new
                dat_s[hp, grp] = jnp.where(tril, _bdot(md_o, _mx(v_new), tb=True), 0.0)
                dvn_s[hp, grp] = dv_new0
                dqd_s[hp, grp] = _bdot(md_o, mstate, tb=True)
                nmat_s[hp, grp] = (_bdot(_mx(q * gates.e_col), md_o, ta=True) - _bdot(mwc, _mx(dv_new0), ta=True))
                kmat_s[hp, grp] = _bdot(mwc, _mx(k * gates.f_col), ta=True)
                dec_s[hp, grp] = jnp.broadcast_to(gates.dec, (GDN_BATCH, 1, LANES))
            return carry

        lax.fori_loop(0, n_groups, solve, 0)

        def recur(idx, dstates):
            n = per_step - 1 - idx
            out = []
            for hp in heads:
                dstate = dstates[hp]
                dst_s[hp, n] = dstate
                out.append(dstate * dec_s[hp, n] + nmat_s[hp, n] - _dot(_mx(kmat_s[hp, n]), _mx(dstate)))
            return tuple(out)

        final = lax.fori_loop(0, per_step, recur, tuple(dstate_ref[hp] for hp in heads))
        for hp in heads:
            dstate_ref[hp] = final[hp]

        def emit_head(hp, n0):
            grp = pl.ds(n0, GDN_BATCH)
            q, k, v = (_load_chunks(qkv_ref, j, n0, hp) for j in range(3))
            gates = _gdn_gates(b_ref[hp, grp], g_ref[hp, grp])
            state, dstate = st_ref[hp, grp], dst_s[hp, grp]
            lmat, at, dat, sol = lmat_s[hp, grp], at_s[hp, grp], dat_s[hp, grp], sol_s[hp, grp]
            v_new, dqd = vn_s[hp, grp], dqd_s[hp, grp]
            dv_new = dvn_s[hp, grp] + _bdot(_mx(k * gates.f_col), _mx(dstate))
            dke = _bdot(_mx(v_new), _mx(dstate), tb=True)
            dwc = -_bdot(_mx(dv_new), _mx(state), tb=True)
            ddec = jnp.sum(jnp.sum(dstate * state, axis=2, keepdims=True), axis=1, keepdims=True)
            drhs = _bdot_x3(_split_bf16(tmat_s[hp, grp]), _split_bf16(jnp.concatenate([dv_new, dwc], axis=2)), ta=True)
            dvb, dkbe = drhs[:, :, :HEAD_DIM], drhs[:, :, HEAD_DIM:]
            dl_mat = jnp.where(strict, -_bdot(_mx(drhs), _mx(sol), tb=True), 0.0)
            dkk = dl_mat * gates.decay
            dqk = dat * gates.decay
            kb = k * gates.beta_col
            mk = _mx(k)
            dkb = _bdot(_mx(dkk), mk) + dkbe * gates.e_col
            dq = _bdot(_mx(dqk), mk) + dqd * gates.e_col
            dk = (_bdot(_mx(dqk), _mx(q), ta=True) + _bdot(_mx(dkk), _mx(kb), ta=True) + dke * gates.f_col
                  + dkb * gates.beta_col)
            _store_chunks(dqkv_ref, 0, n0, hp, dq)
            _store_chunks(dqkv_ref, 1, n0, hp, dk)
            _store_chunks(dqkv_ref, 2, n0, hp, dvb * gates.beta_col)
            dbeta_col = jnp.sum(dkb * k + dvb * v, axis=2, keepdims=True)
            through_decay = dl_mat * lmat + dat * at
            dke_ke = jnp.sum(dke * (k * gates.f_col), axis=2, keepdims=True)
            dgc_col = (jnp.sum(through_decay, axis=2, keepdims=True)
                       - _row_to_col(jnp.sum(through_decay, axis=1, keepdims=True), eye)
                       + jnp.sum(dqd * (q * gates.e_col) + dkbe * (kb * gates.e_col), axis=2, keepdims=True) - dke_ke)
            dg_last = jnp.sum(dke_ke, axis=1, keepdims=True) + ddec * gates.dec
            db_ref[hp, grp] = _col_to_row(dbeta_col, eye)
            dg_ref[hp, grp] = jnp.sum(jnp.where(tril, dgc_col, 0.0), axis=1, keepdims=True) + dg_last

        def emit(gi, carry):
            for hp in heads:
                emit_head(hp, gi * GDN_BATCH)
            return carry

        lax.fori_loop(0, n_groups, emit, 0)

    rowvec, qkv_spec, act_spec, st_spec = _gdn_specs(n_heads, n_steps, per_step, lambda j: n_steps - 1 - j)
    wide = lambda w: pltpu.VMEM((GDN_HEADS, per_step, c, w), F32)
    square = pltpu.VMEM((GDN_HEADS, per_step, HEAD_DIM, HEAD_DIM), F32)
    return pl.pallas_call(
        body, name=name, grid=(n_heads // GDN_HEADS, n_steps),
        in_specs=[act_spec, qkv_spec, rowvec, rowvec, st_spec], out_specs=[qkv_spec, rowvec, rowvec],
        out_shape=[jax.ShapeDtypeStruct((3, s, dl), F32),
                   jax.ShapeDtypeStruct((n_heads, n_chunks, 1, c), F32),
                   jax.ShapeDtypeStruct((n_heads, n_chunks, 1, c), F32)],
        scratch_shapes=[pltpu.VMEM((GDN_HEADS, HEAD_DIM, HEAD_DIM), F32), wide(c), wide(c), wide(c), wide(c),
                        wide(2 * HEAD_DIM), wide(HEAD_DIM), wide(HEAD_DIM), wide(HEAD_DIM),
                        square, square, square, pltpu.VMEM((GDN_HEADS, per_step, 1, LANES), F32)],
        compiler_params=_params("parallel", "arbitrary"))(do, qkv, beta_rows, g_rows, states)


def _pool_counts(tile, ts, extra, win):
    t = tile * ts + lax.broadcasted_iota(jnp.int32, (ts + extra, 1), 0)
    return jnp.minimum(t + 1, win).astype(F32)


def _pooled(cat, p_cols, tile, ts, win):
    acc, span = cat, 1
    while span < win:
        acc = acc + pltpu.roll(acc, span, 0)
        span *= 2
    return acc[POOL_HALO:] / _pool_counts(tile, ts, 0, win) - p_cols


def _merge_fwd(proj, o, gnw, pool_w, pool_scale, *, name):
    s, d = o.shape
    n_heads = d // HEAD_DIM
    n_groups, pg = pool_w.shape[0], pool_w.shape[1]
    assert n_groups == len(POOL_WINDOWS) and n_groups * pg == d and pg % HEAD_DIM == 0
    heads_per_group = pg // HEAD_DIM
    ts = _tile(s, 256, 16)

    def body(o_ref, z_ref, p_ref, halo_ref, ga_ref, gb_ref, gnw_ref, pw_ref, ps_ref, out_ref):
        i = pl.program_id(0)
        gnw_v = gnw_ref[...]
        halo = jnp.where(i > 0, halo_ref[...], 0.0)
        for gi, win in enumerate(POOL_WINDOWS):
            gcols = slice(gi * pg, (gi + 1) * pg)
            pv = p_ref[:, gcols]
            pooled = _pooled(jnp.concatenate([halo[:, gcols], pv], axis=0), pv, i, ts, win)
            yb = _dot(_mx(pooled), pw_ref[gi]) * ps_ref[:, gcols]
            for h in range(gi * heads_per_group, (gi + 1) * heads_per_group):
                cols = slice(h * HEAD_DIM, (h + 1) * HEAD_DIM)
                in_group = slice(h * HEAD_DIM - gi * pg, (h + 1) * HEAD_DIM - gi * pg)
                oh, zh = o_ref[:, cols], z_ref[:, cols]
                r = lax.rsqrt(jnp.mean(oh * oh, axis=-1, keepdims=True) + NORM_EPS)
                ya = oh * r * gnw_v * (zh * _sigmoid(zh))
                out_ref[:, cols] = (_sigmoid(ga_ref[:, cols]) * ya
                                    + _sigmoid(gb_ref[:, cols]) * yb[:, in_group]).astype(BF16)

    blk = lambda col: pl.BlockSpec((ts, d), lambda i, col=col: (i, col))
    vec = lambda width: pl.BlockSpec((1, width), lambda i: (0, 0))
    return pl.pallas_call(
        body, name=name, grid=(s // ts,),
        in_specs=[blk(0), blk(3), blk(4),
                  pl.BlockSpec((POOL_HALO, d), lambda i: (jnp.maximum(i * (ts // POOL_HALO) - 1, 0), 4)),
                  blk(5), blk(6), vec(HEAD_DIM), pl.BlockSpec((n_groups, pg, pg), lambda i: (0, 0, 0)), vec(d)],
        out_specs=blk(0), out_shape=jax.ShapeDtypeStruct((s, d), BF16),
        compiler_params=_params("parallel"))(o, proj, proj, proj, proj, proj, gnw.reshape(1, HEAD_DIM), pool_w,
                                              pool_scale.reshape(1, d))


def _merge_bwd(dmixed, proj, o, gnw, pool_w, pool_scale, *, name):
    s, d = o.shape
    n_heads = d // HEAD_DIM
    n_groups, pg = pool_w.shape[0], pool_w.shape[1]
    ts = _tile(s, 256, 16)

    def body(dm_ref, o_ref, z_ref, p_ref, halo_ref, ga_ref, gb_ref, gnw_ref, pw_ref, ps_ref,
             do_ref, dz_ref, dga_ref, dgb_ref, dpl_ref, dgnw_ref, dpw_ref, dps_ref):
        i = pl.program_id(0)
        first = i == 0
        gnw_v = gnw_ref[...]
        dgnw = jnp.zeros((1, HEAD_DIM), F32)
        for h in range(n_heads):
            cols = slice(h * HEAD_DIM, (h + 1) * HEAD_DIM)
            oh, zh, dm = o_ref[:, cols], z_ref[:, cols], dm_ref[:, cols]
            r = lax.rsqrt(jnp.mean(oh * oh, axis=-1, keepdims=True) + NORM_EPS)
            xh = oh * r
            sz = _sigmoid(zh)
            silu_z = zh * sz
            sa = _sigmoid(ga_ref[:, cols])
            on = xh * gnw_v
            dya = dm * sa
            dga_ref[:, cols] = (dm * on * silu_z * sa * (1.0 - sa)).astype(BF16)
            dz_ref[:, cols] = (dya * on * sz * (1.0 + zh * (1.0 - sz))).astype(BF16)
            don = dya * silu_z
            dgnw = dgnw + jnp.sum(don * xh, axis=0, keepdims=True)
            dxh = don * gnw_v
            do_ref[:, cols] = r * (dxh - xh * jnp.mean(dxh * xh, axis=-1, keepdims=True))
        _accumulate(dgnw_ref, dgnw, first)
        halo = jnp.where(first, 0.0, halo_ref[...])
        for gi, win in enumerate(POOL_WINDOWS):
            cols = slice(gi * pg, (gi + 1) * pg)
            pv, dm = p_ref[:, cols], dm_ref[:, cols]
            pooled = _pooled(jnp.concatenate([halo[:, cols], pv], axis=0), pv, i, ts, win)
            lin = _dot(_mx(pooled), pw_ref[gi])
            psv = ps_ref[:, cols]
            sb = _sigmoid(gb_ref[:, cols])
            dgb_ref[:, cols] = (dm * lin * psv * sb * (1.0 - sb)).astype(BF16)
            dyb = dm * sb
            _accumulate(dps_ref.at[:, cols], jnp.sum(dyb * lin, axis=0, keepdims=True), first)
            dlin = _mx(dyb * psv)
            _accumulate(dpw_ref.at[gi], _dot(_mx(pooled), dlin, ta=True), first)
            dpl_ref[:, cols] = _dot(dlin, pw_ref[gi], tb=True)

    blk = lambda col: pl.BlockSpec((ts, d), lambda i, col=col: (i, col))
    vec = lambda width: pl.BlockSpec((1, width), lambda i: (0, 0))
    pw_spec = pl.BlockSpec((n_groups, pg, pg), lambda i: (0, 0, 0))
    return pl.pallas_call(
        body, name=name, grid=(s // ts,),
        in_specs=[blk(0), blk(0), blk(3), blk(4),
                  pl.BlockSpec((POOL_HALO, d), lambda i: (jnp.maximum(i * (ts // POOL_HALO) - 1, 0), 4)),
                  blk(5), blk(6), vec(HEAD_DIM), pw_spec, vec(d)],
        out_specs=[blk(0), blk(0), blk(0), blk(0), blk(0), vec(HEAD_DIM), pw_spec, vec(d)],
        out_shape=[jax.ShapeDtypeStruct((s, d), F32), jax.ShapeDtypeStruct((s, d), BF16),
                   jax.ShapeDtypeStruct((s, d), BF16), jax.ShapeDtypeStruct((s, d), BF16),
                   jax.ShapeDtypeStruct((s, d), F32), jax.ShapeDtypeStruct((1, HEAD_DIM), F32),
                   jax.ShapeDtypeStruct((n_groups, pg, pg), F32), jax.ShapeDtypeStruct((1, d), F32)],
        compiler_params=_params("arbitrary"))(dmixed, o, proj, proj, proj, proj, proj, gnw.reshape(1, HEAD_DIM),
                                               pool_w, pool_scale.reshape(1, d))


def _pool_bwd(dpooled, *, name):
    s, d = dpooled.shape
    pg = d // len(POOL_WINDOWS)
    ts = _tile(s, 512, 16)
    n_tiles = s // ts
    per = ts // POOL_HALO

    def body(d_ref, next_ref, out_ref):
        i = pl.program_id(0)
        nxt = jnp.where(i < n_tiles - 1, next_ref[...], 0.0)
        for gi, win in enumerate(POOL_WINDOWS):
            cols = slice(gi * pg, (gi + 1) * pg)
            dv = d_ref[:, cols]
            acc = jnp.concatenate([dv, nxt[:, cols]], axis=0) / _pool_counts(i, ts, POOL_HALO, win)
            span = 1
            while span < win:
                acc = acc + pltpu.roll(acc, acc.shape[0] - span, 0)
                span *= 2
            out_ref[:, cols] = (acc[:ts] - dv).astype(BF16)

    return pl.pallas_call(
        body, name=name, grid=(n_tiles,),
        in_specs=[pl.BlockSpec((ts, d), lambda i: (i, 0)),
                  pl.BlockSpec((POOL_HALO, d), lambda i: (jnp.minimum((i + 1) * per, s // POOL_HALO - 1), 0))],
        out_specs=pl.BlockSpec((ts, d), lambda i: (i, 0)), out_shape=jax.ShapeDtypeStruct((s, d), BF16),
        compiler_params=_params("parallel"))(dpooled, dpooled)


def _ffn_tiles(s, f):
    tf = _tile(f, 1408)
    return _tile(s, 512, 16), tf, f // tf


def _ffn_act_fwd(gu, conv_w, conv_b, *, name):
    s, f = gu.shape[0], gu.shape[1] // 2
    width = conv_w.shape[0]
    ts, tf, nf = _ffn_tiles(s, f)

    def body(g_ref, halo_ref, u_ref, w_ref, b_ref, act_ref, gc_ref):
        i = pl.program_id(0)
        gv = g_ref[...]
        cat = jnp.concatenate([jnp.where(i > 0, halo_ref[...], 0.0), gv], axis=0)
        gc = gv * w_ref[pl.ds(width - 1, 1), :] + b_ref[...]
        for sh in range(1, width):
            gc = gc + _rows_before(cat, sh, CONV_HALO) * w_ref[pl.ds(width - 1 - sh, 1), :]
        gc_ref[...] = gc
        act_ref[...] = (_gelu(gc) * u_ref[...]).astype(BF16)

    blk = pl.BlockSpec((ts, tf), lambda i, j: (i, j))
    return pl.pallas_call(
        body, name=name, grid=(s // ts, nf),
        in_specs=[blk, pl.BlockSpec((CONV_HALO, tf), lambda i, j: (jnp.maximum(i * (ts // CONV_HALO) - 1, 0), j)),
                  pl.BlockSpec((ts, tf), lambda i, j: (i, nf + j)),
                  pl.BlockSpec((width, tf), lambda i, j: (0, j)), pl.BlockSpec((1, tf), lambda i, j: (0, j))],
        out_specs=[blk, blk],
        out_shape=[jax.ShapeDtypeStruct((s, f), BF16), jax.ShapeDtypeStruct((s, f), F32)],
        compiler_params=_params("parallel", "parallel"))(gu, gu, gu, conv_w, conv_b.reshape(1, f))


def _ffn_act_bwd(dact, gu, gc, conv_w, *, name):
    s, f = gc.shape
    width = conv_w.shape[0]
    ts, tf, nf = _ffn_tiles(s, f)
    n_tiles = s // ts
    per = ts // CONV_HALO
    rows = ts + CONV_HALO

    def body(da_ref, da_next, gc_ref, gc_next, u_ref, u_next, g_ref, g_prev, w_ref, dg_ref, du_ref, dw_ref, db_ref):
        i = pl.program_id(1)
        first = i == 0
        da = jnp.concatenate([da_ref[...], da_next[...]], axis=0)
        gcv = jnp.concatenate([gc_ref[...], gc_next[...]], axis=0)
        uv = jnp.concatenate([u_ref[...], u_next[...]], axis=0)
        du_ref[...] = (da[:ts] * _gelu(gcv[:ts])).astype(BF16)
        live = jnp.logical_or(lax.broadcasted_iota(jnp.int32, (rows, 1), 0) < ts, i < n_tiles - 1)
        dgc = jnp.where(live, da * uv * _gelu_grad(gcv), 0.0)
        dgate = dgc[:ts] * w_ref[pl.ds(width - 1, 1), :]
        for sh in range(1, width):
            dgate = dgate + _rows_after(dgc, sh, ts) * w_ref[pl.ds(width - 1 - sh, 1), :]
        dg_ref[...] = dgate.astype(BF16)
        gv = g_ref[...]
        cat = jnp.concatenate([jnp.where(first, 0.0, g_prev[...]), gv], axis=0)
        shifted = [gv] + [_rows_before(cat, sh, CONV_HALO) for sh in range(1, width)]
        dw_rows = [jnp.sum(dgc[:ts] * shifted[width - 1 - j], axis=0, keepdims=True) for j in range(width)]
        _accumulate(dw_ref, jnp.concatenate(dw_rows, axis=0), first)
        _accumulate(db_ref, jnp.sum(dgc[:ts], axis=0, keepdims=True), first)

    nxt_row = lambda i: jnp.minimum((i + 1) * per, s // CONV_HALO - 1)
    main = lambda off: pl.BlockSpec((ts, tf), lambda j, i, off=off: (i, off + j))
    nxt = lambda off: pl.BlockSpec((CONV_HALO, tf), lambda j, i, off=off: (nxt_row(i), off + j))
    return pl.pallas_call(
        body, name=name, grid=(nf, n_tiles),
        in_specs=[main(0), nxt(0), main(0), nxt(0), main(nf), nxt(nf), main(0),
                  pl.BlockSpec((CONV_HALO, tf), lambda j, i: (jnp.maximum(i * per - 1, 0), j)),
                  pl.BlockSpec((width, tf), lambda j, i: (0, j))],
        out_specs=[main(0), main(0), pl.BlockSpec((width, tf), lambda j, i: (0, j)),
                   pl.BlockSpec((1, tf), lambda j, i: (0, j))],
        out_shape=[jax.ShapeDtypeStruct((s, f), BF16), jax.ShapeDtypeStruct((s, f), BF16),
                   jax.ShapeDtypeStruct((width, f), F32), jax.ShapeDtypeStruct((1, f), F32)],
        compiler_params=_params("parallel", "arbitrary"))(dact, dact, gc, gc, gu, gu, gu, gu, conv_w)


def _rows_layout(bg, n_heads):
    s = bg.shape[0]
    shape = (n_heads, s // GDN_CHUNK, 1, GDN_CHUNK)
    return bg[:, :n_heads].T.reshape(shape), bg[:, n_heads:2 * n_heads].T.reshape(shape)


def _lane_layout(dbeta_rows, dg_rows):
    n_heads = dbeta_rows.shape[0]
    s = dbeta_rows.shape[1] * GDN_CHUNK
    both = jnp.concatenate([dbeta_rows.reshape(n_heads, s), dg_rows.reshape(n_heads, s)], axis=0).T
    return jnp.pad(both, ((0, 0), (0, LANES - 2 * n_heads)))


def _layer_fwd(x, w, late_weights):
    n_heads = w["n_heads"]
    h = _rmsnorm_fwd(x, w["norm_mix_w"], name="norm_mix_fwd")
    proj = _matmul(h, w["w_main"], name="in_proj_fwd")
    pba = _matmul(h, w["w_ba"], name="ba_proj_fwd")
    qkv = _qkv_fwd(proj, w["conv_qkv_w"], n_heads, name="qkv_fwd")
    bg = _ba_fwd(pba, w["alog_row"], w["dtb_row"], n_heads, name="ba_fwd")
    beta_rows, g_rows = _rows_layout(bg, n_heads)
    o, states = _gdn_fwd(qkv, beta_rows, g_rows, name="gdn_fwd")
    w = dict(w, **late_weights(o))
    mixed = _merge_fwd(proj, o, w["gdn_norm_w"], w["pool_w"], w["pool_scale"], name="merge_fwd")
    x2 = _matmul(mixed, w["w_out"], add=x, name="out_proj_fwd")
    h2 = _rmsnorm_fwd(x2, w["norm_ffn_w"], name="norm_ffn_fwd")
    gu = _matmul(h2, w["w_up_slots"], b_slots=True, name="up_proj_fwd")
    act, gc = _ffn_act_fwd(gu, w["conv_ffn_w"], w["conv_ffn_b"], name="ffn_act_fwd")
    x3 = _matmul(act, w["w_down"], add=x2, tk=1408, name="down_proj_fwd")
    saved = dict(x=x, h=h, proj=proj, pba=pba, qkv=qkv, beta_rows=beta_rows, g_rows=g_rows, o=o, states=states,
                 mixed=mixed, x2=x2, h2=h2, gu=gu, gc=gc, act=act)
    return x3, saved, w


def _layer_bwd_ffn(dx3, dx3_bf, w, sv, after):
    g = {}
    dact = _matmul(dx3_bf, w["w_down"], tb=True, tn=1408, after=after, name="down_proj_dx")
    g["w_down"] = _matmul(sv["act"], dx3_bf, ta=True, out_dtype=BF16, name="down_proj_dw")
    dgate, dup, g["conv_ffn_w"], g["conv_ffn_b"] = _ffn_act_bwd(dact, sv["gu"], sv["gc"], w["conv_ffn_w"],
                                                                 name="ffn_act_bwd")
    dgu = jnp.concatenate([dgate, dup], axis=1)
    dh2 = _matmul(dgu, w["w_up_slots"], b_slots=True, tb=True, name="up_proj_dx")
    g["w_up"] = _matmul(sv["h2"], dgu, ta=True, out_dtype=BF16, name="up_proj_dw")
    dx2, dx2_bf, g["norm_ffn_w"] = _rmsnorm_bwd(dh2, sv["x2"], w["norm_ffn_w"], dx3, name="norm_ffn_bwd")
    return dx2, dx2_bf, g


def _layer_bwd_mix(dx2, dx2_bf, w, sv, after):
    n_heads = w["n_heads"]
    g = {}
    dmixed = _matmul(dx2_bf, w["w_out"], tb=True, after=after, name="out_proj_dx")
    g["w_out"] = _matmul(sv["mixed"], dx2_bf, ta=True, out_dtype=BF16, name="out_proj_dw")
    do, dz, dga, dgb, dpooled, g["gdn_norm_w"], g["pool_w"], g["pool_scale"] = _merge_bwd(
        dmixed, sv["proj"], sv["o"], w["gdn_norm_w"], w["pool_w"], w["pool_scale"], name="merge_bwd")
    dp = _pool_bwd(dpooled, name="pool_bwd")
    dqkv, dbeta_rows, dg_rows = _gdn_bwd(do, sv["qkv"], sv["beta_rows"], sv["g_rows"], sv["states"], name="gdn_bwd")
    dproj_qkv, g["conv_qkv_w"] = _qkv_bwd(dqkv, sv["proj"], w["conv_qkv_w"], n_heads, name="qkv_bwd")
    dpba, g["alog_row"], g["dtb_row"] = _ba_bwd(_lane_layout(dbeta_rows, dg_rows), sv["pba"], w["alog_row"],
                                                w["dtb_row"], n_heads, name="ba_bwd")
    dproj = jnp.concatenate([dproj_qkv, dz, dp, dga, dgb], axis=1)
    g["w_main"] = _matmul(sv["h"], dproj, ta=True, out_dtype=BF16, name="in_proj_dw")
    g["w_ba"] = _matmul(sv["h"], dpba, ta=True, name="ba_proj_dw")
    return dproj, dpba, g


def _layer_bwd_in(dproj, dpba, dx2, w, sv, after):
    dh = _matmul(dproj, w["w_main"], tb=True, after=after, name="in_proj_dx")
    dh = _matmul(dpba, w["w_ba"], tb=True, add=dh, name="ba_proj_dx")
    return _rmsnorm_bwd(dh, sv["x"], w["norm_mix_w"], dx2, name="norm_mix_bwd")


def _here():
    mx, my, mc = (lax.axis_index(a) for a in MESH_AXES)
    return (mx, my, mc), 4 * mx + 2 * my + mc


def _peer(pos, r):
    mx, my, mc = pos
    px = 1 - mx if r & 4 else mx
    py = 1 - my if r & 2 else my
    pc = 1 - mc if r & 1 else mc
    return (px, py, pc), 4 * px + 2 * py + pc


def _run_exchange(n_tensors, src_view, dst_view, sems):
    send_sems, recv_sems, local_sems = sems
    pos, me = _here()
    started = []
    for t in range(n_tensors):
        cp = pltpu.make_async_copy(src_view(t, me), dst_view(t, me), local_sems.at[t])
        cp.start()
        started.append(cp)

    def remote(t, r, landing):
        target, target_lin = _peer(pos, r)
        return pltpu.make_async_remote_copy(
            src_ref=src_view(t, target_lin), dst_ref=dst_view(t, target_lin if landing else me),
            send_sem=send_sems.at[t, r - 1], recv_sem=recv_sems.at[t, r - 1],
            device_id=target, device_id_type=pl.DeviceIdType.MESH)

    sends = []
    for r in range(1, N_DEV):
        for t in range(n_tensors):
            cp = remote(t, r, landing=False)
            cp.start()
            sends.append(cp)
    for r in range(1, N_DEV):
        for t in range(n_tensors):
            remote(t, r, landing=True).wait_recv()
    for cp in sends:
        cp.wait_send()
    for cp in started:
        cp.wait()


def _exchange_scratch(n_tensors):
    return [pltpu.SemaphoreType.DMA((n_tensors, N_DEV - 1)), pltpu.SemaphoreType.DMA((n_tensors, N_DEV - 1)),
            pltpu.SemaphoreType.DMA((n_tensors,))]


def _slot_view(ref, axis, index):
    return ref.at[(slice(None),) * axis + (index,)]


def _gather(srcs, slot_axes, *, name):
    n = len(srcs)

    def body(*refs):
        src_refs, out_refs = refs[:n], refs[n:2 * n]
        _run_exchange(n, lambda t, to: src_refs[t], lambda t, frm: _slot_view(out_refs[t], slot_axes[t], frm),
                      refs[2 * n:])

    hbm = pl.BlockSpec(memory_space=pltpu.HBM)
    out_shape = [jax.ShapeDtypeStruct(s.shape[:a] + (N_DEV,) + s.shape[a:], s.dtype) for s, a in zip(srcs, slot_axes)]
    return pl.pallas_call(body, name=name, in_specs=[hbm] * n, out_specs=[hbm] * n, out_shape=out_shape,
                          scratch_shapes=_exchange_scratch(n))(*srcs)


_SIDE_EFFECT = pltpu.SideEffectType.DATAFLOW_SIDE_EFFECTING


def _split_copy(t, r, pos, src_refs, land_refs, send_sems, recv_sems, src_view, dst_view, landing):
    _, me = _here()
    target, target_lin = _peer(pos, r)
    return pltpu.make_async_remote_copy(
        src_ref=src_view(t, src_refs[t], target_lin), dst_ref=dst_view(t, land_refs[t], target_lin if landing else me),
        send_sem=send_sems.at[t * (N_DEV - 1) + r - 1], recv_sem=recv_sems.at[t * (N_DEV - 1) + r - 1],
        device_id=target, device_id_type=pl.DeviceIdType.MESH)


def _start_exchange(srcs, lands, src_view, dst_view, after, *, name):
    n = len(srcs)
    has_after = after is not None

    def body(*refs):
        src_refs, land_refs = refs[:n], refs[n:2 * n]
        outs = refs[2 * n + has_after:]
        send_sems, recv_sems, token = outs[0], outs[1], outs[2 + 2 * n]
        pos, _ = _here()
        for r in range(1, N_DEV):
            for t in range(n):
                _split_copy(t, r, pos, src_refs, land_refs, send_sems, recv_sems, src_view, dst_view, False).start()
        token[...] = jnp.zeros_like(token)

    hbm = pl.BlockSpec(memory_space=pltpu.HBM)
    sem = pl.BlockSpec(memory_space=pltpu.SEMAPHORE)
    sem_shape = pltpu.SemaphoreType.DMA((n * (N_DEV - 1),))
    through = [pltpu.HBM(t.shape, t.dtype) for t in list(srcs) + list(lands)]
    args = [pltpu.with_memory_space_constraint(t, pltpu.HBM) for t in list(srcs) + list(lands)]
    outs = pl.pallas_call(
        body, name=name, in_specs=[hbm] * (2 * n) + ([pl.BlockSpec(memory_space=pl.ANY)] if has_after else []),
        out_specs=(sem, sem, *[hbm] * (2 * n), pl.BlockSpec(memory_space=pltpu.VMEM)),
        out_shape=(sem_shape, sem_shape, *through, jax.ShapeDtypeStruct((8, LANES), F32)),
        input_output_aliases={i: 2 + i for i in range(2 * n)},
        compiler_params=pltpu.CompilerParams(has_side_effects=_SIDE_EFFECT))(*args, *([after] if has_after else []))
    return outs[0], outs[1], list(outs[2:2 + n]), list(outs[2 + n:2 + 2 * n]), outs[-1]


def _wait_exchange(send_sems, recv_sems, srcs, lands, src_view, dst_view, after, *, name):
    n = len(srcs)

    def body(*refs):
        src_refs, land_refs = refs[:n], refs[n:2 * n]
        send_refs, recv_refs = refs[2 * n], refs[2 * n + 1]
        pos, _ = _here()
        for r in range(1, N_DEV):
            for t in range(n):
                cp = _split_copy(t, r, pos, src_refs, land_refs, send_refs, recv_refs, src_view, dst_view, True)
                cp.wait_send()
                cp.wait_recv()

    hbm = pl.BlockSpec(memory_space=pltpu.HBM)
    sem = pl.BlockSpec(memory_space=pltpu.SEMAPHORE)
    outs = pl.pallas_call(
        body, name=name, in_specs=[hbm] * (2 * n) + [sem, sem, pl.BlockSpec(memory_space=pl.ANY)],
        out_specs=[hbm] * (2 * n), out_shape=[pltpu.HBM(t.shape, t.dtype) for t in list(srcs) + list(lands)],
        input_output_aliases={i: i for i in range(2 * n)},
        compiler_params=pltpu.CompilerParams(has_side_effects=_SIDE_EFFECT))(*srcs, *lands, send_sems, recv_sems, after)
    return list(outs[n:])


def _sum_slots(parts, *, name):
    _, n_lead, r_rows, cols = parts.shape
    tr = _tile(r_rows, max(16, (1 << 17) // cols // 16 * 16), 16)

    def body(p_ref, o_ref):
        total = p_ref[0].astype(F32)
        for p in range(1, N_DEV):
            total = total + p_ref[p].astype(F32)
        o_ref[...] = total

    return pl.pallas_call(
        body, name=name, grid=(n_lead, r_rows // tr),
        in_specs=[pl.BlockSpec((N_DEV, None, tr, cols), lambda a, i: (0, a, i, 0))],
        out_specs=pl.BlockSpec((None, tr, cols), lambda a, i: (a, i, 0)),
        out_shape=jax.ShapeDtypeStruct((n_lead, r_rows, cols), F32),
        compiler_params=_params("parallel", "parallel"))(parts)


_WinLayout = collections.namedtuple("_WinLayout", "shard_w n_main ba_dev ba_off n_ba slot_w")


def _win_layout(shard_w, n_main, ba_start, n_ba):
    ba_dev = ba_start // shard_w
    assert (ba_start + n_ba - 1) // shard_w == ba_dev and n_main % LANES == 0
    slot_w = -(-(LANES - 1 + shard_w) // LANES) * LANES
    return _WinLayout(shard_w, n_main, ba_dev, ba_start - ba_dev * shard_w, n_ba, slot_w)


def _main_start(lay, dev):
    return lay.shard_w * dev - jnp.where(dev > lay.ba_dev, lay.n_ba, 0)


def _slab_origin(lay, dev):
    return jnp.minimum(_main_start(lay, dev) // LANES * LANES, lay.n_main - lay.slot_w)


def _assemble_plan(lay):
    plan = [[] for _ in range(lay.n_main // LANES)]
    for dev in range(N_DEV):
        start = lay.shard_w * dev - (lay.n_ba if dev > lay.ba_dev else 0)
        width = lay.shard_w - (lay.n_ba if dev == lay.ba_dev else 0)
        origin = min(start // LANES, (lay.n_main - lay.slot_w) // LANES)
        pad = start - origin * LANES
        for t in range(pad // LANES, (pad + width - 1) // LANES + 1):
            plan[origin + t].append((dev, t))
    return plan


def _assemble_w_main(slabs, lay, *, name):
    _, d, slot_w = slabs.shape
    plan = _assemble_plan(lay)
    runs = []
    shared = []
    for tile, parts in enumerate(plan):
        if len(parts) != 1:
            shared.append((tile, parts))
        elif runs and runs[-1][2] == parts[0][0] and runs[-1][0] + runs[-1][1] == tile:
            runs[-1][1] += 1
        else:
            runs.append([tile, 1, parts[0][0], parts[0][1]])
    tr = _tile(d, 256, 16)

    def body(in_ref, out_ref):
        for first, count, dev, t0 in runs:
            out_ref[:, first * LANES:(first + count) * LANES] = in_ref[dev, :, t0 * LANES:(t0 + count) * LANES]
        for tile, parts in shared:
            total = in_ref[parts[0][0], :, parts[0][1] * LANES:(parts[0][1] + 1) * LANES]
            for dev, t in parts[1:]:
                total = total + in_ref[dev, :, t * LANES:(t + 1) * LANES]
            out_ref[:, tile * LANES:(tile + 1) * LANES] = total

    return pl.pallas_call(
        body, name=name, grid=(d // tr,),
        in_specs=[pl.BlockSpec((N_DEV, tr, slot_w), lambda i: (0, i, 0))],
        out_specs=pl.BlockSpec((tr, lay.n_main), lambda i: (i, 0)),
        out_shape=jax.ShapeDtypeStruct((d, lay.n_main), slabs.dtype), compiler_params=_params("parallel"))(slabs)


def _adam_update(w, g, m, v):
    nm = ADAM_B1 * m + (1.0 - ADAM_B1) * g
    nv = ADAM_B2 * v + (1.0 - ADAM_B2) * (g * g)
    m_hat = nm / (1.0 - ADAM_B1 ** ADAM_STEP)
    v_hat = nv / (1.0 - ADAM_B2 ** ADAM_STEP)
    return -ADAM_LR * (m_hat / (jnp.sqrt(v_hat) + ADAM_EPS) + ADAM_WD * w), nm, nv


def _adamw(w, g, m, v, *, name):
    rows, cols = w.shape
    tr = _tile(rows, max(8, (1 << 18) // cols // 8 * 8), 8)

    def body(w_ref, g_ref, m_ref, v_ref, d_ref, nm_ref, nv_ref):
        d_ref[...], nm_ref[...], nv_ref[...] = _adam_update(w_ref[...], g_ref[...], m_ref[...], v_ref[...])

    blk = pl.BlockSpec((tr, cols), lambda i: (i, 0))
    out = jax.ShapeDtypeStruct((rows, cols), F32)
    return pl.pallas_call(
        body, name=name, grid=(rows // tr,), in_specs=[blk] * 4, out_specs=[blk] * 3, out_shape=[out] * 3,
        compiler_params=_params("parallel"))(w, g, m, v)


def _adamw_nd(w, g, m, v, *, name):
    two_d = (-1, w.shape[-1])
    outs = _adamw(w.reshape(two_d), g.reshape(two_d), m.reshape(two_d), v.reshape(two_d), name=name)
    return tuple(t.reshape(w.shape) for t in outs)


def _adamw_slots(parts, w, m, v, after, *, name):
    n_layers, rows, cols = w.shape
    tr = _tile(rows, max(16, (1 << 18) // cols // 16 * 16), 16)

    def body(p_ref, w_ref, m_ref, v_ref, after_ref, g_ref, d_ref, nm_ref, nv_ref):
        total = p_ref[0].astype(F32)
        for p in range(1, N_DEV):
            total = total + p_ref[p].astype(F32)
        g_ref[...] = total
        d_ref[...], nm_ref[...], nv_ref[...] = _adam_update(w_ref[...], total, m_ref[...], v_ref[...])

    blk = pl.BlockSpec((None, tr, cols), lambda a, i: (a, i, 0))
    out = jax.ShapeDtypeStruct((n_layers, rows, cols), F32)
    return pl.pallas_call(
        body, name=name, grid=(n_layers, rows // tr),
        in_specs=[pl.BlockSpec((N_DEV, None, tr, cols), lambda a, i: (0, a, i, 0)), blk, blk, blk,
                  pl.BlockSpec(memory_space=pl.ANY)],
        out_specs=[blk] * 4, out_shape=[out] * 4,
        compiler_params=_params("parallel", "parallel"))(parts, w, m, v, after)


def _pack_rows(parts, dtype, quantum_rows):
    flat = jnp.concatenate([p.reshape(-1).astype(dtype) for p in parts])
    n = flat.shape[0]
    padded = -(-n // (LANES * quantum_rows)) * (LANES * quantum_rows)
    return jnp.pad(flat, (0, padded - n)).reshape(padded // LANES, LANES)


def _unpack(flat, shapes):
    lead = flat.shape[:-1]
    out, at = [], 0
    for shape in shapes:
        size = 1
        for dim in shape:
            size *= dim
        out.append(flat[..., at:at + size].reshape(lead + tuple(shape)))
        at += size
    return out


def _whole_from_slots(slots, axis):
    moved = jnp.moveaxis(slots, 0, axis)
    shape = moved.shape
    return moved.reshape(shape[:axis] + (shape[axis] * shape[axis + 1],) + shape[axis + 2:])


def _lane_row(vec, n_heads):
    return jnp.pad(vec, ((0, 0), (n_heads, LANES - 2 * n_heads)))[:, None, :]


REPLICATED = ("norm_mix_w", "a_log", "dt_bias", "gdn_norm_w", "pool_scale", "norm_ffn_w", "conv_ffn_b",
              "norm_final_w")
WEIGHTS = ("norm_mix_w", "w_in", "conv_qkv_w", "a_log", "dt_bias", "gdn_norm_w", "pool_w", "pool_scale", "w_out",
           "norm_ffn_w", "w_up", "conv_ffn_w", "conv_ffn_b", "w_down", "norm_final_w")
SMALL_QUANTUM_ROWS = 512


def kernel(x, norm_mix_w, w_in, conv_qkv_w, a_log, dt_bias, gdn_norm_w, pool_w, pool_scale, w_out, norm_ffn_w, w_up, conv_ffn_w, conv_ffn_b, w_down, norm_final_w, loss_target, m_norm_mix_w, m_w_in, m_conv_qkv_w, m_a_log, m_dt_bias, m_gdn_norm_w, m_pool_w, m_pool_scale, m_w_out, m_norm_ffn_w, m_w_up, m_conv_ffn_w, m_conv_ffn_b, m_w_down, m_norm_final_w, v_norm_mix_w, v_w_in, v_conv_qkv_w, v_a_log, v_dt_bias, v_gdn_norm_w, v_pool_w, v_pool_scale, v_w_out, v_norm_ffn_w, v_w_up, v_conv_ffn_w, v_conv_ffn_b, v_w_down, v_norm_final_w):
    local = dict(norm_mix_w=norm_mix_w, w_in=w_in, conv_qkv_w=conv_qkv_w, a_log=a_log, dt_bias=dt_bias,
                 gdn_norm_w=gdn_norm_w, pool_w=pool_w, pool_scale=pool_scale, w_out=w_out, norm_ffn_w=norm_ffn_w,
                 w_up=w_up, conv_ffn_w=conv_ffn_w, conv_ffn_b=conv_ffn_b, w_down=w_down, norm_final_w=norm_final_w)
    mom_m = dict(norm_mix_w=m_norm_mix_w, w_in=m_w_in, conv_qkv_w=m_conv_qkv_w, a_log=m_a_log, dt_bias=m_dt_bias,
                 gdn_norm_w=m_gdn_norm_w, pool_w=m_pool_w, pool_scale=m_pool_scale, w_out=m_w_out,
                 norm_ffn_w=m_norm_ffn_w, w_up=m_w_up, conv_ffn_w=m_conv_ffn_w, conv_ffn_b=m_conv_ffn_b,
                 w_down=m_w_down, norm_final_w=m_norm_final_w)
    mom_v = dict(norm_mix_w=v_norm_mix_w, w_in=v_w_in, conv_qkv_w=v_conv_qkv_w, a_log=v_a_log, dt_bias=v_dt_bias,
                 gdn_norm_w=v_gdn_norm_w, pool_w=v_pool_w, pool_scale=v_pool_scale, w_out=v_w_out,
                 norm_ffn_w=v_norm_ffn_w, w_up=v_w_up, conv_ffn_w=v_conv_ffn_w, conv_ffn_b=v_conv_ffn_b,
                 w_down=v_w_down, norm_final_w=v_norm_final_w)
    n_layers, n_heads = a_log.shape
    d_model = x.shape[-1]
    dl = n_heads * HEAD_DIM
    n_ba = 2 * n_heads
    assert x.shape[0] == 1 and dl == d_model and pool_scale.shape[1] == d_model
    lay = _win_layout(w_in.shape[2], N_DEV * w_in.shape[2] - n_ba, 4 * dl, n_ba)
    _, me = _here()
    is_ba_dev = me == lay.ba_dev
    my_pad = _main_start(lay, me) - _slab_origin(lay, me)
    ba_cols = slice(lay.ba_off, lay.ba_off + n_ba)

    w_in_bf = w_in.astype(BF16)
    without_ba = jnp.concatenate([w_in_bf[..., :lay.ba_off], w_in_bf[..., lay.ba_off + n_ba:],
                                  jnp.zeros(w_in.shape[:2] + (n_ba,), BF16)], axis=-1)
    slab = lax.dynamic_update_slice(jnp.zeros(w_in.shape[:2] + (lay.slot_w,), BF16),
                                    jnp.where(is_ba_dev, without_ba, w_in_bf), (0, 0, my_pad))
    ba_part = jnp.pad(jnp.where(is_ba_dev, w_in_bf[..., ba_cols], jnp.zeros((), BF16)),
                      ((0, 0), (0, 0), (0, LANES - n_ba)))
    convs = _pack_rows([conv_qkv_w, conv_ffn_w], F32, 16)
    ba_slots, conv_slots = _gather([ba_part, convs], [0, 0], name="gather_small_weights")
    conv_parts = _unpack(conv_slots.reshape(N_DEV, -1), [conv_qkv_w.shape, conv_ffn_w.shape])
    conv_qkv_whole, conv_ffn_whole = (_whole_from_slots(p, 2) for p in conv_parts)
    alog_rows, dtb_rows = _lane_row(a_log, n_heads), _lane_row(dt_bias, n_heads)

    def with_own_slot(own, axis):
        zone = lax.empty(own.shape[:axis] + (N_DEV,) + own.shape[axis:], own.dtype)
        return lax.dynamic_update_slice(zone, jnp.expand_dims(own, axis), (0,) * axis + (me,) + (0,) * (own.ndim - axis))

    slot_axis = dict(slab=0, w_up=0, w_out=0, w_down=0, pool_w=1)
    gather_groups = (("slab",), ("w_up", "w_out", "w_down", "pool_w"))
    gather_src = lambda t, ref, to: ref
    in_flight = {}
    token = conv_slots
    for l in range(n_layers):
        own = dict(slab=slab[l], w_up=w_up[l].astype(BF16), w_out=w_out[l].astype(BF16),
                   w_down=w_down[l].astype(BF16), pool_w=pool_w[l].astype(BF16))
        for part, names in zip("ab", gather_groups):
            axes = [slot_axis[n] for n in names]
            dst = lambda t, ref, frm, axes=axes: _slot_view(ref, axes[t], frm)
            *handles, token = _start_exchange([own[n] for n in names], [with_own_slot(own[n], slot_axis[n]) for n in names],
                                              gather_src, dst, token, name="gather_start_%d%s" % (l, part))
            in_flight[l, part] = (handles, dst)

    def arrived(l, part, after):
        (send_sems, recv_sems, srcs, lands), dst = in_flight[l, part]
        return _wait_exchange(send_sems, recv_sems, srcs, lands, gather_src, dst, after, name="gather_wait_%d%s" % (l, part))

    xc = x[0]
    layer_w, saved = [], []
    after = token
    for l in range(n_layers):
        slabs, = arrived(l, "a", after)
        early = dict(n_heads=n_heads, norm_mix_w=norm_mix_w[l], alog_row=alog_rows[l], dtb_row=dtb_rows[l],
                     w_main=_assemble_w_main(slabs, lay, name="assemble_w_main"), w_ba=ba_slots[lay.ba_dev, l],
                     conv_qkv_w=conv_qkv_whole[l])

        def late_weights(o, l=l):
            up_slots, out_slots, down_slots, pool_slots = arrived(l, "b", o)
            return dict(norm_ffn_w=norm_ffn_w[l], gdn_norm_w=gdn_norm_w[l], pool_scale=pool_scale[l],
                        conv_ffn_b=conv_ffn_b[l], conv_ffn_w=conv_ffn_whole[l], w_up_slots=up_slots,
                        w_out=out_slots.reshape(-1, d_model), w_down=down_slots.reshape(-1, d_model),
                        pool_w=pool_slots.reshape(pool_slots.shape[0], -1, pool_slots.shape[-1]))

        xc, sv, wl = _layer_fwd(xc, early, late_weights)
        layer_w.append(wl)
        saved.append(sv)
        after = xc
    loss_row, dx, dx_bf, d_final = _loss_head(xc, norm_final_w, loss_target[0], name="loss_head")
    loss = lax.psum(loss_row[0, 0], MESH_AXES)

    shard = {n: local[n].shape[1:] for n in ("w_up", "w_out", "w_down", "pool_w")}
    shard["w_main"] = (d_model, lay.slot_w)
    recvs = {n: lax.empty((N_DEV, n_layers) + shard[n], BF16) for n in shard}
    up_w, out_rows, down_rows, pool_rows = shard["w_up"][1], shard["w_out"][0], shard["w_down"][0], shard["pool_w"][1]
    owned = dict(w_main=lambda to: (1, _slab_origin(lay, to), lay.slot_w), w_up=lambda to: (1, to * up_w, up_w),
                 w_out=lambda to: (0, to * out_rows, out_rows), w_down=lambda to: (0, to * down_rows, down_rows),
                 pool_w=lambda to: (1, to * pool_rows, pool_rows))
    scatter_groups = dict(e=("w_up", "w_down"), l=("w_main", "w_out", "pool_w"))
    pending = {}

    def send_grads(part, l, g):
        names = scatter_groups[part]

        def src(t, ref, to):
            axis, first, length = owned[names[t]](to)
            return ref.at[(slice(None),) * axis
                          + (pl.ds(pl.multiple_of(first, LANES if axis == ref.ndim - 1 else 16), length),)]

        dst = lambda t, ref, frm: ref.at[frm, l]
        received(part, g[names[0]])
        lands = []
        for n in names:
            axis, first, length = owned[n](me)
            mine = lax.dynamic_slice_in_dim(g[n], first, length, axis=axis)
            lands.append(lax.dynamic_update_slice(recvs[n], mine[None, None], (me, l) + (0,) * mine.ndim))
        send_sems, recv_sems, grads, lands, token = _start_exchange([g[n] for n in names], lands, src, dst, None,
                                                                    name="scatter_start_%d%s" % (l, part))
        pending[part] = (send_sems, recv_sems, grads, lands, src, dst, "scatter_wait_%d%s" % (l, part))
        return token

    def received(part, after):
        if part in pending:
            *args, name = pending.pop(part)
            recvs.update(zip(scatter_groups[part], _wait_exchange(*args, after=after, name=name)))

    layer_grads = [None] * n_layers
    token = None
    for l in reversed(range(n_layers)):
        dx2, dx2_bf, g = _layer_bwd_ffn(dx, dx_bf, layer_w[l], saved[l], token)
        token = send_grads("e", l, g)
        dproj, dpba, g_mix = _layer_bwd_mix(dx2, dx2_bf, layer_w[l], saved[l], token)
        g.update(g_mix)
        g["pool_w"] = g["pool_w"].astype(BF16)
        token = send_grads("l", l, g)
        dx, dx_bf, g["norm_mix_w"] = _layer_bwd_in(dproj, dpba, dx2, layer_w[l], saved[l], token)
        layer_grads[l] = g
    received("e", token)
    grad_x = dx
    stack = lambda name: jnp.stack([g[name] for g in layer_grads])

    small_names = REPLICATED + ("conv_qkv_w", "conv_ffn_w", "w_ba")
    g_small = dict(norm_mix_w=stack("norm_mix_w")[:, 0], a_log=stack("alog_row")[:, 0, n_heads:n_ba],
                   dt_bias=stack("dtb_row")[:, 0, n_heads:n_ba], gdn_norm_w=stack("gdn_norm_w")[:, 0],
                   pool_scale=stack("pool_scale")[:, 0], norm_ffn_w=stack("norm_ffn_w")[:, 0],
                   conv_ffn_b=stack("conv_ffn_b")[:, 0], norm_final_w=d_final[0], conv_qkv_w=stack("conv_qkv_w"),
                   conv_ffn_w=stack("conv_ffn_w"), w_ba=stack("w_ba")[..., :n_ba])
    small_shapes = [g_small[n].shape for n in small_names]
    small_slots, = _gather([_pack_rows([g_small[n] for n in small_names], F32, SMALL_QUANTUM_ROWS)], [0],
                           name="gather_small_grads")
    small_sum = _sum_slots(small_slots[:, None], name="sum_small_grads")
    grad = dict(zip(small_names, _unpack(small_sum.reshape(-1), small_shapes)))
    for n in ("conv_qkv_w", "conv_ffn_w"):
        width = local[n].shape[2]
        grad[n] = lax.dynamic_slice_in_dim(grad[n], me * width, width, axis=2)

    delta, new_m, new_v = {}, {}, {}
    flat = lambda t, lead: t.reshape(t.shape[:lead] + (-1, t.shape[-1]))

    def update_shard(n, after):
        outs = _adamw_slots(flat(recvs[n], 2), flat(local[n], 1), flat(mom_m[n], 1), flat(mom_v[n], 1), after,
                            name="adamw_" + n)
        grad[n], delta[n], new_m[n], new_v[n] = (t.reshape(local[n].shape) for t in outs)

    update_shard("w_up", token)
    update_shard("w_down", token)
    received("l", new_v["w_down"])
    update_shard("w_out", token)
    update_shard("pool_w", token)
    main_sum = _sum_slots(recvs["w_main"], name="sum_w_main_grads")
    g_main = lax.dynamic_slice_in_dim(main_sum, my_pad, lay.shard_w, axis=2)
    with_ba = jnp.concatenate([g_main[..., :lay.ba_off], grad.pop("w_ba"),
                               g_main[..., lay.ba_off:lay.shard_w - n_ba]], axis=-1)
    grad["w_in"] = jnp.where(is_ba_dev, with_ba, g_main)
    for n in ("w_in", "conv_qkv_w", "conv_ffn_w"):
        delta[n], new_m[n], new_v[n] = _adamw_nd(local[n], grad[n], mom_m[n], mom_v[n], name="adamw_" + n)
    packed = [_pack_rows([src[n] for n in REPLICATED], F32, 8) for src in (local, grad, mom_m, mom_v)]
    rep_out = _adamw(*packed, name="adamw_replicated")
    rep_shapes = [local[n].shape for n in REPLICATED]
    for dst, arr in zip((delta, new_m, new_v), rep_out):
        dst.update(zip(REPLICATED, _unpack(arr.reshape(-1), rep_shapes)))

    return (loss, grad_x[None], *[grad[n] for n in WEIGHTS], *[delta[n] for n in WEIGHTS],
            *[new_m[n] for n in WEIGHTS], *[new_v[n] for n in WEIGHTS])
```

```python
import collections

import jax
import jax.numpy as jnp
from jax import lax
from jax.experimental import pallas as pl
from jax.experimental.pallas import tpu as pltpu

F32 = jnp.float32
BF16 = jnp.bfloat16
MESH_AXES = ("x", "y", "c")
N_DEV = 8

NORM_EPS = 1e-6
HEAD_DIM = 128
GDN_CHUNK = 64
GDN_BATCH = 8
GDN_HEADS = 2
POOL_WINDOWS = (2, 4, 8, 16)
POOL_HALO = 16
CONV_HALO = 16
LANES = 128
V7X_VMEM_LIMIT_BYTES = 56 * 1024 * 1024

ADAM_LR = 0.001
ADAM_B1 = 0.9
ADAM_B2 = 0.999
ADAM_EPS = 1e-08
ADAM_WD = 0.01
ADAM_STEP = 10


def _mx(v):
    return v.astype(BF16)


def _dot(a, b, ta=False, tb=False, precision=None):
    dims = (((0 if ta else 1,), (1 if tb else 0,)), ((), ()))
    return lax.dot_general(a, b, dims, precision=precision, preferred_element_type=F32)


def _tile(dim, target, quantum=LANES):
    if dim <= target:
        return dim
    t = (target // quantum) * quantum
    while t >= quantum:
        if dim % t == 0:
            return t
        t -= quantum
    return dim


def _params(*semantics):
    return pltpu.CompilerParams(dimension_semantics=semantics, vmem_limit_bytes=V7X_VMEM_LIMIT_BYTES)


def _sigmoid(v):
    return 1.0 / (1.0 + jnp.exp(-v))


def _softplus(v):
    return jnp.maximum(v, 0.0) + jnp.log(1.0 + jnp.exp(-jnp.abs(v)))


_ERF_NUM = (-2.72614225801306e-10, 2.77068142495902e-08, -2.10102402082508e-06, -5.69250639462346e-05,
            -7.34990630326855e-04, -2.95459980854025e-03, -1.60960333262415e-02)
_ERF_DEN = (-1.45660718464996e-05, -2.13374055278905e-04, -1.68282697438203e-03, -7.37332916720468e-03,
            -1.42647390514189e-02)


def _erf(v):
    v = jnp.clip(v, -4.0, 4.0)
    v2 = v * v
    num = jnp.full_like(v, _ERF_NUM[0])
    for coef in _ERF_NUM[1:]:
        num = num * v2 + coef
    den = jnp.full_like(v, _ERF_DEN[0])
    for coef in _ERF_DEN[1:]:
        den = den * v2 + coef
    return v * num / den


def _gelu(v):
    return 0.5 * v * (1.0 + _erf(v * (2.0 ** -0.5)))


def _gelu_grad(v):
    return 0.5 * (1.0 + _erf(v * (2.0 ** -0.5))) + v * jnp.exp(-0.5 * v * v) * ((2.0 * jnp.pi) ** -0.5)


def _rows_before(cat, shift, halo):
    return pltpu.roll(cat, shift, 0)[halo:]


def _rows_after(cat, shift, rows):
    return pltpu.roll(cat, cat.shape[0] - shift, 0)[:rows]


def _accumulate(ref, value, first):
    @pl.when(first)
    def _():
        ref[...] = value

    @pl.when(jnp.logical_not(first))
    def _():
        ref[...] += value


def _matmul(a, b, *, name, ta=False, tb=False, add=None, out_dtype=F32, tm=512, tn=1024, tk=2048,
            b_slots=False, after=None):
    m, k = (a.shape[1], a.shape[0]) if ta else a.shape
    b_rows, b_cols = (b.shape[1], N_DEV * b.shape[2]) if b_slots else b.shape
    n, kb = (b_rows, b_cols) if tb else (b_cols, b_rows)
    assert kb == k
    if b_slots:
        tn, tk = (tn, b.shape[2]) if tb else (b.shape[2], tk)
    tm, tn, tk = _tile(m, tm), _tile(n, tn), _tile(k, tk)
    nk = k // tk
    has_add = add is not None
    n_in = 2 + has_add + (after is not None)

    def body(*refs):
        a_ref, b_ref = refs[0], refs[1]
        add_ref = refs[2] if has_add else None
        o_ref, acc_ref = refs[n_in], refs[n_in + 1]
        kk = pl.program_id(2)
        part = _dot(_mx(a_ref[...]), _mx(b_ref[...]), ta, tb)

        def finish(total):
            if has_add:
                total = total + add_ref[...]
            o_ref[...] = total.astype(out_dtype)

        if nk == 1:
            finish(part)
        else:
            _accumulate(acc_ref, part, kk == 0)

            @pl.when(kk == nk - 1)
            def _():
                finish(acc_ref[...])

    a_spec = pl.BlockSpec((tk, tm), lambda j, i, kk: (kk, i)) if ta else pl.BlockSpec((tm, tk), lambda j, i, kk: (i, kk))
    b_block = (tn, tk) if tb else (tk, tn)
    if b_slots:
        b_spec = pl.BlockSpec((None,) + b_block, (lambda j, i, kk: (kk, j, 0)) if tb else (lambda j, i, kk: (j, kk, 0)))
    else:
        b_spec = pl.BlockSpec(b_block, (lambda j, i, kk: (j, kk)) if tb else (lambda j, i, kk: (kk, j)))
    o_spec = pl.BlockSpec((tm, tn), lambda j, i, kk: (i, j))
    in_specs = [a_spec, b_spec] + ([o_spec] if has_add else [])
    args = (a, b) + ((add,) if has_add else ())
    if after is not None:
        in_specs.append(pl.BlockSpec(memory_space=pl.ANY))
        args += (after,)
    acc_shape = (tm, tn) if nk > 1 else (8, LANES)
    return pl.pallas_call(
        body, name=name, grid=(n // tn, m // tm, nk), in_specs=in_specs, out_specs=o_spec,
        out_shape=jax.ShapeDtypeStruct((m, n), out_dtype), scratch_shapes=[pltpu.VMEM(acc_shape, F32)],
        compiler_params=_params("parallel", "parallel", "arbitrary"))(*args)


def _rmsnorm_fwd(x, w, *, name):
    s, d = x.shape
    ts = _tile(s, 512, 16)

    def body(x_ref, w_ref, o_ref):
        xf = x_ref[...]
        r = lax.rsqrt(jnp.mean(xf * xf, axis=-1, keepdims=True) + NORM_EPS)
        o_ref[...] = (xf * r * w_ref[...]).astype(BF16)

    return pl.pallas_call(
        body, name=name, grid=(s // ts,),
        in_specs=[pl.BlockSpec((ts, d), lambda i: (i, 0)), pl.BlockSpec((1, d), lambda i: (0, 0))],
        out_specs=pl.BlockSpec((ts, d), lambda i: (i, 0)),
        out_shape=jax.ShapeDtypeStruct((s, d), BF16), compiler_params=_params("parallel"))(x, w.reshape(1, d))


def _rmsnorm_bwd(dy, x, w, dres, *, name):
    s, d = x.shape
    ts = _tile(s, 256, 16)

    def body(dy_ref, x_ref, w_ref, dres_ref, dx_ref, dxb_ref, dw_ref):
        xf = x_ref[...]
        dyf = dy_ref[...]
        r = lax.rsqrt(jnp.mean(xf * xf, axis=-1, keepdims=True) + NORM_EPS)
        xh = xf * r
        dxh = dyf * w_ref[...]
        dx = dres_ref[...] + r * (dxh - xh * jnp.mean(dxh * xh, axis=-1, keepdims=True))
        dx_ref[...] = dx
        dxb_ref[...] = dx.astype(BF16)
        _accumulate(dw_ref, jnp.sum(dyf * xh, axis=0, keepdims=True), pl.program_id(0) == 0)

    row = pl.BlockSpec((ts, d), lambda i: (i, 0))
    vec = pl.BlockSpec((1, d), lambda i: (0, 0))
    return pl.pallas_call(
        body, name=name, grid=(s // ts,), in_specs=[row, row, vec, row], out_specs=[row, row, vec],
        out_shape=[jax.ShapeDtypeStruct((s, d), F32), jax.ShapeDtypeStruct((s, d), BF16),
                   jax.ShapeDtypeStruct((1, d), F32)],
        compiler_params=_params("arbitrary"))(dy, x, w.reshape(1, d), dres)


def _loss_head(x, w, target, *, name):
    s, d = x.shape
    ts = _tile(s, 256, 16)

    def body(x_ref, w_ref, t_ref, loss_ref, dx_ref, dxb_ref, dw_ref):
        first = pl.program_id(0) == 0
        xf = x_ref[...]
        wv = w_ref[...]
        r = lax.rsqrt(jnp.mean(xf * xf, axis=-1, keepdims=True) + NORM_EPS)
        xh = xf * r
        err = xh * wv - t_ref[...]
        part = 0.5 * jnp.sum(jnp.mean(err * err, axis=-1, keepdims=True), axis=0, keepdims=True)
        _accumulate(loss_ref, jnp.broadcast_to(part, (1, LANES)), first)
        dyf = err * (1.0 / d)
        dxh = dyf * wv
        dx = r * (dxh - xh * jnp.mean(dxh * xh, axis=-1, keepdims=True))
        dx_ref[...] = dx
        dxb_ref[...] = dx.astype(BF16)
        _accumulate(dw_ref, jnp.sum(dyf * xh, axis=0, keepdims=True), first)

    row = pl.BlockSpec((ts, d), lambda i: (i, 0))
    vec = pl.BlockSpec((1, d), lambda i: (0, 0))
    return pl.pallas_call(
        body, name=name, grid=(s // ts,), in_specs=[row, vec, row],
        out_specs=[pl.BlockSpec((1, LANES), lambda i: (0, 0)), row, row, vec],
        out_shape=[jax.ShapeDtypeStruct((1, LANES), F32), jax.ShapeDtypeStruct((s, d), F32),
                   jax.ShapeDtypeStruct((s, d), BF16), jax.ShapeDtypeStruct((1, d), F32)],
        compiler_params=_params("arbitrary"))(x, w.reshape(1, d), target)


def _qkv_fwd(proj, conv_w, n_heads, *, name):
    s = proj.shape[0]
    dl = n_heads * HEAD_DIM
    width = conv_w.shape[0]
    ts = _tile(s, 512, 8)

    def body(x_ref, halo_ref, w_ref, o_ref):
        i, sec = pl.program_id(0), pl.program_id(1)
        xv = x_ref[...].astype(F32)
        cat = jnp.concatenate([jnp.where(i > 0, halo_ref[...].astype(F32), 0.0), xv], axis=0)
        c = xv * w_ref[pl.ds(width - 1, 1), :]
        for sh in range(1, width):
            c = c + _rows_before(cat, sh, CONV_HALO) * w_ref[pl.ds(width - 1 - sh, 1), :]
        act = c * _sigmoid(c)

        @pl.when(sec == 2)
        def _():
            o_ref[...] = act

        @pl.when(sec < 2)
        def _():
            scale = jnp.where(sec == 0, HEAD_DIM ** -0.5, 1.0)
            for h in range(n_heads):
                cols = slice(h * HEAD_DIM, (h + 1) * HEAD_DIM)
                ah = act[:, cols]
                o_ref[:, cols] = ah * lax.rsqrt(jnp.sum(ah * ah, axis=-1, keepdims=True) + NORM_EPS) * scale

    return pl.pallas_call(
        body, name=name, grid=(s // ts, 3),
        in_specs=[pl.BlockSpec((ts, dl), lambda i, sec: (i, sec)),
                  pl.BlockSpec((CONV_HALO, dl), lambda i, sec: (jnp.maximum(i * (ts // CONV_HALO) - 1, 0), sec)),
                  pl.BlockSpec((width, dl), lambda i, sec: (0, sec))],
        out_specs=pl.BlockSpec((None, ts, dl), lambda i, sec: (sec, i, 0)),
        out_shape=jax.ShapeDtypeStruct((3, s, dl), F32),
        compiler_params=_params("parallel", "parallel"))(proj, proj, conv_w)


def _qkv_bwd(dqkv, proj, conv_w, n_heads, *, name):
    s = proj.shape[0]
    dl = n_heads * HEAD_DIM
    width = conv_w.shape[0]
    ts = _tile(s, 256, 16)
    n_tiles = s // ts
    per = ts // CONV_HALO
    rows = ts + CONV_HALO

    def body(d_ref, dnext_ref, x_ref, xprev_ref, xnext_ref, w_ref, dx_ref, dw_ref):
        sec, i = pl.program_id(0), pl.program_id(1)
        xv = x_ref[...].astype(F32)
        cat = jnp.concatenate([jnp.where(i > 0, xprev_ref[...].astype(F32), 0.0), xv, xnext_ref[...].astype(F32)],
                              axis=0)
        shifted = [cat[CONV_HALO:]] + [_rows_before(cat, sh, CONV_HALO) for sh in range(1, width)]
        c = shifted[0] * w_ref[pl.ds(width - 1, 1), :]
        for sh in range(1, width):
            c = c + shifted[sh] * w_ref[pl.ds(width - 1 - sh, 1), :]
        sig = _sigmoid(c)
        act = c * sig
        dout = jnp.concatenate([d_ref[...], dnext_ref[...]], axis=0)
        scale = jnp.where(sec == 0, HEAD_DIM ** -0.5, 1.0)
        is_v = sec == 2
        pieces = []
        for h in range(n_heads):
            cols = slice(h * HEAD_DIM, (h + 1) * HEAD_DIM)
            ah, dh = act[:, cols], dout[:, cols]
            nrm = lax.rsqrt(jnp.sum(ah * ah, axis=-1, keepdims=True) + NORM_EPS)
            dnormed = scale * nrm * (dh - ah * (nrm * nrm) * jnp.sum(dh * ah, axis=-1, keepdims=True))
            pieces.append(jnp.where(is_v, dh, dnormed))
        dact = jnp.concatenate(pieces, axis=1)
        dc = dact * sig * (1.0 + c * (1.0 - sig))
        live = jnp.logical_or(lax.broadcasted_iota(jnp.int32, (rows, 1), 0) < ts, i < n_tiles - 1)
        dc = jnp.where(live, dc, 0.0)
        dx = dc[:ts] * w_ref[pl.ds(width - 1, 1), :]
        for sh in range(1, width):
            dx = dx + _rows_after(dc, sh, ts) * w_ref[pl.ds(width - 1 - sh, 1), :]
        dx_ref[...] = dx.astype(BF16)
        dw_rows = [jnp.sum(dc[:ts] * shifted[width - 1 - j][:ts], axis=0, keepdims=True) for j in range(width)]
        _accumulate(dw_ref, jnp.concatenate(dw_rows, axis=0), i == 0)

    return pl.pallas_call(
        body, name=name, grid=(3, n_tiles),
        in_specs=[pl.BlockSpec((None, ts, dl), lambda sec, i: (sec, i, 0)),
                  pl.BlockSpec((None, CONV_HALO, dl), lambda sec, i: (sec, jnp.minimum((i + 1) * per, s // CONV_HALO - 1), 0)),
                  pl.BlockSpec((ts, dl), lambda sec, i: (i, sec)),
                  pl.BlockSpec((CONV_HALO, dl), lambda sec, i: (jnp.maximum(i * per - 1, 0), sec)),
                  pl.BlockSpec((CONV_HALO, dl), lambda sec, i: (jnp.minimum((i + 1) * per, s // CONV_HALO - 1), sec)),
                  pl.BlockSpec((width, dl), lambda sec, i: (0, sec))],
        out_specs=[pl.BlockSpec((ts, dl), lambda sec, i: (i, sec)), pl.BlockSpec((width, dl), lambda sec, i: (0, sec))],
        out_shape=[jax.ShapeDtypeStruct((s, 3 * dl), BF16), jax.ShapeDtypeStruct((width, 3 * dl), F32)],
        compiler_params=_params("parallel", "arbitrary"))(dqkv, dqkv, proj, proj, proj, conv_w)


def _ba_fwd(pba, alog_row, dtb_row, n_heads, *, name):
    s = pba.shape[0]
    ts = _tile(s, 1024, 8)

    def body(x_ref, alog_ref, dtb_ref, o_ref):
        xv = x_ref[...]
        lane = lax.broadcasted_iota(jnp.int32, xv.shape, 1)
        g = -jnp.exp(alog_ref[...]) * _softplus(xv + dtb_ref[...])
        o_ref[...] = jnp.where(lane < n_heads, _sigmoid(xv), jnp.where(lane < 2 * n_heads, g, 0.0))

    row = pl.BlockSpec((ts, LANES), lambda i: (i, 0))
    vec = pl.BlockSpec((1, LANES), lambda i: (0, 0))
    return pl.pallas_call(
        body, name=name, grid=(s // ts,), in_specs=[row, vec, vec], out_specs=row,
        out_shape=jax.ShapeDtypeStruct((s, LANES), F32), compiler_params=_params("parallel"))(pba, alog_row, dtb_row)


def _ba_bwd(dbg, pba, alog_row, dtb_row, n_heads, *, name):
    s = pba.shape[0]
    ts = _tile(s, 1024, 16)

    def body(d_ref, x_ref, alog_ref, dtb_ref, dx_ref, dalog_ref, ddtb_ref):
        first = pl.program_id(0) == 0
        xv, dv = x_ref[...], d_ref[...]
        lane = lax.broadcasted_iota(jnp.int32, xv.shape, 1)
        beta = _sigmoid(xv)
        neg_a = -jnp.exp(alog_ref[...])
        xa = xv + dtb_ref[...]
        is_a = jnp.logical_and(lane >= n_heads, lane < 2 * n_heads)
        d_xa = jnp.where(is_a, dv * neg_a * _sigmoid(xa), 0.0)
        d_g_times_g = jnp.where(is_a, dv * neg_a * _softplus(xa), 0.0)
        dx_ref[...] = jnp.where(lane < n_heads, dv * beta * (1.0 - beta), d_xa).astype(BF16)
        _accumulate(dalog_ref, jnp.sum(d_g_times_g, axis=0, keepdims=True), first)
        _accumulate(ddtb_ref, jnp.sum(d_xa, axis=0, keepdims=True), first)

    row = pl.BlockSpec((ts, LANES), lambda i: (i, 0))
    vec = pl.BlockSpec((1, LANES), lambda i: (0, 0))
    return pl.pallas_call(
        body, name=name, grid=(s // ts,), in_specs=[row, row, vec, vec], out_specs=[row, vec, vec],
        out_shape=[jax.ShapeDtypeStruct((s, LANES), BF16), jax.ShapeDtypeStruct((1, LANES), F32),
                   jax.ShapeDtypeStruct((1, LANES), F32)],
        compiler_params=_params("arbitrary"))(dbg, pba, alog_row, dtb_row)


def _bdot(a, b, ta=False, tb=False, precision=None):
    dims = (((1 if ta else 2,), (2 if tb else 1,)), ((0,), (0,)))
    return lax.dot_general(a, b, dims, precision=precision, preferred_element_type=F32)


def _split_bf16(v):
    hi = v.astype(BF16)
    return hi, (v - hi.astype(F32)).astype(BF16)


def _bdot_x3(a, b, ta=False, tb=False):
    return _bdot(a[0], b[0], ta, tb) + (_bdot(a[0], b[1], ta, tb) + _bdot(a[1], b[0], ta, tb))


def _chunk_masks():
    ri = lax.broadcasted_iota(jnp.int32, (GDN_CHUNK, GDN_CHUNK), 0)
    ci = lax.broadcasted_iota(jnp.int32, (GDN_CHUNK, GDN_CHUNK), 1)
    return ri == ci, ri >= ci, ri > ci, ri <= ci


def _row_to_col(row, eye):
    return jnp.sum(jnp.where(eye, row, 0.0), axis=2, keepdims=True)


def _col_to_row(col, eye):
    return jnp.sum(jnp.where(eye, col, 0.0), axis=1, keepdims=True)


_Gates = collections.namedtuple("_Gates", "beta_col decay e_col f_col dec")


def _gdn_gates(beta_row, g_row):
    eye, tril, _, triu = _chunk_masks()
    g_col = _row_to_col(g_row, eye)
    gc_col = jnp.sum(jnp.where(tril, g_row, 0.0), axis=2, keepdims=True)
    gc_row = jnp.sum(jnp.where(triu, g_col, 0.0), axis=1, keepdims=True)
    g_last = jnp.sum(g_row, axis=2, keepdims=True)
    decay = jnp.exp(jnp.where(tril, gc_col - gc_row, -jnp.inf))
    return _Gates(_row_to_col(beta_row, eye), decay, jnp.exp(gc_col), jnp.exp(g_last - gc_col), jnp.exp(g_last))


def _unit_lower_inverse(lmat):
    c = GDN_CHUNK
    t = jnp.where(_chunk_masks()[0], 1.0, 0.0) - lmat
    l_parts = _split_bf16(lmat)
    p = _bdot_x3(l_parts, l_parts)
    doublings = c.bit_length() - 2
    for r in range(doublings):
        p_parts = _split_bf16(p)
        if r < doublings - 1:
            both = _bdot_x3(_split_bf16(jnp.concatenate([t, p], axis=1)), p_parts)
            t, p = t + both[:, :c], both[:, c:]
        else:
            t = t + _bdot_x3(_split_bf16(t), p_parts)
    return t


def _gdn_solve(q, k, v, gates):
    strict = _chunk_masks()[2]
    kb = k * gates.beta_col
    lmat = jnp.where(strict, _bdot(_mx(kb), _mx(k), tb=True) * gates.decay, 0.0)
    tmat = _unit_lower_inverse(lmat)
    sol = _bdot_x3(_split_bf16(tmat), _split_bf16(jnp.concatenate([v * gates.beta_col, kb * gates.e_col], axis=2)))
    at = _bdot(_mx(q), _mx(k), tb=True) * gates.decay
    return lmat, tmat, sol, at


def _gdn_blocking(s):
    n_chunks = s // GDN_CHUNK
    per_step = 16 if n_chunks % 16 == 0 else n_chunks
    assert per_step % GDN_BATCH == 0
    return n_chunks, per_step, n_chunks // per_step


def _load_chunks(ref, sec, n0, hp):
    r0 = pl.multiple_of(n0 * GDN_CHUNK, GDN_BATCH * GDN_CHUNK)
    rows = pl.ds(r0, GDN_BATCH * GDN_CHUNK)
    cols = slice(hp * HEAD_DIM, (hp + 1) * HEAD_DIM)
    val = ref[rows, cols] if sec is None else ref[sec, rows, cols]
    return val.reshape(GDN_BATCH, GDN_CHUNK, HEAD_DIM)


def _store_chunks(ref, sec, n0, hp, val):
    r0 = pl.multiple_of(n0 * GDN_CHUNK, GDN_BATCH * GDN_CHUNK)
    rows = pl.ds(r0, GDN_BATCH * GDN_CHUNK)
    cols = slice(hp * HEAD_DIM, (hp + 1) * HEAD_DIM)
    flat = val.reshape(GDN_BATCH * GDN_CHUNK, HEAD_DIM)
    if sec is None:
        ref[rows, cols] = flat
    else:
        ref[sec, rows, cols] = flat


def _gdn_specs(n_heads, n_steps, per_step, order):
    rows, width = per_step * GDN_CHUNK, GDN_HEADS * HEAD_DIM
    rowvec = pl.BlockSpec((GDN_HEADS, per_step, 1, GDN_CHUNK), lambda h, j: (h, order(j), 0, 0))
    qkv = pl.BlockSpec((3, rows, width), lambda h, j: (0, order(j), h))
    act = pl.BlockSpec((rows, width), lambda h, j: (order(j), h))
    states = pl.BlockSpec((GDN_HEADS, per_step, HEAD_DIM, HEAD_DIM), lambda h, j: (h, order(j), 0, 0))
    return rowvec, qkv, act, states


def _gdn_fwd(qkv, beta_rows, g_rows, *, name):
    _, s, dl = qkv.shape
    n_heads = dl // HEAD_DIM
    c = GDN_CHUNK
    n_chunks, per_step, n_steps = _gdn_blocking(s)
    n_groups = per_step // GDN_BATCH
    heads = range(GDN_HEADS)

    def body(qkv_ref, b_ref, g_ref, o_ref, st_ref, state_ref, sol_s, at_s, qd_s, kmat_s, nmat_s, dec_s):
        @pl.when(pl.program_id(1) == 0)
        def _():
            state_ref[...] = jnp.zeros_like(state_ref)

        def solve(gi, carry):
            n0 = gi * GDN_BATCH
            grp = pl.ds(n0, GDN_BATCH)
            for hp in heads:
                q, k, v = (_load_chunks(qkv_ref, j, n0, hp) for j in range(3))
                gates = _gdn_gates(b_ref[hp, grp], g_ref[hp, grp])
                _, _, sol, at = _gdn_solve(q, k, v, gates)
                mke = _mx(k * gates.f_col)
                sol_s[hp, grp] = sol
                at_s[hp, grp] = at
                qd_s[hp, grp] = q * gates.e_col
                nmat_s[hp, grp] = _bdot(mke, _mx(sol[:, :, :HEAD_DIM]), ta=True)
                kmat_s[hp, grp] = _bdot(mke, _mx(sol[:, :, HEAD_DIM:]), ta=True)
                dec_s[hp, grp] = jnp.broadcast_to(gates.dec, (GDN_BATCH, 1, LANES))
            return carry

        lax.fori_loop(0, n_groups, solve, 0)

        def recur(n, states):
            out = []
            for hp in heads:
                state = states[hp]
                st_ref[hp, n] = state
                out.append(state * dec_s[hp, n] + nmat_s[hp, n] - _dot(_mx(kmat_s[hp, n]), _mx(state)))
            return tuple(out)

        final = lax.fori_loop(0, per_step, recur, tuple(state_ref[hp] for hp in heads))
        for hp in heads:
            state_ref[hp] = final[hp]

        def emit(gi, carry):
            n0 = gi * GDN_BATCH
            grp = pl.ds(n0, GDN_BATCH)
            for hp in heads:
                sol, mstate = sol_s[hp, grp], _mx(st_ref[hp, grp])
                v_new = sol[:, :, :HEAD_DIM] - _bdot(_mx(sol[:, :, HEAD_DIM:]), mstate)
                o = _bdot(_mx(qd_s[hp, grp]), mstate) + _bdot(_mx(at_s[hp, grp]), _mx(v_new))
                _store_chunks(o_ref, None, n0, hp, o)
            return carry

        lax.fori_loop(0, n_groups, emit, 0)

    rowvec, qkv_spec, act_spec, st_spec = _gdn_specs(n_heads, n_steps, per_step, lambda j: j)
    wide = lambda w: pltpu.VMEM((GDN_HEADS, per_step, c, w), F32)
    square = pltpu.VMEM((GDN_HEADS, per_step, HEAD_DIM, HEAD_DIM), F32)
    return pl.pallas_call(
        body, name=name, grid=(n_heads // GDN_HEADS, n_steps),
        in_specs=[qkv_spec, rowvec, rowvec], out_specs=[act_spec, st_spec],
        out_shape=[jax.ShapeDtypeStruct((s, dl), F32),
                   jax.ShapeDtypeStruct((n_heads, n_chunks, HEAD_DIM, HEAD_DIM), F32)],
        scratch_shapes=[pltpu.VMEM((GDN_HEADS, HEAD_DIM, HEAD_DIM), F32), wide(2 * HEAD_DIM), wide(c), wide(HEAD_DIM),
                        square, square, pltpu.VMEM((GDN_HEADS, per_step, 1, LANES), F32)],
        compiler_params=_params("parallel", "arbitrary"))(qkv, beta_rows, g_rows)


def _gdn_bwd(do, qkv, beta_rows, g_rows, states, *, name):
    _, s, dl = qkv.shape
    n_heads = dl // HEAD_DIM
    c = GDN_CHUNK
    n_chunks, per_step, n_steps = _gdn_blocking(s)
    n_groups = per_step // GDN_BATCH
    heads = range(GDN_HEADS)

    def body(do_ref, qkv_ref, b_ref, g_ref, st_ref, dqkv_ref, db_ref, dg_ref,
             dstate_ref, lmat_s, tmat_s, at_s, dat_s, sol_s, vn_s, dvn_s, dqd_s, kmat_s, nmat_s, dst_s, dec_s):
        @pl.when(pl.program_id(1) == 0)
        def _():
            dstate_ref[...] = jnp.zeros_like(dstate_ref)

        eye, tril, strict, _ = _chunk_masks()

        def solve(gi, carry):
            n0 = gi * GDN_BATCH
            grp = pl.ds(n0, GDN_BATCH)
            for hp in heads:
                q, k, v = (_load_chunks(qkv_ref, j, n0, hp) for j in range(3))
                gates = _gdn_gates(b_ref[hp, grp], g_ref[hp, grp])
                lmat, tmat, sol, at = _gdn_solve(q, k, v, gates)
                mstate = _mx(st_ref[hp, grp])
                md_o = _mx(_load_chunks(do_ref, None, n0, hp))
                mwc = _mx(sol[:, :, HEAD_DIM:])
                v_new = sol[:, :, :HEAD_DIM] - _bdot(mwc, mstate)
                dv_new0 = _bdot(_mx(at), md_o, ta=True)
                lmat_s[hp, grp] = lmat
                tmat_s[hp, grp] = tmat
                sol_s[hp, grp] = sol
                at_s[hp, grp] = at
                vn_s[hp, grp] = v_new
                dat_s[hp, grp] = jnp.where(tril, _bdot(md_o, _mx(v_new), tb=True), 0.0)
                dvn_s[hp, grp] = dv_new0
                dqd_s[hp, grp] = _bdot(md_o, mstate, tb=True)
                nmat_s[hp, grp] = (_bdot(_mx(q * gates.e_col), md_o, ta=True) - _bdot(mwc, _mx(dv_new0), ta=True))
                kmat_s[hp, grp] = _bdot(mwc, _mx(k * gates.f_col), ta=True)
                dec_s[hp, grp] = jnp.broadcast_to(gates.dec, (GDN_BATCH, 1, LANES))
            return carry

        lax.fori_loop(0, n_groups, solve, 0)

        def recur(idx, dstates):
            n = per_step - 1 - idx
            out = []
            for hp in heads:
                dstate = dstates[hp]
                dst_s[hp, n] = dstate
                out.append(dstate * dec_s[hp, n] + nmat_s[hp, n] - _dot(_mx(kmat_s[hp, n]), _mx(dstate)))
            return tuple(out)

        final = lax.fori_loop(0, per_step, recur, tuple(dstate_ref[hp] for hp in heads))
        for hp in heads:
            dstate_ref[hp] = final[hp]

        def emit_head(hp, n0):
            grp = pl.ds(n0, GDN_BATCH)
            q, k, v = (_load_chunks(qkv_ref, j, n0, hp) for j in range(3))
            gates = _gdn_gates(b_ref[hp, grp], g_ref[hp, grp])
            state, dstate = st_ref[hp, grp], dst_s[hp, grp]
            lmat, at, dat, sol = lmat_s[hp, grp], at_s[hp, grp], dat_s[hp, grp], sol_s[hp, grp]
            v_new, dqd = vn_s[hp, grp], dqd_s[hp, grp]
            dv_new = dvn_s[hp, grp] + _bdot(_mx(k * gates.f_col), _mx(dstate))
            dke = _bdot(_mx(v_new), _mx(dstate), tb=True)
            dwc = -_bdot(_mx(dv_new), _mx(state), tb=True)
            ddec = jnp.sum(jnp.sum(dstate * state, axis=2, keepdims=True), axis=1, keepdims=True)
            drhs = _bdot_x3(_split_bf16(tmat_s[hp, grp]), _split_bf16(jnp.concatenate([dv_new, dwc], axis=2)), ta=True)
            dvb, dkbe = drhs[:, :, :HEAD_DIM], drhs[:, :, HEAD_DIM:]
            dl_mat = jnp.where(strict, -_bdot(_mx(drhs), _mx(sol), tb=True), 0.0)
            dkk = dl_mat * gates.decay
            dqk = dat * gates.decay
            kb = k * gates.beta_col
            mk = _mx(k)
            dkb = _bdot(_mx(dkk), mk) + dkbe * gates.e_col
            dq = _bdot(_mx(dqk), mk) + dqd * gates.e_col
            dk = (_bdot(_mx(dqk), _mx(q), ta=True) + _bdot(_mx(dkk), _mx(kb), ta=True) + dke * gates.f_col
                  + dkb * gates.beta_col)
            _store_chunks(dqkv_ref, 0, n0, hp, dq)
            _store_chunks(dqkv_ref, 1, n0, hp, dk)
            _store_chunks(dqkv_ref, 2, n0, hp, dvb * gates.beta_col)
            dbeta_col = jnp.sum(dkb * k + dvb * v, axis=2, keepdims=True)
            through_decay = dl_mat * lmat + dat * at
            dke_ke = jnp.sum(dke * (k * gates.f_col), axis=2, keepdims=True)
            dgc_col = (jnp.sum(through_decay, axis=2, keepdims=True)
                       - _row_to_col(jnp.sum(through_decay, axis=1, keepdims=True), eye)
                       + jnp.sum(dqd * (q * gates.e_col) + dkbe * (kb * gates.e_col), axis=2, keepdims=True) - dke_ke)
            dg_last = jnp.sum(dke_ke, axis=1, keepdims=True) + ddec * gates.dec
            db_ref[hp, grp] = _col_to_row(dbeta_col, eye)
            dg_ref[hp, grp] = jnp.sum(jnp.where(tril, dgc_col, 0.0), axis=1, keepdims=True) + dg_last

        def emit(gi, carry):
            for hp in heads:
                emit_head(hp, gi * GDN_BATCH)
            return carry

        lax.fori_loop(0, n_groups, emit, 0)

    rowvec, qkv_spec, act_spec, st_spec = _gdn_specs(n_heads, n_steps, per_step, lambda j: n_steps - 1 - j)
    wide = lambda w: pltpu.VMEM((GDN_HEADS, per_step, c, w), F32)
    square = pltpu.VMEM((GDN_HEADS, per_step, HEAD_DIM, HEAD_DIM), F32)
    return pl.pallas_call(
        body, name=name, grid=(n_heads // GDN_HEADS, n_steps),
        in_specs=[act_spec, qkv_spec, rowvec, rowvec, st_spec], out_specs=[qkv_spec, rowvec, rowvec],
        out_shape=[jax.ShapeDtypeStruct((3, s, dl), F32),
                   jax.ShapeDtypeStruct((n_heads, n_chunks, 1, c), F32),
                   jax.ShapeDtypeStruct((n_heads, n_chunks, 1, c), F32)],
        scratch_shapes=[pltpu.VMEM((GDN_HEADS, HEAD_DIM, HEAD_DIM), F32), wide(c), wide(c), wide(c), wide(c),
                        wide(2 * HEAD_DIM), wide(HEAD_DIM), wide(HEAD_DIM), wide(HEAD_DIM),
                        square, square, square, pltpu.VMEM((GDN_HEADS, per_step, 1, LANES), F32)],
        compiler_params=_params("parallel", "arbitrary"))(do, qkv, beta_rows, g_rows, states)


def _pool_counts(tile, ts, extra, win):
    t = tile * ts + lax.broadcasted_iota(jnp.int32, (ts + extra, 1), 0)
    return jnp.minimum(t + 1, win).astype(F32)


def _pooled(cat, p_cols, tile, ts, win):
    acc, span = cat, 1
    while span < win:
        acc = acc + pltpu.roll(acc, span, 0)
        span *= 2
    return acc[POOL_HALO:] / _pool_counts(tile, ts, 0, win) - p_cols


def _merge_fwd(proj, o, gnw, pool_w, pool_scale, *, name):
    s, d = o.shape
    n_heads = d // HEAD_DIM
    n_groups, pg = pool_w.shape[0], pool_w.shape[1]
    assert n_groups == len(POOL_WINDOWS) and n_groups * pg == d and pg % HEAD_DIM == 0
    heads_per_group = pg // HEAD_DIM
    ts = _tile(s, 256, 16)

    def body(o_ref, z_ref, p_ref, halo_ref, ga_ref, gb_ref, gnw_ref, pw_ref, ps_ref, out_ref):
        i = pl.program_id(0)
        gnw_v = gnw_ref[...]
        halo = jnp.where(i > 0, halo_ref[...].astype(F32), 0.0)
        for gi, win in enumerate(POOL_WINDOWS):
            gcols = slice(gi * pg, (gi + 1) * pg)
            pv = p_ref[:, gcols].astype(F32)
            pooled = _pooled(jnp.concatenate([halo[:, gcols], pv], axis=0), pv, i, ts, win)
            yb = _dot(_mx(pooled), pw_ref[gi]) * ps_ref[:, gcols]
            for h in range(gi * heads_per_group, (gi + 1) * heads_per_group):
                cols = slice(h * HEAD_DIM, (h + 1) * HEAD_DIM)
                in_group = slice(h * HEAD_DIM - gi * pg, (h + 1) * HEAD_DIM - gi * pg)
                oh, zh = o_ref[:, cols], z_ref[:, cols].astype(F32)
                r = lax.rsqrt(jnp.mean(oh * oh, axis=-1, keepdims=True) + NORM_EPS)
                ya = oh * r * gnw_v * (zh * _sigmoid(zh))
                out_ref[:, cols] = (_sigmoid(ga_ref[:, cols].astype(F32)) * ya
                                    + _sigmoid(gb_ref[:, cols].astype(F32)) * yb[:, in_group]).astype(BF16)

    blk = lambda col: pl.BlockSpec((ts, d), lambda i, col=col: (i, col))
    vec = lambda width: pl.BlockSpec((1, width), lambda i: (0, 0))
    return pl.pallas_call(
        body, name=name, grid=(s // ts,),
        in_specs=[blk(0), blk(3), blk(4),
                  pl.BlockSpec((POOL_HALO, d), lambda i: (jnp.maximum(i * (ts // POOL_HALO) - 1, 0), 4)),
                  blk(5), blk(6), vec(HEAD_DIM), pl.BlockSpec((n_groups, pg, pg), lambda i: (0, 0, 0)), vec(d)],
        out_specs=blk(0), out_shape=jax.ShapeDtypeStruct((s, d), BF16),
        compiler_params=_params("parallel"))(o, proj, proj, proj, proj, proj, gnw.reshape(1, HEAD_DIM), pool_w,
                                              pool_scale.reshape(1, d))


def _merge_bwd(dmixed, proj, o, gnw, pool_w, pool_scale, *, name):
    s, d = o.shape
    n_heads = d // HEAD_DIM
    n_groups, pg = pool_w.shape[0], pool_w.shape[1]
    ts = _tile(s, 256, 16)

    def body(dm_ref, o_ref, z_ref, p_ref, halo_ref, ga_ref, gb_ref, gnw_ref, pw_ref, ps_ref,
             do_ref, dz_ref, dga_ref, dgb_ref, dpl_ref, dgnw_ref, dpw_ref, dps_ref):
        i = pl.program_id(0)
        first = i == 0
        gnw_v = gnw_ref[...]
        dgnw = jnp.zeros((1, HEAD_DIM), F32)
        for h in range(n_heads):
            cols = slice(h * HEAD_DIM, (h + 1) * HEAD_DIM)
            oh, zh, dm = o_ref[:, cols], z_ref[:, cols].astype(F32), dm_ref[:, cols]
            r = lax.rsqrt(jnp.mean(oh * oh, axis=-1, keepdims=True) + NORM_EPS)
            xh = oh * r
            sz = _sigmoid(zh)
            silu_z = zh * sz
            sa = _sigmoid(ga_ref[:, cols].astype(F32))
            on = xh * gnw_v
            dya = dm * sa
            dga_ref[:, cols] = (dm * on * silu_z * sa * (1.0 - sa)).astype(BF16)
            dz_ref[:, cols] = (dya * on * sz * (1.0 + zh * (1.0 - sz))).astype(BF16)
            don = dya * silu_z
            dgnw = dgnw + jnp.sum(don * xh, axis=0, keepdims=True)
            dxh = don * gnw_v
            do_ref[:, cols] = r * (dxh - xh * jnp.mean(dxh * xh, axis=-1, keepdims=True))
        _accumulate(dgnw_ref, dgnw, first)
        halo = jnp.where(first, 0.0, halo_ref[...].astype(F32))
        for gi, win in enumerate(POOL_WINDOWS):
            cols = slice(gi * pg, (gi + 1) * pg)
            pv, dm = p_ref[:, cols].astype(F32), dm_ref[:, cols]
            pooled = _pooled(jnp.concatenate([halo[:, cols], pv], axis=0), pv, i, ts, win)
            lin = _dot(_mx(pooled), pw_ref[gi])
            psv = ps_ref[:, cols]
            sb = _sigmoid(gb_ref[:, cols].astype(F32))
            dgb_ref[:, cols] = (dm * lin * psv * sb * (1.0 - sb)).astype(BF16)
            dyb = dm * sb
            _accumulate(dps_ref.at[:, cols], jnp.sum(dyb * lin, axis=0, keepdims=True), first)
            dlin = _mx(dyb * psv)
            _accumulate(dpw_ref.at[gi], _dot(_mx(pooled), dlin, ta=True), first)
            dpl_ref[:, cols] = _dot(dlin, pw_ref[gi], tb=True)

    blk = lambda col: pl.BlockSpec((ts, d), lambda i, col=col: (i, col))
    vec = lambda width: pl.BlockSpec((1, width), lambda i: (0, 0))
    pw_spec = pl.BlockSpec((n_groups, pg, pg), lambda i: (0, 0, 0))
    return pl.pallas_call(
        body, name=name, grid=(s // ts,),
        in_specs=[blk(0), blk(0), blk(3), blk(4),
                  pl.BlockSpec((POOL_HALO, d), lambda i: (jnp.maximum(i * (ts // POOL_HALO) - 1, 0), 4)),
                  blk(5), blk(6), vec(HEAD_DIM), pw_spec, vec(d)],
        out_specs=[blk(0), blk(0), blk(0), blk(0), blk(0), vec(HEAD_DIM), pw_spec, vec(d)],
        out_shape=[jax.ShapeDtypeStruct((s, d), F32), jax.ShapeDtypeStruct((s, d), BF16),
                   jax.ShapeDtypeStruct((s, d), BF16), jax.ShapeDtypeStruct((s, d), BF16),
                   jax.ShapeDtypeStruct((s, d), F32), jax.ShapeDtypeStruct((1, HEAD_DIM), F32),
                   jax.ShapeDtypeStruct((n_groups, pg, pg), F32), jax.ShapeDtypeStruct((1, d), F32)],
        compiler_params=_params("arbitrary"))(dmixed, o, proj, proj, proj, proj, proj, gnw.reshape(1, HEAD_DIM),
                                               pool_w, pool_scale.reshape(1, d))


def _pool_bwd(dpooled, *, name):
    s, d = dpooled.shape
    pg = d // len(POOL_WINDOWS)
    ts = _tile(s, 512, 16)
    n_tiles = s // ts
    per = ts // POOL_HALO

    def body(d_ref, next_ref, out_ref):
        i = pl.program_id(0)
        nxt = jnp.where(i < n_tiles - 1, next_ref[...], 0.0)
        for gi, win in enumerate(POOL_WINDOWS):
            cols = slice(gi * pg, (gi + 1) * pg)
            dv = d_ref[:, cols]
            acc = jnp.concatenate([dv, nxt[:, cols]], axis=0) / _pool_counts(i, ts, POOL_HALO, win)
            span = 1
            while span < win:
                acc = acc + pltpu.roll(acc, acc.shape[0] - span, 0)
                span *= 2
            out_ref[:, cols] = (acc[:ts] - dv).astype(BF16)

    return pl.pallas_call(
        body, name=name, grid=(n_tiles,),
        in_specs=[pl.BlockSpec((ts, d), lambda i: (i, 0)),
                  pl.BlockSpec((POOL_HALO, d), lambda i: (jnp.minimum((i + 1) * per, s // POOL_HALO - 1), 0))],
        out_specs=pl.BlockSpec((ts, d), lambda i: (i, 0)), out_shape=jax.ShapeDtypeStruct((s, d), BF16),
        compiler_params=_params("parallel"))(dpooled, dpooled)


def _ffn_tiles(s, f):
    tf = _tile(f, 1408)
    return _tile(s, 512, 16), tf, f // tf


def _ffn_act_fwd(gu, conv_w, conv_b, *, name):
    s, f = gu.shape[0], gu.shape[1] // 2
    width = conv_w.shape[0]
    ts, tf, nf = _ffn_tiles(s, f)

    def body(g_ref, halo_ref, u_ref, w_ref, b_ref, act_ref, gc_ref):
        i = pl.program_id(0)
        gv = g_ref[...].astype(F32)
        cat = jnp.concatenate([jnp.where(i > 0, halo_ref[...].astype(F32), 0.0), gv], axis=0)
        gc = gv * w_ref[pl.ds(width - 1, 1), :] + b_ref[...]
        for sh in range(1, width):
            gc = gc + _rows_before(cat, sh, CONV_HALO) * w_ref[pl.ds(width - 1 - sh, 1), :]
        gc_ref[...] = gc.astype(BF16)
        act_ref[...] = (_gelu(gc) * u_ref[...].astype(F32)).astype(BF16)

    blk = pl.BlockSpec((ts, tf), lambda i, j: (i, j))
    return pl.pallas_call(
        body, name=name, grid=(s // ts, nf),
        in_specs=[blk, pl.BlockSpec((CONV_HALO, tf), lambda i, j: (jnp.maximum(i * (ts // CONV_HALO) - 1, 0), j)),
                  pl.BlockSpec((ts, tf), lambda i, j: (i, nf + j)),
                  pl.BlockSpec((width, tf), lambda i, j: (0, j)), pl.BlockSpec((1, tf), lambda i, j: (0, j))],
        out_specs=[blk, blk],
        out_shape=[jax.ShapeDtypeStruct((s, f), BF16), jax.ShapeDtypeStruct((s, f), BF16)],
        compiler_params=_params("parallel", "parallel"))(gu, gu, gu, conv_w, conv_b.reshape(1, f))


def _ffn_act_bwd(dact, gu, gc, conv_w, *, name):
    s, f = gc.shape
    width = conv_w.shape[0]
    ts, tf, nf = _ffn_tiles(s, f)
    n_tiles = s // ts
    per = ts // CONV_HALO
    rows = ts + CONV_HALO

    def body(da_ref, da_next, gc_ref, gc_next, u_ref, u_next, g_ref, g_prev, w_ref, dg_ref, du_ref, dw_ref, db_ref):
        i = pl.program_id(1)
        first = i == 0
        da = jnp.concatenate([da_ref[...], da_next[...]], axis=0).astype(F32)
        gcv = jnp.concatenate([gc_ref[...], gc_next[...]], axis=0).astype(F32)
        uv = jnp.concatenate([u_ref[...], u_next[...]], axis=0).astype(F32)
        du_ref[...] = (da[:ts] * _gelu(gcv[:ts])).astype(BF16)
        live = jnp.logical_or(lax.broadcasted_iota(jnp.int32, (rows, 1), 0) < ts, i < n_tiles - 1)
        dgc = jnp.where(live, da * uv * _gelu_grad(gcv), 0.0)
        dgate = dgc[:ts] * w_ref[pl.ds(width - 1, 1), :]
        for sh in range(1, width):
            dgate = dgate + _rows_after(dgc, sh, ts) * w_ref[pl.ds(width - 1 - sh, 1), :]
        dg_ref[...] = dgate.astype(BF16)
        gv = g_ref[...].astype(F32)
        cat = jnp.concatenate([jnp.where(first, 0.0, g_prev[...].astype(F32)), gv], axis=0)
        shifted = [gv] + [_rows_before(cat, sh, CONV_HALO) for sh in range(1, width)]
        dw_rows = [jnp.sum(dgc[:ts] * shifted[width - 1 - j], axis=0, keepdims=True) for j in range(width)]
        _accumulate(dw_ref, jnp.concatenate(dw_rows, axis=0), first)
        _accumulate(db_ref, jnp.sum(dgc[:ts], axis=0, keepdims=True), first)

    nxt_row = lambda i: jnp.minimum((i + 1) * per, s // CONV_HALO - 1)
    main = lambda off: pl.BlockSpec((ts, tf), lambda j, i, off=off: (i, off + j))
    nxt = lambda off: pl.BlockSpec((CONV_HALO, tf), lambda j, i, off=off: (nxt_row(i), off + j))
    return pl.pallas_call(
        body, name=name, grid=(nf, n_tiles),
        in_specs=[main(0), nxt(0), main(0), nxt(0), main(nf), nxt(nf), main(0),
                  pl.BlockSpec((CONV_HALO, tf), lambda j, i: (jnp.maximum(i * per - 1, 0), j)),
                  pl.BlockSpec((width, tf), lambda j, i: (0, j))],
        out_specs=[main(0), main(0), pl.BlockSpec((width, tf), lambda j, i: (0, j)),
                   pl.BlockSpec((1, tf), lambda j, i: (0, j))],
        out_shape=[jax.ShapeDtypeStruct((s, f), BF16), jax.ShapeDtypeStruct((s, f), BF16),
                   jax.ShapeDtypeStruct((width, f), F32), jax.ShapeDtypeStruct((1, f), F32)],
        compiler_params=_params("parallel", "arbitrary"))(dact, dact, gc, gc, gu, gu, gu, gu, conv_w)


def _rows_layout(bg, n_heads):
    s = bg.shape[0]
    shape = (n_heads, s // GDN_CHUNK, 1, GDN_CHUNK)
    return bg[:, :n_heads].T.reshape(shape), bg[:, n_heads:2 * n_heads].T.reshape(shape)


def _lane_layout(dbeta_rows, dg_rows):
    n_heads = dbeta_rows.shape[0]
    s = dbeta_rows.shape[1] * GDN_CHUNK
    both = jnp.concatenate([dbeta_rows.reshape(n_heads, s), dg_rows.reshape(n_heads, s)], axis=0).T
    return jnp.pad(both, ((0, 0), (0, LANES - 2 * n_heads)))


def _layer_fwd(x, w, late_weights):
    n_heads = w["n_heads"]
    h = _rmsnorm_fwd(x, w["norm_mix_w"], name="norm_mix_fwd")
    proj = _matmul(h, w["w_main"], out_dtype=BF16, name="in_proj_fwd")
    pba = _matmul(h, w["w_ba"], name="ba_proj_fwd")
    qkv = _qkv_fwd(proj, w["conv_qkv_w"], n_heads, name="qkv_fwd")
    bg = _ba_fwd(pba, w["alog_row"], w["dtb_row"], n_heads, name="ba_fwd")
    beta_rows, g_rows = _rows_layout(bg, n_heads)
    o, states = _gdn_fwd(qkv, beta_rows, g_rows, name="gdn_fwd")
    w = dict(w, **late_weights(o))
    mixed = _merge_fwd(proj, o, w["gdn_norm_w"], w["pool_w"], w["pool_scale"], name="merge_fwd")
    x2 = _matmul(mixed, w["w_out"], add=x, name="out_proj_fwd")
    h2 = _rmsnorm_fwd(x2, w["norm_ffn_w"], name="norm_ffn_fwd")
    gu = _matmul(h2, w["w_up_slots"], b_slots=True, out_dtype=BF16, name="up_proj_fwd")
    act, gc = _ffn_act_fwd(gu, w["conv_ffn_w"], w["conv_ffn_b"], name="ffn_act_fwd")
    x3 = _matmul(act, w["w_down"], add=x2, tk=1408, name="down_proj_fwd")
    saved = dict(x=x, h=h, proj=proj, pba=pba, qkv=qkv, beta_rows=beta_rows, g_rows=g_rows, o=o, states=states,
                 mixed=mixed, x2=x2, h2=h2, gu=gu, gc=gc, act=act)
    return x3, saved, w


def _layer_bwd_ffn(dx3, dx3_bf, w, sv, after):
    g = {}
    dact = _matmul(dx3_bf, w["w_down"], tb=True, tn=1408, out_dtype=BF16, after=after, name="down_proj_dx")
    g["w_down"] = _matmul(sv["act"], dx3_bf, ta=True, out_dtype=BF16, name="down_proj_dw")
    dgate, dup, g["conv_ffn_w"], g["conv_ffn_b"] = _ffn_act_bwd(dact, sv["gu"], sv["gc"], w["conv_ffn_w"],
                                                                 name="ffn_act_bwd")
    dgu = jnp.concatenate([dgate, dup], axis=1)
    dh2 = _matmul(dgu, w["w_up_slots"], b_slots=True, tb=True, name="up_proj_dx")
    g["w_up"] = _matmul(sv["h2"], dgu, ta=True, out_dtype=BF16, name="up_proj_dw")
    dx2, dx2_bf, g["norm_ffn_w"] = _rmsnorm_bwd(dh2, sv["x2"], w["norm_ffn_w"], dx3, name="norm_ffn_bwd")
    return dx2, dx2_bf, g


def _layer_bwd_mix(dx2, dx2_bf, w, sv, after):
    n_heads = w["n_heads"]
    g = {}
    dmixed = _matmul(dx2_bf, w["w_out"], tb=True, after=after, name="out_proj_dx")
    g["w_out"] = _matmul(sv["mixed"], dx2_bf, ta=True, out_dtype=BF16, name="out_proj_dw")
    do, dz, dga, dgb, dpooled, g["gdn_norm_w"], g["pool_w"], g["pool_scale"] = _merge_bwd(
        dmixed, sv["proj"], sv["o"], w["gdn_norm_w"], w["pool_w"], w["pool_scale"], name="merge_bwd")
    dp = _pool_bwd(dpooled, name="pool_bwd")
    dqkv, dbeta_rows, dg_rows = _gdn_bwd(do, sv["qkv"], sv["beta_rows"], sv["g_rows"], sv["states"], name="gdn_bwd")
    dproj_qkv, g["conv_qkv_w"] = _qkv_bwd(dqkv, sv["proj"], w["conv_qkv_w"], n_heads, name="qkv_bwd")
    dpba, g["alog_row"], g["dtb_row"] = _ba_bwd(_lane_layout(dbeta_rows, dg_rows), sv["pba"], w["alog_row"],
                                                w["dtb_row"], n_heads, name="ba_bwd")
    dproj = jnp.concatenate([dproj_qkv, dz, dp, dga, dgb], axis=1)
    g["w_main"] = _matmul(sv["h"], dproj, ta=True, out_dtype=BF16, name="in_proj_dw")
    g["w_ba"] = _matmul(sv["h"], dpba, ta=True, name="ba_proj_dw")
    return dproj, dpba, g


def _layer_bwd_in(dproj, dpba, dx2, w, sv, after):
    dh = _matmul(dproj, w["w_main"], tb=True, after=after, name="in_proj_dx")
    dh = _matmul(dpba, w["w_ba"], tb=True, add=dh, name="ba_proj_dx")
    return _rmsnorm_bwd(dh, sv["x"], w["norm_mix_w"], dx2, name="norm_mix_bwd")


def _here():
    mx, my, mc = (lax.axis_index(a) for a in MESH_AXES)
    return (mx, my, mc), 4 * mx + 2 * my + mc


def _peer(pos, r):
    mx, my, mc = pos
    px = 1 - mx if r & 4 else mx
    py = 1 - my if r & 2 else my
    pc = 1 - mc if r & 1 else mc
    return (px, py, pc), 4 * px + 2 * py + pc


def _run_exchange(n_tensors, src_view, dst_view, sems):
    send_sems, recv_sems, local_sems = sems
    pos, me = _here()
    started = []
    for t in range(n_tensors):
        cp = pltpu.make_async_copy(src_view(t, me), dst_view(t, me), local_sems.at[t])
        cp.start()
        started.append(cp)

    def remote(t, r, landing):
        target, target_lin = _peer(pos, r)
        return pltpu.make_async_remote_copy(
            src_ref=src_view(t, target_lin), dst_ref=dst_view(t, target_lin if landing else me),
            send_sem=send_sems.at[t, r - 1], recv_sem=recv_sems.at[t, r - 1],
            device_id=target, device_id_type=pl.DeviceIdType.MESH)

    sends = []
    for r in range(1, N_DEV):
        for t in range(n_tensors):
            cp = remote(t, r, landing=False)
            cp.start()
            sends.append(cp)
    for r in range(1, N_DEV):
        for t in range(n_tensors):
            remote(t, r, landing=True).wait_recv()
    for cp in sends:
        cp.wait_send()
    for cp in started:
        cp.wait()


def _exchange_scratch(n_tensors):
    return [pltpu.SemaphoreType.DMA((n_tensors, N_DEV - 1)), pltpu.SemaphoreType.DMA((n_tensors, N_DEV - 1)),
            pltpu.SemaphoreType.DMA((n_tensors,))]


def _slot_view(ref, axis, index):
    return ref.at[(slice(None),) * axis + (index,)]


def _gather(srcs, slot_axes, *, name, after=None):
    n = len(srcs)
    n_in = n + (after is not None)

    def body(*refs):
        src_refs, out_refs = refs[:n], refs[n_in:n_in + n]
        _run_exchange(n, lambda t, to: src_refs[t], lambda t, frm: _slot_view(out_refs[t], slot_axes[t], frm),
                      refs[n_in + n:])

    hbm = pl.BlockSpec(memory_space=pltpu.HBM)
    out_shape = [jax.ShapeDtypeStruct(s.shape[:a] + (N_DEV,) + s.shape[a:], s.dtype) for s, a in zip(srcs, slot_axes)]
    in_specs = [hbm] * n + ([pl.BlockSpec(memory_space=pl.ANY)] if after is not None else [])
    return pl.pallas_call(body, name=name, in_specs=in_specs, out_specs=[hbm] * n, out_shape=out_shape,
                          scratch_shapes=_exchange_scratch(n))(*srcs, *([after] if after is not None else []))


_SIDE_EFFECT = pltpu.SideEffectType.DATAFLOW_SIDE_EFFECTING


def _split_copy(t, r, pos, src_refs, land_refs, send_sems, recv_sems, src_view, dst_view, landing):
    _, me = _here()
    target, target_lin = _peer(pos, r)
    return pltpu.make_async_remote_copy(
        src_ref=src_view(t, src_refs[t], target_lin), dst_ref=dst_view(t, land_refs[t], target_lin if landing else me),
        send_sem=send_sems.at[t * (N_DEV - 1) + r - 1], recv_sem=recv_sems.at[t * (N_DEV - 1) + r - 1],
        device_id=target, device_id_type=pl.DeviceIdType.MESH)


def _start_exchange(srcs, lands, src_view, dst_view, after, *, name):
    n = len(srcs)
    has_after = after is not None

    def body(*refs):
        src_refs, land_refs = refs[:n], refs[n:2 * n]
        outs = refs[2 * n + has_after:]
        send_sems, recv_sems, token = outs[0], outs[1], outs[2 + 2 * n]
        pos, _ = _here()
        for r in range(1, N_DEV):
            for t in range(n):
                _split_copy(t, r, pos, src_refs, land_refs, send_sems, recv_sems, src_view, dst_view, False).start()
        token[...] = jnp.zeros_like(token)

    hbm = pl.BlockSpec(memory_space=pltpu.HBM)
    sem = pl.BlockSpec(memory_space=pltpu.SEMAPHORE)
    sem_shape = pltpu.SemaphoreType.DMA((n * (N_DEV - 1),))
    through = [pltpu.HBM(t.shape, t.dtype) for t in list(srcs) + list(lands)]
    args = [pltpu.with_memory_space_constraint(t, pltpu.HBM) for t in list(srcs) + list(lands)]
    outs = pl.pallas_call(
        body, name=name, in_specs=[hbm] * (2 * n) + ([pl.BlockSpec(memory_space=pl.ANY)] if has_after else []),
        out_specs=(sem, sem, *[hbm] * (2 * n), pl.BlockSpec(memory_space=pltpu.VMEM)),
        out_shape=(sem_shape, sem_shape, *through, jax.ShapeDtypeStruct((8, LANES), F32)),
        input_output_aliases={i: 2 + i for i in range(2 * n)},
        compiler_params=pltpu.CompilerParams(has_side_effects=_SIDE_EFFECT))(*args, *([after] if has_after else []))
    return outs[0], outs[1], list(outs[2:2 + n]), list(outs[2 + n:2 + 2 * n]), outs[-1]


def _wait_exchange(send_sems, recv_sems, srcs, lands, src_view, dst_view, after, *, name):
    n = len(srcs)

    def body(*refs):
        src_refs, land_refs = refs[:n], refs[n:2 * n]
        send_refs, recv_refs = refs[2 * n], refs[2 * n + 1]
        pos, _ = _here()
        for r in range(1, N_DEV):
            for t in range(n):
                cp = _split_copy(t, r, pos, src_refs, land_refs, send_refs, recv_refs, src_view, dst_view, True)
                cp.wait_send()
                cp.wait_recv()

    hbm = pl.BlockSpec(memory_space=pltpu.HBM)
    sem = pl.BlockSpec(memory_space=pltpu.SEMAPHORE)
    outs = pl.pallas_call(
        body, name=name, in_specs=[hbm] * (2 * n) + [sem, sem, pl.BlockSpec(memory_space=pl.ANY)],
        out_specs=[hbm] * (2 * n), out_shape=[pltpu.HBM(t.shape, t.dtype) for t in list(srcs) + list(lands)],
        input_output_aliases={i: i for i in range(2 * n)},
        compiler_params=pltpu.CompilerParams(has_side_effects=_SIDE_EFFECT))(*srcs, *lands, send_sems, recv_sems, after)
    return list(outs[n:])


def _sum_slots(parts, *, name):
    _, n_lead, r_rows, cols = parts.shape
    tr = _tile(r_rows, max(16, (1 << 17) // cols // 16 * 16), 16)

    def body(p_ref, o_ref):
        total = p_ref[0].astype(F32)
        for p in range(1, N_DEV):
            total = total + p_ref[p].astype(F32)
        o_ref[...] = total

    return pl.pallas_call(
        body, name=name, grid=(n_lead, r_rows // tr),
        in_specs=[pl.BlockSpec((N_DEV, None, tr, cols), lambda a, i: (0, a, i, 0))],
        out_specs=pl.BlockSpec((None, tr, cols), lambda a, i: (a, i, 0)),
        out_shape=jax.ShapeDtypeStruct((n_lead, r_rows, cols), F32),
        compiler_params=_params("parallel", "parallel"))(parts)


_WinLayout = collections.namedtuple("_WinLayout", "shard_w n_main ba_dev ba_off n_ba slot_w")


def _win_layout(shard_w, n_main, ba_start, n_ba):
    ba_dev = ba_start // shard_w
    assert (ba_start + n_ba - 1) // shard_w == ba_dev and n_main % LANES == 0
    slot_w = -(-(LANES - 1 + shard_w) // LANES) * LANES
    return _WinLayout(shard_w, n_main, ba_dev, ba_start - ba_dev * shard_w, n_ba, slot_w)


def _main_start(lay, dev):
    return lay.shard_w * dev - jnp.where(dev > lay.ba_dev, lay.n_ba, 0)


def _slab_origin(lay, dev):
    return jnp.minimum(_main_start(lay, dev) // LANES * LANES, lay.n_main - lay.slot_w)


def _assemble_plan(lay):
    plan = [[] for _ in range(lay.n_main // LANES)]
    for dev in range(N_DEV):
        start = lay.shard_w * dev - (lay.n_ba if dev > lay.ba_dev else 0)
        width = lay.shard_w - (lay.n_ba if dev == lay.ba_dev else 0)
        origin = min(start // LANES, (lay.n_main - lay.slot_w) // LANES)
        pad = start - origin * LANES
        for t in range(pad // LANES, (pad + width - 1) // LANES + 1):
            plan[origin + t].append((dev, t))
    return plan


def _assemble_w_main(slabs, lay, *, name):
    _, d, slot_w = slabs.shape
    plan = _assemble_plan(lay)
    runs = []
    shared = []
    for tile, parts in enumerate(plan):
        if len(parts) != 1:
            shared.append((tile, parts))
        elif runs and runs[-1][2] == parts[0][0] and runs[-1][0] + runs[-1][1] == tile:
            runs[-1][1] += 1
        else:
            runs.append([tile, 1, parts[0][0], parts[0][1]])
    tr = _tile(d, 256, 16)

    def body(in_ref, out_ref):
        for first, count, dev, t0 in runs:
            out_ref[:, first * LANES:(first + count) * LANES] = in_ref[dev, :, t0 * LANES:(t0 + count) * LANES]
        for tile, parts in shared:
            total = in_ref[parts[0][0], :, parts[0][1] * LANES:(parts[0][1] + 1) * LANES]
            for dev, t in parts[1:]:
                total = total + in_ref[dev, :, t * LANES:(t + 1) * LANES]
            out_ref[:, tile * LANES:(tile + 1) * LANES] = total

    return pl.pallas_call(
        body, name=name, grid=(d // tr,),
        in_specs=[pl.BlockSpec((N_DEV, tr, slot_w), lambda i: (0, i, 0))],
        out_specs=pl.BlockSpec((tr, lay.n_main), lambda i: (i, 0)),
        out_shape=jax.ShapeDtypeStruct((d, lay.n_main), slabs.dtype), compiler_params=_params("parallel"))(slabs)


def _adam_update(w, g, m, v):
    nm = ADAM_B1 * m + (1.0 - ADAM_B1) * g
    nv = ADAM_B2 * v + (1.0 - ADAM_B2) * (g * g)
    m_hat = nm / (1.0 - ADAM_B1 ** ADAM_STEP)
    v_hat = nv / (1.0 - ADAM_B2 ** ADAM_STEP)
    return -ADAM_LR * (m_hat / (jnp.sqrt(v_hat) + ADAM_EPS) + ADAM_WD * w), nm, nv


def _adamw(w, g, m, v, *, name):
    rows, cols = w.shape
    tr = _tile(rows, max(8, (1 << 18) // cols // 8 * 8), 8)

    def body(w_ref, g_ref, m_ref, v_ref, d_ref, nm_ref, nv_ref):
        d_ref[...], nm_ref[...], nv_ref[...] = _adam_update(w_ref[...], g_ref[...], m_ref[...], v_ref[...])

    blk = pl.BlockSpec((tr, cols), lambda i: (i, 0))
    out = jax.ShapeDtypeStruct((rows, cols), F32)
    return pl.pallas_call(
        body, name=name, grid=(rows // tr,), in_specs=[blk] * 4, out_specs=[blk] * 3, out_shape=[out] * 3,
        compiler_params=_params("parallel"))(w, g, m, v)


def _adamw_nd(w, g, m, v, *, name):
    two_d = (-1, w.shape[-1])
    outs = _adamw(w.reshape(two_d), g.reshape(two_d), m.reshape(two_d), v.reshape(two_d), name=name)
    return tuple(t.reshape(w.shape) for t in outs)


def _adamw_slots(parts, w, m, v, after, *, name):
    n_layers, rows, cols = w.shape
    tr = _tile(rows, max(16, (1 << 18) // cols // 16 * 16), 16)

    def body(p_ref, w_ref, m_ref, v_ref, after_ref, g_ref, d_ref, nm_ref, nv_ref):
        total = p_ref[0].astype(F32)
        for p in range(1, N_DEV):
            total = total + p_ref[p].astype(F32)
        g_ref[...] = total
        d_ref[...], nm_ref[...], nv_ref[...] = _adam_update(w_ref[...], total, m_ref[...], v_ref[...])

    blk = pl.BlockSpec((None, tr, cols), lambda a, i: (a, i, 0))
    out = jax.ShapeDtypeStruct((n_layers, rows, cols), F32)
    return pl.pallas_call(
        body, name=name, grid=(n_layers, rows // tr),
        in_specs=[pl.BlockSpec((N_DEV, None, tr, cols), lambda a, i: (0, a, i, 0)), blk, blk, blk,
                  pl.BlockSpec(memory_space=pl.ANY)],
        out_specs=[blk] * 4, out_shape=[out] * 4,
        compiler_params=_params("parallel", "parallel"))(parts, w, m, v, after)


def _pack_rows(parts, dtype, quantum_rows):
    flat = jnp.concatenate([p.reshape(-1).astype(dtype) for p in parts])
    n = flat.shape[0]
    padded = -(-n // (LANES * quantum_rows)) * (LANES * quantum_rows)
    return jnp.pad(flat, (0, padded - n)).reshape(padded // LANES, LANES)


def _unpack(flat, shapes):
    lead = flat.shape[:-1]
    out, at = [], 0
    for shape in shapes:
        size = 1
        for dim in shape:
            size *= dim
        out.append(flat[..., at:at + size].reshape(lead + tuple(shape)))
        at += size
    return out


def _whole_from_slots(slots, axis):
    moved = jnp.moveaxis(slots, 0, axis)
    shape = moved.shape
    return moved.reshape(shape[:axis] + (shape[axis] * shape[axis + 1],) + shape[axis + 2:])


def _lane_row(vec, n_heads):
    return jnp.pad(vec, ((0, 0), (n_heads, LANES - 2 * n_heads)))[:, None, :]


REPLICATED = ("norm_mix_w", "a_log", "dt_bias", "gdn_norm_w", "pool_scale", "norm_ffn_w", "conv_ffn_b",
              "norm_final_w")
WEIGHTS = ("norm_mix_w", "w_in", "conv_qkv_w", "a_log", "dt_bias", "gdn_norm_w", "pool_w", "pool_scale", "w_out",
           "norm_ffn_w", "w_up", "conv_ffn_w", "conv_ffn_b", "w_down", "norm_final_w")
SMALL_QUANTUM_ROWS = 512


def kernel(x, norm_mix_w, w_in, conv_qkv_w, a_log, dt_bias, gdn_norm_w, pool_w, pool_scale, w_out, norm_ffn_w, w_up, conv_ffn_w, conv_ffn_b, w_down, norm_final_w, loss_target, m_norm_mix_w, m_w_in, m_conv_qkv_w, m_a_log, m_dt_bias, m_gdn_norm_w, m_pool_w, m_pool_scale, m_w_out, m_norm_ffn_w, m_w_up, m_conv_ffn_w, m_conv_ffn_b, m_w_down, m_norm_final_w, v_norm_mix_w, v_w_in, v_conv_qkv_w, v_a_log, v_dt_bias, v_gdn_norm_w, v_pool_w, v_pool_scale, v_w_out, v_norm_ffn_w, v_w_up, v_conv_ffn_w, v_conv_ffn_b, v_w_down, v_norm_final_w):
    local = dict(norm_mix_w=norm_mix_w, w_in=w_in, conv_qkv_w=conv_qkv_w, a_log=a_log, dt_bias=dt_bias,
                 gdn_norm_w=gdn_norm_w, pool_w=pool_w, pool_scale=pool_scale, w_out=w_out, norm_ffn_w=norm_ffn_w,
                 w_up=w_up, conv_ffn_w=conv_ffn_w, conv_ffn_b=conv_ffn_b, w_down=w_down, norm_final_w=norm_final_w)
    mom_m = dict(norm_mix_w=m_norm_mix_w, w_in=m_w_in, conv_qkv_w=m_conv_qkv_w, a_log=m_a_log, dt_bias=m_dt_bias,
                 gdn_norm_w=m_gdn_norm_w, pool_w=m_pool_w, pool_scale=m_pool_scale, w_out=m_w_out,
                 norm_ffn_w=m_norm_ffn_w, w_up=m_w_up, conv_ffn_w=m_conv_ffn_w, conv_ffn_b=m_conv_ffn_b,
                 w_down=m_w_down, norm_final_w=m_norm_final_w)
    mom_v = dict(norm_mix_w=v_norm_mix_w, w_in=v_w_in, conv_qkv_w=v_conv_qkv_w, a_log=v_a_log, dt_bias=v_dt_bias,
                 gdn_norm_w=v_gdn_norm_w, pool_w=v_pool_w, pool_scale=v_pool_scale, w_out=v_w_out,
                 norm_ffn_w=v_norm_ffn_w, w_up=v_w_up, conv_ffn_w=v_conv_ffn_w, conv_ffn_b=v_conv_ffn_b,
                 w_down=v_w_down, norm_final_w=v_norm_final_w)
    n_layers, n_heads = a_log.shape
    d_model = x.shape[-1]
    dl = n_heads * HEAD_DIM
    n_ba = 2 * n_heads
    assert x.shape[0] == 1 and dl == d_model and pool_scale.shape[1] == d_model
    lay = _win_layout(w_in.shape[2], N_DEV * w_in.shape[2] - n_ba, 4 * dl, n_ba)
    _, me = _here()
    is_ba_dev = me == lay.ba_dev
    my_pad = _main_start(lay, me) - _slab_origin(lay, me)
    ba_cols = slice(lay.ba_off, lay.ba_off + n_ba)

    w_in_bf = w_in.astype(BF16)
    without_ba = jnp.concatenate([w_in_bf[..., :lay.ba_off], w_in_bf[..., lay.ba_off + n_ba:],
                                  jnp.zeros(w_in.shape[:2] + (n_ba,), BF16)], axis=-1)
    slab = lax.dynamic_update_slice(jnp.zeros(w_in.shape[:2] + (lay.slot_w,), BF16),
                                    jnp.where(is_ba_dev, without_ba, w_in_bf), (0, 0, my_pad))
    ba_part = jnp.pad(jnp.where(is_ba_dev, w_in_bf[..., ba_cols], jnp.zeros((), BF16)),
                      ((0, 0), (0, 0), (0, LANES - n_ba)))
    convs = _pack_rows([conv_qkv_w, conv_ffn_w], F32, 16)
    ba_slots, conv_slots = _gather([ba_part, convs], [0, 0], name="gather_small_weights")
    conv_parts = _unpack(conv_slots.reshape(N_DEV, -1), [conv_qkv_w.shape, conv_ffn_w.shape])
    conv_qkv_whole, conv_ffn_whole = (_whole_from_slots(p, 2) for p in conv_parts)
    alog_rows, dtb_rows = _lane_row(a_log, n_heads), _lane_row(dt_bias, n_heads)

    def with_own_slot(own, axis):
        zone = lax.empty(own.shape[:axis] + (N_DEV,) + own.shape[axis:], own.dtype)
        return lax.dynamic_update_slice(zone, jnp.expand_dims(own, axis), (0,) * axis + (me,) + (0,) * (own.ndim - axis))

    slot_axis = dict(slab=0, w_up=0, w_out=0, w_down=0, pool_w=1)
    gather_groups = (("slab",), ("w_up", "w_out", "w_down", "pool_w"))
    gather_src = lambda t, ref, to: ref
    in_flight = {}
    token = conv_slots
    for l in range(n_layers):
        own = dict(slab=slab[l], w_up=w_up[l].astype(BF16), w_out=w_out[l].astype(BF16),
                   w_down=w_down[l].astype(BF16), pool_w=pool_w[l].astype(BF16))
        for part, names in zip("ab", gather_groups):
            axes = [slot_axis[n] for n in names]
            dst = lambda t, ref, frm, axes=axes: _slot_view(ref, axes[t], frm)
            *handles, token = _start_exchange([own[n] for n in names], [with_own_slot(own[n], slot_axis[n]) for n in names],
                                              gather_src, dst, token, name="gather_start_%d%s" % (l, part))
            in_flight[l, part] = (handles, dst)

    def arrived(l, part, after):
        (send_sems, recv_sems, srcs, lands), dst = in_flight[l, part]
        return _wait_exchange(send_sems, recv_sems, srcs, lands, gather_src, dst, after, name="gather_wait_%d%s" % (l, part))

    xc = x[0]
    layer_w, saved = [], []
    after = token
    for l in range(n_layers):
        slabs, = arrived(l, "a", after)
        early = dict(n_heads=n_heads, norm_mix_w=norm_mix_w[l], alog_row=alog_rows[l], dtb_row=dtb_rows[l],
                     w_main=_assemble_w_main(slabs, lay, name="assemble_w_main"), w_ba=ba_slots[lay.ba_dev, l],
                     conv_qkv_w=conv_qkv_whole[l])

        def late_weights(o, l=l):
            up_slots, out_slots, down_slots, pool_slots = arrived(l, "b", o)
            return dict(norm_ffn_w=norm_ffn_w[l], gdn_norm_w=gdn_norm_w[l], pool_scale=pool_scale[l],
                        conv_ffn_b=conv_ffn_b[l], conv_ffn_w=conv_ffn_whole[l], w_up_slots=up_slots,
                        w_out=out_slots.reshape(-1, d_model), w_down=down_slots.reshape(-1, d_model),
                        pool_w=pool_slots.reshape(pool_slots.shape[0], -1, pool_slots.shape[-1]))

        xc, sv, wl = _layer_fwd(xc, early, late_weights)
        layer_w.append(wl)
        saved.append(sv)
        after = xc
    loss_row, dx, dx_bf, d_final = _loss_head(xc, norm_final_w, loss_target[0], name="loss_head")
    loss = lax.psum(loss_row[0, 0], MESH_AXES)

    shard = {n: local[n].shape[1:] for n in ("w_up", "w_out", "w_down", "pool_w")}
    shard["w_main"] = (d_model, lay.slot_w)
    recvs = {n: lax.empty((N_DEV, n_layers) + shard[n], BF16) for n in shard}
    up_w, out_rows, down_rows, pool_rows = shard["w_up"][1], shard["w_out"][0], shard["w_down"][0], shard["pool_w"][1]
    owned = dict(w_main=lambda to: (1, _slab_origin(lay, to), lay.slot_w), w_up=lambda to: (1, to * up_w, up_w),
                 w_out=lambda to: (0, to * out_rows, out_rows), w_down=lambda to: (0, to * down_rows, down_rows),
                 pool_w=lambda to: (1, to * pool_rows, pool_rows))
    scatter_groups = dict(e=("w_up", "w_down"), l=("w_main", "w_out", "pool_w"))
    pending = {}

    def send_grads(part, l, g):
        names = scatter_groups[part]

        def src(t, ref, to):
            axis, first, length = owned[names[t]](to)
            return ref.at[(slice(None),) * axis
                          + (pl.ds(pl.multiple_of(first, LANES if axis == ref.ndim - 1 else 16), length),)]

        dst = lambda t, ref, frm: ref.at[frm, l]
        received(part, g[names[0]])
        lands = []
        for n in names:
            axis, first, length = owned[n](me)
            mine = lax.dynamic_slice_in_dim(g[n], first, length, axis=axis)
            lands.append(lax.dynamic_update_slice(recvs[n], mine[None, None], (me, l) + (0,) * mine.ndim))
        send_sems, recv_sems, grads, lands, token = _start_exchange([g[n] for n in names], lands, src, dst, None,
                                                                    name="scatter_start_%d%s" % (l, part))
        pending[part] = (send_sems, recv_sems, grads, lands, src, dst, "scatter_wait_%d%s" % (l, part))
        return token

    def received(part, after):
        if part in pending:
            *args, name = pending.pop(part)
            recvs.update(zip(scatter_groups[part], _wait_exchange(*args, after=after, name=name)))

    layer_grads = [None] * n_layers
    token = None
    for l in reversed(range(n_layers)):
        dx2, dx2_bf, g = _layer_bwd_ffn(dx, dx_bf, layer_w[l], saved[l], token)
        token = send_grads("e", l, g)
        dproj, dpba, g_mix = _layer_bwd_mix(dx2, dx2_bf, layer_w[l], saved[l], token)
        g.update(g_mix)
        g["pool_w"] = g["pool_w"].astype(BF16)
        token = send_grads("l", l, g)
        dx, dx_bf, g["norm_mix_w"] = _layer_bwd_in(dproj, dpba, dx2, layer_w[l], saved[l], token)
        layer_grads[l] = g
    received("e", token)
    grad_x = dx
    stack = lambda name: jnp.stack([g[name] for g in layer_grads])

    grad, delta, new_m, new_v = {}, {}, {}, {}
    flat = lambda t, lead: t.reshape(t.shape[:lead] + (-1, t.shape[-1]))

    def update_shard(n, after):
        outs = _adamw_slots(flat(recvs[n], 2), flat(local[n], 1), flat(mom_m[n], 1), flat(mom_v[n], 1), after,
                            name="adamw_" + n)
        grad[n], delta[n], new_m[n], new_v[n] = (t.reshape(local[n].shape) for t in outs)

    update_shard("w_up", token)
    update_shard("w_down", token)

    small_names = REPLICATED + ("conv_qkv_w", "conv_ffn_w", "w_ba")
    g_small = dict(norm_mix_w=stack("norm_mix_w")[:, 0], a_log=stack("alog_row")[:, 0, n_heads:n_ba],
                   dt_bias=stack("dtb_row")[:, 0, n_heads:n_ba], gdn_norm_w=stack("gdn_norm_w")[:, 0],
                   pool_scale=stack("pool_scale")[:, 0], norm_ffn_w=stack("norm_ffn_w")[:, 0],
                   conv_ffn_b=stack("conv_ffn_b")[:, 0], norm_final_w=d_final[0], conv_qkv_w=stack("conv_qkv_w"),
                   conv_ffn_w=stack("conv_ffn_w"), w_ba=stack("w_ba")[..., :n_ba])
    small_shapes = [g_small[n].shape for n in small_names]
    small_slots, = _gather([_pack_rows([g_small[n] for n in small_names], F32, SMALL_QUANTUM_ROWS)], [0],
                           name="gather_small_grads", after=new_v["w_down"])
    small_sum = _sum_slots(small_slots[:, None], name="sum_small_grads")
    grad.update(zip(small_names, _unpack(small_sum.reshape(-1), small_shapes)))
    for n in ("conv_qkv_w", "conv_ffn_w"):
        width = local[n].shape[2]
        grad[n] = lax.dynamic_slice_in_dim(grad[n], me * width, width, axis=2)

    received("l", small_sum)
    update_shard("w_out", token)
    update_shard("pool_w", token)
    main_sum = _sum_slots(recvs["w_main"], name="sum_w_main_grads")
    g_main = lax.dynamic_slice_in_dim(main_sum, my_pad, lay.shard_w, axis=2)
    with_ba = jnp.concatenate([g_main[..., :lay.ba_off], grad.pop("w_ba"),
                               g_main[..., lay.ba_off:lay.shard_w - n_ba]], axis=-1)
    grad["w_in"] = jnp.where(is_ba_dev, with_ba, g_main)
    for n in ("w_in", "conv_qkv_w", "conv_ffn_w"):
        delta[n], new_m[n], new_v[n] = _adamw_nd(local[n], grad[n], mom_m[n], mom_v[n], name="adamw_" + n)
    packed = [_pack_rows([src[n] for n in REPLICATED], F32, 8) for src in (local, grad, mom_m, mom_v)]
    rep_out = _adamw(*packed, name="adamw_replicated")
    rep_shapes = [local[n].shape for n in REPLICATED]
    for dst, arr in zip((delta, new_m, new_v), rep_out):
        dst.update(zip(REPLICATED, _unpack(arr.reshape(-1), rep_shapes)))

    return (loss, grad_x[None], *[grad[n] for n in WEIGHTS], *[delta[n] for n in WEIGHTS],
            *[new_m[n] for n in WEIGHTS], *[new_v[n] for n in WEIGHTS])
```

```python
import collections

import jax
import jax.numpy as jnp
from jax import lax
from jax.experimental import pallas as pl
from jax.experimental.pallas import tpu as pltpu

F32 = jnp.float32
BF16 = jnp.bfloat16
MESH_AXES = ("x", "y", "c")
N_DEV = 8

NORM_EPS = 1e-6
HEAD_DIM = 128
GDN_CHUNK = 64
GDN_BATCH = 16
GDN_HEADS = 2
POOL_WINDOWS = (2, 4, 8, 16)
POOL_HALO = 16
CONV_HALO = 16
LANES = 128
V7X_VMEM_LIMIT_BYTES = 56 * 1024 * 1024

ADAM_LR = 0.001
ADAM_B1 = 0.9
ADAM_B2 = 0.999
ADAM_EPS = 1e-08
ADAM_WD = 0.01
ADAM_STEP = 10


def _mx(v):
    return v.astype(BF16)


def _dot(a, b, ta=False, tb=False, precision=None):
    dims = (((0 if ta else 1,), (1 if tb else 0,)), ((), ()))
    return lax.dot_general(a, b, dims, precision=precision, preferred_element_type=F32)


def _tile(dim, target, quantum=LANES):
    if dim <= target:
        return dim
    t = (target // quantum) * quantum
    while t >= quantum:
        if dim % t == 0:
            return t
        t -= quantum
    return dim


def _params(*semantics):
    return pltpu.CompilerParams(dimension_semantics=semantics, vmem_limit_bytes=V7X_VMEM_LIMIT_BYTES)


def _sigmoid(v):
    return 0.5 * jnp.tanh(0.5 * v) + 0.5


def _softplus(v):
    return jnp.maximum(v, 0.0) + jnp.log(1.0 + jnp.exp(-jnp.abs(v)))


def _recip(v):
    r = pl.reciprocal(v, approx=True)
    return r * (2.0 - v * r)


_ERFC_P = 0.3275911
_ERFC_A = (0.254829592, -0.284496736, 1.421413741, -1.453152027, 1.061405429)


def _normal_cdf(v):
    e = jnp.exp(-0.5 * v * v)
    t = _recip(1.0 + (_ERFC_P * 2.0 ** -0.5) * jnp.abs(v))
    poly = jnp.full_like(v, _ERFC_A[-1])
    for coef in _ERFC_A[-2::-1]:
        poly = poly * t + coef
    half_tail = (0.5 * t) * poly * e
    return jnp.where(v >= 0.0, 1.0 - half_tail, half_tail), e


def _gelu(v):
    return v * _normal_cdf(v)[0]


def _gelu_and_grad(v):
    cdf, e = _normal_cdf(v)
    return v * cdf, cdf + v * e * ((2.0 * jnp.pi) ** -0.5)


def _rows_before(cat, shift, halo):
    return pltpu.roll(cat, shift, 0)[halo:]


def _rows_after(cat, shift, rows):
    return pltpu.roll(cat, cat.shape[0] - shift, 0)[:rows]


def _accumulate(ref, value, first):
    @pl.when(first)
    def _():
        ref[...] = value

    @pl.when(jnp.logical_not(first))
    def _():
        ref[...] += value


def _matmul(a, b, *, name, ta=False, tb=False, add=None, out_dtype=F32, tm=512, tn=1024, tk=2048,
            b_slots=False, after=None):
    m, k = (a.shape[1], a.shape[0]) if ta else a.shape
    b_rows, b_cols = (b.shape[1], N_DEV * b.shape[2]) if b_slots else b.shape
    n, kb = (b_rows, b_cols) if tb else (b_cols, b_rows)
    assert kb == k
    if b_slots:
        tn, tk = (tn, b.shape[2]) if tb else (b.shape[2], tk)
    tm, tn, tk = _tile(m, tm), _tile(n, tn), _tile(k, tk)
    nk = k // tk
    has_add = add is not None
    n_in = 2 + has_add + (after is not None)

    def body(*refs):
        a_ref, b_ref = refs[0], refs[1]
        add_ref = refs[2] if has_add else None
        o_ref, acc_ref = refs[n_in], refs[n_in + 1]
        kk = pl.program_id(2)
        part = _dot(_mx(a_ref[...]), _mx(b_ref[...]), ta, tb)

        def finish(total):
            if has_add:
                total = total + add_ref[...]
            o_ref[...] = total.astype(out_dtype)

        if nk == 1:
            finish(part)
        else:
            _accumulate(acc_ref, part, kk == 0)

            @pl.when(kk == nk - 1)
            def _():
                finish(acc_ref[...])

    a_spec = pl.BlockSpec((tk, tm), lambda j, i, kk: (kk, i)) if ta else pl.BlockSpec((tm, tk), lambda j, i, kk: (i, kk))
    b_block = (tn, tk) if tb else (tk, tn)
    if b_slots:
        b_spec = pl.BlockSpec((None,) + b_block, (lambda j, i, kk: (kk, j, 0)) if tb else (lambda j, i, kk: (j, kk, 0)))
    else:
        b_spec = pl.BlockSpec(b_block, (lambda j, i, kk: (j, kk)) if tb else (lambda j, i, kk: (kk, j)))
    o_spec = pl.BlockSpec((tm, tn), lambda j, i, kk: (i, j))
    in_specs = [a_spec, b_spec] + ([o_spec] if has_add else [])
    args = (a, b) + ((add,) if has_add else ())
    if after is not None:
        in_specs.append(pl.BlockSpec(memory_space=pl.ANY))
        args += (after,)
    acc_shape = (tm, tn) if nk > 1 else (8, LANES)
    return pl.pallas_call(
        body, name=name, grid=(n // tn, m // tm, nk), in_specs=in_specs, out_specs=o_spec,
        out_shape=jax.ShapeDtypeStruct((m, n), out_dtype), scratch_shapes=[pltpu.VMEM(acc_shape, F32)],
        compiler_params=_params("parallel", "parallel", "arbitrary"))(*args)


def _rmsnorm_fwd(x, w, *, name):
    s, d = x.shape
    ts = _tile(s, 512, 16)

    def body(x_ref, w_ref, o_ref):
        xf = x_ref[...]
        r = lax.rsqrt(jnp.mean(xf * xf, axis=-1, keepdims=True) + NORM_EPS)
        o_ref[...] = (xf * r * w_ref[...]).astype(BF16)

    return pl.pallas_call(
        body, name=name, grid=(s // ts,),
        in_specs=[pl.BlockSpec((ts, d), lambda i: (i, 0)), pl.BlockSpec((1, d), lambda i: (0, 0))],
        out_specs=pl.BlockSpec((ts, d), lambda i: (i, 0)),
        out_shape=jax.ShapeDtypeStruct((s, d), BF16), compiler_params=_params("parallel"))(x, w.reshape(1, d))


def _rmsnorm_bwd(dy, x, w, dres, *, name):
    s, d = x.shape
    ts = _tile(s, 256, 16)

    def body(dy_ref, x_ref, w_ref, dres_ref, dx_ref, dxb_ref, dw_ref):
        xf = x_ref[...]
        dyf = dy_ref[...]
        r = lax.rsqrt(jnp.mean(xf * xf, axis=-1, keepdims=True) + NORM_EPS)
        xh = xf * r
        dxh = dyf * w_ref[...]
        dx = dres_ref[...] + r * (dxh - xh * jnp.mean(dxh * xh, axis=-1, keepdims=True))
        dx_ref[...] = dx
        dxb_ref[...] = dx.astype(BF16)
        _accumulate(dw_ref, jnp.sum(dyf * xh, axis=0, keepdims=True), pl.program_id(0) == 0)

    row = pl.BlockSpec((ts, d), lambda i: (i, 0))
    vec = pl.BlockSpec((1, d), lambda i: (0, 0))
    return pl.pallas_call(
        body, name=name, grid=(s // ts,), in_specs=[row, row, vec, row], out_specs=[row, row, vec],
        out_shape=[jax.ShapeDtypeStruct((s, d), F32), jax.ShapeDtypeStruct((s, d), BF16),
                   jax.ShapeDtypeStruct((1, d), F32)],
        compiler_params=_params("arbitrary"))(dy, x, w.reshape(1, d), dres)


def _loss_head(x, w, target, *, name):
    s, d = x.shape
    ts = _tile(s, 256, 16)

    def body(x_ref, w_ref, t_ref, loss_ref, dx_ref, dxb_ref, dw_ref):
        first = pl.program_id(0) == 0
        xf = x_ref[...]
        wv = w_ref[...]
        r = lax.rsqrt(jnp.mean(xf * xf, axis=-1, keepdims=True) + NORM_EPS)
        xh = xf * r
        err = xh * wv - t_ref[...]
        part = 0.5 * jnp.sum(jnp.mean(err * err, axis=-1, keepdims=True), axis=0, keepdims=True)
        _accumulate(loss_ref, jnp.broadcast_to(part, (1, LANES)), first)
        dyf = err * (1.0 / d)
        dxh = dyf * wv
        dx = r * (dxh - xh * jnp.mean(dxh * xh, axis=-1, keepdims=True))
        dx_ref[...] = dx
        dxb_ref[...] = dx.astype(BF16)
        _accumulate(dw_ref, jnp.sum(dyf * xh, axis=0, keepdims=True), first)

    row = pl.BlockSpec((ts, d), lambda i: (i, 0))
    vec = pl.BlockSpec((1, d), lambda i: (0, 0))
    return pl.pallas_call(
        body, name=name, grid=(s // ts,), in_specs=[row, vec, row],
        out_specs=[pl.BlockSpec((1, LANES), lambda i: (0, 0)), row, row, vec],
        out_shape=[jax.ShapeDtypeStruct((1, LANES), F32), jax.ShapeDtypeStruct((s, d), F32),
                   jax.ShapeDtypeStruct((s, d), BF16), jax.ShapeDtypeStruct((1, d), F32)],
        compiler_params=_params("arbitrary"))(x, w.reshape(1, d), target)


def _qkv_fwd(proj, conv_w, n_heads, *, name):
    s = proj.shape[0]
    dl = n_heads * HEAD_DIM
    width = conv_w.shape[0]
    ts = _tile(s, 512, 8)

    def body(x_ref, halo_ref, w_ref, o_ref):
        i, sec = pl.program_id(0), pl.program_id(1)
        xv = x_ref[...].astype(F32)
        cat = jnp.concatenate([jnp.where(i > 0, halo_ref[...].astype(F32), 0.0), xv], axis=0)
        c = xv * w_ref[pl.ds(width - 1, 1), :]
        for sh in range(1, width):
            c = c + _rows_before(cat, sh, CONV_HALO) * w_ref[pl.ds(width - 1 - sh, 1), :]
        act = c * _sigmoid(c)

        @pl.when(sec == 2)
        def _():
            o_ref[...] = act

        @pl.when(sec < 2)
        def _():
            scale = jnp.where(sec == 0, HEAD_DIM ** -0.5, 1.0)
            for h in range(n_heads):
                cols = slice(h * HEAD_DIM, (h + 1) * HEAD_DIM)
                ah = act[:, cols]
                o_ref[:, cols] = ah * lax.rsqrt(jnp.sum(ah * ah, axis=-1, keepdims=True) + NORM_EPS) * scale

    return pl.pallas_call(
        body, name=name, grid=(s // ts, 3),
        in_specs=[pl.BlockSpec((ts, dl), lambda i, sec: (i, sec)),
                  pl.BlockSpec((CONV_HALO, dl), lambda i, sec: (jnp.maximum(i * (ts // CONV_HALO) - 1, 0), sec)),
                  pl.BlockSpec((width, dl), lambda i, sec: (0, sec))],
        out_specs=pl.BlockSpec((None, ts, dl), lambda i, sec: (sec, i, 0)),
        out_shape=jax.ShapeDtypeStruct((3, s, dl), F32),
        compiler_params=_params("parallel", "parallel"))(proj, proj, conv_w)


def _qkv_bwd(dqkv, proj, conv_w, n_heads, *, name):
    s = proj.shape[0]
    dl = n_heads * HEAD_DIM
    width = conv_w.shape[0]
    ts = _tile(s, 256, 16)
    n_tiles = s // ts
    per = ts // CONV_HALO
    rows = ts + CONV_HALO

    def body(d_ref, dnext_ref, x_ref, xprev_ref, xnext_ref, w_ref, dx_ref, dw_ref):
        sec, i = pl.program_id(0), pl.program_id(1)
        xv = x_ref[...].astype(F32)
        cat = jnp.concatenate([jnp.where(i > 0, xprev_ref[...].astype(F32), 0.0), xv, xnext_ref[...].astype(F32)],
                              axis=0)
        shifted = [cat[CONV_HALO:]] + [_rows_before(cat, sh, CONV_HALO) for sh in range(1, width)]
        c = shifted[0] * w_ref[pl.ds(width - 1, 1), :]
        for sh in range(1, width):
            c = c + shifted[sh] * w_ref[pl.ds(width - 1 - sh, 1), :]
        sig = _sigmoid(c)
        act = c * sig
        dout = jnp.concatenate([d_ref[...], dnext_ref[...]], axis=0)
        scale = jnp.where(sec == 0, HEAD_DIM ** -0.5, 1.0)
        is_v = sec == 2
        pieces = []
        for h in range(n_heads):
            cols = slice(h * HEAD_DIM, (h + 1) * HEAD_DIM)
            ah, dh = act[:, cols], dout[:, cols]
            nrm = lax.rsqrt(jnp.sum(ah * ah, axis=-1, keepdims=True) + NORM_EPS)
            dnormed = scale * nrm * (dh - ah * (nrm * nrm) * jnp.sum(dh * ah, axis=-1, keepdims=True))
            pieces.append(jnp.where(is_v, dh, dnormed))
        dact = jnp.concatenate(pieces, axis=1)
        dc = dact * sig * (1.0 + c * (1.0 - sig))
        live = jnp.logical_or(lax.broadcasted_iota(jnp.int32, (rows, 1), 0) < ts, i < n_tiles - 1)
        dc = jnp.where(live, dc, 0.0)
        dx = dc[:ts] * w_ref[pl.ds(width - 1, 1), :]
        for sh in range(1, width):
            dx = dx + _rows_after(dc, sh, ts) * w_ref[pl.ds(width - 1 - sh, 1), :]
        dx_ref[...] = dx.astype(BF16)
        dw_rows = [jnp.sum(dc[:ts] * shifted[width - 1 - j][:ts], axis=0, keepdims=True) for j in range(width)]
        _accumulate(dw_ref, jnp.concatenate(dw_rows, axis=0), i == 0)

    return pl.pallas_call(
        body, name=name, grid=(3, n_tiles),
        in_specs=[pl.BlockSpec((None, ts, dl), lambda sec, i: (sec, i, 0)),
                  pl.BlockSpec((None, CONV_HALO, dl), lambda sec, i: (sec, jnp.minimum((i + 1) * per, s // CONV_HALO - 1), 0)),
                  pl.BlockSpec((ts, dl), lambda sec, i: (i, sec)),
                  pl.BlockSpec((CONV_HALO, dl), lambda sec, i: (jnp.maximum(i * per - 1, 0), sec)),
                  pl.BlockSpec((CONV_HALO, dl), lambda sec, i: (jnp.minimum((i + 1) * per, s // CONV_HALO - 1), sec)),
                  pl.BlockSpec((width, dl), lambda sec, i: (0, sec))],
        out_specs=[pl.BlockSpec((ts, dl), lambda sec, i: (i, sec)), pl.BlockSpec((width, dl), lambda sec, i: (0, sec))],
        out_shape=[jax.ShapeDtypeStruct((s, 3 * dl), BF16), jax.ShapeDtypeStruct((width, 3 * dl), F32)],
        compiler_params=_params("parallel", "arbitrary"))(dqkv, dqkv, proj, proj, proj, conv_w)


def _ba_fwd(pba, alog_row, dtb_row, n_heads, *, name):
    s = pba.shape[0]
    ts = _tile(s, 1024, 8)

    def body(x_ref, alog_ref, dtb_ref, o_ref):
        xv = x_ref[...]
        lane = lax.broadcasted_iota(jnp.int32, xv.shape, 1)
        g = -jnp.exp(alog_ref[...]) * _softplus(xv + dtb_ref[...])
        o_ref[...] = jnp.where(lane < n_heads, _sigmoid(xv), jnp.where(lane < 2 * n_heads, g, 0.0))

    row = pl.BlockSpec((ts, LANES), lambda i: (i, 0))
    vec = pl.BlockSpec((1, LANES), lambda i: (0, 0))
    return pl.pallas_call(
        body, name=name, grid=(s // ts,), in_specs=[row, vec, vec], out_specs=row,
        out_shape=jax.ShapeDtypeStruct((s, LANES), F32), compiler_params=_params("parallel"))(pba, alog_row, dtb_row)


def _ba_bwd(dbg, pba, alog_row, dtb_row, n_heads, *, name):
    s = pba.shape[0]
    ts = _tile(s, 1024, 16)

    def body(d_ref, x_ref, alog_ref, dtb_ref, dx_ref, dalog_ref, ddtb_ref):
        first = pl.program_id(0) == 0
        xv, dv = x_ref[...], d_ref[...]
        lane = lax.broadcasted_iota(jnp.int32, xv.shape, 1)
        beta = _sigmoid(xv)
        neg_a = -jnp.exp(alog_ref[...])
        xa = xv + dtb_ref[...]
        is_a = jnp.logical_and(lane >= n_heads, lane < 2 * n_heads)
        d_xa = jnp.where(is_a, dv * neg_a * _sigmoid(xa), 0.0)
        d_g_times_g = jnp.where(is_a, dv * neg_a * _softplus(xa), 0.0)
        dx_ref[...] = jnp.where(lane < n_heads, dv * beta * (1.0 - beta), d_xa).astype(BF16)
        _accumulate(dalog_ref, jnp.sum(d_g_times_g, axis=0, keepdims=True), first)
        _accumulate(ddtb_ref, jnp.sum(d_xa, axis=0, keepdims=True), first)

    row = pl.BlockSpec((ts, LANES), lambda i: (i, 0))
    vec = pl.BlockSpec((1, LANES), lambda i: (0, 0))
    return pl.pallas_call(
        body, name=name, grid=(s // ts,), in_specs=[row, row, vec, vec], out_specs=[row, vec, vec],
        out_shape=[jax.ShapeDtypeStruct((s, LANES), BF16), jax.ShapeDtypeStruct((1, LANES), F32),
                   jax.ShapeDtypeStruct((1, LANES), F32)],
        compiler_params=_params("arbitrary"))(dbg, pba, alog_row, dtb_row)


def _bdot(a, b, ta=False, tb=False, precision=None):
    dims = (((1 if ta else 2,), (2 if tb else 1,)), ((0,), (0,)))
    return lax.dot_general(a, b, dims, precision=precision, preferred_element_type=F32)


def _split_bf16(v):
    hi = v.astype(BF16)
    return hi, (v - hi.astype(F32)).astype(BF16)


def _bdot_x3(a, b, ta=False, tb=False):
    return _bdot(a[0], b[0], ta, tb) + (_bdot(a[0], b[1], ta, tb) + _bdot(a[1], b[0], ta, tb))


def _chunk_masks():
    ri = lax.broadcasted_iota(jnp.int32, (GDN_CHUNK, GDN_CHUNK), 0)
    ci = lax.broadcasted_iota(jnp.int32, (GDN_CHUNK, GDN_CHUNK), 1)
    return ri == ci, ri >= ci, ri > ci, ri <= ci


def _row_to_col(row, eye):
    return jnp.sum(jnp.where(eye, row, 0.0), axis=2, keepdims=True)


def _col_to_row(col, eye):
    return jnp.sum(jnp.where(eye, col, 0.0), axis=1, keepdims=True)


_Gates = collections.namedtuple("_Gates", "beta_col decay e_col f_col dec")


def _gdn_gates(beta_row, g_row):
    eye, tril, _, triu = _chunk_masks()
    g_col = _row_to_col(g_row, eye)
    gc_col = jnp.sum(jnp.where(tril, g_row, 0.0), axis=2, keepdims=True)
    gc_row = jnp.sum(jnp.where(triu, g_col, 0.0), axis=1, keepdims=True)
    g_last = jnp.sum(g_row, axis=2, keepdims=True)
    decay = jnp.exp(jnp.where(tril, gc_col - gc_row, -jnp.inf))
    return _Gates(_row_to_col(beta_row, eye), decay, jnp.exp(gc_col), jnp.exp(g_last - gc_col), jnp.exp(g_last))


def _unit_lower_inverse(lmat):
    c = GDN_CHUNK
    t = jnp.where(_chunk_masks()[0], 1.0, 0.0) - lmat
    l_parts = _split_bf16(lmat)
    p = _bdot_x3(l_parts, l_parts)
    doublings = c.bit_length() - 2
    for r in range(doublings):
        p_parts = _split_bf16(p)
        if r < doublings - 1:
            both = _bdot_x3(_split_bf16(jnp.concatenate([t, p], axis=1)), p_parts)
            t, p = t + both[:, :c], both[:, c:]
        else:
            t = t + _bdot_x3(_split_bf16(t), p_parts)
    return t


def _gdn_solve(q, k, v, gates):
    strict = _chunk_masks()[2]
    kb = k * gates.beta_col
    lmat = jnp.where(strict, _bdot(_mx(kb), _mx(k), tb=True) * gates.decay, 0.0)
    tmat = _unit_lower_inverse(lmat)
    sol = _bdot_x3(_split_bf16(tmat), _split_bf16(jnp.concatenate([v * gates.beta_col, kb * gates.e_col], axis=2)))
    at = _bdot(_mx(q), _mx(k), tb=True) * gates.decay
    return lmat, tmat, sol, at


def _gdn_blocking(s):
    n_chunks = s // GDN_CHUNK
    per_step = 16 if n_chunks % 16 == 0 else n_chunks
    assert per_step % GDN_BATCH == 0
    return n_chunks, per_step, n_chunks // per_step


def _load_chunks(ref, sec, n0, hp):
    r0 = pl.multiple_of(n0 * GDN_CHUNK, GDN_BATCH * GDN_CHUNK)
    rows = pl.ds(r0, GDN_BATCH * GDN_CHUNK)
    cols = slice(hp * HEAD_DIM, (hp + 1) * HEAD_DIM)
    val = ref[rows, cols] if sec is None else ref[sec, rows, cols]
    return val.reshape(GDN_BATCH, GDN_CHUNK, HEAD_DIM)


def _store_chunks(ref, sec, n0, hp, val):
    r0 = pl.multiple_of(n0 * GDN_CHUNK, GDN_BATCH * GDN_CHUNK)
    rows = pl.ds(r0, GDN_BATCH * GDN_CHUNK)
    cols = slice(hp * HEAD_DIM, (hp + 1) * HEAD_DIM)
    flat = val.reshape(GDN_BATCH * GDN_CHUNK, HEAD_DIM)
    if sec is None:
        ref[rows, cols] = flat
    else:
        ref[sec, rows, cols] = flat


def _gdn_specs(n_heads, n_steps, per_step, order):
    rows, width = per_step * GDN_CHUNK, GDN_HEADS * HEAD_DIM
    rowvec = pl.BlockSpec((GDN_HEADS, per_step, 1, GDN_CHUNK), lambda h, j: (h, order(j), 0, 0))
    qkv = pl.BlockSpec((3, rows, width), lambda h, j: (0, order(j), h))
    act = pl.BlockSpec((rows, width), lambda h, j: (order(j), h))
    states = pl.BlockSpec((GDN_HEADS, per_step, HEAD_DIM, HEAD_DIM), lambda h, j: (h, order(j), 0, 0))
    return rowvec, qkv, act, states


def _gdn_fwd(qkv, beta_rows, g_rows, *, name):
    _, s, dl = qkv.shape
    n_heads = dl // HEAD_DIM
    c = GDN_CHUNK
    n_chunks, per_step, n_steps = _gdn_blocking(s)
    n_groups = per_step // GDN_BATCH
    heads = range(GDN_HEADS)

    def body(qkv_ref, b_ref, g_ref, o_ref, st_ref, state_ref, sol_s, at_s, qd_s, kmat_s, nmat_s, dec_s):
        @pl.when(pl.program_id(1) == 0)
        def _():
            state_ref[...] = jnp.zeros_like(state_ref)

        def solve(gi, carry):
            n0 = gi * GDN_BATCH
            grp = pl.ds(n0, GDN_BATCH)
            for hp in heads:
                q, k, v = (_load_chunks(qkv_ref, j, n0, hp) for j in range(3))
                gates = _gdn_gates(b_ref[hp, grp], g_ref[hp, grp])
                _, _, sol, at = _gdn_solve(q, k, v, gates)
                mke = _mx(k * gates.f_col)
                sol_s[hp, grp] = sol
                at_s[hp, grp] = at
                qd_s[hp, grp] = q * gates.e_col
                nmat_s[hp, grp] = _bdot(mke, _mx(sol[:, :, :HEAD_DIM]), ta=True)
                kmat_s[hp, grp] = _bdot(mke, _mx(sol[:, :, HEAD_DIM:]), ta=True)
                dec_s[hp, grp] = jnp.broadcast_to(gates.dec, (GDN_BATCH, 1, LANES))
            return carry

        lax.fori_loop(0, n_groups, solve, 0)

        def recur(n, states):
            out = []
            for hp in heads:
                state = states[hp]
                st_ref[hp, n] = state
                out.append(state * dec_s[hp, n] + nmat_s[hp, n] - _dot(_mx(kmat_s[hp, n]), _mx(state)))
            return tuple(out)

        final = lax.fori_loop(0, per_step, recur, tuple(state_ref[hp] for hp in heads))
        for hp in heads:
            state_ref[hp] = final[hp]

        def emit(gi, carry):
            n0 = gi * GDN_BATCH
            grp = pl.ds(n0, GDN_BATCH)
            for hp in heads:
                sol, mstate = sol_s[hp, grp], _mx(st_ref[hp, grp])
                v_new = sol[:, :, :HEAD_DIM] - _bdot(_mx(sol[:, :, HEAD_DIM:]), mstate)
                o = _bdot(_mx(qd_s[hp, grp]), mstate) + _bdot(_mx(at_s[hp, grp]), _mx(v_new))
                _store_chunks(o_ref, None, n0, hp, o)
            return carry

        lax.fori_loop(0, n_groups, emit, 0)

    rowvec, qkv_spec, act_spec, st_spec = _gdn_specs(n_heads, n_steps, per_step, lambda j: j)
    wide = lambda w: pltpu.VMEM((GDN_HEADS, per_step, c, w), F32)
    square = pltpu.VMEM((GDN_HEADS, per_step, HEAD_DIM, HEAD_DIM), F32)
    return pl.pallas_call(
        body, name=name, grid=(n_heads // GDN_HEADS, n_steps),
        in_specs=[qkv_spec, rowvec, rowvec], out_specs=[act_spec, st_spec],
        out_shape=[jax.ShapeDtypeStruct((s, dl), F32),
                   jax.ShapeDtypeStruct((n_heads, n_chunks, HEAD_DIM, HEAD_DIM), F32)],
        scratch_shapes=[pltpu.VMEM((GDN_HEADS, HEAD_DIM, HEAD_DIM), F32), wide(2 * HEAD_DIM), wide(c), wide(HEAD_DIM),
                        square, square, pltpu.VMEM((GDN_HEADS, per_step, 1, LANES), F32)],
        compiler_params=_params("parallel", "arbitrary"))(qkv, beta_rows, g_rows)


def _gdn_bwd(do, qkv, beta_rows, g_rows, states, *, name):
    _, s, dl = qkv.shape
    n_heads = dl // HEAD_DIM
    c = GDN_CHUNK
    n_chunks, per_step, n_steps = _gdn_blocking(s)
    n_groups = per_step // GDN_BATCH
    heads = range(GDN_HEADS)

    def body(do_ref, qkv_ref, b_ref, g_ref, st_ref, dqkv_ref, db_ref, dg_ref,
             dstate_ref, lmat_s, tmat_s, at_s, dat_s, sol_s, vn_s, dvn_s, dqd_s, kmat_s, nmat_s, dst_s, dec_s):
        @pl.when(pl.program_id(1) == 0)
        def _():
            dstate_ref[...] = jnp.zeros_like(dstate_ref)

        eye, tril, strict, _ = _chunk_masks()

        def solve(gi, carry):
            n0 = gi * GDN_BATCH
            grp = pl.ds(n0, GDN_BATCH)
            for hp in heads:
                q, k, v = (_load_chunks(qkv_ref, j, n0, hp) for j in range(3))
                gates = _gdn_gates(b_ref[hp, grp], g_ref[hp, grp])
                lmat, tmat, sol, at = _gdn_solve(q, k, v, gates)
                mstate = _mx(st_ref[hp, grp])
                md_o = _mx(_load_chunks(do_ref, None, n0, hp))
                mwc = _mx(sol[:, :, HEAD_DIM:])
                v_new = sol[:, :, :HEAD_DIM] - _bdot(mwc, mstate)
                dv_new0 = _bdot(_mx(at), md_o, ta=True)
                lmat_s[hp, grp] = lmat
                tmat_s[hp, grp] = tmat
                sol_s[hp, grp] = sol
                at_s[hp, grp] = at
                vn_s[hp, grp] = v_new
                dat_s[hp, grp] = jnp.where(tril, _bdot(md_o, _mx(v_new), tb=True), 0.0)
                dvn_s[hp, grp] = dv_new0
                dqd_s[hp, grp] = _bdot(md_o, mstate, tb=True)
                nmat_s[hp, grp] = (_bdot(_mx(q * gates.e_col), md_o, ta=True) - _bdot(mwc, _mx(dv_new0), ta=True))
                kmat_s[hp, grp] = _bdot(mwc, _mx(k * gates.f_col), ta=True)
                dec_s[hp, grp] = jnp.broadcast_to(gates.dec, (GDN_BATCH, 1, LANES))
            return carry

        lax.fori_loop(0, n_groups, solve, 0)

        def recur(idx, dstates):
            n = per_step - 1 - idx
            out = []
            for hp in heads:
                dstate = dstates[hp]
                dst_s[hp, n] = dstate
                out.append(dstate * dec_s[hp, n] + nmat_s[hp, n] - _dot(_mx(kmat_s[hp, n]), _mx(dstate)))
            return tuple(out)

        final = lax.fori_loop(0, per_step, recur, tuple(dstate_ref[hp] for hp in heads))
        for hp in heads:
            dstate_ref[hp] = final[hp]

        def emit_head(hp, n0):
            grp = pl.ds(n0, GDN_BATCH)
            q, k, v = (_load_chunks(qkv_ref, j, n0, hp) for j in range(3))
            gates = _gdn_gates(b_ref[hp, grp], g_ref[hp, grp])
            state, dstate = st_ref[hp, grp], dst_s[hp, grp]
            lmat, at, dat, sol = lmat_s[hp, grp], at_s[hp, grp], dat_s[hp, grp], sol_s[hp, grp]
            v_new, dqd = vn_s[hp, grp], dqd_s[hp, grp]
            dv_new = dvn_s[hp, grp] + _bdot(_mx(k * gates.f_col), _mx(dstate))
            dke = _bdot(_mx(v_new), _mx(dstate), tb=True)
            dwc = -_bdot(_mx(dv_new), _mx(state), tb=True)
            ddec = jnp.sum(jnp.sum(dstate * state, axis=2, keepdims=True), axis=1, keepdims=True)
            drhs = _bdot_x3(_split_bf16(tmat_s[hp, grp]), _split_bf16(jnp.concatenate([dv_new, dwc], axis=2)), ta=True)
            dvb, dkbe = drhs[:, :, :HEAD_DIM], drhs[:, :, HEAD_DIM:]
            dl_mat = jnp.where(strict, -_bdot(_mx(drhs), _mx(sol), tb=True), 0.0)
            dkk = dl_mat * gates.decay
            dqk = dat * gates.decay
            kb = k * gates.beta_col
            mk = _mx(k)
            dkb = _bdot(_mx(dkk), mk) + dkbe * gates.e_col
            dq = _bdot(_mx(dqk), mk) + dqd * gates.e_col
            dk = (_bdot(_mx(dqk), _mx(q), ta=True) + _bdot(_mx(dkk), _mx(kb), ta=True) + dke * gates.f_col
                  + dkb * gates.beta_col)
            _store_chunks(dqkv_ref, 0, n0, hp, dq)
            _store_chunks(dqkv_ref, 1, n0, hp, dk)
            _store_chunks(dqkv_ref, 2, n0, hp, dvb * gates.beta_col)
            dbeta_col = jnp.sum(dkb * k + dvb * v, axis=2, keepdims=True)
            through_decay = dl_mat * lmat + dat * at
            dke_ke = jnp.sum(dke * (k * gates.f_col), axis=2, keepdims=True)
            dgc_col = (jnp.sum(through_decay, axis=2, keepdims=True)
                       - _row_to_col(jnp.sum(through_decay, axis=1, keepdims=True), eye)
                       + jnp.sum(dqd * (q * gates.e_col) + dkbe * (kb * gates.e_col), axis=2, keepdims=True) - dke_ke)
            dg_last = jnp.sum(dke_ke, axis=1, keepdims=True) + ddec * gates.dec
            db_ref[hp, grp] = _col_to_row(dbeta_col, eye)
            dg_ref[hp, grp] = jnp.sum(jnp.where(tril, dgc_col, 0.0), axis=1, keepdims=True) + dg_last

        def emit(gi, carry):
            for hp in heads:
                emit_head(hp, gi * GDN_BATCH)
            return carry

        lax.fori_loop(0, n_groups, emit, 0)

    rowvec, qkv_spec, act_spec, st_spec = _gdn_specs(n_heads, n_steps, per_step, lambda j: n_steps - 1 - j)
    wide = lambda w: pltpu.VMEM((GDN_HEADS, per_step, c, w), F32)
    square = pltpu.VMEM((GDN_HEADS, per_step, HEAD_DIM, HEAD_DIM), F32)
    return pl.pallas_call(
        body, name=name, grid=(n_heads // GDN_HEADS, n_steps),
        in_specs=[act_spec, qkv_spec, rowvec, rowvec, st_spec], out_specs=[qkv_spec, rowvec, rowvec],
        out_shape=[jax.ShapeDtypeStruct((3, s, dl), F32),
                   jax.ShapeDtypeStruct((n_heads, n_chunks, 1, c), F32),
                   jax.ShapeDtypeStruct((n_heads, n_chunks, 1, c), F32)],
        scratch_shapes=[pltpu.VMEM((GDN_HEADS, HEAD_DIM, HEAD_DIM), F32), wide(c), wide(c), wide(c), wide(c),
                        wide(2 * HEAD_DIM), wide(HEAD_DIM), wide(HEAD_DIM), wide(HEAD_DIM),
                        square, square, square, pltpu.VMEM((GDN_HEADS, per_step, 1, LANES), F32)],
        compiler_params=_params("parallel", "arbitrary"))(do, qkv, beta_rows, g_rows, states)


def _pool_counts(tile, ts, extra, win):
    t = tile * ts + lax.broadcasted_iota(jnp.int32, (ts + extra, 1), 0)
    return jnp.minimum(t + 1, win).astype(F32)


def _pooled(cat, p_cols, tile, ts, win):
    acc, span = cat, 1
    while span < win:
        acc = acc + pltpu.roll(acc, span, 0)
        span *= 2
    return acc[POOL_HALO:] / _pool_counts(tile, ts, 0, win) - p_cols


def _merge_fwd(proj, o, gnw, pool_w, pool_scale, *, name):
    s, d = o.shape
    n_heads = d // HEAD_DIM
    n_groups, pg = pool_w.shape[0], pool_w.shape[1]
    assert n_groups == len(POOL_WINDOWS) and n_groups * pg == d and pg % HEAD_DIM == 0
    heads_per_group = pg // HEAD_DIM
    ts = _tile(s, 256, 16)

    def body(o_ref, z_ref, p_ref, halo_ref, ga_ref, gb_ref, gnw_ref, pw_ref, ps_ref, out_ref):
        i = pl.program_id(0)
        gnw_v = gnw_ref[...]
        halo = jnp.where(i > 0, halo_ref[...].astype(F32), 0.0)
        for gi, win in enumerate(POOL_WINDOWS):
            gcols = slice(gi * pg, (gi + 1) * pg)
            pv = p_ref[:, gcols].astype(F32)
            pooled = _pooled(jnp.concatenate([halo[:, gcols], pv], axis=0), pv, i, ts, win)
            yb = _dot(_mx(pooled), pw_ref[gi]) * ps_ref[:, gcols]
            for h in range(gi * heads_per_group, (gi + 1) * heads_per_group):
                cols = slice(h * HEAD_DIM, (h + 1) * HEAD_DIM)
                in_group = slice(h * HEAD_DIM - gi * pg, (h + 1) * HEAD_DIM - gi * pg)
                oh, zh = o_ref[:, cols], z_ref[:, cols].astype(F32)
                r = lax.rsqrt(jnp.mean(oh * oh, axis=-1, keepdims=True) + NORM_EPS)
                ya = oh * r * gnw_v * (zh * _sigmoid(zh))
                out_ref[:, cols] = (_sigmoid(ga_ref[:, cols].astype(F32)) * ya
                                    + _sigmoid(gb_ref[:, cols].astype(F32)) * yb[:, in_group]).astype(BF16)

    blk = lambda col: pl.BlockSpec((ts, d), lambda i, col=col: (i, col))
    vec = lambda width: pl.BlockSpec((1, width), lambda i: (0, 0))
    return pl.pallas_call(
        body, name=name, grid=(s // ts,),
        in_specs=[blk(0), blk(3), blk(4),
                  pl.BlockSpec((POOL_HALO, d), lambda i: (jnp.maximum(i * (ts // POOL_HALO) - 1, 0), 4)),
                  blk(5), blk(6), vec(HEAD_DIM), pl.BlockSpec((n_groups, pg, pg), lambda i: (0, 0, 0)), vec(d)],
        out_specs=blk(0), out_shape=jax.ShapeDtypeStruct((s, d), BF16),
        compiler_params=_params("parallel"))(o, proj, proj, proj, proj, proj, gnw.reshape(1, HEAD_DIM), pool_w,
                                              pool_scale.reshape(1, d))


def _merge_bwd(dmixed, proj, o, gnw, pool_w, pool_scale, *, name):
    s, d = o.shape
    n_heads = d // HEAD_DIM
    n_groups, pg = pool_w.shape[0], pool_w.shape[1]
    ts = _tile(s, 256, 16)

    def body(dm_ref, o_ref, z_ref, p_ref, halo_ref, ga_ref, gb_ref, gnw_ref, pw_ref, ps_ref,
             do_ref, dz_ref, dga_ref, dgb_ref, dpl_ref, dgnw_ref, dpw_ref, dps_ref):
        i = pl.program_id(0)
        first = i == 0
        gnw_v = gnw_ref[...]
        dgnw = jnp.zeros((1, HEAD_DIM), F32)
        for h in range(n_heads):
            cols = slice(h * HEAD_DIM, (h + 1) * HEAD_DIM)
            oh, zh, dm = o_ref[:, cols], z_ref[:, cols].astype(F32), dm_ref[:, cols]
            r = lax.rsqrt(jnp.mean(oh * oh, axis=-1, keepdims=True) + NORM_EPS)
            xh = oh * r
            sz = _sigmoid(zh)
            silu_z = zh * sz
            sa = _sigmoid(ga_ref[:, cols].astype(F32))
            on = xh * gnw_v
            dya = dm * sa
            dga_ref[:, cols] = (dm * on * silu_z * sa * (1.0 - sa)).astype(BF16)
            dz_ref[:, cols] = (dya * on * sz * (1.0 + zh * (1.0 - sz))).astype(BF16)
            don = dya * silu_z
            dgnw = dgnw + jnp.sum(don * xh, axis=0, keepdims=True)
            dxh = don * gnw_v
            do_ref[:, cols] = r * (dxh - xh * jnp.mean(dxh * xh, axis=-1, keepdims=True))
        _accumulate(dgnw_ref, dgnw, first)
        halo = jnp.where(first, 0.0, halo_ref[...].astype(F32))
        for gi, win in enumerate(POOL_WINDOWS):
            cols = slice(gi * pg, (gi + 1) * pg)
            pv, dm = p_ref[:, cols].astype(F32), dm_ref[:, cols]
            pooled = _pooled(jnp.concatenate([halo[:, cols], pv], axis=0), pv, i, ts, win)
            lin = _dot(_mx(pooled), pw_ref[gi])
            psv = ps_ref[:, cols]
            sb = _sigmoid(gb_ref[:, cols].astype(F32))
            dgb_ref[:, cols] = (dm * lin * psv * sb * (1.0 - sb)).astype(BF16)
            dyb = dm * sb
            _accumulate(dps_ref.at[:, cols], jnp.sum(dyb * lin, axis=0, keepdims=True), first)
            dlin = _mx(dyb * psv)
            _accumulate(dpw_ref.at[gi], _dot(_mx(pooled), dlin, ta=True), first)
            dpl_ref[:, cols] = _dot(dlin, pw_ref[gi], tb=True)

    blk = lambda col: pl.BlockSpec((ts, d), lambda i, col=col: (i, col))
    vec = lambda width: pl.BlockSpec((1, width), lambda i: (0, 0))
    pw_spec = pl.BlockSpec((n_groups, pg, pg), lambda i: (0, 0, 0))
    return pl.pallas_call(
        body, name=name, grid=(s // ts,),
        in_specs=[blk(0), blk(0), blk(3), blk(4),
                  pl.BlockSpec((POOL_HALO, d), lambda i: (jnp.maximum(i * (ts // POOL_HALO) - 1, 0), 4)),
                  blk(5), blk(6), vec(HEAD_DIM), pw_spec, vec(d)],
        out_specs=[blk(0), blk(0), blk(0), blk(0), blk(0), vec(HEAD_DIM), pw_spec, vec(d)],
        out_shape=[jax.ShapeDtypeStruct((s, d), F32), jax.ShapeDtypeStruct((s, d), BF16),
                   jax.ShapeDtypeStruct((s, d), BF16), jax.ShapeDtypeStruct((s, d), BF16),
                   jax.ShapeDtypeStruct((s, d), F32), jax.ShapeDtypeStruct((1, HEAD_DIM), F32),
                   jax.ShapeDtypeStruct((n_groups, pg, pg), F32), jax.ShapeDtypeStruct((1, d), F32)],
        compiler_params=_params("arbitrary"))(dmixed, o, proj, proj, proj, proj, proj, gnw.reshape(1, HEAD_DIM),
                                               pool_w, pool_scale.reshape(1, d))


def _pool_bwd(dpooled, *, name):
    s, d = dpooled.shape
    pg = d // len(POOL_WINDOWS)
    ts = _tile(s, 512, 16)
    n_tiles = s // ts
    per = ts // POOL_HALO

    def body(d_ref, next_ref, out_ref):
        i = pl.program_id(0)
        nxt = jnp.where(i < n_tiles - 1, next_ref[...], 0.0)
        for gi, win in enumerate(POOL_WINDOWS):
            cols = slice(gi * pg, (gi + 1) * pg)
            dv = d_ref[:, cols]
            acc = jnp.concatenate([dv, nxt[:, cols]], axis=0) / _pool_counts(i, ts, POOL_HALO, win)
            span = 1
            while span < win:
                acc = acc + pltpu.roll(acc, acc.shape[0] - span, 0)
                span *= 2
            out_ref[:, cols] = (acc[:ts] - dv).astype(BF16)

    return pl.pallas_call(
        body, name=name, grid=(n_tiles,),
        in_specs=[pl.BlockSpec((ts, d), lambda i: (i, 0)),
                  pl.BlockSpec((POOL_HALO, d), lambda i: (jnp.minimum((i + 1) * per, s // POOL_HALO - 1), 0))],
        out_specs=pl.BlockSpec((ts, d), lambda i: (i, 0)), out_shape=jax.ShapeDtypeStruct((s, d), BF16),
        compiler_params=_params("parallel"))(dpooled, dpooled)


def _ffn_tiles(s, f):
    tf = _tile(f, 1408)
    return _tile(s, 512, 16), tf, f // tf


def _ffn_act_fwd(gu, conv_w, conv_b, *, name):
    s, f = gu.shape[0], gu.shape[1] // 2
    width = conv_w.shape[0]
    ts, tf, nf = _ffn_tiles(s, f)

    def body(g_ref, halo_ref, u_ref, w_ref, b_ref, act_ref, gc_ref):
        i = pl.program_id(0)
        gv = g_ref[...].astype(F32)
        cat = jnp.concatenate([jnp.where(i > 0, halo_ref[...].astype(F32), 0.0), gv], axis=0)
        gc = gv * w_ref[pl.ds(width - 1, 1), :] + b_ref[...]
        for sh in range(1, width):
            gc = gc + _rows_before(cat, sh, CONV_HALO) * w_ref[pl.ds(width - 1 - sh, 1), :]
        gc_ref[...] = gc.astype(BF16)
        act_ref[...] = (_gelu(gc) * u_ref[...].astype(F32)).astype(BF16)

    blk = pl.BlockSpec((ts, tf), lambda i, j: (i, j))
    return pl.pallas_call(
        body, name=name, grid=(s // ts, nf),
        in_specs=[blk, pl.BlockSpec((CONV_HALO, tf), lambda i, j: (jnp.maximum(i * (ts // CONV_HALO) - 1, 0), j)),
                  pl.BlockSpec((ts, tf), lambda i, j: (i, nf + j)),
                  pl.BlockSpec((width, tf), lambda i, j: (0, j)), pl.BlockSpec((1, tf), lambda i, j: (0, j))],
        out_specs=[blk, blk],
        out_shape=[jax.ShapeDtypeStruct((s, f), BF16), jax.ShapeDtypeStruct((s, f), BF16)],
        compiler_params=_params("parallel", "parallel"))(gu, gu, gu, conv_w, conv_b.reshape(1, f))


def _ffn_act_bwd(dact, gu, gc, conv_w, *, name):
    s, f = gc.shape
    width = conv_w.shape[0]
    ts, tf, nf = _ffn_tiles(s, f)
    n_tiles = s // ts
    per = ts // CONV_HALO
    rows = ts + CONV_HALO

    def body(da_ref, da_next, gc_ref, gc_next, u_ref, u_next, g_ref, g_prev, w_ref, dg_ref, du_ref, dw_ref, db_ref):
        i = pl.program_id(1)
        first = i == 0
        da = jnp.concatenate([da_ref[...], da_next[...]], axis=0).astype(F32)
        gcv = jnp.concatenate([gc_ref[...], gc_next[...]], axis=0).astype(F32)
        uv = jnp.concatenate([u_ref[...], u_next[...]], axis=0).astype(F32)
        gelu, gelu_grad = _gelu_and_grad(gcv)
        du_ref[...] = (da[:ts] * gelu[:ts]).astype(BF16)
        live = jnp.logical_or(lax.broadcasted_iota(jnp.int32, (rows, 1), 0) < ts, i < n_tiles - 1)
        dgc = jnp.where(live, da * uv * gelu_grad, 0.0)
        dgate = dgc[:ts] * w_ref[pl.ds(width - 1, 1), :]
        for sh in range(1, width):
            dgate = dgate + _rows_after(dgc, sh, ts) * w_ref[pl.ds(width - 1 - sh, 1), :]
        dg_ref[...] = dgate.astype(BF16)
        gv = g_ref[...].astype(F32)
        cat = jnp.concatenate([jnp.where(first, 0.0, g_prev[...].astype(F32)), gv], axis=0)
        shifted = [gv] + [_rows_before(cat, sh, CONV_HALO) for sh in range(1, width)]
        dw_rows = [jnp.sum(dgc[:ts] * shifted[width - 1 - j], axis=0, keepdims=True) for j in range(width)]
        _accumulate(dw_ref, jnp.concatenate(dw_rows, axis=0), first)
        _accumulate(db_ref, jnp.sum(dgc[:ts], axis=0, keepdims=True), first)

    nxt_row = lambda i: jnp.minimum((i + 1) * per, s // CONV_HALO - 1)
    main = lambda off: pl.BlockSpec((ts, tf), lambda j, i, off=off: (i, off + j))
    nxt = lambda off: pl.BlockSpec((CONV_HALO, tf), lambda j, i, off=off: (nxt_row(i), off + j))
    return pl.pallas_call(
        body, name=name, grid=(nf, n_tiles),
        in_specs=[main(0), nxt(0), main(0), nxt(0), main(nf), nxt(nf), main(0),
                  pl.BlockSpec((CONV_HALO, tf), lambda j, i: (jnp.maximum(i * per - 1, 0), j)),
                  pl.BlockSpec((width, tf), lambda j, i: (0, j))],
        out_specs=[main(0), main(0), pl.BlockSpec((width, tf), lambda j, i: (0, j)),
                   pl.BlockSpec((1, tf), lambda j, i: (0, j))],
        out_shape=[jax.ShapeDtypeStruct((s, f), BF16), jax.ShapeDtypeStruct((s, f), BF16),
                   jax.ShapeDtypeStruct((width, f), F32), jax.ShapeDtypeStruct((1, f), F32)],
        compiler_params=_params("parallel", "arbitrary"))(dact, dact, gc, gc, gu, gu, gu, gu, conv_w)


def _rows_layout(bg, n_heads):
    s = bg.shape[0]
    shape = (n_heads, s // GDN_CHUNK, 1, GDN_CHUNK)
    return bg[:, :n_heads].T.reshape(shape), bg[:, n_heads:2 * n_heads].T.reshape(shape)


def _lane_layout(dbeta_rows, dg_rows):
    n_heads = dbeta_rows.shape[0]
    s = dbeta_rows.shape[1] * GDN_CHUNK
    both = jnp.concatenate([dbeta_rows.reshape(n_heads, s), dg_rows.reshape(n_heads, s)], axis=0).T
    return jnp.pad(both, ((0, 0), (0, LANES - 2 * n_heads)))


def _layer_fwd(x, w, late_weights):
    n_heads = w["n_heads"]
    h = _rmsnorm_fwd(x, w["norm_mix_w"], name="norm_mix_fwd")
    proj = _matmul(h, w["w_main"], out_dtype=BF16, name="in_proj_fwd")
    pba = _matmul(h, w["w_ba"], name="ba_proj_fwd")
    qkv = _qkv_fwd(proj, w["conv_qkv_w"], n_heads, name="qkv_fwd")
    bg = _ba_fwd(pba, w["alog_row"], w["dtb_row"], n_heads, name="ba_fwd")
    beta_rows, g_rows = _rows_layout(bg, n_heads)
    o, states = _gdn_fwd(qkv, beta_rows, g_rows, name="gdn_fwd")
    w = dict(w, **late_weights(o))
    mixed = _merge_fwd(proj, o, w["gdn_norm_w"], w["pool_w"], w["pool_scale"], name="merge_fwd")
    x2 = _matmul(mixed, w["w_out"], add=x, name="out_proj_fwd")
    h2 = _rmsnorm_fwd(x2, w["norm_ffn_w"], name="norm_ffn_fwd")
    gu = _matmul(h2, w["w_up_slots"], b_slots=True, out_dtype=BF16, name="up_proj_fwd")
    act, gc = _ffn_act_fwd(gu, w["conv_ffn_w"], w["conv_ffn_b"], name="ffn_act_fwd")
    x3 = _matmul(act, w["w_down"], add=x2, tk=1408, name="down_proj_fwd")
    saved = dict(x=x, h=h, proj=proj, pba=pba, qkv=qkv, beta_rows=beta_rows, g_rows=g_rows, o=o, states=states,
                 mixed=mixed, x2=x2, h2=h2, gu=gu, gc=gc, act=act)
    return x3, saved, w


def _layer_bwd_ffn(dx3, dx3_bf, w, sv, after):
    g = {}
    dact = _matmul(dx3_bf, w["w_down"], tb=True, tn=1408, out_dtype=BF16, after=after, name="down_proj_dx")
    g["w_down"] = _matmul(sv["act"], dx3_bf, ta=True, out_dtype=BF16, name="down_proj_dw")
    dgate, dup, g["conv_ffn_w"], g["conv_ffn_b"] = _ffn_act_bwd(dact, sv["gu"], sv["gc"], w["conv_ffn_w"],
                                                                 name="ffn_act_bwd")
    dgu = jnp.concatenate([dgate, dup], axis=1)
    dh2 = _matmul(dgu, w["w_up_slots"], b_slots=True, tb=True, name="up_proj_dx")
    g["w_up"] = _matmul(sv["h2"], dgu, ta=True, out_dtype=BF16, name="up_proj_dw")
    dx2, dx2_bf, g["norm_ffn_w"] = _rmsnorm_bwd(dh2, sv["x2"], w["norm_ffn_w"], dx3, name="norm_ffn_bwd")
    return dx2, dx2_bf, g


def _layer_bwd_mix(dx2, dx2_bf, w, sv, after):
    n_heads = w["n_heads"]
    g = {}
    dmixed = _matmul(dx2_bf, w["w_out"], tb=True, after=after, name="out_proj_dx")
    g["w_out"] = _matmul(sv["mixed"], dx2_bf, ta=True, out_dtype=BF16, name="out_proj_dw")
    do, dz, dga, dgb, dpooled, g["gdn_norm_w"], g["pool_w"], g["pool_scale"] = _merge_bwd(
        dmixed, sv["proj"], sv["o"], w["gdn_norm_w"], w["pool_w"], w["pool_scale"], name="merge_bwd")
    dp = _pool_bwd(dpooled, name="pool_bwd")
    dqkv, dbeta_rows, dg_rows = _gdn_bwd(do, sv["qkv"], sv["beta_rows"], sv["g_rows"], sv["states"], name="gdn_bwd")
    dproj_qkv, g["conv_qkv_w"] = _qkv_bwd(dqkv, sv["proj"], w["conv_qkv_w"], n_heads, name="qkv_bwd")
    dpba, g["alog_row"], g["dtb_row"] = _ba_bwd(_lane_layout(dbeta_rows, dg_rows), sv["pba"], w["alog_row"],
                                                w["dtb_row"], n_heads, name="ba_bwd")
    dproj = jnp.concatenate([dproj_qkv, dz, dp, dga, dgb], axis=1)
    g["w_main"] = _matmul(sv["h"], dproj, ta=True, out_dtype=BF16, name="in_proj_dw")
    g["w_ba"] = _matmul(sv["h"], dpba, ta=True, name="ba_proj_dw")
    return dproj, dpba, g


def _layer_bwd_in(dproj, dpba, dx2, w, sv, after):
    dh = _matmul(dproj, w["w_main"], tb=True, after=after, name="in_proj_dx")
    dh = _matmul(dpba, w["w_ba"], tb=True, add=dh, name="ba_proj_dx")
    return _rmsnorm_bwd(dh, sv["x"], w["norm_mix_w"], dx2, name="norm_mix_bwd")


def _here():
    mx, my, mc = (lax.axis_index(a) for a in MESH_AXES)
    return (mx, my, mc), 4 * mx + 2 * my + mc


def _peer(pos, r):
    mx, my, mc = pos
    px = 1 - mx if r & 4 else mx
    py = 1 - my if r & 2 else my
    pc = 1 - mc if r & 1 else mc
    return (px, py, pc), 4 * px + 2 * py + pc


def _run_exchange(n_tensors, src_view, dst_view, sems):
    send_sems, recv_sems, local_sems = sems
    pos, me = _here()
    started = []
    for t in range(n_tensors):
        cp = pltpu.make_async_copy(src_view(t, me), dst_view(t, me), local_sems.at[t])
        cp.start()
        started.append(cp)

    def remote(t, r, landing):
        target, target_lin = _peer(pos, r)
        return pltpu.make_async_remote_copy(
            src_ref=src_view(t, target_lin), dst_ref=dst_view(t, target_lin if landing else me),
            send_sem=send_sems.at[t, r - 1], recv_sem=recv_sems.at[t, r - 1],
            device_id=target, device_id_type=pl.DeviceIdType.MESH)

    sends = []
    for r in range(1, N_DEV):
        for t in range(n_tensors):
            cp = remote(t, r, landing=False)
            cp.start()
            sends.append(cp)
    for r in range(1, N_DEV):
        for t in range(n_tensors):
            remote(t, r, landing=True).wait_recv()
    for cp in sends:
        cp.wait_send()
    for cp in started:
        cp.wait()


def _exchange_scratch(n_tensors):
    return [pltpu.SemaphoreType.DMA((n_tensors, N_DEV - 1)), pltpu.SemaphoreType.DMA((n_tensors, N_DEV - 1)),
            pltpu.SemaphoreType.DMA((n_tensors,))]


def _slot_view(ref, axis, index):
    return ref.at[(slice(None),) * axis + (index,)]


def _gather(srcs, slot_axes, *, name, after=None):
    n = len(srcs)
    n_in = n + (after is not None)

    def body(*refs):
        src_refs, out_refs = refs[:n], refs[n_in:n_in + n]
        _run_exchange(n, lambda t, to: src_refs[t], lambda t, frm: _slot_view(out_refs[t], slot_axes[t], frm),
                      refs[n_in + n:])

    hbm = pl.BlockSpec(memory_space=pltpu.HBM)
    out_shape = [jax.ShapeDtypeStruct(s.shape[:a] + (N_DEV,) + s.shape[a:], s.dtype) for s, a in zip(srcs, slot_axes)]
    in_specs = [hbm] * n + ([pl.BlockSpec(memory_space=pl.ANY)] if after is not None else [])
    return pl.pallas_call(body, name=name, in_specs=in_specs, out_specs=[hbm] * n, out_shape=out_shape,
                          scratch_shapes=_exchange_scratch(n))(*srcs, *([after] if after is not None else []))


_SIDE_EFFECT = pltpu.SideEffectType.DATAFLOW_SIDE_EFFECTING


def _split_copy(t, r, pos, src_refs, land_refs, send_sems, recv_sems, src_view, dst_view, landing):
    _, me = _here()
    target, target_lin = _peer(pos, r)
    return pltpu.make_async_remote_copy(
        src_ref=src_view(t, src_refs[t], target_lin), dst_ref=dst_view(t, land_refs[t], target_lin if landing else me),
        send_sem=send_sems.at[t * (N_DEV - 1) + r - 1], recv_sem=recv_sems.at[t * (N_DEV - 1) + r - 1],
        device_id=target, device_id_type=pl.DeviceIdType.MESH)


def _start_exchange(srcs, lands, src_view, dst_view, after, *, name):
    n = len(srcs)
    has_after = after is not None

    def body(*refs):
        src_refs, land_refs = refs[:n], refs[n:2 * n]
        outs = refs[2 * n + has_after:]
        send_sems, recv_sems, token = outs[0], outs[1], outs[2 + 2 * n]
        pos, _ = _here()
        for r in range(1, N_DEV):
            for t in range(n):
                _split_copy(t, r, pos, src_refs, land_refs, send_sems, recv_sems, src_view, dst_view, False).start()
        token[...] = jnp.zeros_like(token)

    hbm = pl.BlockSpec(memory_space=pltpu.HBM)
    sem = pl.BlockSpec(memory_space=pltpu.SEMAPHORE)
    sem_shape = pltpu.SemaphoreType.DMA((n * (N_DEV - 1),))
    through = [pltpu.HBM(t.shape, t.dtype) for t in list(srcs) + list(lands)]
    args = [pltpu.with_memory_space_constraint(t, pltpu.HBM) for t in list(srcs) + list(lands)]
    outs = pl.pallas_call(
        body, name=name, in_specs=[hbm] * (2 * n) + ([pl.BlockSpec(memory_space=pl.ANY)] if has_after else []),
        out_specs=(sem, sem, *[hbm] * (2 * n), pl.BlockSpec(memory_space=pltpu.VMEM)),
        out_shape=(sem_shape, sem_shape, *through, jax.ShapeDtypeStruct((8, LANES), F32)),
        input_output_aliases={i: 2 + i for i in range(2 * n)},
        compiler_params=pltpu.CompilerParams(has_side_effects=_SIDE_EFFECT))(*args, *([after] if has_after else []))
    return outs[0], outs[1], list(outs[2:2 + n]), list(outs[2 + n:2 + 2 * n]), outs[-1]


def _wait_exchange(send_sems, recv_sems, srcs, lands, src_view, dst_view, after, *, name):
    n = len(srcs)

    def body(*refs):
        src_refs, land_refs = refs[:n], refs[n:2 * n]
        send_refs, recv_refs = refs[2 * n], refs[2 * n + 1]
        pos, _ = _here()
        for r in range(1, N_DEV):
            for t in range(n):
                cp = _split_copy(t, r, pos, src_refs, land_refs, send_refs, recv_refs, src_view, dst_view, True)
                cp.wait_send()
                cp.wait_recv()

    hbm = pl.BlockSpec(memory_space=pltpu.HBM)
    sem = pl.BlockSpec(memory_space=pltpu.SEMAPHORE)
    outs = pl.pallas_call(
        body, name=name, in_specs=[hbm] * (2 * n) + [sem, sem, pl.BlockSpec(memory_space=pl.ANY)],
        out_specs=[hbm] * (2 * n), out_shape=[pltpu.HBM(t.shape, t.dtype) for t in list(srcs) + list(lands)],
        input_output_aliases={i: i for i in range(2 * n)},
        compiler_params=pltpu.CompilerParams(has_side_effects=_SIDE_EFFECT))(*srcs, *lands, send_sems, recv_sems, after)
    return list(outs[n:])


def _sum_slots(parts, *, name):
    _, n_lead, r_rows, cols = parts.shape
    tr = _tile(r_rows, max(16, (1 << 17) // cols // 16 * 16), 16)

    def body(p_ref, o_ref):
        total = p_ref[0].astype(F32)
        for p in range(1, N_DEV):
            total = total + p_ref[p].astype(F32)
        o_ref[...] = total

    return pl.pallas_call(
        body, name=name, grid=(n_lead, r_rows // tr),
        in_specs=[pl.BlockSpec((N_DEV, None, tr, cols), lambda a, i: (0, a, i, 0))],
        out_specs=pl.BlockSpec((None, tr, cols), lambda a, i: (a, i, 0)),
        out_shape=jax.ShapeDtypeStruct((n_lead, r_rows, cols), F32),
        compiler_params=_params("parallel", "parallel"))(parts)


_WinLayout = collections.namedtuple("_WinLayout", "shard_w n_main ba_dev ba_off n_ba slot_w")


def _win_layout(shard_w, n_main, ba_start, n_ba):
    ba_dev = ba_start // shard_w
    assert (ba_start + n_ba - 1) // shard_w == ba_dev and n_main % LANES == 0
    slot_w = -(-(LANES - 1 + shard_w) // LANES) * LANES
    return _WinLayout(shard_w, n_main, ba_dev, ba_start - ba_dev * shard_w, n_ba, slot_w)


def _main_start(lay, dev):
    return lay.shard_w * dev - jnp.where(dev > lay.ba_dev, lay.n_ba, 0)


def _slab_origin(lay, dev):
    return jnp.minimum(_main_start(lay, dev) // LANES * LANES, lay.n_main - lay.slot_w)


def _assemble_plan(lay):
    plan = [[] for _ in range(lay.n_main // LANES)]
    for dev in range(N_DEV):
        start = lay.shard_w * dev - (lay.n_ba if dev > lay.ba_dev else 0)
        width = lay.shard_w - (lay.n_ba if dev == lay.ba_dev else 0)
        origin = min(start // LANES, (lay.n_main - lay.slot_w) // LANES)
        pad = start - origin * LANES
        for t in range(pad // LANES, (pad + width - 1) // LANES + 1):
            plan[origin + t].append((dev, t))
    return plan


def _assemble_w_main(slabs, lay, *, name):
    _, d, slot_w = slabs.shape
    plan = _assemble_plan(lay)
    runs = []
    shared = []
    for tile, parts in enumerate(plan):
        if len(parts) != 1:
            shared.append((tile, parts))
        elif runs and runs[-1][2] == parts[0][0] and runs[-1][0] + runs[-1][1] == tile:
            runs[-1][1] += 1
        else:
            runs.append([tile, 1, parts[0][0], parts[0][1]])
    tr = _tile(d, 256, 16)

    def body(in_ref, out_ref):
        for first, count, dev, t0 in runs:
            out_ref[:, first * LANES:(first + count) * LANES] = in_ref[dev, :, t0 * LANES:(t0 + count) * LANES]
        for tile, parts in shared:
            total = in_ref[parts[0][0], :, parts[0][1] * LANES:(parts[0][1] + 1) * LANES]
            for dev, t in parts[1:]:
                total = total + in_ref[dev, :, t * LANES:(t + 1) * LANES]
            out_ref[:, tile * LANES:(tile + 1) * LANES] = total

    return pl.pallas_call(
        body, name=name, grid=(d // tr,),
        in_specs=[pl.BlockSpec((N_DEV, tr, slot_w), lambda i: (0, i, 0))],
        out_specs=pl.BlockSpec((tr, lay.n_main), lambda i: (i, 0)),
        out_shape=jax.ShapeDtypeStruct((d, lay.n_main), slabs.dtype), compiler_params=_params("parallel"))(slabs)


def _adam_update(w, g, m, v):
    nm = ADAM_B1 * m + (1.0 - ADAM_B1) * g
    nv = ADAM_B2 * v + (1.0 - ADAM_B2) * (g * g)
    m_hat = nm / (1.0 - ADAM_B1 ** ADAM_STEP)
    v_hat = nv / (1.0 - ADAM_B2 ** ADAM_STEP)
    return -ADAM_LR * (m_hat / (jnp.sqrt(v_hat) + ADAM_EPS) + ADAM_WD * w), nm, nv


def _adamw(w, g, m, v, *, name):
    rows, cols = w.shape
    tr = _tile(rows, max(8, (1 << 18) // cols // 8 * 8), 8)

    def body(w_ref, g_ref, m_ref, v_ref, d_ref, nm_ref, nv_ref):
        d_ref[...], nm_ref[...], nv_ref[...] = _adam_update(w_ref[...], g_ref[...], m_ref[...], v_ref[...])

    blk = pl.BlockSpec((tr, cols), lambda i: (i, 0))
    out = jax.ShapeDtypeStruct((rows, cols), F32)
    return pl.pallas_call(
        body, name=name, grid=(rows // tr,), in_specs=[blk] * 4, out_specs=[blk] * 3, out_shape=[out] * 3,
        compiler_params=_params("parallel"))(w, g, m, v)


def _adamw_nd(w, g, m, v, *, name):
    two_d = (-1, w.shape[-1])
    outs = _adamw(w.reshape(two_d), g.reshape(two_d), m.reshape(two_d), v.reshape(two_d), name=name)
    return tuple(t.reshape(w.shape) for t in outs)


def _adamw_slots(parts, w, m, v, after, *, name):
    n_layers, rows, cols = w.shape
    tr = _tile(rows, max(16, (1 << 18) // cols // 16 * 16), 16)

    def body(p_ref, w_ref, m_ref, v_ref, after_ref, g_ref, d_ref, nm_ref, nv_ref):
        total = p_ref[0].astype(F32)
        for p in range(1, N_DEV):
            total = total + p_ref[p].astype(F32)
        g_ref[...] = total
        d_ref[...], nm_ref[...], nv_ref[...] = _adam_update(w_ref[...], total, m_ref[...], v_ref[...])

    blk = pl.BlockSpec((None, tr, cols), lambda a, i: (a, i, 0))
    out = jax.ShapeDtypeStruct((n_layers, rows, cols), F32)
    return pl.pallas_call(
        body, name=name, grid=(n_layers, rows // tr),
        in_specs=[pl.BlockSpec((N_DEV, None, tr, cols), lambda a, i: (0, a, i, 0)), blk, blk, blk,
                  pl.BlockSpec(memory_space=pl.ANY)],
        out_specs=[blk] * 4, out_shape=[out] * 4,
        compiler_params=_params("parallel", "parallel"))(parts, w, m, v, after)


def _pack_rows(parts, dtype, quantum_rows):
    flat = jnp.concatenate([p.reshape(-1).astype(dtype) for p in parts])
    n = flat.shape[0]
    padded = -(-n // (LANES * quantum_rows)) * (LANES * quantum_rows)
    return jnp.pad(flat, (0, padded - n)).reshape(padded // LANES, LANES)


def _unpack(flat, shapes):
    lead = flat.shape[:-1]
    out, at = [], 0
    for shape in shapes:
        size = 1
        for dim in shape:
            size *= dim
        out.append(flat[..., at:at + size].reshape(lead + tuple(shape)))
        at += size
    return out


def _whole_from_slots(slots, axis):
    moved = jnp.moveaxis(slots, 0, axis)
    shape = moved.shape
    return moved.reshape(shape[:axis] + (shape[axis] * shape[axis + 1],) + shape[axis + 2:])


def _lane_row(vec, n_heads):
    return jnp.pad(vec, ((0, 0), (n_heads, LANES - 2 * n_heads)))[:, None, :]


REPLICATED = ("norm_mix_w", "a_log", "dt_bias", "gdn_norm_w", "pool_scale", "norm_ffn_w", "conv_ffn_b",
              "norm_final_w")
WEIGHTS = ("norm_mix_w", "w_in", "conv_qkv_w", "a_log", "dt_bias", "gdn_norm_w", "pool_w", "pool_scale", "w_out",
           "norm_ffn_w", "w_up", "conv_ffn_w", "conv_ffn_b", "w_down", "norm_final_w")
SMALL_QUANTUM_ROWS = 512


def kernel(x, norm_mix_w, w_in, conv_qkv_w, a_log, dt_bias, gdn_norm_w, pool_w, pool_scale, w_out, norm_ffn_w, w_up, conv_ffn_w, conv_ffn_b, w_down, norm_final_w, loss_target, m_norm_mix_w, m_w_in, m_conv_qkv_w, m_a_log, m_dt_bias, m_gdn_norm_w, m_pool_w, m_pool_scale, m_w_out, m_norm_ffn_w, m_w_up, m_conv_ffn_w, m_conv_ffn_b, m_w_down, m_norm_final_w, v_norm_mix_w, v_w_in, v_conv_qkv_w, v_a_log, v_dt_bias, v_gdn_norm_w, v_pool_w, v_pool_scale, v_w_out, v_norm_ffn_w, v_w_up, v_conv_ffn_w, v_conv_ffn_b, v_w_down, v_norm_final_w):
    local = dict(norm_mix_w=norm_mix_w, w_in=w_in, conv_qkv_w=conv_qkv_w, a_log=a_log, dt_bias=dt_bias,
                 gdn_norm_w=gdn_norm_w, pool_w=pool_w, pool_scale=pool_scale, w_out=w_out, norm_ffn_w=norm_ffn_w,
                 w_up=w_up, conv_ffn_w=conv_ffn_w, conv_ffn_b=conv_ffn_b, w_down=w_down, norm_final_w=norm_final_w)
    mom_m = dict(norm_mix_w=m_norm_mix_w, w_in=m_w_in, conv_qkv_w=m_conv_qkv_w, a_log=m_a_log, dt_bias=m_dt_bias,
                 gdn_norm_w=m_gdn_norm_w, pool_w=m_pool_w, pool_scale=m_pool_scale, w_out=m_w_out,
                 norm_ffn_w=m_norm_ffn_w, w_up=m_w_up, conv_ffn_w=m_conv_ffn_w, conv_ffn_b=m_conv_ffn_b,
                 w_down=m_w_down, norm_final_w=m_norm_final_w)
    mom_v = dict(norm_mix_w=v_norm_mix_w, w_in=v_w_in, conv_qkv_w=v_conv_qkv_w, a_log=v_a_log, dt_bias=v_dt_bias,
                 gdn_norm_w=v_gdn_norm_w, pool_w=v_pool_w, pool_scale=v_pool_scale, w_out=v_w_out,
                 norm_ffn_w=v_norm_ffn_w, w_up=v_w_up, conv_ffn_w=v_conv_ffn_w, conv_ffn_b=v_conv_ffn_b,
                 w_down=v_w_down, norm_final_w=v_norm_final_w)
    n_layers, n_heads = a_log.shape
    d_model = x.shape[-1]
    dl = n_heads * HEAD_DIM
    n_ba = 2 * n_heads
    assert x.shape[0] == 1 and dl == d_model and pool_scale.shape[1] == d_model
    lay = _win_layout(w_in.shape[2], N_DEV * w_in.shape[2] - n_ba, 4 * dl, n_ba)
    _, me = _here()
    is_ba_dev = me == lay.ba_dev
    my_pad = _main_start(lay, me) - _slab_origin(lay, me)
    ba_cols = slice(lay.ba_off, lay.ba_off + n_ba)

    w_in_bf = w_in.astype(BF16)
    without_ba = jnp.concatenate([w_in_bf[..., :lay.ba_off], w_in_bf[..., lay.ba_off + n_ba:],
                                  jnp.zeros(w_in.shape[:2] + (n_ba,), BF16)], axis=-1)
    slab = lax.dynamic_update_slice(jnp.zeros(w_in.shape[:2] + (lay.slot_w,), BF16),
                                    jnp.where(is_ba_dev, without_ba, w_in_bf), (0, 0, my_pad))
    ba_part = jnp.pad(jnp.where(is_ba_dev, w_in_bf[..., ba_cols], jnp.zeros((), BF16)),
                      ((0, 0), (0, 0), (0, LANES - n_ba)))
    convs = _pack_rows([conv_qkv_w, conv_ffn_w], F32, 16)
    ba_slots, conv_slots = _gather([ba_part, convs], [0, 0], name="gather_small_weights")
    conv_parts = _unpack(conv_slots.reshape(N_DEV, -1), [conv_qkv_w.shape, conv_ffn_w.shape])
    conv_qkv_whole, conv_ffn_whole = (_whole_from_slots(p, 2) for p in conv_parts)
    alog_rows, dtb_rows = _lane_row(a_log, n_heads), _lane_row(dt_bias, n_heads)

    def with_own_slot(own, axis):
        zone = lax.empty(own.shape[:axis] + (N_DEV,) + own.shape[axis:], own.dtype)
        return lax.dynamic_update_slice(zone, jnp.expand_dims(own, axis), (0,) * axis + (me,) + (0,) * (own.ndim - axis))

    slot_axis = dict(slab=0, w_up=0, w_out=0, w_down=0, pool_w=1)
    gather_groups = (("slab",), ("w_up", "w_out", "w_down", "pool_w"))
    gather_src = lambda t, ref, to: ref
    in_flight = {}
    token = conv_slots
    for l in range(n_layers):
        own = dict(slab=slab[l], w_up=w_up[l].astype(BF16), w_out=w_out[l].astype(BF16),
                   w_down=w_down[l].astype(BF16), pool_w=pool_w[l].astype(BF16))
        for part, names in zip("ab", gather_groups):
            axes = [slot_axis[n] for n in names]
            dst = lambda t, ref, frm, axes=axes: _slot_view(ref, axes[t], frm)
            *handles, token = _start_exchange([own[n] for n in names], [with_own_slot(own[n], slot_axis[n]) for n in names],
                                              gather_src, dst, token, name="gather_start_%d%s" % (l, part))
            in_flight[l, part] = (handles, dst)

    def arrived(l, part, after):
        (send_sems, recv_sems, srcs, lands), dst = in_flight[l, part]
        return _wait_exchange(send_sems, recv_sems, srcs, lands, gather_src, dst, after, name="gather_wait_%d%s" % (l, part))

    xc = x[0]
    layer_w, saved = [], []
    after = token
    for l in range(n_layers):
        slabs, = arrived(l, "a", after)
        early = dict(n_heads=n_heads, norm_mix_w=norm_mix_w[l], alog_row=alog_rows[l], dtb_row=dtb_rows[l],
                     w_main=_assemble_w_main(slabs, lay, name="assemble_w_main"), w_ba=ba_slots[lay.ba_dev, l],
                     conv_qkv_w=conv_qkv_whole[l])

        def late_weights(o, l=l):
            up_slots, out_slots, down_slots, pool_slots = arrived(l, "b", o)
            return dict(norm_ffn_w=norm_ffn_w[l], gdn_norm_w=gdn_norm_w[l], pool_scale=pool_scale[l],
                        conv_ffn_b=conv_ffn_b[l], conv_ffn_w=conv_ffn_whole[l], w_up_slots=up_slots,
                        w_out=out_slots.reshape(-1, d_model), w_down=down_slots.reshape(-1, d_model),
                        pool_w=pool_slots.reshape(pool_slots.shape[0], -1, pool_slots.shape[-1]))

        xc, sv, wl = _layer_fwd(xc, early, late_weights)
        layer_w.append(wl)
        saved.append(sv)
        after = xc
    loss_row, dx, dx_bf, d_final = _loss_head(xc, norm_final_w, loss_target[0], name="loss_head")
    loss = lax.psum(loss_row[0, 0], MESH_AXES)

    shard = {n: local[n].shape[1:] for n in ("w_up", "w_out", "w_down", "pool_w")}
    shard["w_main"] = (d_model, lay.slot_w)
    recvs = {n: lax.empty((N_DEV, n_layers) + shard[n], BF16) for n in shard}
    up_w, out_rows, down_rows, pool_rows = shard["w_up"][1], shard["w_out"][0], shard["w_down"][0], shard["pool_w"][1]
    owned = dict(w_main=lambda to: (1, _slab_origin(lay, to), lay.slot_w), w_up=lambda to: (1, to * up_w, up_w),
                 w_out=lambda to: (0, to * out_rows, out_rows), w_down=lambda to: (0, to * down_rows, down_rows),
                 pool_w=lambda to: (1, to * pool_rows, pool_rows))
    scatter_groups = dict(e=("w_up", "w_down"), l=("w_main", "w_out", "pool_w"))
    pending = {}

    def send_grads(part, l, g):
        names = scatter_groups[part]

        def src(t, ref, to):
            axis, first, length = owned[names[t]](to)
            return ref.at[(slice(None),) * axis
                          + (pl.ds(pl.multiple_of(first, LANES if axis == ref.ndim - 1 else 16), length),)]

        dst = lambda t, ref, frm: ref.at[frm, l]
        received(part, g[names[0]])
        lands = []
        for n in names:
            axis, first, length = owned[n](me)
            mine = lax.dynamic_slice_in_dim(g[n], first, length, axis=axis)
            lands.append(lax.dynamic_update_slice(recvs[n], mine[None, None], (me, l) + (0,) * mine.ndim))
        send_sems, recv_sems, grads, lands, token = _start_exchange([g[n] for n in names], lands, src, dst, None,
                                                                    name="scatter_start_%d%s" % (l, part))
        pending[part] = (send_sems, recv_sems, grads, lands, src, dst, "scatter_wait_%d%s" % (l, part))
        return token

    def received(part, after):
        if part in pending:
            *args, name = pending.pop(part)
            recvs.update(zip(scatter_groups[part], _wait_exchange(*args, after=after, name=name)))

    layer_grads = [None] * n_layers
    token = None
    for l in reversed(range(n_layers)):
        dx2, dx2_bf, g = _layer_bwd_ffn(dx, dx_bf, layer_w[l], saved[l], token)
        token = send_grads("e", l, g)
        dproj, dpba, g_mix = _layer_bwd_mix(dx2, dx2_bf, layer_w[l], saved[l], token)
        g.update(g_mix)
        g["pool_w"] = g["pool_w"].astype(BF16)
        token = send_grads("l", l, g)
        dx, dx_bf, g["norm_mix_w"] = _layer_bwd_in(dproj, dpba, dx2, layer_w[l], saved[l], token)
        layer_grads[l] = g
    received("e", token)
    grad_x = dx
    stack = lambda name: jnp.stack([g[name] for g in layer_grads])

    grad, delta, new_m, new_v = {}, {}, {}, {}
    flat = lambda t, lead: t.reshape(t.shape[:lead] + (-1, t.shape[-1]))

    def update_shard(n, after):
        outs = _adamw_slots(flat(recvs[n], 2), flat(local[n], 1), flat(mom_m[n], 1), flat(mom_v[n], 1), after,
                            name="adamw_" + n)
        grad[n], delta[n], new_m[n], new_v[n] = (t.reshape(local[n].shape) for t in outs)

    update_shard("w_up", token)
    update_shard("w_down", token)

    small_names = REPLICATED + ("conv_qkv_w", "conv_ffn_w", "w_ba")
    g_small = dict(norm_mix_w=stack("norm_mix_w")[:, 0], a_log=stack("alog_row")[:, 0, n_heads:n_ba],
                   dt_bias=stack("dtb_row")[:, 0, n_heads:n_ba], gdn_norm_w=stack("gdn_norm_w")[:, 0],
                   pool_scale=stack("pool_scale")[:, 0], norm_ffn_w=stack("norm_ffn_w")[:, 0],
                   conv_ffn_b=stack("conv_ffn_b")[:, 0], norm_final_w=d_final[0], conv_qkv_w=stack("conv_qkv_w"),
                   conv_ffn_w=stack("conv_ffn_w"), w_ba=stack("w_ba")[..., :n_ba])
    small_shapes = [g_small[n].shape for n in small_names]
    small_slots, = _gather([_pack_rows([g_small[n] for n in small_names], F32, SMALL_QUANTUM_ROWS)], [0],
                           name="gather_small_grads", after=new_v["w_down"])
    small_sum = _sum_slots(small_slots[:, None], name="sum_small_grads")
    grad.update(zip(small_names, _unpack(small_sum.reshape(-1), small_shapes)))
    for n in ("conv_qkv_w", "conv_ffn_w"):
        width = local[n].shape[2]
        grad[n] = lax.dynamic_slice_in_dim(grad[n], me * width, width, axis=2)

    received("l", small_sum)
    update_shard("w_out", token)
    update_shard("pool_w", token)
    main_sum = _sum_slots(recvs["w_main"], name="sum_w_main_grads")
    g_main = lax.dynamic_slice_in_dim(main_sum, my_pad, lay.shard_w, axis=2)
    with_ba = jnp.concatenate([g_main[..., :lay.ba_off], grad.pop("w_ba"),
                               g_main[..., lay.ba_off:lay.shard_w - n_ba]], axis=-1)
    grad["w_in"] = jnp.where(is_ba_dev, with_ba, g_main)
    for n in ("w_in", "conv_qkv_w", "conv_ffn_w"):
        delta[n], new_m[n], new_v[n] = _adamw_nd(local[n], grad[n], mom_m[n], mom_v[n], name="adamw_" + n)
    packed = [_pack_rows([src[n] for n in REPLICATED], F32, 8) for src in (local, grad, mom_m, mom_v)]
    rep_out = _adamw(*packed, name="adamw_replicated")
    rep_shapes = [local[n].shape for n in REPLICATED]
    for dst, arr in zip((delta, new_m, new_v), rep_out):
        dst.update(zip(REPLICATED, _unpack(arr.reshape(-1), rep_shapes)))

    return (loss, grad_x[None], *[grad[n] for n in WEIGHTS], *[delta[n] for n in WEIGHTS],
            *[new_m[n] for n in WEIGHTS], *[new_v[n] for n in WEIGHTS])
```

```python
import collections

import jax
import jax.numpy as jnp
from jax import lax
from jax.experimental import pallas as pl
from jax.experimental.pallas import tpu as pltpu

F32 = jnp.float32
BF16 = jnp.bfloat16
MESH_AXES = ("x", "y", "c")
N_DEV = 8

NORM_EPS = 1e-6
HEAD_DIM = 128
GDN_CHUNK = 64
GDN_BATCH = 16
GDN_HEADS = 2
POOL_WINDOWS = (2, 4, 8, 16)
POOL_HALO = 16
CONV_HALO = 16
LANES = 128
V7X_VMEM_LIMIT_BYTES = 56 * 1024 * 1024
WHOLE_DIM = 1 << 30

ADAM_LR = 0.001
ADAM_B1 = 0.9
ADAM_B2 = 0.999
ADAM_EPS = 1e-08
ADAM_WD = 0.01
ADAM_STEP = 10


def _mx(v):
    return v.astype(BF16)


def _dot(a, b, ta=False, tb=False, precision=None):
    dims = (((0 if ta else 1,), (1 if tb else 0,)), ((), ()))
    return lax.dot_general(a, b, dims, precision=precision, preferred_element_type=F32)


def _tile(dim, target, quantum=LANES):
    if dim <= target:
        return dim
    t = (target // quantum) * quantum
    while t >= quantum:
        if dim % t == 0:
            return t
        t -= quantum
    return dim


def _params(*semantics):
    return pltpu.CompilerParams(dimension_semantics=semantics, vmem_limit_bytes=V7X_VMEM_LIMIT_BYTES)


def _sigmoid(v):
    return 0.5 * jnp.tanh(0.5 * v) + 0.5


def _softplus(v):
    return jnp.maximum(v, 0.0) + jnp.log(1.0 + jnp.exp(-jnp.abs(v)))


def _recip(v):
    r = pl.reciprocal(v, approx=True)
    return r * (2.0 - v * r)


_ERFC_P = 0.3275911
_ERFC_A = (0.254829592, -0.284496736, 1.421413741, -1.453152027, 1.061405429)


def _normal_cdf(v):
    e = jnp.exp(-0.5 * v * v)
    t = _recip(1.0 + (_ERFC_P * 2.0 ** -0.5) * jnp.abs(v))
    poly = jnp.full_like(v, _ERFC_A[-1])
    for coef in _ERFC_A[-2::-1]:
        poly = poly * t + coef
    half_tail = (0.5 * t) * poly * e
    return jnp.where(v >= 0.0, 1.0 - half_tail, half_tail), e


def _gelu(v):
    return v * _normal_cdf(v)[0]


def _gelu_and_grad(v):
    cdf, e = _normal_cdf(v)
    return v * cdf, cdf + v * e * ((2.0 * jnp.pi) ** -0.5)


def _rows_before(cat, shift, halo):
    return pltpu.roll(cat, shift, 0)[halo:]


def _rows_after(cat, shift, rows):
    return pltpu.roll(cat, cat.shape[0] - shift, 0)[:rows]


def _accumulate(ref, value, first):
    @pl.when(first)
    def _():
        ref[...] = value

    @pl.when(jnp.logical_not(first))
    def _():
        ref[...] += value


def _matmul(a, b, *, name, ta=False, tb=False, add=None, out_dtype=F32, tm=512, tn=1024, tk=2048,
            b_slots=False, after=None):
    m, k = (a.shape[1], a.shape[0]) if ta else a.shape
    b_rows, b_cols = (b.shape[1], N_DEV * b.shape[2]) if b_slots else b.shape
    n, kb = (b_rows, b_cols) if tb else (b_cols, b_rows)
    assert kb == k
    if b_slots:
        tn, tk = (tn, b.shape[2]) if tb else (b.shape[2], tk)
    tm, tn, tk = _tile(m, tm), _tile(n, tn), _tile(k, tk)
    nk = k // tk
    has_add = add is not None
    n_in = 2 + has_add + (after is not None)

    def body(*refs):
        a_ref, b_ref = refs[0], refs[1]
        add_ref = refs[2] if has_add else None
        o_ref, acc_ref = refs[n_in], refs[n_in + 1]
        kk = pl.program_id(2)
        part = _dot(_mx(a_ref[...]), _mx(b_ref[...]), ta, tb)

        def finish(total):
            if has_add:
                total = total + add_ref[...]
            o_ref[...] = total.astype(out_dtype)

        if nk == 1:
            finish(part)
        else:
            _accumulate(acc_ref, part, kk == 0)

            @pl.when(kk == nk - 1)
            def _():
                finish(acc_ref[...])

    a_spec = pl.BlockSpec((tk, tm), lambda j, i, kk: (kk, i)) if ta else pl.BlockSpec((tm, tk), lambda j, i, kk: (i, kk))
    b_block = (tn, tk) if tb else (tk, tn)
    if b_slots:
        b_spec = pl.BlockSpec((None,) + b_block, (lambda j, i, kk: (kk, j, 0)) if tb else (lambda j, i, kk: (j, kk, 0)))
    else:
        b_spec = pl.BlockSpec(b_block, (lambda j, i, kk: (j, kk)) if tb else (lambda j, i, kk: (kk, j)))
    o_spec = pl.BlockSpec((tm, tn), lambda j, i, kk: (i, j))
    in_specs = [a_spec, b_spec] + ([o_spec] if has_add else [])
    args = (a, b) + ((add,) if has_add else ())
    if after is not None:
        in_specs.append(pl.BlockSpec(memory_space=pl.ANY))
        args += (after,)
    acc_shape = (tm, tn) if nk > 1 else (8, LANES)
    return pl.pallas_call(
        body, name=name, grid=(n // tn, m // tm, nk), in_specs=in_specs, out_specs=o_spec,
        out_shape=jax.ShapeDtypeStruct((m, n), out_dtype), scratch_shapes=[pltpu.VMEM(acc_shape, F32)],
        compiler_params=_params("parallel", "parallel", "arbitrary"))(*args)


def _rmsnorm_fwd(x, w, *, name):
    s, d = x.shape
    ts = _tile(s, 512, 16)

    def body(x_ref, w_ref, o_ref):
        xf = x_ref[...]
        r = lax.rsqrt(jnp.mean(xf * xf, axis=-1, keepdims=True) + NORM_EPS)
        o_ref[...] = (xf * r * w_ref[...]).astype(BF16)

    return pl.pallas_call(
        body, name=name, grid=(s // ts,),
        in_specs=[pl.BlockSpec((ts, d), lambda i: (i, 0)), pl.BlockSpec((1, d), lambda i: (0, 0))],
        out_specs=pl.BlockSpec((ts, d), lambda i: (i, 0)),
        out_shape=jax.ShapeDtypeStruct((s, d), BF16), compiler_params=_params("parallel"))(x, w.reshape(1, d))


def _rmsnorm_bwd(dy, x, w, dres, *, name):
    s, d = x.shape
    ts = _tile(s, 256, 16)

    def body(dy_ref, x_ref, w_ref, dres_ref, dx_ref, dxb_ref, dw_ref):
        xf = x_ref[...]
        dyf = dy_ref[...]
        r = lax.rsqrt(jnp.mean(xf * xf, axis=-1, keepdims=True) + NORM_EPS)
        xh = xf * r
        dxh = dyf * w_ref[...]
        dx = dres_ref[...] + r * (dxh - xh * jnp.mean(dxh * xh, axis=-1, keepdims=True))
        dx_ref[...] = dx
        dxb_ref[...] = dx.astype(BF16)
        _accumulate(dw_ref, jnp.sum(dyf * xh, axis=0, keepdims=True), pl.program_id(0) == 0)

    row = pl.BlockSpec((ts, d), lambda i: (i, 0))
    vec = pl.BlockSpec((1, d), lambda i: (0, 0))
    return pl.pallas_call(
        body, name=name, grid=(s // ts,), in_specs=[row, row, vec, row], out_specs=[row, row, vec],
        out_shape=[jax.ShapeDtypeStruct((s, d), F32), jax.ShapeDtypeStruct((s, d), BF16),
                   jax.ShapeDtypeStruct((1, d), F32)],
        compiler_params=_params("arbitrary"))(dy, x, w.reshape(1, d), dres)


def _loss_head(x, w, target, *, name):
    s, d = x.shape
    ts = _tile(s, 256, 16)

    def body(x_ref, w_ref, t_ref, loss_ref, dx_ref, dxb_ref, dw_ref):
        first = pl.program_id(0) == 0
        xf = x_ref[...]
        wv = w_ref[...]
        r = lax.rsqrt(jnp.mean(xf * xf, axis=-1, keepdims=True) + NORM_EPS)
        xh = xf * r
        err = xh * wv - t_ref[...]
        part = 0.5 * jnp.sum(jnp.mean(err * err, axis=-1, keepdims=True), axis=0, keepdims=True)
        _accumulate(loss_ref, jnp.broadcast_to(part, (1, LANES)), first)
        dyf = err * (1.0 / d)
        dxh = dyf * wv
        dx = r * (dxh - xh * jnp.mean(dxh * xh, axis=-1, keepdims=True))
        dx_ref[...] = dx
        dxb_ref[...] = dx.astype(BF16)
        _accumulate(dw_ref, jnp.sum(dyf * xh, axis=0, keepdims=True), first)

    row = pl.BlockSpec((ts, d), lambda i: (i, 0))
    vec = pl.BlockSpec((1, d), lambda i: (0, 0))
    return pl.pallas_call(
        body, name=name, grid=(s // ts,), in_specs=[row, vec, row],
        out_specs=[pl.BlockSpec((1, LANES), lambda i: (0, 0)), row, row, vec],
        out_shape=[jax.ShapeDtypeStruct((1, LANES), F32), jax.ShapeDtypeStruct((s, d), F32),
                   jax.ShapeDtypeStruct((s, d), BF16), jax.ShapeDtypeStruct((1, d), F32)],
        compiler_params=_params("arbitrary"))(x, w.reshape(1, d), target)


def _qkv_fwd(proj, conv_w, n_heads, *, name):
    s = proj.shape[0]
    dl = n_heads * HEAD_DIM
    width = conv_w.shape[0]
    ts = _tile(s, 512, 8)

    def body(x_ref, halo_ref, w_ref, o_ref):
        i, sec = pl.program_id(0), pl.program_id(1)
        xv = x_ref[...].astype(F32)
        cat = jnp.concatenate([jnp.where(i > 0, halo_ref[...].astype(F32), 0.0), xv], axis=0)
        c = xv * w_ref[pl.ds(width - 1, 1), :]
        for sh in range(1, width):
            c = c + _rows_before(cat, sh, CONV_HALO) * w_ref[pl.ds(width - 1 - sh, 1), :]
        act = c * _sigmoid(c)

        @pl.when(sec == 2)
        def _():
            o_ref[...] = act

        @pl.when(sec < 2)
        def _():
            scale = jnp.where(sec == 0, HEAD_DIM ** -0.5, 1.0)
            for h in range(n_heads):
                cols = slice(h * HEAD_DIM, (h + 1) * HEAD_DIM)
                ah = act[:, cols]
                o_ref[:, cols] = ah * lax.rsqrt(jnp.sum(ah * ah, axis=-1, keepdims=True) + NORM_EPS) * scale

    return pl.pallas_call(
        body, name=name, grid=(s // ts, 3),
        in_specs=[pl.BlockSpec((ts, dl), lambda i, sec: (i, sec)),
                  pl.BlockSpec((CONV_HALO, dl), lambda i, sec: (jnp.maximum(i * (ts // CONV_HALO) - 1, 0), sec)),
                  pl.BlockSpec((width, dl), lambda i, sec: (0, sec))],
        out_specs=pl.BlockSpec((None, ts, dl), lambda i, sec: (sec, i, 0)),
        out_shape=jax.ShapeDtypeStruct((3, s, dl), F32),
        compiler_params=_params("parallel", "parallel"))(proj, proj, conv_w)


def _qkv_bwd(dqkv, proj, conv_w, n_heads, *, name):
    s = proj.shape[0]
    dl = n_heads * HEAD_DIM
    width = conv_w.shape[0]
    ts = _tile(s, 256, 16)
    n_tiles = s // ts
    per = ts // CONV_HALO
    rows = ts + CONV_HALO

    def body(d_ref, dnext_ref, x_ref, xprev_ref, xnext_ref, w_ref, dx_ref, dw_ref):
        sec, i = pl.program_id(0), pl.program_id(1)
        xv = x_ref[...].astype(F32)
        cat = jnp.concatenate([jnp.where(i > 0, xprev_ref[...].astype(F32), 0.0), xv, xnext_ref[...].astype(F32)],
                              axis=0)
        shifted = [cat[CONV_HALO:]] + [_rows_before(cat, sh, CONV_HALO) for sh in range(1, width)]
        c = shifted[0] * w_ref[pl.ds(width - 1, 1), :]
        for sh in range(1, width):
            c = c + shifted[sh] * w_ref[pl.ds(width - 1 - sh, 1), :]
        sig = _sigmoid(c)
        act = c * sig
        dout = jnp.concatenate([d_ref[...], dnext_ref[...]], axis=0)
        scale = jnp.where(sec == 0, HEAD_DIM ** -0.5, 1.0)
        is_v = sec == 2
        pieces = []
        for h in range(n_heads):
            cols = slice(h * HEAD_DIM, (h + 1) * HEAD_DIM)
            ah, dh = act[:, cols], dout[:, cols]
            nrm = lax.rsqrt(jnp.sum(ah * ah, axis=-1, keepdims=True) + NORM_EPS)
            dnormed = scale * nrm * (dh - ah * (nrm * nrm) * jnp.sum(dh * ah, axis=-1, keepdims=True))
            pieces.append(jnp.where(is_v, dh, dnormed))
        dact = jnp.concatenate(pieces, axis=1)
        dc = dact * sig * (1.0 + c * (1.0 - sig))
        live = jnp.logical_or(lax.broadcasted_iota(jnp.int32, (rows, 1), 0) < ts, i < n_tiles - 1)
        dc = jnp.where(live, dc, 0.0)
        dx = dc[:ts] * w_ref[pl.ds(width - 1, 1), :]
        for sh in range(1, width):
            dx = dx + _rows_after(dc, sh, ts) * w_ref[pl.ds(width - 1 - sh, 1), :]
        dx_ref[...] = dx.astype(BF16)
        dw_rows = [jnp.sum(dc[:ts] * shifted[width - 1 - j][:ts], axis=0, keepdims=True) for j in range(width)]
        _accumulate(dw_ref, jnp.concatenate(dw_rows, axis=0), i == 0)

    return pl.pallas_call(
        body, name=name, grid=(3, n_tiles),
        in_specs=[pl.BlockSpec((None, ts, dl), lambda sec, i: (sec, i, 0)),
                  pl.BlockSpec((None, CONV_HALO, dl), lambda sec, i: (sec, jnp.minimum((i + 1) * per, s // CONV_HALO - 1), 0)),
                  pl.BlockSpec((ts, dl), lambda sec, i: (i, sec)),
                  pl.BlockSpec((CONV_HALO, dl), lambda sec, i: (jnp.maximum(i * per - 1, 0), sec)),
                  pl.BlockSpec((CONV_HALO, dl), lambda sec, i: (jnp.minimum((i + 1) * per, s // CONV_HALO - 1), sec)),
                  pl.BlockSpec((width, dl), lambda sec, i: (0, sec))],
        out_specs=[pl.BlockSpec((ts, dl), lambda sec, i: (i, sec)), pl.BlockSpec((width, dl), lambda sec, i: (0, sec))],
        out_shape=[jax.ShapeDtypeStruct((s, 3 * dl), BF16), jax.ShapeDtypeStruct((width, 3 * dl), F32)],
        compiler_params=_params("parallel", "arbitrary"))(dqkv, dqkv, proj, proj, proj, conv_w)


def _ba_fwd(pba, alog_row, dtb_row, n_heads, *, name):
    s = pba.shape[0]
    ts = _tile(s, 1024, 8)

    def body(x_ref, alog_ref, dtb_ref, o_ref):
        xv = x_ref[...]
        lane = lax.broadcasted_iota(jnp.int32, xv.shape, 1)
        g = -jnp.exp(alog_ref[...]) * _softplus(xv + dtb_ref[...])
        o_ref[...] = jnp.where(lane < n_heads, _sigmoid(xv), jnp.where(lane < 2 * n_heads, g, 0.0))

    row = pl.BlockSpec((ts, LANES), lambda i: (i, 0))
    vec = pl.BlockSpec((1, LANES), lambda i: (0, 0))
    return pl.pallas_call(
        body, name=name, grid=(s // ts,), in_specs=[row, vec, vec], out_specs=row,
        out_shape=jax.ShapeDtypeStruct((s, LANES), F32), compiler_params=_params("parallel"))(pba, alog_row, dtb_row)


def _ba_bwd(dbg, pba, alog_row, dtb_row, n_heads, *, name):
    s = pba.shape[0]
    ts = _tile(s, 1024, 16)

    def body(d_ref, x_ref, alog_ref, dtb_ref, dx_ref, dalog_ref, ddtb_ref):
        first = pl.program_id(0) == 0
        xv, dv = x_ref[...], d_ref[...]
        lane = lax.broadcasted_iota(jnp.int32, xv.shape, 1)
        beta = _sigmoid(xv)
        neg_a = -jnp.exp(alog_ref[...])
        xa = xv + dtb_ref[...]
        is_a = jnp.logical_and(lane >= n_heads, lane < 2 * n_heads)
        d_xa = jnp.where(is_a, dv * neg_a * _sigmoid(xa), 0.0)
        d_g_times_g = jnp.where(is_a, dv * neg_a * _softplus(xa), 0.0)
        dx_ref[...] = jnp.where(lane < n_heads, dv * beta * (1.0 - beta), d_xa).astype(BF16)
        _accumulate(dalog_ref, jnp.sum(d_g_times_g, axis=0, keepdims=True), first)
        _accumulate(ddtb_ref, jnp.sum(d_xa, axis=0, keepdims=True), first)

    row = pl.BlockSpec((ts, LANES), lambda i: (i, 0))
    vec = pl.BlockSpec((1, LANES), lambda i: (0, 0))
    return pl.pallas_call(
        body, name=name, grid=(s // ts,), in_specs=[row, row, vec, vec], out_specs=[row, vec, vec],
        out_shape=[jax.ShapeDtypeStruct((s, LANES), BF16), jax.ShapeDtypeStruct((1, LANES), F32),
                   jax.ShapeDtypeStruct((1, LANES), F32)],
        compiler_params=_params("arbitrary"))(dbg, pba, alog_row, dtb_row)


def _bdot(a, b, ta=False, tb=False, precision=None):
    dims = (((1 if ta else 2,), (2 if tb else 1,)), ((0,), (0,)))
    return lax.dot_general(a, b, dims, precision=precision, preferred_element_type=F32)


def _split_bf16(v):
    hi = v.astype(BF16)
    return hi, (v - hi.astype(F32)).astype(BF16)


def _bdot_x3(a, b, ta=False, tb=False):
    return _bdot(a[0], b[0], ta, tb) + (_bdot(a[0], b[1], ta, tb) + _bdot(a[1], b[0], ta, tb))


def _chunk_masks():
    ri = lax.broadcasted_iota(jnp.int32, (GDN_CHUNK, GDN_CHUNK), 0)
    ci = lax.broadcasted_iota(jnp.int32, (GDN_CHUNK, GDN_CHUNK), 1)
    return ri == ci, ri >= ci, ri > ci, ri <= ci


def _row_to_col(row, eye):
    return jnp.sum(jnp.where(eye, row, 0.0), axis=2, keepdims=True)


def _col_to_row(col, eye):
    return jnp.sum(jnp.where(eye, col, 0.0), axis=1, keepdims=True)


_Gates = collections.namedtuple("_Gates", "beta_col decay e_col f_col dec")


def _gdn_gates(beta_row, g_row):
    eye, tril, _, triu = _chunk_masks()
    g_col = _row_to_col(g_row, eye)
    gc_col = jnp.sum(jnp.where(tril, g_row, 0.0), axis=2, keepdims=True)
    gc_row = jnp.sum(jnp.where(triu, g_col, 0.0), axis=1, keepdims=True)
    g_last = jnp.sum(g_row, axis=2, keepdims=True)
    decay = jnp.exp(jnp.where(tril, gc_col - gc_row, -jnp.inf))
    return _Gates(_row_to_col(beta_row, eye), decay, jnp.exp(gc_col), jnp.exp(g_last - gc_col), jnp.exp(g_last))


def _unit_lower_inverse(lmat):
    c = GDN_CHUNK
    t = jnp.where(_chunk_masks()[0], 1.0, 0.0) - lmat
    l_parts = _split_bf16(lmat)
    p = _bdot_x3(l_parts, l_parts)
    doublings = c.bit_length() - 2
    for r in range(doublings):
        p_parts = _split_bf16(p)
        if r < doublings - 1:
            both = _bdot_x3(_split_bf16(jnp.concatenate([t, p], axis=1)), p_parts)
            t, p = t + both[:, :c], both[:, c:]
        else:
            t = t + _bdot_x3(_split_bf16(t), p_parts)
    return t


def _gdn_solve(q, k, v, gates):
    strict = _chunk_masks()[2]
    kb = k * gates.beta_col
    lmat = jnp.where(strict, _bdot(_mx(kb), _mx(k), tb=True) * gates.decay, 0.0)
    tmat = _unit_lower_inverse(lmat)
    sol = _bdot_x3(_split_bf16(tmat), _split_bf16(jnp.concatenate([v * gates.beta_col, kb * gates.e_col], axis=2)))
    at = _bdot(_mx(q), _mx(k), tb=True) * gates.decay
    return lmat, tmat, sol, at


def _gdn_blocking(s):
    n_chunks = s // GDN_CHUNK
    per_step = 16 if n_chunks % 16 == 0 else n_chunks
    assert per_step % GDN_BATCH == 0
    return n_chunks, per_step, n_chunks // per_step


def _load_chunks(ref, sec, n0, hp):
    r0 = pl.multiple_of(n0 * GDN_CHUNK, GDN_BATCH * GDN_CHUNK)
    rows = pl.ds(r0, GDN_BATCH * GDN_CHUNK)
    cols = slice(hp * HEAD_DIM, (hp + 1) * HEAD_DIM)
    val = ref[rows, cols] if sec is None else ref[sec, rows, cols]
    return val.reshape(GDN_BATCH, GDN_CHUNK, HEAD_DIM)


def _store_chunks(ref, sec, n0, hp, val):
    r0 = pl.multiple_of(n0 * GDN_CHUNK, GDN_BATCH * GDN_CHUNK)
    rows = pl.ds(r0, GDN_BATCH * GDN_CHUNK)
    cols = slice(hp * HEAD_DIM, (hp + 1) * HEAD_DIM)
    flat = val.reshape(GDN_BATCH * GDN_CHUNK, HEAD_DIM)
    if sec is None:
        ref[rows, cols] = flat
    else:
        ref[sec, rows, cols] = flat


def _gdn_specs(n_heads, n_steps, per_step, order):
    rows, width = per_step * GDN_CHUNK, GDN_HEADS * HEAD_DIM
    rowvec = pl.BlockSpec((GDN_HEADS, per_step, 1, GDN_CHUNK), lambda h, j: (h, order(j), 0, 0))
    qkv = pl.BlockSpec((3, rows, width), lambda h, j: (0, order(j), h))
    act = pl.BlockSpec((rows, width), lambda h, j: (order(j), h))
    states = pl.BlockSpec((GDN_HEADS, per_step, HEAD_DIM, HEAD_DIM), lambda h, j: (h, order(j), 0, 0))
    return rowvec, qkv, act, states


def _gdn_fwd(qkv, beta_rows, g_rows, *, name):
    _, s, dl = qkv.shape
    n_heads = dl // HEAD_DIM
    c = GDN_CHUNK
    n_chunks, per_step, n_steps = _gdn_blocking(s)
    n_groups = per_step // GDN_BATCH
    heads = range(GDN_HEADS)

    def body(qkv_ref, b_ref, g_ref, o_ref, st_ref, state_ref, sol_s, at_s, qd_s, kmat_s, nmat_s, dec_s):
        @pl.when(pl.program_id(1) == 0)
        def _():
            state_ref[...] = jnp.zeros_like(state_ref)

        def solve(gi, carry):
            n0 = gi * GDN_BATCH
            grp = pl.ds(n0, GDN_BATCH)
            for hp in heads:
                q, k, v = (_load_chunks(qkv_ref, j, n0, hp) for j in range(3))
                gates = _gdn_gates(b_ref[hp, grp], g_ref[hp, grp])
                _, _, sol, at = _gdn_solve(q, k, v, gates)
                mke = _mx(k * gates.f_col)
                sol_s[hp, grp] = sol
                at_s[hp, grp] = at
                qd_s[hp, grp] = q * gates.e_col
                nmat_s[hp, grp] = _bdot(mke, _mx(sol[:, :, :HEAD_DIM]), ta=True)
                kmat_s[hp, grp] = _bdot(mke, _mx(sol[:, :, HEAD_DIM:]), ta=True)
                dec_s[hp, grp] = jnp.broadcast_to(gates.dec, (GDN_BATCH, 1, LANES))
            return carry

        lax.fori_loop(0, n_groups, solve, 0)

        def recur(n, states):
            out = []
            for hp in heads:
                state = states[hp]
                st_ref[hp, n] = state
                out.append(state * dec_s[hp, n] + nmat_s[hp, n] - _dot(_mx(kmat_s[hp, n]), _mx(state)))
            return tuple(out)

        final = lax.fori_loop(0, per_step, recur, tuple(state_ref[hp] for hp in heads))
        for hp in heads:
            state_ref[hp] = final[hp]

        def emit(gi, carry):
            n0 = gi * GDN_BATCH
            grp = pl.ds(n0, GDN_BATCH)
            for hp in heads:
                sol, mstate = sol_s[hp, grp], _mx(st_ref[hp, grp])
                v_new = sol[:, :, :HEAD_DIM] - _bdot(_mx(sol[:, :, HEAD_DIM:]), mstate)
                o = _bdot(_mx(qd_s[hp, grp]), mstate) + _bdot(_mx(at_s[hp, grp]), _mx(v_new))
                _store_chunks(o_ref, None, n0, hp, o)
            return carry

        lax.fori_loop(0, n_groups, emit, 0)

    rowvec, qkv_spec, act_spec, st_spec = _gdn_specs(n_heads, n_steps, per_step, lambda j: j)
    wide = lambda w: pltpu.VMEM((GDN_HEADS, per_step, c, w), F32)
    square = pltpu.VMEM((GDN_HEADS, per_step, HEAD_DIM, HEAD_DIM), F32)
    return pl.pallas_call(
        body, name=name, grid=(n_heads // GDN_HEADS, n_steps),
        in_specs=[qkv_spec, rowvec, rowvec], out_specs=[act_spec, st_spec],
        out_shape=[jax.ShapeDtypeStruct((s, dl), F32),
                   jax.ShapeDtypeStruct((n_heads, n_chunks, HEAD_DIM, HEAD_DIM), F32)],
        scratch_shapes=[pltpu.VMEM((GDN_HEADS, HEAD_DIM, HEAD_DIM), F32), wide(2 * HEAD_DIM), wide(c), wide(HEAD_DIM),
                        square, square, pltpu.VMEM((GDN_HEADS, per_step, 1, LANES), F32)],
        compiler_params=_params("parallel", "arbitrary"))(qkv, beta_rows, g_rows)


def _gdn_bwd(do, qkv, beta_rows, g_rows, states, *, name):
    _, s, dl = qkv.shape
    n_heads = dl // HEAD_DIM
    c = GDN_CHUNK
    n_chunks, per_step, n_steps = _gdn_blocking(s)
    n_groups = per_step // GDN_BATCH
    heads = range(GDN_HEADS)

    def body(do_ref, qkv_ref, b_ref, g_ref, st_ref, dqkv_ref, db_ref, dg_ref,
             dstate_ref, lmat_s, tmat_s, at_s, dat_s, sol_s, vn_s, dvn_s, dqd_s, kmat_s, nmat_s, dst_s, dec_s):
        @pl.when(pl.program_id(1) == 0)
        def _():
            dstate_ref[...] = jnp.zeros_like(dstate_ref)

        eye, tril, strict, _ = _chunk_masks()

        def solve(gi, carry):
            n0 = gi * GDN_BATCH
            grp = pl.ds(n0, GDN_BATCH)
            for hp in heads:
                q, k, v = (_load_chunks(qkv_ref, j, n0, hp) for j in range(3))
                gates = _gdn_gates(b_ref[hp, grp], g_ref[hp, grp])
                lmat, tmat, sol, at = _gdn_solve(q, k, v, gates)
                mstate = _mx(st_ref[hp, grp])
                md_o = _mx(_load_chunks(do_ref, None, n0, hp))
                mwc = _mx(sol[:, :, HEAD_DIM:])
                v_new = sol[:, :, :HEAD_DIM] - _bdot(mwc, mstate)
                dv_new0 = _bdot(_mx(at), md_o, ta=True)
                lmat_s[hp, grp] = lmat
                tmat_s[hp, grp] = tmat
                sol_s[hp, grp] = sol
                at_s[hp, grp] = at
                vn_s[hp, grp] = v_new
                dat_s[hp, grp] = jnp.where(tril, _bdot(md_o, _mx(v_new), tb=True), 0.0)
                dvn_s[hp, grp] = dv_new0
                dqd_s[hp, grp] = _bdot(md_o, mstate, tb=True)
                nmat_s[hp, grp] = (_bdot(_mx(q * gates.e_col), md_o, ta=True) - _bdot(mwc, _mx(dv_new0), ta=True))
                kmat_s[hp, grp] = _bdot(mwc, _mx(k * gates.f_col), ta=True)
                dec_s[hp, grp] = jnp.broadcast_to(gates.dec, (GDN_BATCH, 1, LANES))
            return carry

        lax.fori_loop(0, n_groups, solve, 0)

        def recur(idx, dstates):
            n = per_step - 1 - idx
            out = []
            for hp in heads:
                dstate = dstates[hp]
                dst_s[hp, n] = dstate
                out.append(dstate * dec_s[hp, n] + nmat_s[hp, n] - _dot(_mx(kmat_s[hp, n]), _mx(dstate)))
            return tuple(out)

        final = lax.fori_loop(0, per_step, recur, tuple(dstate_ref[hp] for hp in heads))
        for hp in heads:
            dstate_ref[hp] = final[hp]

        def emit_head(hp, n0):
            grp = pl.ds(n0, GDN_BATCH)
            q, k, v = (_load_chunks(qkv_ref, j, n0, hp) for j in range(3))
            gates = _gdn_gates(b_ref[hp, grp], g_ref[hp, grp])
            state, dstate = st_ref[hp, grp], dst_s[hp, grp]
            lmat, at, dat, sol = lmat_s[hp, grp], at_s[hp, grp], dat_s[hp, grp], sol_s[hp, grp]
            v_new, dqd = vn_s[hp, grp], dqd_s[hp, grp]
            dv_new = dvn_s[hp, grp] + _bdot(_mx(k * gates.f_col), _mx(dstate))
            dke = _bdot(_mx(v_new), _mx(dstate), tb=True)
            dwc = -_bdot(_mx(dv_new), _mx(state), tb=True)
            ddec = jnp.sum(jnp.sum(dstate * state, axis=2, keepdims=True), axis=1, keepdims=True)
            drhs = _bdot_x3(_split_bf16(tmat_s[hp, grp]), _split_bf16(jnp.concatenate([dv_new, dwc], axis=2)), ta=True)
            dvb, dkbe = drhs[:, :, :HEAD_DIM], drhs[:, :, HEAD_DIM:]
            dl_mat = jnp.where(strict, -_bdot(_mx(drhs), _mx(sol), tb=True), 0.0)
            dkk = dl_mat * gates.decay
            dqk = dat * gates.decay
            kb = k * gates.beta_col
            mk = _mx(k)
            dkb = _bdot(_mx(dkk), mk) + dkbe * gates.e_col
            dq = _bdot(_mx(dqk), mk) + dqd * gates.e_col
            dk = (_bdot(_mx(dqk), _mx(q), ta=True) + _bdot(_mx(dkk), _mx(kb), ta=True) + dke * gates.f_col
                  + dkb * gates.beta_col)
            _store_chunks(dqkv_ref, 0, n0, hp, dq)
            _store_chunks(dqkv_ref, 1, n0, hp, dk)
            _store_chunks(dqkv_ref, 2, n0, hp, dvb * gates.beta_col)
            dbeta_col = jnp.sum(dkb * k + dvb * v, axis=2, keepdims=True)
            through_decay = dl_mat * lmat + dat * at
            dke_ke = jnp.sum(dke * (k * gates.f_col), axis=2, keepdims=True)
            dgc_col = (jnp.sum(through_decay, axis=2, keepdims=True)
                       - _row_to_col(jnp.sum(through_decay, axis=1, keepdims=True), eye)
                       + jnp.sum(dqd * (q * gates.e_col) + dkbe * (kb * gates.e_col), axis=2, keepdims=True) - dke_ke)
            dg_last = jnp.sum(dke_ke, axis=1, keepdims=True) + ddec * gates.dec
            db_ref[hp, grp] = _col_to_row(dbeta_col, eye)
            dg_ref[hp, grp] = jnp.sum(jnp.where(tril, dgc_col, 0.0), axis=1, keepdims=True) + dg_last

        def emit(gi, carry):
            for hp in heads:
                emit_head(hp, gi * GDN_BATCH)
            return carry

        lax.fori_loop(0, n_groups, emit, 0)

    rowvec, qkv_spec, act_spec, st_spec = _gdn_specs(n_heads, n_steps, per_step, lambda j: n_steps - 1 - j)
    wide = lambda w: pltpu.VMEM((GDN_HEADS, per_step, c, w), F32)
    square = pltpu.VMEM((GDN_HEADS, per_step, HEAD_DIM, HEAD_DIM), F32)
    return pl.pallas_call(
        body, name=name, grid=(n_heads // GDN_HEADS, n_steps),
        in_specs=[act_spec, qkv_spec, rowvec, rowvec, st_spec], out_specs=[qkv_spec, rowvec, rowvec],
        out_shape=[jax.ShapeDtypeStruct((3, s, dl), F32),
                   jax.ShapeDtypeStruct((n_heads, n_chunks, 1, c), F32),
                   jax.ShapeDtypeStruct((n_heads, n_chunks, 1, c), F32)],
        scratch_shapes=[pltpu.VMEM((GDN_HEADS, HEAD_DIM, HEAD_DIM), F32), wide(c), wide(c), wide(c), wide(c),
                        wide(2 * HEAD_DIM), wide(HEAD_DIM), wide(HEAD_DIM), wide(HEAD_DIM),
                        square, square, square, pltpu.VMEM((GDN_HEADS, per_step, 1, LANES), F32)],
        compiler_params=_params("parallel", "arbitrary"))(do, qkv, beta_rows, g_rows, states)


def _pool_counts(tile, ts, extra, win):
    t = tile * ts + lax.broadcasted_iota(jnp.int32, (ts + extra, 1), 0)
    return jnp.minimum(t + 1, win).astype(F32)


def _pooled(cat, p_cols, tile, ts, win):
    acc, span = cat, 1
    while span < win:
        acc = acc + pltpu.roll(acc, span, 0)
        span *= 2
    return acc[POOL_HALO:] / _pool_counts(tile, ts, 0, win) - p_cols


def _merge_fwd(proj, o, gnw, pool_w, pool_scale, *, name):
    s, d = o.shape
    n_heads = d // HEAD_DIM
    n_groups, pg = pool_w.shape[0], pool_w.shape[1]
    assert n_groups == len(POOL_WINDOWS) and n_groups * pg == d and pg % HEAD_DIM == 0
    heads_per_group = pg // HEAD_DIM
    ts = _tile(s, 256, 16)

    def body(o_ref, z_ref, p_ref, halo_ref, ga_ref, gb_ref, gnw_ref, pw_ref, ps_ref, out_ref):
        i = pl.program_id(0)
        gnw_v = gnw_ref[...]
        halo = jnp.where(i > 0, halo_ref[...].astype(F32), 0.0)
        for gi, win in enumerate(POOL_WINDOWS):
            gcols = slice(gi * pg, (gi + 1) * pg)
            pv = p_ref[:, gcols].astype(F32)
            pooled = _pooled(jnp.concatenate([halo[:, gcols], pv], axis=0), pv, i, ts, win)
            yb = _dot(_mx(pooled), pw_ref[gi]) * ps_ref[:, gcols]
            for h in range(gi * heads_per_group, (gi + 1) * heads_per_group):
                cols = slice(h * HEAD_DIM, (h + 1) * HEAD_DIM)
                in_group = slice(h * HEAD_DIM - gi * pg, (h + 1) * HEAD_DIM - gi * pg)
                oh, zh = o_ref[:, cols], z_ref[:, cols].astype(F32)
                r = lax.rsqrt(jnp.mean(oh * oh, axis=-1, keepdims=True) + NORM_EPS)
                ya = oh * r * gnw_v * (zh * _sigmoid(zh))
                out_ref[:, cols] = (_sigmoid(ga_ref[:, cols].astype(F32)) * ya
                                    + _sigmoid(gb_ref[:, cols].astype(F32)) * yb[:, in_group]).astype(BF16)

    blk = lambda col: pl.BlockSpec((ts, d), lambda i, col=col: (i, col))
    vec = lambda width: pl.BlockSpec((1, width), lambda i: (0, 0))
    return pl.pallas_call(
        body, name=name, grid=(s // ts,),
        in_specs=[blk(0), blk(3), blk(4),
                  pl.BlockSpec((POOL_HALO, d), lambda i: (jnp.maximum(i * (ts // POOL_HALO) - 1, 0), 4)),
                  blk(5), blk(6), vec(HEAD_DIM), pl.BlockSpec((n_groups, pg, pg), lambda i: (0, 0, 0)), vec(d)],
        out_specs=blk(0), out_shape=jax.ShapeDtypeStruct((s, d), BF16),
        compiler_params=_params("parallel"))(o, proj, proj, proj, proj, proj, gnw.reshape(1, HEAD_DIM), pool_w,
                                              pool_scale.reshape(1, d))


def _merge_bwd(dmixed, proj, o, gnw, pool_w, pool_scale, *, name):
    s, d = o.shape
    n_heads = d // HEAD_DIM
    n_groups, pg = pool_w.shape[0], pool_w.shape[1]
    ts = _tile(s, 256, 16)

    def body(dm_ref, o_ref, z_ref, p_ref, halo_ref, ga_ref, gb_ref, gnw_ref, pw_ref, ps_ref,
             do_ref, dz_ref, dga_ref, dgb_ref, dpl_ref, dgnw_ref, dpw_ref, dps_ref):
        i = pl.program_id(0)
        first = i == 0
        gnw_v = gnw_ref[...]
        dgnw = jnp.zeros((1, HEAD_DIM), F32)
        for h in range(n_heads):
            cols = slice(h * HEAD_DIM, (h + 1) * HEAD_DIM)
            oh, zh, dm = o_ref[:, cols], z_ref[:, cols].astype(F32), dm_ref[:, cols]
            r = lax.rsqrt(jnp.mean(oh * oh, axis=-1, keepdims=True) + NORM_EPS)
            xh = oh * r
            sz = _sigmoid(zh)
            silu_z = zh * sz
            sa = _sigmoid(ga_ref[:, cols].astype(F32))
            on = xh * gnw_v
            dya = dm * sa
            dga_ref[:, cols] = (dm * on * silu_z * sa * (1.0 - sa)).astype(BF16)
            dz_ref[:, cols] = (dya * on * sz * (1.0 + zh * (1.0 - sz))).astype(BF16)
            don = dya * silu_z
            dgnw = dgnw + jnp.sum(don * xh, axis=0, keepdims=True)
            dxh = don * gnw_v
            do_ref[:, cols] = r * (dxh - xh * jnp.mean(dxh * xh, axis=-1, keepdims=True))
        _accumulate(dgnw_ref, dgnw, first)
        halo = jnp.where(first, 0.0, halo_ref[...].astype(F32))
        for gi, win in enumerate(POOL_WINDOWS):
            cols = slice(gi * pg, (gi + 1) * pg)
            pv, dm = p_ref[:, cols].astype(F32), dm_ref[:, cols]
            pooled = _pooled(jnp.concatenate([halo[:, cols], pv], axis=0), pv, i, ts, win)
            lin = _dot(_mx(pooled), pw_ref[gi])
            psv = ps_ref[:, cols]
            sb = _sigmoid(gb_ref[:, cols].astype(F32))
            dgb_ref[:, cols] = (dm * lin * psv * sb * (1.0 - sb)).astype(BF16)
            dyb = dm * sb
            _accumulate(dps_ref.at[:, cols], jnp.sum(dyb * lin, axis=0, keepdims=True), first)
            dlin = _mx(dyb * psv)
            _accumulate(dpw_ref.at[gi], _dot(_mx(pooled), dlin, ta=True), first)
            dpl_ref[:, cols] = _dot(dlin, pw_ref[gi], tb=True)

    blk = lambda col: pl.BlockSpec((ts, d), lambda i, col=col: (i, col))
    vec = lambda width: pl.BlockSpec((1, width), lambda i: (0, 0))
    pw_spec = pl.BlockSpec((n_groups, pg, pg), lambda i: (0, 0, 0))
    return pl.pallas_call(
        body, name=name, grid=(s // ts,),
        in_specs=[blk(0), blk(0), blk(3), blk(4),
                  pl.BlockSpec((POOL_HALO, d), lambda i: (jnp.maximum(i * (ts // POOL_HALO) - 1, 0), 4)),
                  blk(5), blk(6), vec(HEAD_DIM), pw_spec, vec(d)],
        out_specs=[blk(0), blk(0), blk(0), blk(0), blk(0), vec(HEAD_DIM), pw_spec, vec(d)],
        out_shape=[jax.ShapeDtypeStruct((s, d), F32), jax.ShapeDtypeStruct((s, d), BF16),
                   jax.ShapeDtypeStruct((s, d), BF16), jax.ShapeDtypeStruct((s, d), BF16),
                   jax.ShapeDtypeStruct((s, d), F32), jax.ShapeDtypeStruct((1, HEAD_DIM), F32),
                   jax.ShapeDtypeStruct((n_groups, pg, pg), F32), jax.ShapeDtypeStruct((1, d), F32)],
        compiler_params=_params("arbitrary"))(dmixed, o, proj, proj, proj, proj, proj, gnw.reshape(1, HEAD_DIM),
                                               pool_w, pool_scale.reshape(1, d))


def _pool_bwd(dpooled, *, name):
    s, d = dpooled.shape
    pg = d // len(POOL_WINDOWS)
    ts = _tile(s, 512, 16)
    n_tiles = s // ts
    per = ts // POOL_HALO

    def body(d_ref, next_ref, out_ref):
        i = pl.program_id(0)
        nxt = jnp.where(i < n_tiles - 1, next_ref[...], 0.0)
        for gi, win in enumerate(POOL_WINDOWS):
            cols = slice(gi * pg, (gi + 1) * pg)
            dv = d_ref[:, cols]
            acc = jnp.concatenate([dv, nxt[:, cols]], axis=0) / _pool_counts(i, ts, POOL_HALO, win)
            span = 1
            while span < win:
                acc = acc + pltpu.roll(acc, acc.shape[0] - span, 0)
                span *= 2
            out_ref[:, cols] = (acc[:ts] - dv).astype(BF16)

    return pl.pallas_call(
        body, name=name, grid=(n_tiles,),
        in_specs=[pl.BlockSpec((ts, d), lambda i: (i, 0)),
                  pl.BlockSpec((POOL_HALO, d), lambda i: (jnp.minimum((i + 1) * per, s // POOL_HALO - 1), 0))],
        out_specs=pl.BlockSpec((ts, d), lambda i: (i, 0)), out_shape=jax.ShapeDtypeStruct((s, d), BF16),
        compiler_params=_params("parallel"))(dpooled, dpooled)


def _ffn_tiles(s, f):
    tf = _tile(f, 1408)
    return _tile(s, 512, 16), tf, f // tf


def _ffn_act_fwd(gu, conv_w, conv_b, *, name):
    s, f = gu.shape[0], gu.shape[1] // 2
    width = conv_w.shape[0]
    ts, tf, nf = _ffn_tiles(s, f)

    def body(g_ref, halo_ref, u_ref, w_ref, b_ref, act_ref, gc_ref):
        i = pl.program_id(0)
        gv = g_ref[...].astype(F32)
        cat = jnp.concatenate([jnp.where(i > 0, halo_ref[...].astype(F32), 0.0), gv], axis=0)
        gc = gv * w_ref[pl.ds(width - 1, 1), :] + b_ref[...]
        for sh in range(1, width):
            gc = gc + _rows_before(cat, sh, CONV_HALO) * w_ref[pl.ds(width - 1 - sh, 1), :]
        gc_ref[...] = gc.astype(BF16)
        act_ref[...] = (_gelu(gc) * u_ref[...].astype(F32)).astype(BF16)

    blk = pl.BlockSpec((ts, tf), lambda i, j: (i, j))
    return pl.pallas_call(
        body, name=name, grid=(s // ts, nf),
        in_specs=[blk, pl.BlockSpec((CONV_HALO, tf), lambda i, j: (jnp.maximum(i * (ts // CONV_HALO) - 1, 0), j)),
                  pl.BlockSpec((ts, tf), lambda i, j: (i, nf + j)),
                  pl.BlockSpec((width, tf), lambda i, j: (0, j)), pl.BlockSpec((1, tf), lambda i, j: (0, j))],
        out_specs=[blk, blk],
        out_shape=[jax.ShapeDtypeStruct((s, f), BF16), jax.ShapeDtypeStruct((s, f), BF16)],
        compiler_params=_params("parallel", "parallel"))(gu, gu, gu, conv_w, conv_b.reshape(1, f))


def _ffn_act_bwd(dact, gu, gc, conv_w, *, name):
    s, f = gc.shape
    width = conv_w.shape[0]
    ts, tf, nf = _ffn_tiles(s, f)
    n_tiles = s // ts
    per = ts // CONV_HALO
    rows = ts + CONV_HALO

    def body(da_ref, da_next, gc_ref, gc_next, u_ref, u_next, g_ref, g_prev, w_ref, dg_ref, du_ref, dw_ref, db_ref):
        i = pl.program_id(1)
        first = i == 0
        da = jnp.concatenate([da_ref[...], da_next[...]], axis=0).astype(F32)
        gcv = jnp.concatenate([gc_ref[...], gc_next[...]], axis=0).astype(F32)
        uv = jnp.concatenate([u_ref[...], u_next[...]], axis=0).astype(F32)
        gelu, gelu_grad = _gelu_and_grad(gcv)
        du_ref[...] = (da[:ts] * gelu[:ts]).astype(BF16)
        live = jnp.logical_or(lax.broadcasted_iota(jnp.int32, (rows, 1), 0) < ts, i < n_tiles - 1)
        dgc = jnp.where(live, da * uv * gelu_grad, 0.0)
        dgate = dgc[:ts] * w_ref[pl.ds(width - 1, 1), :]
        for sh in range(1, width):
            dgate = dgate + _rows_after(dgc, sh, ts) * w_ref[pl.ds(width - 1 - sh, 1), :]
        dg_ref[...] = dgate.astype(BF16)
        gv = g_ref[...].astype(F32)
        cat = jnp.concatenate([jnp.where(first, 0.0, g_prev[...].astype(F32)), gv], axis=0)
        shifted = [gv] + [_rows_before(cat, sh, CONV_HALO) for sh in range(1, width)]
        dw_rows = [jnp.sum(dgc[:ts] * shifted[width - 1 - j], axis=0, keepdims=True) for j in range(width)]
        _accumulate(dw_ref, jnp.concatenate(dw_rows, axis=0), first)
        _accumulate(db_ref, jnp.sum(dgc[:ts], axis=0, keepdims=True), first)

    nxt_row = lambda i: jnp.minimum((i + 1) * per, s // CONV_HALO - 1)
    main = lambda off: pl.BlockSpec((ts, tf), lambda j, i, off=off: (i, off + j))
    nxt = lambda off: pl.BlockSpec((CONV_HALO, tf), lambda j, i, off=off: (nxt_row(i), off + j))
    return pl.pallas_call(
        body, name=name, grid=(nf, n_tiles),
        in_specs=[main(0), nxt(0), main(0), nxt(0), main(nf), nxt(nf), main(0),
                  pl.BlockSpec((CONV_HALO, tf), lambda j, i: (jnp.maximum(i * per - 1, 0), j)),
                  pl.BlockSpec((width, tf), lambda j, i: (0, j))],
        out_specs=[main(0), main(0), pl.BlockSpec((width, tf), lambda j, i: (0, j)),
                   pl.BlockSpec((1, tf), lambda j, i: (0, j))],
        out_shape=[jax.ShapeDtypeStruct((s, f), BF16), jax.ShapeDtypeStruct((s, f), BF16),
                   jax.ShapeDtypeStruct((width, f), F32), jax.ShapeDtypeStruct((1, f), F32)],
        compiler_params=_params("parallel", "arbitrary"))(dact, dact, gc, gc, gu, gu, gu, gu, conv_w)


def _rows_layout(bg, n_heads):
    s = bg.shape[0]
    shape = (n_heads, s // GDN_CHUNK, 1, GDN_CHUNK)
    return bg[:, :n_heads].T.reshape(shape), bg[:, n_heads:2 * n_heads].T.reshape(shape)


def _lane_layout(dbeta_rows, dg_rows):
    n_heads = dbeta_rows.shape[0]
    s = dbeta_rows.shape[1] * GDN_CHUNK
    both = jnp.concatenate([dbeta_rows.reshape(n_heads, s), dg_rows.reshape(n_heads, s)], axis=0).T
    return jnp.pad(both, ((0, 0), (0, LANES - 2 * n_heads)))


def _layer_fwd(x, w, late_weights):
    n_heads = w["n_heads"]
    h = _rmsnorm_fwd(x, w["norm_mix_w"], name="norm_mix_fwd")
    proj = _matmul(h, w["w_main"], out_dtype=BF16, name="in_proj_fwd")
    pba = _matmul(h, w["w_ba"], name="ba_proj_fwd")
    qkv = _qkv_fwd(proj, w["conv_qkv_w"], n_heads, name="qkv_fwd")
    bg = _ba_fwd(pba, w["alog_row"], w["dtb_row"], n_heads, name="ba_fwd")
    beta_rows, g_rows = _rows_layout(bg, n_heads)
    o, states = _gdn_fwd(qkv, beta_rows, g_rows, name="gdn_fwd")
    w = dict(w, **late_weights(o))
    mixed = _merge_fwd(proj, o, w["gdn_norm_w"], w["pool_w"], w["pool_scale"], name="merge_fwd")
    x2 = _matmul(mixed, w["w_out"], add=x, name="out_proj_fwd")
    h2 = _rmsnorm_fwd(x2, w["norm_ffn_w"], name="norm_ffn_fwd")
    gu = _matmul(h2, w["w_up_slots"], b_slots=True, out_dtype=BF16, name="up_proj_fwd")
    act, gc = _ffn_act_fwd(gu, w["conv_ffn_w"], w["conv_ffn_b"], name="ffn_act_fwd")
    x3 = _matmul(act, w["w_down"], add=x2, tk=WHOLE_DIM, name="down_proj_fwd")
    saved = dict(x=x, h=h, proj=proj, pba=pba, qkv=qkv, beta_rows=beta_rows, g_rows=g_rows, o=o, states=states,
                 mixed=mixed, x2=x2, h2=h2, gu=gu, gc=gc, act=act)
    return x3, saved, w


def _layer_bwd_ffn(dx3, dx3_bf, w, sv, after):
    g = {}
    dact = _matmul(dx3_bf, w["w_down"], tb=True, tn=1408, out_dtype=BF16, after=after, name="down_proj_dx")
    g["w_down"] = _matmul(sv["act"], dx3_bf, ta=True, tk=WHOLE_DIM, out_dtype=BF16, name="down_proj_dw")
    dgate, dup, g["conv_ffn_w"], g["conv_ffn_b"] = _ffn_act_bwd(dact, sv["gu"], sv["gc"], w["conv_ffn_w"],
                                                                 name="ffn_act_bwd")
    dgu = jnp.concatenate([dgate, dup], axis=1)
    dh2 = _matmul(dgu, w["w_up_slots"], b_slots=True, tb=True, tm=1024, name="up_proj_dx")
    g["w_up"] = _matmul(sv["h2"], dgu, ta=True, tk=WHOLE_DIM, out_dtype=BF16, name="up_proj_dw")
    dx2, dx2_bf, g["norm_ffn_w"] = _rmsnorm_bwd(dh2, sv["x2"], w["norm_ffn_w"], dx3, name="norm_ffn_bwd")
    return dx2, dx2_bf, g


def _layer_bwd_mix(dx2, dx2_bf, w, sv, after):
    n_heads = w["n_heads"]
    g = {}
    dmixed = _matmul(dx2_bf, w["w_out"], tb=True, after=after, name="out_proj_dx")
    g["w_out"] = _matmul(sv["mixed"], dx2_bf, ta=True, tk=WHOLE_DIM, out_dtype=BF16, name="out_proj_dw")
    do, dz, dga, dgb, dpooled, g["gdn_norm_w"], g["pool_w"], g["pool_scale"] = _merge_bwd(
        dmixed, sv["proj"], sv["o"], w["gdn_norm_w"], w["pool_w"], w["pool_scale"], name="merge_bwd")
    dp = _pool_bwd(dpooled, name="pool_bwd")
    dqkv, dbeta_rows, dg_rows = _gdn_bwd(do, sv["qkv"], sv["beta_rows"], sv["g_rows"], sv["states"], name="gdn_bwd")
    dproj_qkv, g["conv_qkv_w"] = _qkv_bwd(dqkv, sv["proj"], w["conv_qkv_w"], n_heads, name="qkv_bwd")
    dpba, g["alog_row"], g["dtb_row"] = _ba_bwd(_lane_layout(dbeta_rows, dg_rows), sv["pba"], w["alog_row"],
                                                w["dtb_row"], n_heads, name="ba_bwd")
    dproj = jnp.concatenate([dproj_qkv, dz, dp, dga, dgb], axis=1)
    g["w_main"] = _matmul(sv["h"], dproj, ta=True, tk=WHOLE_DIM, out_dtype=BF16, name="in_proj_dw")
    g["w_ba"] = _matmul(sv["h"], dpba, ta=True, name="ba_proj_dw")
    return dproj, dpba, g


def _layer_bwd_in(dproj, dpba, dx2, w, sv, after):
    dh = _matmul(dproj, w["w_main"], tb=True, tm=1024, after=after, name="in_proj_dx")
    dh = _matmul(dpba, w["w_ba"], tb=True, add=dh, name="ba_proj_dx")
    return _rmsnorm_bwd(dh, sv["x"], w["norm_mix_w"], dx2, name="norm_mix_bwd")


def _here():
    mx, my, mc = (lax.axis_index(a) for a in MESH_AXES)
    return (mx, my, mc), 4 * mx + 2 * my + mc


def _peer(pos, r):
    mx, my, mc = pos
    px = 1 - mx if r & 4 else mx
    py = 1 - my if r & 2 else my
    pc = 1 - mc if r & 1 else mc
    return (px, py, pc), 4 * px + 2 * py + pc


def _run_exchange(n_tensors, src_view, dst_view, sems):
    send_sems, recv_sems, local_sems = sems
    pos, me = _here()
    started = []
    for t in range(n_tensors):
        cp = pltpu.make_async_copy(src_view(t, me), dst_view(t, me), local_sems.at[t])
        cp.start()
        started.append(cp)

    def remote(t, r, landing):
        target, target_lin = _peer(pos, r)
        return pltpu.make_async_remote_copy(
            src_ref=src_view(t, target_lin), dst_ref=dst_view(t, target_lin if landing else me),
            send_sem=send_sems.at[t, r - 1], recv_sem=recv_sems.at[t, r - 1],
            device_id=target, device_id_type=pl.DeviceIdType.MESH)

    sends = []
    for r in range(1, N_DEV):
        for t in range(n_tensors):
            cp = remote(t, r, landing=False)
            cp.start()
            sends.append(cp)
    for r in range(1, N_DEV):
        for t in range(n_tensors):
            remote(t, r, landing=True).wait_recv()
    for cp in sends:
        cp.wait_send()
    for cp in started:
        cp.wait()


def _exchange_scratch(n_tensors):
    return [pltpu.SemaphoreType.DMA((n_tensors, N_DEV - 1)), pltpu.SemaphoreType.DMA((n_tensors, N_DEV - 1)),
            pltpu.SemaphoreType.DMA((n_tensors,))]


def _slot_view(ref, axis, index):
    return ref.at[(slice(None),) * axis + (index,)]


def _gather(srcs, slot_axes, *, name, after=None):
    n = len(srcs)
    n_in = n + (after is not None)

    def body(*refs):
        src_refs, out_refs = refs[:n], refs[n_in:n_in + n]
        _run_exchange(n, lambda t, to: src_refs[t], lambda t, frm: _slot_view(out_refs[t], slot_axes[t], frm),
                      refs[n_in + n:])

    hbm = pl.BlockSpec(memory_space=pltpu.HBM)
    out_shape = [jax.ShapeDtypeStruct(s.shape[:a] + (N_DEV,) + s.shape[a:], s.dtype) for s, a in zip(srcs, slot_axes)]
    in_specs = [hbm] * n + ([pl.BlockSpec(memory_space=pl.ANY)] if after is not None else [])
    return pl.pallas_call(body, name=name, in_specs=in_specs, out_specs=[hbm] * n, out_shape=out_shape,
                          scratch_shapes=_exchange_scratch(n))(*srcs, *([after] if after is not None else []))


_SIDE_EFFECT = pltpu.SideEffectType.DATAFLOW_SIDE_EFFECTING


def _split_copy(t, r, pos, src_refs, land_refs, send_sems, recv_sems, src_view, dst_view, landing):
    _, me = _here()
    target, target_lin = _peer(pos, r)
    return pltpu.make_async_remote_copy(
        src_ref=src_view(t, src_refs[t], target_lin), dst_ref=dst_view(t, land_refs[t], target_lin if landing else me),
        send_sem=send_sems.at[t * (N_DEV - 1) + r - 1], recv_sem=recv_sems.at[t * (N_DEV - 1) + r - 1],
        device_id=target, device_id_type=pl.DeviceIdType.MESH)


def _start_exchange(srcs, lands, src_view, dst_view, after, *, name):
    n = len(srcs)
    has_after = after is not None

    def body(*refs):
        src_refs, land_refs = refs[:n], refs[n:2 * n]
        outs = refs[2 * n + has_after:]
        send_sems, recv_sems, token = outs[0], outs[1], outs[2 + 2 * n]
        pos, _ = _here()
        for r in range(1, N_DEV):
            for t in range(n):
                _split_copy(t, r, pos, src_refs, land_refs, send_sems, recv_sems, src_view, dst_view, False).start()
        token[...] = jnp.zeros_like(token)

    hbm = pl.BlockSpec(memory_space=pltpu.HBM)
    sem = pl.BlockSpec(memory_space=pltpu.SEMAPHORE)
    sem_shape = pltpu.SemaphoreType.DMA((n * (N_DEV - 1),))
    through = [pltpu.HBM(t.shape, t.dtype) for t in list(srcs) + list(lands)]
    args = [pltpu.with_memory_space_constraint(t, pltpu.HBM) for t in list(srcs) + list(lands)]
    outs = pl.pallas_call(
        body, name=name, in_specs=[hbm] * (2 * n) + ([pl.BlockSpec(memory_space=pl.ANY)] if has_after else []),
        out_specs=(sem, sem, *[hbm] * (2 * n), pl.BlockSpec(memory_space=pltpu.VMEM)),
        out_shape=(sem_shape, sem_shape, *through, jax.ShapeDtypeStruct((8, LANES), F32)),
        input_output_aliases={i: 2 + i for i in range(2 * n)},
        compiler_params=pltpu.CompilerParams(has_side_effects=_SIDE_EFFECT))(*args, *([after] if has_after else []))
    return outs[0], outs[1], list(outs[2:2 + n]), list(outs[2 + n:2 + 2 * n]), outs[-1]


def _wait_exchange(send_sems, recv_sems, srcs, lands, src_view, dst_view, after, *, name):
    n = len(srcs)

    def body(*refs):
        src_refs, land_refs = refs[:n], refs[n:2 * n]
        send_refs, recv_refs = refs[2 * n], refs[2 * n + 1]
        pos, _ = _here()
        for r in range(1, N_DEV):
            for t in range(n):
                cp = _split_copy(t, r, pos, src_refs, land_refs, send_refs, recv_refs, src_view, dst_view, True)
                cp.wait_send()
                cp.wait_recv()

    hbm = pl.BlockSpec(memory_space=pltpu.HBM)
    sem = pl.BlockSpec(memory_space=pltpu.SEMAPHORE)
    outs = pl.pallas_call(
        body, name=name, in_specs=[hbm] * (2 * n) + [sem, sem, pl.BlockSpec(memory_space=pl.ANY)],
        out_specs=[hbm] * (2 * n), out_shape=[pltpu.HBM(t.shape, t.dtype) for t in list(srcs) + list(lands)],
        input_output_aliases={i: i for i in range(2 * n)},
        compiler_params=pltpu.CompilerParams(has_side_effects=_SIDE_EFFECT))(*srcs, *lands, send_sems, recv_sems, after)
    return list(outs[n:])


def _sum_slots(parts, *, name):
    _, n_lead, r_rows, cols = parts.shape
    tr = _tile(r_rows, max(16, (1 << 17) // cols // 16 * 16), 16)

    def body(p_ref, o_ref):
        total = p_ref[0].astype(F32)
        for p in range(1, N_DEV):
            total = total + p_ref[p].astype(F32)
        o_ref[...] = total

    return pl.pallas_call(
        body, name=name, grid=(n_lead, r_rows // tr),
        in_specs=[pl.BlockSpec((N_DEV, None, tr, cols), lambda a, i: (0, a, i, 0))],
        out_specs=pl.BlockSpec((None, tr, cols), lambda a, i: (a, i, 0)),
        out_shape=jax.ShapeDtypeStruct((n_lead, r_rows, cols), F32),
        compiler_params=_params("parallel", "parallel"))(parts)


_WinLayout = collections.namedtuple("_WinLayout", "shard_w n_main ba_dev ba_off n_ba slot_w")


def _win_layout(shard_w, n_main, ba_start, n_ba):
    ba_dev = ba_start // shard_w
    assert (ba_start + n_ba - 1) // shard_w == ba_dev and n_main % LANES == 0
    slot_w = -(-(LANES - 1 + shard_w) // LANES) * LANES
    return _WinLayout(shard_w, n_main, ba_dev, ba_start - ba_dev * shard_w, n_ba, slot_w)


def _main_start(lay, dev):
    return lay.shard_w * dev - jnp.where(dev > lay.ba_dev, lay.n_ba, 0)


def _slab_origin(lay, dev):
    return jnp.minimum(_main_start(lay, dev) // LANES * LANES, lay.n_main - lay.slot_w)


def _assemble_plan(lay):
    plan = [[] for _ in range(lay.n_main // LANES)]
    for dev in range(N_DEV):
        start = lay.shard_w * dev - (lay.n_ba if dev > lay.ba_dev else 0)
        width = lay.shard_w - (lay.n_ba if dev == lay.ba_dev else 0)
        origin = min(start // LANES, (lay.n_main - lay.slot_w) // LANES)
        pad = start - origin * LANES
        for t in range(pad // LANES, (pad + width - 1) // LANES + 1):
            plan[origin + t].append((dev, t))
    return plan


def _assemble_w_main(slabs, lay, *, name):
    _, d, slot_w = slabs.shape
    plan = _assemble_plan(lay)
    runs = []
    shared = []
    for tile, parts in enumerate(plan):
        if len(parts) != 1:
            shared.append((tile, parts))
        elif runs and runs[-1][2] == parts[0][0] and runs[-1][0] + runs[-1][1] == tile:
            runs[-1][1] += 1
        else:
            runs.append([tile, 1, parts[0][0], parts[0][1]])
    tr = _tile(d, 256, 16)

    def body(in_ref, out_ref):
        for first, count, dev, t0 in runs:
            out_ref[:, first * LANES:(first + count) * LANES] = in_ref[dev, :, t0 * LANES:(t0 + count) * LANES]
        for tile, parts in shared:
            total = in_ref[parts[0][0], :, parts[0][1] * LANES:(parts[0][1] + 1) * LANES]
            for dev, t in parts[1:]:
                total = total + in_ref[dev, :, t * LANES:(t + 1) * LANES]
            out_ref[:, tile * LANES:(tile + 1) * LANES] = total

    return pl.pallas_call(
        body, name=name, grid=(d // tr,),
        in_specs=[pl.BlockSpec((N_DEV, tr, slot_w), lambda i: (0, i, 0))],
        out_specs=pl.BlockSpec((tr, lay.n_main), lambda i: (i, 0)),
        out_shape=jax.ShapeDtypeStruct((d, lay.n_main), slabs.dtype), compiler_params=_params("parallel"))(slabs)


def _adam_update(w, g, m, v):
    nm = ADAM_B1 * m + (1.0 - ADAM_B1) * g
    nv = ADAM_B2 * v + (1.0 - ADAM_B2) * (g * g)
    m_hat = nm / (1.0 - ADAM_B1 ** ADAM_STEP)
    v_hat = nv / (1.0 - ADAM_B2 ** ADAM_STEP)
    return -ADAM_LR * (m_hat / (jnp.sqrt(v_hat) + ADAM_EPS) + ADAM_WD * w), nm, nv


def _adamw(w, g, m, v, *, name):
    rows, cols = w.shape
    tr = _tile(rows, max(8, (1 << 18) // cols // 8 * 8), 8)

    def body(w_ref, g_ref, m_ref, v_ref, d_ref, nm_ref, nv_ref):
        d_ref[...], nm_ref[...], nv_ref[...] = _adam_update(w_ref[...], g_ref[...], m_ref[...], v_ref[...])

    blk = pl.BlockSpec((tr, cols), lambda i: (i, 0))
    out = jax.ShapeDtypeStruct((rows, cols), F32)
    return pl.pallas_call(
        body, name=name, grid=(rows // tr,), in_specs=[blk] * 4, out_specs=[blk] * 3, out_shape=[out] * 3,
        compiler_params=_params("parallel"))(w, g, m, v)


def _adamw_nd(w, g, m, v, *, name):
    two_d = (-1, w.shape[-1])
    outs = _adamw(w.reshape(two_d), g.reshape(two_d), m.reshape(two_d), v.reshape(two_d), name=name)
    return tuple(t.reshape(w.shape) for t in outs)


def _adamw_slots(parts, w, m, v, after, *, name):
    n_layers, rows, cols = w.shape
    tr = _tile(rows, max(16, (1 << 18) // cols // 16 * 16), 16)

    def body(p_ref, w_ref, m_ref, v_ref, after_ref, g_ref, d_ref, nm_ref, nv_ref):
        total = p_ref[0].astype(F32)
        for p in range(1, N_DEV):
            total = total + p_ref[p].astype(F32)
        g_ref[...] = total
        d_ref[...], nm_ref[...], nv_ref[...] = _adam_update(w_ref[...], total, m_ref[...], v_ref[...])

    blk = pl.BlockSpec((None, tr, cols), lambda a, i: (a, i, 0))
    out = jax.ShapeDtypeStruct((n_layers, rows, cols), F32)
    return pl.pallas_call(
        body, name=name, grid=(n_layers, rows // tr),
        in_specs=[pl.BlockSpec((N_DEV, None, tr, cols), lambda a, i: (0, a, i, 0)), blk, blk, blk,
                  pl.BlockSpec(memory_space=pl.ANY)],
        out_specs=[blk] * 4, out_shape=[out] * 4,
        compiler_params=_params("parallel", "parallel"))(parts, w, m, v, after)


def _pack_rows(parts, dtype, quantum_rows):
    flat = jnp.concatenate([p.reshape(-1).astype(dtype) for p in parts])
    n = flat.shape[0]
    padded = -(-n // (LANES * quantum_rows)) * (LANES * quantum_rows)
    return jnp.pad(flat, (0, padded - n)).reshape(padded // LANES, LANES)


def _unpack(flat, shapes):
    lead = flat.shape[:-1]
    out, at = [], 0
    for shape in shapes:
        size = 1
        for dim in shape:
            size *= dim
        out.append(flat[..., at:at + size].reshape(lead + tuple(shape)))
        at += size
    return out


def _whole_from_slots(slots, axis):
    moved = jnp.moveaxis(slots, 0, axis)
    shape = moved.shape
    return moved.reshape(shape[:axis] + (shape[axis] * shape[axis + 1],) + shape[axis + 2:])


def _lane_row(vec, n_heads):
    return jnp.pad(vec, ((0, 0), (n_heads, LANES - 2 * n_heads)))[:, None, :]


REPLICATED = ("norm_mix_w", "a_log", "dt_bias", "gdn_norm_w", "pool_scale", "norm_ffn_w", "conv_ffn_b",
              "norm_final_w")
WEIGHTS = ("norm_mix_w", "w_in", "conv_qkv_w", "a_log", "dt_bias", "gdn_norm_w", "pool_w", "pool_scale", "w_out",
           "norm_ffn_w", "w_up", "conv_ffn_w", "conv_ffn_b", "w_down", "norm_final_w")
SMALL_QUANTUM_ROWS = 512


def kernel(x, norm_mix_w, w_in, conv_qkv_w, a_log, dt_bias, gdn_norm_w, pool_w, pool_scale, w_out, norm_ffn_w, w_up, conv_ffn_w, conv_ffn_b, w_down, norm_final_w, loss_target, m_norm_mix_w, m_w_in, m_conv_qkv_w, m_a_log, m_dt_bias, m_gdn_norm_w, m_pool_w, m_pool_scale, m_w_out, m_norm_ffn_w, m_w_up, m_conv_ffn_w, m_conv_ffn_b, m_w_down, m_norm_final_w, v_norm_mix_w, v_w_in, v_conv_qkv_w, v_a_log, v_dt_bias, v_gdn_norm_w, v_pool_w, v_pool_scale, v_w_out, v_norm_ffn_w, v_w_up, v_conv_ffn_w, v_conv_ffn_b, v_w_down, v_norm_final_w):
    local = dict(norm_mix_w=norm_mix_w, w_in=w_in, conv_qkv_w=conv_qkv_w, a_log=a_log, dt_bias=dt_bias,
                 gdn_norm_w=gdn_norm_w, pool_w=pool_w, pool_scale=pool_scale, w_out=w_out, norm_ffn_w=norm_ffn_w,
                 w_up=w_up, conv_ffn_w=conv_ffn_w, conv_ffn_b=conv_ffn_b, w_down=w_down, norm_final_w=norm_final_w)
    mom_m = dict(norm_mix_w=m_norm_mix_w, w_in=m_w_in, conv_qkv_w=m_conv_qkv_w, a_log=m_a_log, dt_bias=m_dt_bias,
                 gdn_norm_w=m_gdn_norm_w, pool_w=m_pool_w, pool_scale=m_pool_scale, w_out=m_w_out,
                 norm_ffn_w=m_norm_ffn_w, w_up=m_w_up, conv_ffn_w=m_conv_ffn_w, conv_ffn_b=m_conv_ffn_b,
                 w_down=m_w_down, norm_final_w=m_norm_final_w)
    mom_v = dict(norm_mix_w=v_norm_mix_w, w_in=v_w_in, conv_qkv_w=v_conv_qkv_w, a_log=v_a_log, dt_bias=v_dt_bias,
                 gdn_norm_w=v_gdn_norm_w, pool_w=v_pool_w, pool_scale=v_pool_scale, w_out=v_w_out,
                 norm_ffn_w=v_norm_ffn_w, w_up=v_w_up, conv_ffn_w=v_conv_ffn_w, conv_ffn_b=v_conv_ffn_b,
                 w_down=v_w_down, norm_final_w=v_norm_final_w)
    n_layers, n_heads = a_log.shape
    d_model = x.shape[-1]
    dl = n_heads * HEAD_DIM
    n_ba = 2 * n_heads
    assert x.shape[0] == 1 and dl == d_model and pool_scale.shape[1] == d_model
    lay = _win_layout(w_in.shape[2], N_DEV * w_in.shape[2] - n_ba, 4 * dl, n_ba)
    _, me = _here()
    is_ba_dev = me == lay.ba_dev
    my_pad = _main_start(lay, me) - _slab_origin(lay, me)
    ba_cols = slice(lay.ba_off, lay.ba_off + n_ba)

    w_in_bf = w_in.astype(BF16)
    without_ba = jnp.concatenate([w_in_bf[..., :lay.ba_off], w_in_bf[..., lay.ba_off + n_ba:],
                                  jnp.zeros(w_in.shape[:2] + (n_ba,), BF16)], axis=-1)
    slab = lax.dynamic_update_slice(jnp.zeros(w_in.shape[:2] + (lay.slot_w,), BF16),
                                    jnp.where(is_ba_dev, without_ba, w_in_bf), (0, 0, my_pad))
    ba_part = jnp.pad(jnp.where(is_ba_dev, w_in_bf[..., ba_cols], jnp.zeros((), BF16)),
                      ((0, 0), (0, 0), (0, LANES - n_ba)))
    convs = _pack_rows([conv_qkv_w, conv_ffn_w], F32, 16)
    ba_slots, conv_slots = _gather([ba_part, convs], [0, 0], name="gather_small_weights")
    conv_parts = _unpack(conv_slots.reshape(N_DEV, -1), [conv_qkv_w.shape, conv_ffn_w.shape])
    conv_qkv_whole, conv_ffn_whole = (_whole_from_slots(p, 2) for p in conv_parts)
    alog_rows, dtb_rows = _lane_row(a_log, n_heads), _lane_row(dt_bias, n_heads)

    def with_own_slot(own, axis):
        zone = lax.empty(own.shape[:axis] + (N_DEV,) + own.shape[axis:], own.dtype)
        return lax.dynamic_update_slice(zone, jnp.expand_dims(own, axis), (0,) * axis + (me,) + (0,) * (own.ndim - axis))

    slot_axis = dict(slab=0, w_up=0, w_out=0, w_down=0, pool_w=1)
    gather_groups = (("slab",), ("w_up", "w_out", "w_down", "pool_w"))
    gather_src = lambda t, ref, to: ref
    in_flight = {}
    token = conv_slots
    for l in range(n_layers):
        own = dict(slab=slab[l], w_up=w_up[l].astype(BF16), w_out=w_out[l].astype(BF16),
                   w_down=w_down[l].astype(BF16), pool_w=pool_w[l].astype(BF16))
        for part, names in zip("ab", gather_groups):
            axes = [slot_axis[n] for n in names]
            dst = lambda t, ref, frm, axes=axes: _slot_view(ref, axes[t], frm)
            *handles, token = _start_exchange([own[n] for n in names], [with_own_slot(own[n], slot_axis[n]) for n in names],
                                              gather_src, dst, token, name="gather_start_%d%s" % (l, part))
            in_flight[l, part] = (handles, dst)

    def arrived(l, part, after):
        (send_sems, recv_sems, srcs, lands), dst = in_flight[l, part]
        return _wait_exchange(send_sems, recv_sems, srcs, lands, gather_src, dst, after, name="gather_wait_%d%s" % (l, part))

    xc = x[0]
    layer_w, saved = [], []
    after = token
    for l in range(n_layers):
        slabs, = arrived(l, "a", after)
        early = dict(n_heads=n_heads, norm_mix_w=norm_mix_w[l], alog_row=alog_rows[l], dtb_row=dtb_rows[l],
                     w_main=_assemble_w_main(slabs, lay, name="assemble_w_main"), w_ba=ba_slots[lay.ba_dev, l],
                     conv_qkv_w=conv_qkv_whole[l])

        def late_weights(o, l=l):
            up_slots, out_slots, down_slots, pool_slots = arrived(l, "b", o)
            return dict(norm_ffn_w=norm_ffn_w[l], gdn_norm_w=gdn_norm_w[l], pool_scale=pool_scale[l],
                        conv_ffn_b=conv_ffn_b[l], conv_ffn_w=conv_ffn_whole[l], w_up_slots=up_slots,
                        w_out=out_slots.reshape(-1, d_model), w_down=down_slots.reshape(-1, d_model),
                        pool_w=pool_slots.reshape(pool_slots.shape[0], -1, pool_slots.shape[-1]))

        xc, sv, wl = _layer_fwd(xc, early, late_weights)
        layer_w.append(wl)
        saved.append(sv)
        after = xc
    loss_row, dx, dx_bf, d_final = _loss_head(xc, norm_final_w, loss_target[0], name="loss_head")
    loss = lax.psum(loss_row[0, 0], MESH_AXES)

    shard = {n: local[n].shape[1:] for n in ("w_up", "w_out", "w_down", "pool_w")}
    shard["w_main"] = (d_model, lay.slot_w)
    recvs = {n: lax.empty((N_DEV, n_layers) + shard[n], BF16) for n in shard}
    up_w, out_rows, down_rows, pool_rows = shard["w_up"][1], shard["w_out"][0], shard["w_down"][0], shard["pool_w"][1]
    owned = dict(w_main=lambda to: (1, _slab_origin(lay, to), lay.slot_w), w_up=lambda to: (1, to * up_w, up_w),
                 w_out=lambda to: (0, to * out_rows, out_rows), w_down=lambda to: (0, to * down_rows, down_rows),
                 pool_w=lambda to: (1, to * pool_rows, pool_rows))
    scatter_groups = dict(e=("w_up", "w_down"), l=("w_main", "w_out", "pool_w"))
    pending = {}

    def send_grads(part, l, g):
        names = scatter_groups[part]

        def src(t, ref, to):
            axis, first, length = owned[names[t]](to)
            return ref.at[(slice(None),) * axis
                          + (pl.ds(pl.multiple_of(first, LANES if axis == ref.ndim - 1 else 16), length),)]

        dst = lambda t, ref, frm: ref.at[frm, l]
        received(part, g[names[0]])
        lands = []
        for n in names:
            axis, first, length = owned[n](me)
            mine = lax.dynamic_slice_in_dim(g[n], first, length, axis=axis)
            lands.append(lax.dynamic_update_slice(recvs[n], mine[None, None], (me, l) + (0,) * mine.ndim))
        send_sems, recv_sems, grads, lands, token = _start_exchange([g[n] for n in names], lands, src, dst, None,
                                                                    name="scatter_start_%d%s" % (l, part))
        pending[part] = (send_sems, recv_sems, grads, lands, src, dst, "scatter_wait_%d%s" % (l, part))
        return token

    def received(part, after):
        if part in pending:
            *args, name = pending.pop(part)
            recvs.update(zip(scatter_groups[part], _wait_exchange(*args, after=after, name=name)))

    layer_grads = [None] * n_layers
    token = None
    for l in reversed(range(n_layers)):
        dx2, dx2_bf, g = _layer_bwd_ffn(dx, dx_bf, layer_w[l], saved[l], token)
        token = send_grads("e", l, g)
        dproj, dpba, g_mix = _layer_bwd_mix(dx2, dx2_bf, layer_w[l], saved[l], token)
        g.update(g_mix)
        g["pool_w"] = g["pool_w"].astype(BF16)
        token = send_grads("l", l, g)
        dx, dx_bf, g["norm_mix_w"] = _layer_bwd_in(dproj, dpba, dx2, layer_w[l], saved[l], token)
        layer_grads[l] = g
    received("e", token)
    grad_x = dx
    stack = lambda name: jnp.stack([g[name] for g in layer_grads])

    grad, delta, new_m, new_v = {}, {}, {}, {}
    flat = lambda t, lead: t.reshape(t.shape[:lead] + (-1, t.shape[-1]))

    def update_shard(n, after):
        outs = _adamw_slots(flat(recvs[n], 2), flat(local[n], 1), flat(mom_m[n], 1), flat(mom_v[n], 1), after,
                            name="adamw_" + n)
        grad[n], delta[n], new_m[n], new_v[n] = (t.reshape(local[n].shape) for t in outs)

    update_shard("w_up", token)
    update_shard("w_down", token)

    small_names = REPLICATED + ("conv_qkv_w", "conv_ffn_w", "w_ba")
    g_small = dict(norm_mix_w=stack("norm_mix_w")[:, 0], a_log=stack("alog_row")[:, 0, n_heads:n_ba],
                   dt_bias=stack("dtb_row")[:, 0, n_heads:n_ba], gdn_norm_w=stack("gdn_norm_w")[:, 0],
                   pool_scale=stack("pool_scale")[:, 0], norm_ffn_w=stack("norm_ffn_w")[:, 0],
                   conv_ffn_b=stack("conv_ffn_b")[:, 0], norm_final_w=d_final[0], conv_qkv_w=stack("conv_qkv_w"),
                   conv_ffn_w=stack("conv_ffn_w"), w_ba=stack("w_ba")[..., :n_ba])
    small_shapes = [g_small[n].shape for n in small_names]
    small_slots, = _gather([_pack_rows([g_small[n] for n in small_names], F32, SMALL_QUANTUM_ROWS)], [0],
                           name="gather_small_grads", after=new_v["w_down"])
    small_sum = _sum_slots(small_slots[:, None], name="sum_small_grads")
    grad.update(zip(small_names, _unpack(small_sum.reshape(-1), small_shapes)))
    for n in ("conv_qkv_w", "conv_ffn_w"):
        width = local[n].shape[2]
        grad[n] = lax.dynamic_slice_in_dim(grad[n], me * width, width, axis=2)

    received("l", small_sum)
    update_shard("w_out", token)
    update_shard("pool_w", token)
    main_sum = _sum_slots(recvs["w_main"], name="sum_w_main_grads")
    g_main = lax.dynamic_slice_in_dim(main_sum, my_pad, lay.shard_w, axis=2)
    with_ba = jnp.concatenate([g_main[..., :lay.ba_off], grad.pop("w_ba"),
                               g_main[..., lay.ba_off:lay.shard_w - n_ba]], axis=-1)
    grad["w_in"] = jnp.where(is_ba_dev, with_ba, g_main)
    for n in ("w_in", "conv_qkv_w", "conv_ffn_w"):
        delta[n], new_m[n], new_v[n] = _adamw_nd(local[n], grad[n], mom_m[n], mom_v[n], name="adamw_" + n)
    packed = [_pack_rows([src[n] for n in REPLICATED], F32, 8) for src in (local, grad, mom_m, mom_v)]
    rep_out = _adamw(*packed, name="adamw_replicated")
    rep_shapes = [local[n].shape for n in REPLICATED]
    for dst, arr in zip((delta, new_m, new_v), rep_out):
        dst.update(zip(REPLICATED, _unpack(arr.reshape(-1), rep_shapes)))

    return (loss, grad_x[None], *[grad[n] for n in WEIGHTS], *[delta[n] for n in WEIGHTS],
            *[new_m[n] for n in WEIGHTS], *[new_v[n] for n in WEIGHTS])
```

```python
import collections

import jax
import jax.numpy as jnp
from jax import lax
from jax.experimental import pallas as pl
from jax.experimental.pallas import tpu as pltpu

F32 = jnp.float32
BF16 = jnp.bfloat16
MESH_AXES = ("x", "y", "c")
N_DEV = 8

NORM_EPS = 1e-6
HEAD_DIM = 128
GDN_CHUNK = 64
GDN_BATCH = 16
GDN_HEADS = 2
POOL_WINDOWS = (2, 4, 8, 16)
POOL_HALO = 16
CONV_HALO = 16
LANES = 128
V7X_VMEM_LIMIT_BYTES = 56 * 1024 * 1024
WHOLE_DIM = 1 << 30

ADAM_LR = 0.001
ADAM_B1 = 0.9
ADAM_B2 = 0.999
ADAM_EPS = 1e-08
ADAM_WD = 0.01
ADAM_STEP = 10


def _mx(v):
    return v.astype(BF16)


def _dot(a, b, ta=False, tb=False, precision=None):
    dims = (((0 if ta else 1,), (1 if tb else 0,)), ((), ()))
    return lax.dot_general(a, b, dims, precision=precision, preferred_element_type=F32)


def _tile(dim, target, quantum=LANES):
    if dim <= target:
        return dim
    t = (target // quantum) * quantum
    while t >= quantum:
        if dim % t == 0:
            return t
        t -= quantum
    return dim


def _params(*semantics):
    return pltpu.CompilerParams(dimension_semantics=semantics, vmem_limit_bytes=V7X_VMEM_LIMIT_BYTES)


def _sigmoid(v):
    return 0.5 * jnp.tanh(0.5 * v) + 0.5


def _softplus(v):
    return jnp.maximum(v, 0.0) + jnp.log(1.0 + jnp.exp(-jnp.abs(v)))


def _recip(v):
    r = pl.reciprocal(v, approx=True)
    return r * (2.0 - v * r)


_ERFC_P = 0.3275911
_ERFC_A = (0.254829592, -0.284496736, 1.421413741, -1.453152027, 1.061405429)


def _normal_cdf(v):
    e = jnp.exp(-0.5 * v * v)
    t = _recip(1.0 + (_ERFC_P * 2.0 ** -0.5) * jnp.abs(v))
    poly = jnp.full_like(v, _ERFC_A[-1])
    for coef in _ERFC_A[-2::-1]:
        poly = poly * t + coef
    half_tail = (0.5 * t) * poly * e
    return jnp.where(v >= 0.0, 1.0 - half_tail, half_tail), e


def _gelu(v):
    return v * _normal_cdf(v)[0]


def _gelu_and_grad(v):
    cdf, e = _normal_cdf(v)
    return v * cdf, cdf + v * e * ((2.0 * jnp.pi) ** -0.5)


def _rows_before(cat, shift, halo):
    return pltpu.roll(cat, shift, 0)[halo:]


def _rows_after(cat, shift, rows):
    return pltpu.roll(cat, cat.shape[0] - shift, 0)[:rows]


def _accumulate(ref, value, first):
    @pl.when(first)
    def _():
        ref[...] = value

    @pl.when(jnp.logical_not(first))
    def _():
        ref[...] += value


def _matmul(a, b, *, name, ta=False, tb=False, add=None, out_dtype=F32, tm=512, tn=1024, tk=2048,
            b_slots=False, a_halves=False, b_halves=False, after=None):
    assert not (a_halves and ta) and not (b_halves and (tb or b_slots))
    if a_halves:
        m, k = a.shape[1], 2 * a.shape[2]
    else:
        m, k = (a.shape[1], a.shape[0]) if ta else a.shape
    if b_slots:
        b_rows, b_cols = b.shape[1], N_DEV * b.shape[2]
    elif b_halves:
        b_rows, b_cols = b.shape[1], 2 * b.shape[2]
    else:
        b_rows, b_cols = b.shape
    n, kb = (b_rows, b_cols) if tb else (b_cols, b_rows)
    assert kb == k
    if b_slots:
        tn, tk = (tn, b.shape[2]) if tb else (b.shape[2], tk)
    if b_halves:
        tn = _tile(b.shape[2], tn)
    tm, tn, tk = _tile(m, tm), _tile(n, tn), _tile(k, tk)
    assert not a_halves or a.shape[2] % tk == 0
    nk = k // tk
    has_add = add is not None
    n_in = 2 + has_add + (after is not None)

    def body(*refs):
        a_ref, b_ref = refs[0], refs[1]
        add_ref = refs[2] if has_add else None
        o_ref, acc_ref = refs[n_in], refs[n_in + 1]
        kk = pl.program_id(2)
        part = _dot(_mx(a_ref[...]), _mx(b_ref[...]), ta, tb)

        def finish(total):
            if has_add:
                total = total + add_ref[...]
            o_ref[...] = total.astype(out_dtype)

        if nk == 1:
            finish(part)
        else:
            _accumulate(acc_ref, part, kk == 0)

            @pl.when(kk == nk - 1)
            def _():
                finish(acc_ref[...])

    if a_halves:
        per_half = a.shape[2] // tk
        a_spec = pl.BlockSpec((None, tm, tk), lambda j, i, kk: (kk // per_half, i, kk % per_half))
    elif ta:
        a_spec = pl.BlockSpec((tk, tm), lambda j, i, kk: (kk, i))
    else:
        a_spec = pl.BlockSpec((tm, tk), lambda j, i, kk: (i, kk))
    b_block = (tn, tk) if tb else (tk, tn)
    if b_halves:
        per_half = b.shape[2] // tn
        b_spec = pl.BlockSpec((None, tk, tn), lambda j, i, kk: (j // per_half, kk, j % per_half))
    elif b_slots:
        b_spec = pl.BlockSpec((None,) + b_block, (lambda j, i, kk: (kk, j, 0)) if tb else (lambda j, i, kk: (j, kk, 0)))
    else:
        b_spec = pl.BlockSpec(b_block, (lambda j, i, kk: (j, kk)) if tb else (lambda j, i, kk: (kk, j)))
    o_spec = pl.BlockSpec((tm, tn), lambda j, i, kk: (i, j))
    in_specs = [a_spec, b_spec] + ([o_spec] if has_add else [])
    args = (a, b) + ((add,) if has_add else ())
    if after is not None:
        in_specs.append(pl.BlockSpec(memory_space=pl.ANY))
        args += (after,)
    acc_shape = (tm, tn) if nk > 1 else (8, LANES)
    return pl.pallas_call(
        body, name=name, grid=(n // tn, m // tm, nk), in_specs=in_specs, out_specs=o_spec,
        out_shape=jax.ShapeDtypeStruct((m, n), out_dtype), scratch_shapes=[pltpu.VMEM(acc_shape, F32)],
        compiler_params=_params("parallel", "parallel", "arbitrary"))(*args)


def _rmsnorm_fwd(x, w, *, name):
    s, d = x.shape
    ts = _tile(s, 512, 16)

    def body(x_ref, w_ref, o_ref):
        xf = x_ref[...]
        r = lax.rsqrt(jnp.mean(xf * xf, axis=-1, keepdims=True) + NORM_EPS)
        o_ref[...] = (xf * r * w_ref[...]).astype(BF16)

    return pl.pallas_call(
        body, name=name, grid=(s // ts,),
        in_specs=[pl.BlockSpec((ts, d), lambda i: (i, 0)), pl.BlockSpec((1, d), lambda i: (0, 0))],
        out_specs=pl.BlockSpec((ts, d), lambda i: (i, 0)),
        out_shape=jax.ShapeDtypeStruct((s, d), BF16), compiler_params=_params("parallel"))(x, w.reshape(1, d))


def _rmsnorm_bwd(dy, x, w, dres, *, name):
    s, d = x.shape
    ts = _tile(s, 256, 16)

    def body(dy_ref, x_ref, w_ref, dres_ref, dx_ref, dxb_ref, dw_ref):
        xf = x_ref[...]
        dyf = dy_ref[...]
        r = lax.rsqrt(jnp.mean(xf * xf, axis=-1, keepdims=True) + NORM_EPS)
        xh = xf * r
        dxh = dyf * w_ref[...]
        dx = dres_ref[...] + r * (dxh - xh * jnp.mean(dxh * xh, axis=-1, keepdims=True))
        dx_ref[...] = dx
        dxb_ref[...] = dx.astype(BF16)
        _accumulate(dw_ref, jnp.sum(dyf * xh, axis=0, keepdims=True), pl.program_id(0) == 0)

    row = pl.BlockSpec((ts, d), lambda i: (i, 0))
    vec = pl.BlockSpec((1, d), lambda i: (0, 0))
    return pl.pallas_call(
        body, name=name, grid=(s // ts,), in_specs=[row, row, vec, row], out_specs=[row, row, vec],
        out_shape=[jax.ShapeDtypeStruct((s, d), F32), jax.ShapeDtypeStruct((s, d), BF16),
                   jax.ShapeDtypeStruct((1, d), F32)],
        compiler_params=_params("arbitrary"))(dy, x, w.reshape(1, d), dres)


def _loss_head(x, w, target, *, name):
    s, d = x.shape
    ts = _tile(s, 256, 16)

    def body(x_ref, w_ref, t_ref, loss_ref, dx_ref, dxb_ref, dw_ref):
        first = pl.program_id(0) == 0
        xf = x_ref[...]
        wv = w_ref[...]
        r = lax.rsqrt(jnp.mean(xf * xf, axis=-1, keepdims=True) + NORM_EPS)
        xh = xf * r
        err = xh * wv - t_ref[...]
        part = 0.5 * jnp.sum(jnp.mean(err * err, axis=-1, keepdims=True), axis=0, keepdims=True)
        _accumulate(loss_ref, jnp.broadcast_to(part, (1, LANES)), first)
        dyf = err * (1.0 / d)
        dxh = dyf * wv
        dx = r * (dxh - xh * jnp.mean(dxh * xh, axis=-1, keepdims=True))
        dx_ref[...] = dx
        dxb_ref[...] = dx.astype(BF16)
        _accumulate(dw_ref, jnp.sum(dyf * xh, axis=0, keepdims=True), first)

    row = pl.BlockSpec((ts, d), lambda i: (i, 0))
    vec = pl.BlockSpec((1, d), lambda i: (0, 0))
    return pl.pallas_call(
        body, name=name, grid=(s // ts,), in_specs=[row, vec, row],
        out_specs=[pl.BlockSpec((1, LANES), lambda i: (0, 0)), row, row, vec],
        out_shape=[jax.ShapeDtypeStruct((1, LANES), F32), jax.ShapeDtypeStruct((s, d), F32),
                   jax.ShapeDtypeStruct((s, d), BF16), jax.ShapeDtypeStruct((1, d), F32)],
        compiler_params=_params("arbitrary"))(x, w.reshape(1, d), target)


def _qkv_fwd(proj, conv_w, n_heads, *, name):
    s = proj.shape[0]
    dl = n_heads * HEAD_DIM
    width = conv_w.shape[0]
    ts = _tile(s, 512, 8)

    def body(x_ref, halo_ref, w_ref, o_ref):
        i, sec = pl.program_id(0), pl.program_id(1)
        xv = x_ref[...].astype(F32)
        cat = jnp.concatenate([jnp.where(i > 0, halo_ref[...].astype(F32), 0.0), xv], axis=0)
        c = xv * w_ref[pl.ds(width - 1, 1), :]
        for sh in range(1, width):
            c = c + _rows_before(cat, sh, CONV_HALO) * w_ref[pl.ds(width - 1 - sh, 1), :]
        act = c * _sigmoid(c)

        @pl.when(sec == 2)
        def _():
            o_ref[...] = act

        @pl.when(sec < 2)
        def _():
            scale = jnp.where(sec == 0, HEAD_DIM ** -0.5, 1.0)
            for h in range(n_heads):
                cols = slice(h * HEAD_DIM, (h + 1) * HEAD_DIM)
                ah = act[:, cols]
                o_ref[:, cols] = ah * lax.rsqrt(jnp.sum(ah * ah, axis=-1, keepdims=True) + NORM_EPS) * scale

    return pl.pallas_call(
        body, name=name, grid=(s // ts, 3),
        in_specs=[pl.BlockSpec((ts, dl), lambda i, sec: (i, sec)),
                  pl.BlockSpec((CONV_HALO, dl), lambda i, sec: (jnp.maximum(i * (ts // CONV_HALO) - 1, 0), sec)),
                  pl.BlockSpec((width, dl), lambda i, sec: (0, sec))],
        out_specs=pl.BlockSpec((None, ts, dl), lambda i, sec: (sec, i, 0)),
        out_shape=jax.ShapeDtypeStruct((3, s, dl), F32),
        compiler_params=_params("parallel", "parallel"))(proj, proj, conv_w)


def _qkv_bwd(dqkv, proj, conv_w, n_heads, *, name):
    s = proj.shape[0]
    dl = n_heads * HEAD_DIM
    width = conv_w.shape[0]
    ts = _tile(s, 256, 16)
    n_tiles = s // ts
    per = ts // CONV_HALO
    rows = ts + CONV_HALO

    def body(d_ref, dnext_ref, x_ref, xprev_ref, xnext_ref, w_ref, dx_ref, dw_ref):
        sec, i = pl.program_id(0), pl.program_id(1)
        xv = x_ref[...].astype(F32)
        cat = jnp.concatenate([jnp.where(i > 0, xprev_ref[...].astype(F32), 0.0), xv, xnext_ref[...].astype(F32)],
                              axis=0)
        shifted = [cat[CONV_HALO:]] + [_rows_before(cat, sh, CONV_HALO) for sh in range(1, width)]
        c = shifted[0] * w_ref[pl.ds(width - 1, 1), :]
        for sh in range(1, width):
            c = c + shifted[sh] * w_ref[pl.ds(width - 1 - sh, 1), :]
        sig = _sigmoid(c)
        act = c * sig
        dout = jnp.concatenate([d_ref[...], dnext_ref[...]], axis=0)
        scale = jnp.where(sec == 0, HEAD_DIM ** -0.5, 1.0)
        is_v = sec == 2
        pieces = []
        for h in range(n_heads):
            cols = slice(h * HEAD_DIM, (h + 1) * HEAD_DIM)
            ah, dh = act[:, cols], dout[:, cols]
            nrm = lax.rsqrt(jnp.sum(ah * ah, axis=-1, keepdims=True) + NORM_EPS)
            dnormed = scale * nrm * (dh - ah * (nrm * nrm) * jnp.sum(dh * ah, axis=-1, keepdims=True))
            pieces.append(jnp.where(is_v, dh, dnormed))
        dact = jnp.concatenate(pieces, axis=1)
        dc = dact * sig * (1.0 + c * (1.0 - sig))
        live = jnp.logical_or(lax.broadcasted_iota(jnp.int32, (rows, 1), 0) < ts, i < n_tiles - 1)
        dc = jnp.where(live, dc, 0.0)
        dx = dc[:ts] * w_ref[pl.ds(width - 1, 1), :]
        for sh in range(1, width):
            dx = dx + _rows_after(dc, sh, ts) * w_ref[pl.ds(width - 1 - sh, 1), :]
        dx_ref[...] = dx.astype(BF16)
        dw_rows = [jnp.sum(dc[:ts] * shifted[width - 1 - j][:ts], axis=0, keepdims=True) for j in range(width)]
        _accumulate(dw_ref, jnp.concatenate(dw_rows, axis=0), i == 0)

    return pl.pallas_call(
        body, name=name, grid=(3, n_tiles),
        in_specs=[pl.BlockSpec((None, ts, dl), lambda sec, i: (sec, i, 0)),
                  pl.BlockSpec((None, CONV_HALO, dl), lambda sec, i: (sec, jnp.minimum((i + 1) * per, s // CONV_HALO - 1), 0)),
                  pl.BlockSpec((ts, dl), lambda sec, i: (i, sec)),
                  pl.BlockSpec((CONV_HALO, dl), lambda sec, i: (jnp.maximum(i * per - 1, 0), sec)),
                  pl.BlockSpec((CONV_HALO, dl), lambda sec, i: (jnp.minimum((i + 1) * per, s // CONV_HALO - 1), sec)),
                  pl.BlockSpec((width, dl), lambda sec, i: (0, sec))],
        out_specs=[pl.BlockSpec((ts, dl), lambda sec, i: (i, sec)), pl.BlockSpec((width, dl), lambda sec, i: (0, sec))],
        out_shape=[jax.ShapeDtypeStruct((s, 3 * dl), BF16), jax.ShapeDtypeStruct((width, 3 * dl), F32)],
        compiler_params=_params("parallel", "arbitrary"))(dqkv, dqkv, proj, proj, proj, conv_w)


def _ba_fwd(pba, alog_row, dtb_row, n_heads, *, name):
    s = pba.shape[0]
    ts = _tile(s, 1024, 8)

    def body(x_ref, alog_ref, dtb_ref, o_ref):
        xv = x_ref[...]
        lane = lax.broadcasted_iota(jnp.int32, xv.shape, 1)
        g = -jnp.exp(alog_ref[...]) * _softplus(xv + dtb_ref[...])
        o_ref[...] = jnp.where(lane < n_heads, _sigmoid(xv), jnp.where(lane < 2 * n_heads, g, 0.0))

    row = pl.BlockSpec((ts, LANES), lambda i: (i, 0))
    vec = pl.BlockSpec((1, LANES), lambda i: (0, 0))
    return pl.pallas_call(
        body, name=name, grid=(s // ts,), in_specs=[row, vec, vec], out_specs=row,
        out_shape=jax.ShapeDtypeStruct((s, LANES), F32), compiler_params=_params("parallel"))(pba, alog_row, dtb_row)


def _ba_bwd(dbg, pba, alog_row, dtb_row, n_heads, *, name):
    s = pba.shape[0]
    ts = _tile(s, 1024, 16)

    def body(d_ref, x_ref, alog_ref, dtb_ref, dx_ref, dalog_ref, ddtb_ref):
        first = pl.program_id(0) == 0
        xv, dv = x_ref[...], d_ref[...]
        lane = lax.broadcasted_iota(jnp.int32, xv.shape, 1)
        beta = _sigmoid(xv)
        neg_a = -jnp.exp(alog_ref[...])
        xa = xv + dtb_ref[...]
        is_a = jnp.logical_and(lane >= n_heads, lane < 2 * n_heads)
        d_xa = jnp.where(is_a, dv * neg_a * _sigmoid(xa), 0.0)
        d_g_times_g = jnp.where(is_a, dv * neg_a * _softplus(xa), 0.0)
        dx_ref[...] = jnp.where(lane < n_heads, dv * beta * (1.0 - beta), d_xa).astype(BF16)
        _accumulate(dalog_ref, jnp.sum(d_g_times_g, axis=0, keepdims=True), first)
        _accumulate(ddtb_ref, jnp.sum(d_xa, axis=0, keepdims=True), first)

    row = pl.BlockSpec((ts, LANES), lambda i: (i, 0))
    vec = pl.BlockSpec((1, LANES), lambda i: (0, 0))
    return pl.pallas_call(
        body, name=name, grid=(s // ts,), in_specs=[row, row, vec, vec], out_specs=[row, vec, vec],
        out_shape=[jax.ShapeDtypeStruct((s, LANES), BF16), jax.ShapeDtypeStruct((1, LANES), F32),
                   jax.ShapeDtypeStruct((1, LANES), F32)],
        compiler_params=_params("arbitrary"))(dbg, pba, alog_row, dtb_row)


def _bdot(a, b, ta=False, tb=False, precision=None):
    dims = (((1 if ta else 2,), (2 if tb else 1,)), ((0,), (0,)))
    return lax.dot_general(a, b, dims, precision=precision, preferred_element_type=F32)


def _split_bf16(v):
    hi = v.astype(BF16)
    return hi, (v - hi.astype(F32)).astype(BF16)


def _bdot_x3(a, b, ta=False, tb=False):
    return _bdot(a[0], b[0], ta, tb) + (_bdot(a[0], b[1], ta, tb) + _bdot(a[1], b[0], ta, tb))


def _chunk_masks():
    ri = lax.broadcasted_iota(jnp.int32, (GDN_CHUNK, GDN_CHUNK), 0)
    ci = lax.broadcasted_iota(jnp.int32, (GDN_CHUNK, GDN_CHUNK), 1)
    return ri == ci, ri >= ci, ri > ci, ri <= ci


def _row_to_col(row, eye):
    return jnp.sum(jnp.where(eye, row, 0.0), axis=2, keepdims=True)


def _col_to_row(col, eye):
    return jnp.sum(jnp.where(eye, col, 0.0), axis=1, keepdims=True)


_Gates = collections.namedtuple("_Gates", "beta_col decay e_col f_col dec")


def _gdn_gates(beta_row, g_row):
    eye, tril, _, triu = _chunk_masks()
    g_col = _row_to_col(g_row, eye)
    gc_col = jnp.sum(jnp.where(tril, g_row, 0.0), axis=2, keepdims=True)
    gc_row = jnp.sum(jnp.where(triu, g_col, 0.0), axis=1, keepdims=True)
    g_last = jnp.sum(g_row, axis=2, keepdims=True)
    decay = jnp.exp(jnp.where(tril, gc_col - gc_row, -jnp.inf))
    return _Gates(_row_to_col(beta_row, eye), decay, jnp.exp(gc_col), jnp.exp(g_last - gc_col), jnp.exp(g_last))


def _unit_lower_inverse(lmat):
    c = GDN_CHUNK
    t = jnp.where(_chunk_masks()[0], 1.0, 0.0) - lmat
    l_parts = _split_bf16(lmat)
    p = _bdot_x3(l_parts, l_parts)
    doublings = c.bit_length() - 2
    for r in range(doublings):
        p_parts = _split_bf16(p)
        if r < doublings - 1:
            both = _bdot_x3(_split_bf16(jnp.concatenate([t, p], axis=1)), p_parts)
            t, p = t + both[:, :c], both[:, c:]
        else:
            t = t + _bdot_x3(_split_bf16(t), p_parts)
    return t


def _gdn_solve(q, k, v, gates):
    strict = _chunk_masks()[2]
    kb = k * gates.beta_col
    lmat = jnp.where(strict, _bdot(_mx(kb), _mx(k), tb=True) * gates.decay, 0.0)
    tmat = _unit_lower_inverse(lmat)
    sol = _bdot_x3(_split_bf16(tmat), _split_bf16(jnp.concatenate([v * gates.beta_col, kb * gates.e_col], axis=2)))
    at = _bdot(_mx(q), _mx(k), tb=True) * gates.decay
    return lmat, tmat, sol, at


def _gdn_blocking(s):
    n_chunks = s // GDN_CHUNK
    per_step = 16 if n_chunks % 16 == 0 else n_chunks
    assert per_step % GDN_BATCH == 0
    return n_chunks, per_step, n_chunks // per_step


def _load_chunks(ref, sec, n0, hp):
    r0 = pl.multiple_of(n0 * GDN_CHUNK, GDN_BATCH * GDN_CHUNK)
    rows = pl.ds(r0, GDN_BATCH * GDN_CHUNK)
    cols = slice(hp * HEAD_DIM, (hp + 1) * HEAD_DIM)
    val = ref[rows, cols] if sec is None else ref[sec, rows, cols]
    return val.reshape(GDN_BATCH, GDN_CHUNK, HEAD_DIM)


def _store_chunks(ref, sec, n0, hp, val):
    r0 = pl.multiple_of(n0 * GDN_CHUNK, GDN_BATCH * GDN_CHUNK)
    rows = pl.ds(r0, GDN_BATCH * GDN_CHUNK)
    cols = slice(hp * HEAD_DIM, (hp + 1) * HEAD_DIM)
    flat = val.reshape(GDN_BATCH * GDN_CHUNK, HEAD_DIM)
    if sec is None:
        ref[rows, cols] = flat
    else:
        ref[sec, rows, cols] = flat


def _gdn_specs(n_heads, n_steps, per_step, order):
    rows, width = per_step * GDN_CHUNK, GDN_HEADS * HEAD_DIM
    rowvec = pl.BlockSpec((GDN_HEADS, per_step, 1, GDN_CHUNK), lambda h, j: (h, order(j), 0, 0))
    qkv = pl.BlockSpec((3, rows, width), lambda h, j: (0, order(j), h))
    act = pl.BlockSpec((rows, width), lambda h, j: (order(j), h))
    states = pl.BlockSpec((GDN_HEADS, per_step, HEAD_DIM, HEAD_DIM), lambda h, j: (h, order(j), 0, 0))
    return rowvec, qkv, act, states


def _gdn_fwd(qkv, beta_rows, g_rows, *, name):
    _, s, dl = qkv.shape
    n_heads = dl // HEAD_DIM
    c = GDN_CHUNK
    n_chunks, per_step, n_steps = _gdn_blocking(s)
    n_groups = per_step // GDN_BATCH
    heads = range(GDN_HEADS)

    def body(qkv_ref, b_ref, g_ref, o_ref, st_ref, state_ref, sol_s, at_s, qd_s, kmat_s, nmat_s, dec_s):
        @pl.when(pl.program_id(1) == 0)
        def _():
            state_ref[...] = jnp.zeros_like(state_ref)

        def solve(gi, carry):
            n0 = gi * GDN_BATCH
            grp = pl.ds(n0, GDN_BATCH)
            for hp in heads:
                q, k, v = (_load_chunks(qkv_ref, j, n0, hp) for j in range(3))
                gates = _gdn_gates(b_ref[hp, grp], g_ref[hp, grp])
                _, _, sol, at = _gdn_solve(q, k, v, gates)
                mke = _mx(k * gates.f_col)
                sol_s[hp, grp] = sol
                at_s[hp, grp] = at
                qd_s[hp, grp] = q * gates.e_col
                nmat_s[hp, grp] = _bdot(mke, _mx(sol[:, :, :HEAD_DIM]), ta=True)
                kmat_s[hp, grp] = _bdot(mke, _mx(sol[:, :, HEAD_DIM:]), ta=True)
                dec_s[hp, grp] = jnp.broadcast_to(gates.dec, (GDN_BATCH, 1, LANES))
            return carry

        lax.fori_loop(0, n_groups, solve, 0)

        def recur(n, states):
            out = []
            for hp in heads:
                state = states[hp]
                st_ref[hp, n] = state
                out.append(state * dec_s[hp, n] + nmat_s[hp, n] - _dot(_mx(kmat_s[hp, n]), _mx(state)))
            return tuple(out)

        final = lax.fori_loop(0, per_step, recur, tuple(state_ref[hp] for hp in heads))
        for hp in heads:
            state_ref[hp] = final[hp]

        def emit(gi, carry):
            n0 = gi * GDN_BATCH
            grp = pl.ds(n0, GDN_BATCH)
            for hp in heads:
                sol, mstate = sol_s[hp, grp], _mx(st_ref[hp, grp])
                v_new = sol[:, :, :HEAD_DIM] - _bdot(_mx(sol[:, :, HEAD_DIM:]), mstate)
                o = _bdot(_mx(qd_s[hp, grp]), mstate) + _bdot(_mx(at_s[hp, grp]), _mx(v_new))
                _store_chunks(o_ref, None, n0, hp, o)
            return carry

        lax.fori_loop(0, n_groups, emit, 0)

    rowvec, qkv_spec, act_spec, st_spec = _gdn_specs(n_heads, n_steps, per_step, lambda j: j)
    wide = lambda w: pltpu.VMEM((GDN_HEADS, per_step, c, w), F32)
    square = pltpu.VMEM((GDN_HEADS, per_step, HEAD_DIM, HEAD_DIM), F32)
    return pl.pallas_call(
        body, name=name, grid=(n_heads // GDN_HEADS, n_steps),
        in_specs=[qkv_spec, rowvec, rowvec], out_specs=[act_spec, st_spec],
        out_shape=[jax.ShapeDtypeStruct((s, dl), F32),
                   jax.ShapeDtypeStruct((n_heads, n_chunks, HEAD_DIM, HEAD_DIM), F32)],
        scratch_shapes=[pltpu.VMEM((GDN_HEADS, HEAD_DIM, HEAD_DIM), F32), wide(2 * HEAD_DIM), wide(c), wide(HEAD_DIM),
                        square, square, pltpu.VMEM((GDN_HEADS, per_step, 1, LANES), F32)],
        compiler_params=_params("parallel", "arbitrary"))(qkv, beta_rows, g_rows)


def _gdn_bwd(do, qkv, beta_rows, g_rows, states, *, name):
    _, s, dl = qkv.shape
    n_heads = dl // HEAD_DIM
    c = GDN_CHUNK
    n_chunks, per_step, n_steps = _gdn_blocking(s)
    n_groups = per_step // GDN_BATCH
    heads = range(GDN_HEADS)

    def body(do_ref, qkv_ref, b_ref, g_ref, st_ref, dqkv_ref, db_ref, dg_ref,
             dstate_ref, lmat_s, tmat_s, at_s, dat_s, sol_s, vn_s, dvn_s, dqd_s, kmat_s, nmat_s, dst_s, dec_s):
        @pl.when(pl.program_id(1) == 0)
        def _():
            dstate_ref[...] = jnp.zeros_like(dstate_ref)

        eye, tril, strict, _ = _chunk_masks()

        def solve(gi, carry):
            n0 = gi * GDN_BATCH
            grp = pl.ds(n0, GDN_BATCH)
            for hp in heads:
                q, k, v = (_load_chunks(qkv_ref, j, n0, hp) for j in range(3))
                gates = _gdn_gates(b_ref[hp, grp], g_ref[hp, grp])
                lmat, tmat, sol, at = _gdn_solve(q, k, v, gates)
                mstate = _mx(st_ref[hp, grp])
                md_o = _mx(_load_chunks(do_ref, None, n0, hp))
                mwc = _mx(sol[:, :, HEAD_DIM:])
                v_new = sol[:, :, :HEAD_DIM] - _bdot(mwc, mstate)
                dv_new0 = _bdot(_mx(at), md_o, ta=True)
                lmat_s[hp, grp] = lmat
                tmat_s[hp, grp] = tmat
                sol_s[hp, grp] = sol
                at_s[hp, grp] = at
                vn_s[hp, grp] = v_new
                dat_s[hp, grp] = jnp.where(tril, _bdot(md_o, _mx(v_new), tb=True), 0.0)
                dvn_s[hp, grp] = dv_new0
                dqd_s[hp, grp] = _bdot(md_o, mstate, tb=True)
                nmat_s[hp, grp] = (_bdot(_mx(q * gates.e_col), md_o, ta=True) - _bdot(mwc, _mx(dv_new0), ta=True))
                kmat_s[hp, grp] = _bdot(mwc, _mx(k * gates.f_col), ta=True)
                dec_s[hp, grp] = jnp.broadcast_to(gates.dec, (GDN_BATCH, 1, LANES))
            return carry

        lax.fori_loop(0, n_groups, solve, 0)

        def recur(idx, dstates):
            n = per_step - 1 - idx
            out = []
            for hp in heads:
                dstate = dstates[hp]
                dst_s[hp, n] = dstate
                out.append(dstate * dec_s[hp, n] + nmat_s[hp, n] - _dot(_mx(kmat_s[hp, n]), _mx(dstate)))
            return tuple(out)

        final = lax.fori_loop(0, per_step, recur, tuple(dstate_ref[hp] for hp in heads))
        for hp in heads:
            dstate_ref[hp] = final[hp]

        def emit_head(hp, n0):
            grp = pl.ds(n0, GDN_BATCH)
            q, k, v = (_load_chunks(qkv_ref, j, n0, hp) for j in range(3))
            gates = _gdn_gates(b_ref[hp, grp], g_ref[hp, grp])
            state, dstate = st_ref[hp, grp], dst_s[hp, grp]
            lmat, at, dat, sol = lmat_s[hp, grp], at_s[hp, grp], dat_s[hp, grp], sol_s[hp, grp]
            v_new, dqd = vn_s[hp, grp], dqd_s[hp, grp]
            dv_new = dvn_s[hp, grp] + _bdot(_mx(k * gates.f_col), _mx(dstate))
            dke = _bdot(_mx(v_new), _mx(dstate), tb=True)
            dwc = -_bdot(_mx(dv_new), _mx(state), tb=True)
            ddec = jnp.sum(jnp.sum(dstate * state, axis=2, keepdims=True), axis=1, keepdims=True)
            drhs = _bdot_x3(_split_bf16(tmat_s[hp, grp]), _split_bf16(jnp.concatenate([dv_new, dwc], axis=2)), ta=True)
            dvb, dkbe = drhs[:, :, :HEAD_DIM], drhs[:, :, HEAD_DIM:]
            dl_mat = jnp.where(strict, -_bdot(_mx(drhs), _mx(sol), tb=True), 0.0)
            dkk = dl_mat * gates.decay
            dqk = dat * gates.decay
            kb = k * gates.beta_col
            mk = _mx(k)
            dkb = _bdot(_mx(dkk), mk) + dkbe * gates.e_col
            dq = _bdot(_mx(dqk), mk) + dqd * gates.e_col
            dk = (_bdot(_mx(dqk), _mx(q), ta=True) + _bdot(_mx(dkk), _mx(kb), ta=True) + dke * gates.f_col
                  + dkb * gates.beta_col)
            _store_chunks(dqkv_ref, 0, n0, hp, dq)
            _store_chunks(dqkv_ref, 1, n0, hp, dk)
            _store_chunks(dqkv_ref, 2, n0, hp, dvb * gates.beta_col)
            dbeta_col = jnp.sum(dkb * k + dvb * v, axis=2, keepdims=True)
            through_decay = dl_mat * lmat + dat * at
            dke_ke = jnp.sum(dke * (k * gates.f_col), axis=2, keepdims=True)
            dgc_col = (jnp.sum(through_decay, axis=2, keepdims=True)
                       - _row_to_col(jnp.sum(through_decay, axis=1, keepdims=True), eye)
                       + jnp.sum(dqd * (q * gates.e_col) + dkbe * (kb * gates.e_col), axis=2, keepdims=True) - dke_ke)
            dg_last = jnp.sum(dke_ke, axis=1, keepdims=True) + ddec * gates.dec
            db_ref[hp, grp] = _col_to_row(dbeta_col, eye)
            dg_ref[hp, grp] = jnp.sum(jnp.where(tril, dgc_col, 0.0), axis=1, keepdims=True) + dg_last

        def emit(gi, carry):
            for hp in heads:
                emit_head(hp, gi * GDN_BATCH)
            return carry

        lax.fori_loop(0, n_groups, emit, 0)

    rowvec, qkv_spec, act_spec, st_spec = _gdn_specs(n_heads, n_steps, per_step, lambda j: n_steps - 1 - j)
    wide = lambda w: pltpu.VMEM((GDN_HEADS, per_step, c, w), F32)
    square = pltpu.VMEM((GDN_HEADS, per_step, HEAD_DIM, HEAD_DIM), F32)
    return pl.pallas_call(
        body, name=name, grid=(n_heads // GDN_HEADS, n_steps),
        in_specs=[act_spec, qkv_spec, rowvec, rowvec, st_spec], out_specs=[qkv_spec, rowvec, rowvec],
        out_shape=[jax.ShapeDtypeStruct((3, s, dl), F32),
                   jax.ShapeDtypeStruct((n_heads, n_chunks, 1, c), F32),
                   jax.ShapeDtypeStruct((n_heads, n_chunks, 1, c), F32)],
        scratch_shapes=[pltpu.VMEM((GDN_HEADS, HEAD_DIM, HEAD_DIM), F32), wide(c), wide(c), wide(c), wide(c),
                        wide(2 * HEAD_DIM), wide(HEAD_DIM), wide(HEAD_DIM), wide(HEAD_DIM),
                        square, square, square, pltpu.VMEM((GDN_HEADS, per_step, 1, LANES), F32)],
        compiler_params=_params("parallel", "arbitrary"))(do, qkv, beta_rows, g_rows, states)


def _pool_counts(tile, ts, extra, win):
    t = tile * ts + lax.broadcasted_iota(jnp.int32, (ts + extra, 1), 0)
    return jnp.minimum(t + 1, win).astype(F32)


def _pooled(cat, p_cols, tile, ts, win):
    acc, span = cat, 1
    while span < win:
        acc = acc + pltpu.roll(acc, span, 0)
        span *= 2
    return acc[POOL_HALO:] / _pool_counts(tile, ts, 0, win) - p_cols


def _merge_fwd(proj, o, gnw, pool_w, pool_scale, *, name):
    s, d = o.shape
    n_heads = d // HEAD_DIM
    n_groups, pg = pool_w.shape[0], pool_w.shape[1]
    assert n_groups == len(POOL_WINDOWS) and n_groups * pg == d and pg % HEAD_DIM == 0
    heads_per_group = pg // HEAD_DIM
    ts = _tile(s, 256, 16)

    def body(o_ref, z_ref, p_ref, halo_ref, ga_ref, gb_ref, gnw_ref, pw_ref, ps_ref, out_ref):
        i = pl.program_id(0)
        gnw_v = gnw_ref[...]
        halo = jnp.where(i > 0, halo_ref[...].astype(F32), 0.0)
        for gi, win in enumerate(POOL_WINDOWS):
            gcols = slice(gi * pg, (gi + 1) * pg)
            pv = p_ref[:, gcols].astype(F32)
            pooled = _pooled(jnp.concatenate([halo[:, gcols], pv], axis=0), pv, i, ts, win)
            yb = _dot(_mx(pooled), pw_ref[gi]) * ps_ref[:, gcols]
            for h in range(gi * heads_per_group, (gi + 1) * heads_per_group):
                cols = slice(h * HEAD_DIM, (h + 1) * HEAD_DIM)
                in_group = slice(h * HEAD_DIM - gi * pg, (h + 1) * HEAD_DIM - gi * pg)
                oh, zh = o_ref[:, cols], z_ref[:, cols].astype(F32)
                r = lax.rsqrt(jnp.mean(oh * oh, axis=-1, keepdims=True) + NORM_EPS)
                ya = oh * r * gnw_v * (zh * _sigmoid(zh))
                out_ref[:, cols] = (_sigmoid(ga_ref[:, cols].astype(F32)) * ya
                                    + _sigmoid(gb_ref[:, cols].astype(F32)) * yb[:, in_group]).astype(BF16)

    blk = lambda col: pl.BlockSpec((ts, d), lambda i, col=col: (i, col))
    vec = lambda width: pl.BlockSpec((1, width), lambda i: (0, 0))
    return pl.pallas_call(
        body, name=name, grid=(s // ts,),
        in_specs=[blk(0), blk(3), blk(4),
                  pl.BlockSpec((POOL_HALO, d), lambda i: (jnp.maximum(i * (ts // POOL_HALO) - 1, 0), 4)),
                  blk(5), blk(6), vec(HEAD_DIM), pl.BlockSpec((n_groups, pg, pg), lambda i: (0, 0, 0)), vec(d)],
        out_specs=blk(0), out_shape=jax.ShapeDtypeStruct((s, d), BF16),
        compiler_params=_params("parallel"))(o, proj, proj, proj, proj, proj, gnw.reshape(1, HEAD_DIM), pool_w,
                                              pool_scale.reshape(1, d))


def _merge_bwd(dmixed, proj, o, gnw, pool_w, pool_scale, *, name):
    s, d = o.shape
    n_heads = d // HEAD_DIM
    n_groups, pg = pool_w.shape[0], pool_w.shape[1]
    ts = _tile(s, 256, 16)

    def body(dm_ref, o_ref, z_ref, p_ref, halo_ref, ga_ref, gb_ref, gnw_ref, pw_ref, ps_ref,
             do_ref, dz_ref, dga_ref, dgb_ref, dpl_ref, dgnw_ref, dpw_ref, dps_ref):
        i = pl.program_id(0)
        first = i == 0
        gnw_v = gnw_ref[...]
        dgnw = jnp.zeros((1, HEAD_DIM), F32)
        for h in range(n_heads):
            cols = slice(h * HEAD_DIM, (h + 1) * HEAD_DIM)
            oh, zh, dm = o_ref[:, cols], z_ref[:, cols].astype(F32), dm_ref[:, cols]
            r = lax.rsqrt(jnp.mean(oh * oh, axis=-1, keepdims=True) + NORM_EPS)
            xh = oh * r
            sz = _sigmoid(zh)
            silu_z = zh * sz
            sa = _sigmoid(ga_ref[:, cols].astype(F32))
            on = xh * gnw_v
            dya = dm * sa
            dga_ref[:, cols] = (dm * on * silu_z * sa * (1.0 - sa)).astype(BF16)
            dz_ref[:, cols] = (dya * on * sz * (1.0 + zh * (1.0 - sz))).astype(BF16)
            don = dya * silu_z
            dgnw = dgnw + jnp.sum(don * xh, axis=0, keepdims=True)
            dxh = don * gnw_v
            do_ref[:, cols] = r * (dxh - xh * jnp.mean(dxh * xh, axis=-1, keepdims=True))
        _accumulate(dgnw_ref, dgnw, first)
        halo = jnp.where(first, 0.0, halo_ref[...].astype(F32))
        for gi, win in enumerate(POOL_WINDOWS):
            cols = slice(gi * pg, (gi + 1) * pg)
            pv, dm = p_ref[:, cols].astype(F32), dm_ref[:, cols]
            pooled = _pooled(jnp.concatenate([halo[:, cols], pv], axis=0), pv, i, ts, win)
            lin = _dot(_mx(pooled), pw_ref[gi])
            psv = ps_ref[:, cols]
            sb = _sigmoid(gb_ref[:, cols].astype(F32))
            dgb_ref[:, cols] = (dm * lin * psv * sb * (1.0 - sb)).astype(BF16)
            dyb = dm * sb
            _accumulate(dps_ref.at[:, cols], jnp.sum(dyb * lin, axis=0, keepdims=True), first)
            dlin = _mx(dyb * psv)
            _accumulate(dpw_ref.at[gi], _dot(_mx(pooled), dlin, ta=True), first)
            dpl_ref[:, cols] = _dot(dlin, pw_ref[gi], tb=True)

    blk = lambda col: pl.BlockSpec((ts, d), lambda i, col=col: (i, col))
    vec = lambda width: pl.BlockSpec((1, width), lambda i: (0, 0))
    pw_spec = pl.BlockSpec((n_groups, pg, pg), lambda i: (0, 0, 0))
    return pl.pallas_call(
        body, name=name, grid=(s // ts,),
        in_specs=[blk(0), blk(0), blk(3), blk(4),
                  pl.BlockSpec((POOL_HALO, d), lambda i: (jnp.maximum(i * (ts // POOL_HALO) - 1, 0), 4)),
                  blk(5), blk(6), vec(HEAD_DIM), pw_spec, vec(d)],
        out_specs=[blk(0), blk(0), blk(0), blk(0), blk(0), vec(HEAD_DIM), pw_spec, vec(d)],
        out_shape=[jax.ShapeDtypeStruct((s, d), F32), jax.ShapeDtypeStruct((s, d), BF16),
                   jax.ShapeDtypeStruct((s, d), BF16), jax.ShapeDtypeStruct((s, d), BF16),
                   jax.ShapeDtypeStruct((s, d), F32), jax.ShapeDtypeStruct((1, HEAD_DIM), F32),
                   jax.ShapeDtypeStruct((n_groups, pg, pg), F32), jax.ShapeDtypeStruct((1, d), F32)],
        compiler_params=_params("arbitrary"))(dmixed, o, proj, proj, proj, proj, proj, gnw.reshape(1, HEAD_DIM),
                                               pool_w, pool_scale.reshape(1, d))


def _pool_bwd(dpooled, *, name):
    s, d = dpooled.shape
    pg = d // len(POOL_WINDOWS)
    ts = _tile(s, 512, 16)
    n_tiles = s // ts
    per = ts // POOL_HALO

    def body(d_ref, next_ref, out_ref):
        i = pl.program_id(0)
        nxt = jnp.where(i < n_tiles - 1, next_ref[...], 0.0)
        for gi, win in enumerate(POOL_WINDOWS):
            cols = slice(gi * pg, (gi + 1) * pg)
            dv = d_ref[:, cols]
            acc = jnp.concatenate([dv, nxt[:, cols]], axis=0) / _pool_counts(i, ts, POOL_HALO, win)
            span = 1
            while span < win:
                acc = acc + pltpu.roll(acc, acc.shape[0] - span, 0)
                span *= 2
            out_ref[:, cols] = (acc[:ts] - dv).astype(BF16)

    return pl.pallas_call(
        body, name=name, grid=(n_tiles,),
        in_specs=[pl.BlockSpec((ts, d), lambda i: (i, 0)),
                  pl.BlockSpec((POOL_HALO, d), lambda i: (jnp.minimum((i + 1) * per, s // POOL_HALO - 1), 0))],
        out_specs=pl.BlockSpec((ts, d), lambda i: (i, 0)), out_shape=jax.ShapeDtypeStruct((s, d), BF16),
        compiler_params=_params("parallel"))(dpooled, dpooled)


def _ffn_tiles(s, f):
    tf = _tile(f, 1408)
    return _tile(s, 512, 16), tf, f // tf


def _ffn_act_fwd(gu, conv_w, conv_b, *, name):
    s, f = gu.shape[0], gu.shape[1] // 2
    width = conv_w.shape[0]
    ts, tf, nf = _ffn_tiles(s, f)

    def body(g_ref, halo_ref, u_ref, w_ref, b_ref, act_ref, gc_ref):
        i = pl.program_id(0)
        gv = g_ref[...].astype(F32)
        cat = jnp.concatenate([jnp.where(i > 0, halo_ref[...].astype(F32), 0.0), gv], axis=0)
        gc = gv * w_ref[pl.ds(width - 1, 1), :] + b_ref[...]
        for sh in range(1, width):
            gc = gc + _rows_before(cat, sh, CONV_HALO) * w_ref[pl.ds(width - 1 - sh, 1), :]
        gc_ref[...] = gc.astype(BF16)
        act_ref[...] = (_gelu(gc) * u_ref[...].astype(F32)).astype(BF16)

    blk = pl.BlockSpec((ts, tf), lambda i, j: (i, j))
    return pl.pallas_call(
        body, name=name, grid=(s // ts, nf),
        in_specs=[blk, pl.BlockSpec((CONV_HALO, tf), lambda i, j: (jnp.maximum(i * (ts // CONV_HALO) - 1, 0), j)),
                  pl.BlockSpec((ts, tf), lambda i, j: (i, nf + j)),
                  pl.BlockSpec((width, tf), lambda i, j: (0, j)), pl.BlockSpec((1, tf), lambda i, j: (0, j))],
        out_specs=[blk, blk],
        out_shape=[jax.ShapeDtypeStruct((s, f), BF16), jax.ShapeDtypeStruct((s, f), BF16)],
        compiler_params=_params("parallel", "parallel"))(gu, gu, gu, conv_w, conv_b.reshape(1, f))


def _ffn_act_bwd(dact, gu, gc, conv_w, *, name):
    s, f = gc.shape
    width = conv_w.shape[0]
    ts, tf, nf = _ffn_tiles(s, f)
    n_tiles = s // ts
    per = ts // CONV_HALO
    rows = ts + CONV_HALO

    def body(da_ref, da_next, gc_ref, gc_next, u_ref, u_next, g_ref, g_prev, w_ref, dgu_ref, dw_ref, db_ref):
        i = pl.program_id(1)
        first = i == 0
        da = jnp.concatenate([da_ref[...], da_next[...]], axis=0).astype(F32)
        gcv = jnp.concatenate([gc_ref[...], gc_next[...]], axis=0).astype(F32)
        uv = jnp.concatenate([u_ref[...], u_next[...]], axis=0).astype(F32)
        gelu, gelu_grad = _gelu_and_grad(gcv)
        dgu_ref[1] = (da[:ts] * gelu[:ts]).astype(BF16)
        live = jnp.logical_or(lax.broadcasted_iota(jnp.int32, (rows, 1), 0) < ts, i < n_tiles - 1)
        dgc = jnp.where(live, da * uv * gelu_grad, 0.0)
        dgate = dgc[:ts] * w_ref[pl.ds(width - 1, 1), :]
        for sh in range(1, width):
            dgate = dgate + _rows_after(dgc, sh, ts) * w_ref[pl.ds(width - 1 - sh, 1), :]
        dgu_ref[0] = dgate.astype(BF16)
        gv = g_ref[...].astype(F32)
        cat = jnp.concatenate([jnp.where(first, 0.0, g_prev[...].astype(F32)), gv], axis=0)
        shifted = [gv] + [_rows_before(cat, sh, CONV_HALO) for sh in range(1, width)]
        dw_rows = [jnp.sum(dgc[:ts] * shifted[width - 1 - j], axis=0, keepdims=True) for j in range(width)]
        _accumulate(dw_ref, jnp.concatenate(dw_rows, axis=0), first)
        _accumulate(db_ref, jnp.sum(dgc[:ts], axis=0, keepdims=True), first)

    nxt_row = lambda i: jnp.minimum((i + 1) * per, s // CONV_HALO - 1)
    main = lambda off: pl.BlockSpec((ts, tf), lambda j, i, off=off: (i, off + j))
    nxt = lambda off: pl.BlockSpec((CONV_HALO, tf), lambda j, i, off=off: (nxt_row(i), off + j))
    return pl.pallas_call(
        body, name=name, grid=(nf, n_tiles),
        in_specs=[main(0), nxt(0), main(0), nxt(0), main(nf), nxt(nf), main(0),
                  pl.BlockSpec((CONV_HALO, tf), lambda j, i: (jnp.maximum(i * per - 1, 0), j)),
                  pl.BlockSpec((width, tf), lambda j, i: (0, j))],
        out_specs=[pl.BlockSpec((2, ts, tf), lambda j, i: (0, i, j)), pl.BlockSpec((width, tf), lambda j, i: (0, j)),
                   pl.BlockSpec((1, tf), lambda j, i: (0, j))],
        out_shape=[jax.ShapeDtypeStruct((2, s, f), BF16),
                   jax.ShapeDtypeStruct((width, f), F32), jax.ShapeDtypeStruct((1, f), F32)],
        compiler_params=_params("parallel", "arbitrary"))(dact, dact, gc, gc, gu, gu, gu, gu, conv_w)


def _rows_layout(bg, n_heads):
    s = bg.shape[0]
    shape = (n_heads, s // GDN_CHUNK, 1, GDN_CHUNK)
    return bg[:, :n_heads].T.reshape(shape), bg[:, n_heads:2 * n_heads].T.reshape(shape)


def _lane_layout(dbeta_rows, dg_rows):
    n_heads = dbeta_rows.shape[0]
    s = dbeta_rows.shape[1] * GDN_CHUNK
    both = jnp.concatenate([dbeta_rows.reshape(n_heads, s), dg_rows.reshape(n_heads, s)], axis=0).T
    return jnp.pad(both, ((0, 0), (0, LANES - 2 * n_heads)))


def _layer_fwd(x, w, late_weights):
    n_heads = w["n_heads"]
    h = _rmsnorm_fwd(x, w["norm_mix_w"], name="norm_mix_fwd")
    proj = _matmul(h, w["w_main"], out_dtype=BF16, name="in_proj_fwd")
    pba = _matmul(h, w["w_ba"], name="ba_proj_fwd")
    qkv = _qkv_fwd(proj, w["conv_qkv_w"], n_heads, name="qkv_fwd")
    bg = _ba_fwd(pba, w["alog_row"], w["dtb_row"], n_heads, name="ba_fwd")
    beta_rows, g_rows = _rows_layout(bg, n_heads)
    o, states = _gdn_fwd(qkv, beta_rows, g_rows, name="gdn_fwd")
    w = dict(w, **late_weights(o))
    mixed = _merge_fwd(proj, o, w["gdn_norm_w"], w["pool_w"], w["pool_scale"], name="merge_fwd")
    x2 = _matmul(mixed, w["w_out"], add=x, name="out_proj_fwd")
    h2 = _rmsnorm_fwd(x2, w["norm_ffn_w"], name="norm_ffn_fwd")
    gu = _matmul(h2, w["w_up_slots"], b_slots=True, out_dtype=BF16, name="up_proj_fwd")
    act, gc = _ffn_act_fwd(gu, w["conv_ffn_w"], w["conv_ffn_b"], name="ffn_act_fwd")
    x3 = _matmul(act, w["w_down"], add=x2, tk=WHOLE_DIM, name="down_proj_fwd")
    saved = dict(x=x, h=h, proj=proj, pba=pba, qkv=qkv, beta_rows=beta_rows, g_rows=g_rows, o=o, states=states,
                 mixed=mixed, x2=x2, h2=h2, gu=gu, gc=gc, act=act)
    return x3, saved, w


def _layer_bwd_ffn(dx3, dx3_bf, w, sv, after):
    g = {}
    dact = _matmul(dx3_bf, w["w_down"], tb=True, tn=1408, out_dtype=BF16, after=after, name="down_proj_dx")
    g["w_down"] = _matmul(sv["act"], dx3_bf, ta=True, tk=WHOLE_DIM, out_dtype=BF16, name="down_proj_dw")
    dgu, g["conv_ffn_w"], g["conv_ffn_b"] = _ffn_act_bwd(dact, sv["gu"], sv["gc"], w["conv_ffn_w"], name="ffn_act_bwd")
    dh2 = _matmul(dgu, w["w_up_slots"], b_slots=True, tb=True, tm=1024, a_halves=True, name="up_proj_dx")
    g["w_up"] = _matmul(sv["h2"], dgu, ta=True, tk=WHOLE_DIM, tn=1408, b_halves=True, out_dtype=BF16,
                        name="up_proj_dw")
    dx2, dx2_bf, g["norm_ffn_w"] = _rmsnorm_bwd(dh2, sv["x2"], w["norm_ffn_w"], dx3, name="norm_ffn_bwd")
    return dx2, dx2_bf, g


def _layer_bwd_mix(dx2, dx2_bf, w, sv, after):
    n_heads = w["n_heads"]
    g = {}
    dmixed = _matmul(dx2_bf, w["w_out"], tb=True, after=after, name="out_proj_dx")
    g["w_out"] = _matmul(sv["mixed"], dx2_bf, ta=True, tk=WHOLE_DIM, out_dtype=BF16, name="out_proj_dw")
    do, dz, dga, dgb, dpooled, g["gdn_norm_w"], g["pool_w"], g["pool_scale"] = _merge_bwd(
        dmixed, sv["proj"], sv["o"], w["gdn_norm_w"], w["pool_w"], w["pool_scale"], name="merge_bwd")
    dp = _pool_bwd(dpooled, name="pool_bwd")
    dqkv, dbeta_rows, dg_rows = _gdn_bwd(do, sv["qkv"], sv["beta_rows"], sv["g_rows"], sv["states"], name="gdn_bwd")
    dproj_qkv, g["conv_qkv_w"] = _qkv_bwd(dqkv, sv["proj"], w["conv_qkv_w"], n_heads, name="qkv_bwd")
    dpba, g["alog_row"], g["dtb_row"] = _ba_bwd(_lane_layout(dbeta_rows, dg_rows), sv["pba"], w["alog_row"],
                                                w["dtb_row"], n_heads, name="ba_bwd")
    dproj = jnp.concatenate([dproj_qkv, dz, dp, dga, dgb], axis=1)
    g["w_main"] = _matmul(sv["h"], dproj, ta=True, tk=WHOLE_DIM, out_dtype=BF16, name="in_proj_dw")
    g["w_ba"] = _matmul(sv["h"], dpba, ta=True, name="ba_proj_dw")
    return dproj, dpba, g


def _layer_bwd_in(dproj, dpba, dx2, w, sv, after):
    dh = _matmul(dproj, w["w_main"], tb=True, tm=1024, after=after, name="in_proj_dx")
    dh = _matmul(dpba, w["w_ba"], tb=True, add=dh, name="ba_proj_dx")
    return _rmsnorm_bwd(dh, sv["x"], w["norm_mix_w"], dx2, name="norm_mix_bwd")


def _here():
    mx, my, mc = (lax.axis_index(a) for a in MESH_AXES)
    return (mx, my, mc), 4 * mx + 2 * my + mc


def _peer(pos, r):
    mx, my, mc = pos
    px = 1 - mx if r & 4 else mx
    py = 1 - my if r & 2 else my
    pc = 1 - mc if r & 1 else mc
    return (px, py, pc), 4 * px + 2 * py + pc


def _run_exchange(n_tensors, src_view, dst_view, sems):
    send_sems, recv_sems, local_sems = sems
    pos, me = _here()
    started = []
    for t in range(n_tensors):
        cp = pltpu.make_async_copy(src_view(t, me), dst_view(t, me), local_sems.at[t])
        cp.start()
        started.append(cp)

    def remote(t, r, landing):
        target, target_lin = _peer(pos, r)
        return pltpu.make_async_remote_copy(
            src_ref=src_view(t, target_lin), dst_ref=dst_view(t, target_lin if landing else me),
            send_sem=send_sems.at[t, r - 1], recv_sem=recv_sems.at[t, r - 1],
            device_id=target, device_id_type=pl.DeviceIdType.MESH)

    sends = []
    for r in range(1, N_DEV):
        for t in range(n_tensors):
            cp = remote(t, r, landing=False)
            cp.start()
            sends.append(cp)
    for r in range(1, N_DEV):
        for t in range(n_tensors):
            remote(t, r, landing=True).wait_recv()
    for cp in sends:
        cp.wait_send()
    for cp in started:
        cp.wait()


def _exchange_scratch(n_tensors):
    return [pltpu.SemaphoreType.DMA((n_tensors, N_DEV - 1)), pltpu.SemaphoreType.DMA((n_tensors, N_DEV - 1)),
            pltpu.SemaphoreType.DMA((n_tensors,))]


def _slot_view(ref, axis, index):
    return ref.at[(slice(None),) * axis + (index,)]


def _gather(srcs, slot_axes, *, name, after=None):
    n = len(srcs)
    n_in = n + (after is not None)

    def body(*refs):
        src_refs, out_refs = refs[:n], refs[n_in:n_in + n]
        _run_exchange(n, lambda t, to: src_refs[t], lambda t, frm: _slot_view(out_refs[t], slot_axes[t], frm),
                      refs[n_in + n:])

    hbm = pl.BlockSpec(memory_space=pltpu.HBM)
    out_shape = [jax.ShapeDtypeStruct(s.shape[:a] + (N_DEV,) + s.shape[a:], s.dtype) for s, a in zip(srcs, slot_axes)]
    in_specs = [hbm] * n + ([pl.BlockSpec(memory_space=pl.ANY)] if after is not None else [])
    return pl.pallas_call(body, name=name, in_specs=in_specs, out_specs=[hbm] * n, out_shape=out_shape,
                          scratch_shapes=_exchange_scratch(n))(*srcs, *([after] if after is not None else []))


_SIDE_EFFECT = pltpu.SideEffectType.DATAFLOW_SIDE_EFFECTING


def _split_copy(t, r, pos, src_refs, land_refs, send_sems, recv_sems, src_view, dst_view, landing):
    _, me = _here()
    target, target_lin = _peer(pos, r)
    return pltpu.make_async_remote_copy(
        src_ref=src_view(t, src_refs[t], target_lin), dst_ref=dst_view(t, land_refs[t], target_lin if landing else me),
        send_sem=send_sems.at[t * (N_DEV - 1) + r - 1], recv_sem=recv_sems.at[t * (N_DEV - 1) + r - 1],
        device_id=target, device_id_type=pl.DeviceIdType.MESH)


def _start_exchange(srcs, lands, src_view, dst_view, after, *, name):
    n = len(srcs)
    has_after = after is not None

    def body(*refs):
        src_refs, land_refs = refs[:n], refs[n:2 * n]
        outs = refs[2 * n + has_after:]
        send_sems, recv_sems, token = outs[0], outs[1], outs[2 + 2 * n]
        pos, _ = _here()
        for r in range(1, N_DEV):
            for t in range(n):
                _split_copy(t, r, pos, src_refs, land_refs, send_sems, recv_sems, src_view, dst_view, False).start()
        token[...] = jnp.zeros_like(token)

    hbm = pl.BlockSpec(memory_space=pltpu.HBM)
    sem = pl.BlockSpec(memory_space=pltpu.SEMAPHORE)
    sem_shape = pltpu.SemaphoreType.DMA((n * (N_DEV - 1),))
    through = [pltpu.HBM(t.shape, t.dtype) for t in list(srcs) + list(lands)]
    args = [pltpu.with_memory_space_constraint(t, pltpu.HBM) for t in list(srcs) + list(lands)]
    outs = pl.pallas_call(
        body, name=name, in_specs=[hbm] * (2 * n) + ([pl.BlockSpec(memory_space=pl.ANY)] if has_after else []),
        out_specs=(sem, sem, *[hbm] * (2 * n), pl.BlockSpec(memory_space=pltpu.VMEM)),
        out_shape=(sem_shape, sem_shape, *through, jax.ShapeDtypeStruct((8, LANES), F32)),
        input_output_aliases={i: 2 + i for i in range(2 * n)},
        compiler_params=pltpu.CompilerParams(has_side_effects=_SIDE_EFFECT))(*args, *([after] if has_after else []))
    return outs[0], outs[1], list(outs[2:2 + n]), list(outs[2 + n:2 + 2 * n]), outs[-1]


def _wait_exchange(send_sems, recv_sems, srcs, lands, src_view, dst_view, after, *, name):
    n = len(srcs)

    def body(*refs):
        src_refs, land_refs = refs[:n], refs[n:2 * n]
        send_refs, recv_refs = refs[2 * n], refs[2 * n + 1]
        pos, _ = _here()
        for r in range(1, N_DEV):
            for t in range(n):
                cp = _split_copy(t, r, pos, src_refs, land_refs, send_refs, recv_refs, src_view, dst_view, True)
                cp.wait_send()
                cp.wait_recv()

    hbm = pl.BlockSpec(memory_space=pltpu.HBM)
    sem = pl.BlockSpec(memory_space=pltpu.SEMAPHORE)
    outs = pl.pallas_call(
        body, name=name, in_specs=[hbm] * (2 * n) + [sem, sem, pl.BlockSpec(memory_space=pl.ANY)],
        out_specs=[hbm] * (2 * n), out_shape=[pltpu.HBM(t.shape, t.dtype) for t in list(srcs) + list(lands)],
        input_output_aliases={i: i for i in range(2 * n)},
        compiler_params=pltpu.CompilerParams(has_side_effects=_SIDE_EFFECT))(*srcs, *lands, send_sems, recv_sems, after)
    return list(outs[n:])


def _sum_slots(parts, *, name):
    _, n_lead, r_rows, cols = parts.shape
    tr = _tile(r_rows, max(16, (1 << 17) // cols // 16 * 16), 16)

    def body(p_ref, o_ref):
        total = p_ref[0].astype(F32)
        for p in range(1, N_DEV):
            total = total + p_ref[p].astype(F32)
        o_ref[...] = total

    return pl.pallas_call(
        body, name=name, grid=(n_lead, r_rows // tr),
        in_specs=[pl.BlockSpec((N_DEV, None, tr, cols), lambda a, i: (0, a, i, 0))],
        out_specs=pl.BlockSpec((None, tr, cols), lambda a, i: (a, i, 0)),
        out_shape=jax.ShapeDtypeStruct((n_lead, r_rows, cols), F32),
        compiler_params=_params("parallel", "parallel"))(parts)


_WinLayout = collections.namedtuple("_WinLayout", "shard_w n_main ba_dev ba_off n_ba slot_w")


def _win_layout(shard_w, n_main, ba_start, n_ba):
    ba_dev = ba_start // shard_w
    assert (ba_start + n_ba - 1) // shard_w == ba_dev and n_main % LANES == 0
    slot_w = -(-(LANES - 1 + shard_w) // LANES) * LANES
    return _WinLayout(shard_w, n_main, ba_dev, ba_start - ba_dev * shard_w, n_ba, slot_w)


def _main_start(lay, dev):
    return lay.shard_w * dev - jnp.where(dev > lay.ba_dev, lay.n_ba, 0)


def _slab_origin(lay, dev):
    return jnp.minimum(_main_start(lay, dev) // LANES * LANES, lay.n_main - lay.slot_w)


def _assemble_plan(lay):
    plan = [[] for _ in range(lay.n_main // LANES)]
    for dev in range(N_DEV):
        start = lay.shard_w * dev - (lay.n_ba if dev > lay.ba_dev else 0)
        width = lay.shard_w - (lay.n_ba if dev == lay.ba_dev else 0)
        origin = min(start // LANES, (lay.n_main - lay.slot_w) // LANES)
        pad = start - origin * LANES
        for t in range(pad // LANES, (pad + width - 1) // LANES + 1):
            plan[origin + t].append((dev, t))
    return plan


def _assemble_w_main(slabs, lay, *, name):
    _, d, slot_w = slabs.shape
    plan = _assemble_plan(lay)
    runs = []
    shared = []
    for tile, parts in enumerate(plan):
        if len(parts) != 1:
            shared.append((tile, parts))
        elif runs and runs[-1][2] == parts[0][0] and runs[-1][0] + runs[-1][1] == tile:
            runs[-1][1] += 1
        else:
            runs.append([tile, 1, parts[0][0], parts[0][1]])
    tr = _tile(d, 256, 16)

    def body(in_ref, out_ref):
        for first, count, dev, t0 in runs:
            out_ref[:, first * LANES:(first + count) * LANES] = in_ref[dev, :, t0 * LANES:(t0 + count) * LANES]
        for tile, parts in shared:
            total = in_ref[parts[0][0], :, parts[0][1] * LANES:(parts[0][1] + 1) * LANES]
            for dev, t in parts[1:]:
                total = total + in_ref[dev, :, t * LANES:(t + 1) * LANES]
            out_ref[:, tile * LANES:(tile + 1) * LANES] = total

    return pl.pallas_call(
        body, name=name, grid=(d // tr,),
        in_specs=[pl.BlockSpec((N_DEV, tr, slot_w), lambda i: (0, i, 0))],
        out_specs=pl.BlockSpec((tr, lay.n_main), lambda i: (i, 0)),
        out_shape=jax.ShapeDtypeStruct((d, lay.n_main), slabs.dtype), compiler_params=_params("parallel"))(slabs)


def _adam_update(w, g, m, v):
    nm = ADAM_B1 * m + (1.0 - ADAM_B1) * g
    nv = ADAM_B2 * v + (1.0 - ADAM_B2) * (g * g)
    m_hat = nm / (1.0 - ADAM_B1 ** ADAM_STEP)
    v_hat = nv / (1.0 - ADAM_B2 ** ADAM_STEP)
    return -ADAM_LR * (m_hat / (jnp.sqrt(v_hat) + ADAM_EPS) + ADAM_WD * w), nm, nv


def _adamw(w, g, m, v, *, name):
    rows, cols = w.shape
    tr = _tile(rows, max(8, (1 << 18) // cols // 8 * 8), 8)

    def body(w_ref, g_ref, m_ref, v_ref, d_ref, nm_ref, nv_ref):
        d_ref[...], nm_ref[...], nv_ref[...] = _adam_update(w_ref[...], g_ref[...], m_ref[...], v_ref[...])

    blk = pl.BlockSpec((tr, cols), lambda i: (i, 0))
    out = jax.ShapeDtypeStruct((rows, cols), F32)
    return pl.pallas_call(
        body, name=name, grid=(rows // tr,), in_specs=[blk] * 4, out_specs=[blk] * 3, out_shape=[out] * 3,
        compiler_params=_params("parallel"))(w, g, m, v)


def _adamw_nd(w, g, m, v, *, name):
    two_d = (-1, w.shape[-1])
    outs = _adamw(w.reshape(two_d), g.reshape(two_d), m.reshape(two_d), v.reshape(two_d), name=name)
    return tuple(t.reshape(w.shape) for t in outs)


def _adamw_slots(parts, w, m, v, after, *, name):
    n_layers, rows, cols = w.shape
    tr = _tile(rows, max(16, (1 << 18) // cols // 16 * 16), 16)

    def body(p_ref, w_ref, m_ref, v_ref, after_ref, g_ref, d_ref, nm_ref, nv_ref):
        total = p_ref[0].astype(F32)
        for p in range(1, N_DEV):
            total = total + p_ref[p].astype(F32)
        g_ref[...] = total
        d_ref[...], nm_ref[...], nv_ref[...] = _adam_update(w_ref[...], total, m_ref[...], v_ref[...])

    blk = pl.BlockSpec((None, tr, cols), lambda a, i: (a, i, 0))
    out = jax.ShapeDtypeStruct((n_layers, rows, cols), F32)
    return pl.pallas_call(
        body, name=name, grid=(n_layers, rows // tr),
        in_specs=[pl.BlockSpec((N_DEV, None, tr, cols), lambda a, i: (0, a, i, 0)), blk, blk, blk,
                  pl.BlockSpec(memory_space=pl.ANY)],
        out_specs=[blk] * 4, out_shape=[out] * 4,
        compiler_params=_params("parallel", "parallel"))(parts, w, m, v, after)


def _pack_rows(parts, dtype, quantum_rows):
    flat = jnp.concatenate([p.reshape(-1).astype(dtype) for p in parts])
    n = flat.shape[0]
    padded = -(-n // (LANES * quantum_rows)) * (LANES * quantum_rows)
    return jnp.pad(flat, (0, padded - n)).reshape(padded // LANES, LANES)


def _unpack(flat, shapes):
    lead = flat.shape[:-1]
    out, at = [], 0
    for shape in shapes:
        size = 1
        for dim in shape:
            size *= dim
        out.append(flat[..., at:at + size].reshape(lead + tuple(shape)))
        at += size
    return out


def _whole_from_slots(slots, axis):
    moved = jnp.moveaxis(slots, 0, axis)
    shape = moved.shape
    return moved.reshape(shape[:axis] + (shape[axis] * shape[axis + 1],) + shape[axis + 2:])


def _lane_row(vec, n_heads):
    return jnp.pad(vec, ((0, 0), (n_heads, LANES - 2 * n_heads)))[:, None, :]


REPLICATED = ("norm_mix_w", "a_log", "dt_bias", "gdn_norm_w", "pool_scale", "norm_ffn_w", "conv_ffn_b",
              "norm_final_w")
WEIGHTS = ("norm_mix_w", "w_in", "conv_qkv_w", "a_log", "dt_bias", "gdn_norm_w", "pool_w", "pool_scale", "w_out",
           "norm_ffn_w", "w_up", "conv_ffn_w", "conv_ffn_b", "w_down", "norm_final_w")
SMALL_QUANTUM_ROWS = 512


def kernel(x, norm_mix_w, w_in, conv_qkv_w, a_log, dt_bias, gdn_norm_w, pool_w, pool_scale, w_out, norm_ffn_w, w_up, conv_ffn_w, conv_ffn_b, w_down, norm_final_w, loss_target, m_norm_mix_w, m_w_in, m_conv_qkv_w, m_a_log, m_dt_bias, m_gdn_norm_w, m_pool_w, m_pool_scale, m_w_out, m_norm_ffn_w, m_w_up, m_conv_ffn_w, m_conv_ffn_b, m_w_down, m_norm_final_w, v_norm_mix_w, v_w_in, v_conv_qkv_w, v_a_log, v_dt_bias, v_gdn_norm_w, v_pool_w, v_pool_scale, v_w_out, v_norm_ffn_w, v_w_up, v_conv_ffn_w, v_conv_ffn_b, v_w_down, v_norm_final_w):
    local = dict(norm_mix_w=norm_mix_w, w_in=w_in, conv_qkv_w=conv_qkv_w, a_log=a_log, dt_bias=dt_bias,
                 gdn_norm_w=gdn_norm_w, pool_w=pool_w, pool_scale=pool_scale, w_out=w_out, norm_ffn_w=norm_ffn_w,
                 w_up=w_up, conv_ffn_w=conv_ffn_w, conv_ffn_b=conv_ffn_b, w_down=w_down, norm_final_w=norm_final_w)
    mom_m = dict(norm_mix_w=m_norm_mix_w, w_in=m_w_in, conv_qkv_w=m_conv_qkv_w, a_log=m_a_log, dt_bias=m_dt_bias,
                 gdn_norm_w=m_gdn_norm_w, pool_w=m_pool_w, pool_scale=m_pool_scale, w_out=m_w_out,
                 norm_ffn_w=m_norm_ffn_w, w_up=m_w_up, conv_ffn_w=m_conv_ffn_w, conv_ffn_b=m_conv_ffn_b,
                 w_down=m_w_down, norm_final_w=m_norm_final_w)
    mom_v = dict(norm_mix_w=v_norm_mix_w, w_in=v_w_in, conv_qkv_w=v_conv_qkv_w, a_log=v_a_log, dt_bias=v_dt_bias,
                 gdn_norm_w=v_gdn_norm_w, pool_w=v_pool_w, pool_scale=v_pool_scale, w_out=v_w_out,
                 norm_ffn_w=v_norm_ffn_w, w_up=v_w_up, conv_ffn_w=v_conv_ffn_w, conv_ffn_b=v_conv_ffn_b,
                 w_down=v_w_down, norm_final_w=v_norm_final_w)
    n_layers, n_heads = a_log.shape
    d_model = x.shape[-1]
    dl = n_heads * HEAD_DIM
    n_ba = 2 * n_heads
    assert x.shape[0] == 1 and dl == d_model and pool_scale.shape[1] == d_model
    lay = _win_layout(w_in.shape[2], N_DEV * w_in.shape[2] - n_ba, 4 * dl, n_ba)
    _, me = _here()
    is_ba_dev = me == lay.ba_dev
    my_pad = _main_start(lay, me) - _slab_origin(lay, me)
    ba_cols = slice(lay.ba_off, lay.ba_off + n_ba)

    w_in_bf = w_in.astype(BF16)
    without_ba = jnp.concatenate([w_in_bf[..., :lay.ba_off], w_in_bf[..., lay.ba_off + n_ba:],
                                  jnp.zeros(w_in.shape[:2] + (n_ba,), BF16)], axis=-1)
    slab = lax.dynamic_update_slice(jnp.zeros(w_in.shape[:2] + (lay.slot_w,), BF16),
                                    jnp.where(is_ba_dev, without_ba, w_in_bf), (0, 0, my_pad))
    ba_part = jnp.pad(jnp.where(is_ba_dev, w_in_bf[..., ba_cols], jnp.zeros((), BF16)),
                      ((0, 0), (0, 0), (0, LANES - n_ba)))
    convs = _pack_rows([conv_qkv_w, conv_ffn_w], F32, 16)
    ba_slots, conv_slots = _gather([ba_part, convs], [0, 0], name="gather_small_weights")
    conv_parts = _unpack(conv_slots.reshape(N_DEV, -1), [conv_qkv_w.shape, conv_ffn_w.shape])
    conv_qkv_whole, conv_ffn_whole = (_whole_from_slots(p, 2) for p in conv_parts)
    alog_rows, dtb_rows = _lane_row(a_log, n_heads), _lane_row(dt_bias, n_heads)

    def with_own_slot(own, axis):
        zone = lax.empty(own.shape[:axis] + (N_DEV,) + own.shape[axis:], own.dtype)
        return lax.dynamic_update_slice(zone, jnp.expand_dims(own, axis), (0,) * axis + (me,) + (0,) * (own.ndim - axis))

    slot_axis = dict(slab=0, w_up=0, w_out=0, w_down=0, pool_w=1)
    gather_groups = (("slab",), ("w_up", "w_out", "w_down", "pool_w"))
    gather_src = lambda t, ref, to: ref
    in_flight = {}
    token = conv_slots
    for l in range(n_layers):
        own = dict(slab=slab[l], w_up=w_up[l].astype(BF16), w_out=w_out[l].astype(BF16),
                   w_down=w_down[l].astype(BF16), pool_w=pool_w[l].astype(BF16))
        for part, names in zip("ab", gather_groups):
            axes = [slot_axis[n] for n in names]
            dst = lambda t, ref, frm, axes=axes: _slot_view(ref, axes[t], frm)
            *handles, token = _start_exchange([own[n] for n in names], [with_own_slot(own[n], slot_axis[n]) for n in names],
                                              gather_src, dst, token, name="gather_start_%d%s" % (l, part))
            in_flight[l, part] = (handles, dst)

    def arrived(l, part, after):
        (send_sems, recv_sems, srcs, lands), dst = in_flight[l, part]
        return _wait_exchange(send_sems, recv_sems, srcs, lands, gather_src, dst, after, name="gather_wait_%d%s" % (l, part))

    xc = x[0]
    layer_w, saved = [], []
    after = token
    for l in range(n_layers):
        slabs, = arrived(l, "a", after)
        early = dict(n_heads=n_heads, norm_mix_w=norm_mix_w[l], alog_row=alog_rows[l], dtb_row=dtb_rows[l],
                     w_main=_assemble_w_main(slabs, lay, name="assemble_w_main"), w_ba=ba_slots[lay.ba_dev, l],
                     conv_qkv_w=conv_qkv_whole[l])

        def late_weights(o, l=l):
            up_slots, out_slots, down_slots, pool_slots = arrived(l, "b", o)
            return dict(norm_ffn_w=norm_ffn_w[l], gdn_norm_w=gdn_norm_w[l], pool_scale=pool_scale[l],
                        conv_ffn_b=conv_ffn_b[l], conv_ffn_w=conv_ffn_whole[l], w_up_slots=up_slots,
                        w_out=out_slots.reshape(-1, d_model), w_down=down_slots.reshape(-1, d_model),
                        pool_w=pool_slots.reshape(pool_slots.shape[0], -1, pool_slots.shape[-1]))

        xc, sv, wl = _layer_fwd(xc, early, late_weights)
        layer_w.append(wl)
        saved.append(sv)
        after = xc
    loss_row, dx, dx_bf, d_final = _loss_head(xc, norm_final_w, loss_target[0], name="loss_head")
    loss = lax.psum(loss_row[0, 0], MESH_AXES)

    shard = {n: local[n].shape[1:] for n in ("w_up", "w_out", "w_down", "pool_w")}
    shard["w_main"] = (d_model, lay.slot_w)
    recvs = {n: lax.empty((N_DEV, n_layers) + shard[n], BF16) for n in shard}
    up_w, out_rows, down_rows, pool_rows = shard["w_up"][1], shard["w_out"][0], shard["w_down"][0], shard["pool_w"][1]
    owned = dict(w_main=lambda to: (1, _slab_origin(lay, to), lay.slot_w), w_up=lambda to: (1, to * up_w, up_w),
                 w_out=lambda to: (0, to * out_rows, out_rows), w_down=lambda to: (0, to * down_rows, down_rows),
                 pool_w=lambda to: (1, to * pool_rows, pool_rows))
    scatter_groups = dict(e=("w_up", "w_down"), l=("w_main", "w_out", "pool_w"))
    pending = {}

    def send_grads(part, l, g):
        names = scatter_groups[part]

        def src(t, ref, to):
            axis, first, length = owned[names[t]](to)
            return ref.at[(slice(None),) * axis
                          + (pl.ds(pl.multiple_of(first, LANES if axis == ref.ndim - 1 else 16), length),)]

        dst = lambda t, ref, frm: ref.at[frm, l]
        received(part, g[names[0]])
        lands = []
        for n in names:
            axis, first, length = owned[n](me)
            mine = lax.dynamic_slice_in_dim(g[n], first, length, axis=axis)
            lands.append(lax.dynamic_update_slice(recvs[n], mine[None, None], (me, l) + (0,) * mine.ndim))
        send_sems, recv_sems, grads, lands, token = _start_exchange([g[n] for n in names], lands, src, dst, None,
                                                                    name="scatter_start_%d%s" % (l, part))
        pending[part] = (send_sems, recv_sems, grads, lands, src, dst, "scatter_wait_%d%s" % (l, part))
        return token

    def received(part, after):
        if part in pending:
            *args, name = pending.pop(part)
            recvs.update(zip(scatter_groups[part], _wait_exchange(*args, after=after, name=name)))

    layer_grads = [None] * n_layers
    token = None
    for l in reversed(range(n_layers)):
        dx2, dx2_bf, g = _layer_bwd_ffn(dx, dx_bf, layer_w[l], saved[l], token)
        token = send_grads("e", l, g)
        dproj, dpba, g_mix = _layer_bwd_mix(dx2, dx2_bf, layer_w[l], saved[l], token)
        g.update(g_mix)
        g["pool_w"] = g["pool_w"].astype(BF16)
        token = send_grads("l", l, g)
        dx, dx_bf, g["norm_mix_w"] = _layer_bwd_in(dproj, dpba, dx2, layer_w[l], saved[l], token)
        layer_grads[l] = g
    received("e", token)
    grad_x = dx
    stack = lambda name: jnp.stack([g[name] for g in layer_grads])

    grad, delta, new_m, new_v = {}, {}, {}, {}
    flat = lambda t, lead: t.reshape(t.shape[:lead] + (-1, t.shape[-1]))

    def update_shard(n, after):
        outs = _adamw_slots(flat(recvs[n], 2), flat(local[n], 1), flat(mom_m[n], 1), flat(mom_v[n], 1), after,
                            name="adamw_" + n)
        grad[n], delta[n], new_m[n], new_v[n] = (t.reshape(local[n].shape) for t in outs)

    update_shard("w_up", token)
    update_shard("w_down", token)

    small_names = REPLICATED + ("conv_qkv_w", "conv_ffn_w", "w_ba")
    g_small = dict(norm_mix_w=stack("norm_mix_w")[:, 0], a_log=stack("alog_row")[:, 0, n_heads:n_ba],
                   dt_bias=stack("dtb_row")[:, 0, n_heads:n_ba], gdn_norm_w=stack("gdn_norm_w")[:, 0],
                   pool_scale=stack("pool_scale")[:, 0], norm_ffn_w=stack("norm_ffn_w")[:, 0],
                   conv_ffn_b=stack("conv_ffn_b")[:, 0], norm_final_w=d_final[0], conv_qkv_w=stack("conv_qkv_w"),
                   conv_ffn_w=stack("conv_ffn_w"), w_ba=stack("w_ba")[..., :n_ba])
    small_shapes = [g_small[n].shape for n in small_names]
    small_slots, = _gather([_pack_rows([g_small[n] for n in small_names], F32, SMALL_QUANTUM_ROWS)], [0],
                           name="gather_small_grads", after=new_v["w_down"])
    small_sum = _sum_slots(small_slots[:, None], name="sum_small_grads")
    grad.update(zip(small_names, _unpack(small_sum.reshape(-1), small_shapes)))
    for n in ("conv_qkv_w", "conv_ffn_w"):
        width = local[n].shape[2]
        grad[n] = lax.dynamic_slice_in_dim(grad[n], me * width, width, axis=2)

    received("l", small_sum)
    update_shard("w_out", token)
    update_shard("pool_w", token)
    main_sum = _sum_slots(recvs["w_main"], name="sum_w_main_grads")
    g_main = lax.dynamic_slice_in_dim(main_sum, my_pad, lay.shard_w, axis=2)
    with_ba = jnp.concatenate([g_main[..., :lay.ba_off], grad.pop("w_ba"),
                               g_main[..., lay.ba_off:lay.shard_w - n_ba]], axis=-1)
    grad["w_in"] = jnp.where(is_ba_dev, with_ba, g_main)
    for n in ("w_in", "conv_qkv_w", "conv_ffn_w"):
        delta[n], new_m[n], new_v[n] = _adamw_nd(local[n], grad[n], mom_m[n], mom_v[n], name="adamw_" + n)
    packed = [_pack_rows([src[n] for n in REPLICATED], F32, 8) for src in (local, grad, mom_m, mom_v)]
    rep_out = _adamw(*packed, name="adamw_replicated")
    rep_shapes = [local[n].shape for n in REPLICATED]
    for dst, arr in zip((delta, new_m, new_v), rep_out):
        dst.update(zip(REPLICATED, _unpack(arr.reshape(-1), rep_shapes)))

    return (loss, grad_x[None], *[grad[n] for n in WEIGHTS], *[delta[n] for n in WEIGHTS],
            *[new_m[n] for n in WEIGHTS], *[new_v[n] for n in WEIGHTS])
```
